```python
import math
import jax, jax.numpy as jnp
from jax import lax
import numpy as np

D_MODEL = 2048
BATCH = 4
SEQ = 2048
DEPTH = 2
DEC_BATCH = 128
DEC_SEQ = 8
PAST_LEN = 16384
PAGE_SIZE = 128

N_META = 16
EPS = 1e-6
MLSTM_HEADS = 4
MLSTM_DV = D_MODEL // MLSTM_HEADS
MLSTM_DK = MLSTM_DV // 2
MLSTM_CHUNK = 64
QK_WIDTH = MLSTM_HEADS * MLSTM_DK
V_WIDTH = MLSTM_HEADS * MLSTM_DV
MLSTM_IN_WIDTH = 2 * QK_WIDTH + 2 * V_WIDTH + 2 * MLSTM_HEADS
POOL_WINDOWS = (2, 4, 8, 16)
POOL_GROUPS = len(POOL_WINDOWS)
POOL_GROUP_DIM = D_MODEL // POOL_GROUPS
POOL_PAD = max(POOL_WINDOWS) - 1
D_FF = (8 * D_MODEL // 3 + 127) // 128 * 128
CONV_WIDTH = 3
CONV_PAD = CONV_WIDTH - 1
N_MLSTM_LAYERS = (DEPTH + 1) // 2
N_POOL_LAYERS = DEPTH // 2

kernel_name = 'mlstm_pool_hybrid_decode_step'


def rms_norm(x, g):
    xf = x.astype(jnp.float32)
    y = xf * lax.rsqrt(jnp.mean(xf * xf, axis=-1, keepdims=True) + EPS)
    return (y * g.astype(jnp.float32)).astype(x.dtype)


def mlstm_chunk(carry, inp):
    C, n, m = carry
    q, k, v, ig, lf = inp
    L = q.shape[2]
    b = jnp.cumsum(lf, axis=-1)
    causal = jnp.tril(jnp.ones((L, L), dtype=bool))
    d = jnp.where(causal, b[..., :, None] - b[..., None, :] + ig[..., None, :], -jnp.inf)
    inter = b + m[..., None]
    m_t = jnp.maximum(inter, jnp.max(d, axis=-1))
    w_inter = jnp.exp(inter - m_t)
    s = jnp.einsum('bhtk,bhsk->bhts', q, k) * jnp.exp(d - m_t[..., None])
    num = w_inter[..., None] * jnp.einsum('bhtk,bhkv->bhtv', q, C) + jnp.einsum('bhts,bhsv->bhtv', s, v)
    den = w_inter * jnp.einsum('bhtk,bhk->bht', q, n) + jnp.sum(s, axis=-1)
    h = num / jnp.maximum(jnp.abs(den), jnp.exp(-m_t))[..., None]
    m_new = m_t[..., -1]
    decay_prev = jnp.exp(b[..., -1] + m - m_new)
    w_k = jnp.exp(b[..., -1:] - b + ig - m_new[..., None])
    wk = w_k[..., None] * k
    C_new = decay_prev[..., None, None] * C + jnp.einsum('bhsk,bhsv->bhkv', wk, v)
    n_new = decay_prev[..., None] * n + jnp.sum(wk, axis=2)
    return (C_new, n_new, m_new), h


def mlstm_scan(q, k, v, ig, lf, state):
    B, H, S, _ = q.shape
    L = math.gcd(S, MLSTM_CHUNK)
    nc = S // L

    def split(a):
        a = a.reshape(a.shape[:2] + (nc, L) + a.shape[3:])
        return jnp.moveaxis(a, 2, 0)

    state, h = lax.scan(mlstm_chunk, state, (split(q), split(k), split(v), split(ig), split(lf)))
    h = jnp.moveaxis(h, 0, 2).reshape(B, H, S, MLSTM_DV)
    return h, state


def mlstm_mixer(u, w_in, b_gate, g_out, w_out, state, lead):
    B, S, _ = u.shape
    f32 = jnp.float32
    proj = u @ w_in
    q, k, v, o, gates = jnp.split(proj, [QK_WIDTH, 2 * QK_WIDTH, 2 * QK_WIDTH + V_WIDTH, 2 * QK_WIDTH + 2 * V_WIDTH], axis=-1)

    def heads(a, dh):
        return a.reshape(B, S, MLSTM_HEADS, dh).transpose(0, 2, 1, 3).astype(f32)

    q = heads(q, MLSTM_DK) * (MLSTM_DK ** -0.5)
    k = heads(k, MLSTM_DK)
    v = heads(v, MLSTM_DV)
    gates = (gates.astype(f32) + b_gate.astype(f32)).transpose(0, 2, 1)
    ig = gates[:, :MLSTM_HEADS]
    lf = jax.nn.log_sigmoid(gates[:, MLSTM_HEADS:])
    state = tuple(s.astype(f32) for s in state)
    if lead > 0:
        h0, state = mlstm_scan(q[:, :, :lead], k[:, :, :lead], v[:, :, :lead], ig[:, :, :lead], lf[:, :, :lead], state)
        h1, state = mlstm_scan(q[:, :, lead:], k[:, :, lead:], v[:, :, lead:], ig[:, :, lead:], lf[:, :, lead:], state)
        h = jnp.concatenate([h0, h1], axis=2)
    else:
        h, state = mlstm_scan(q, k, v, ig, lf, state)
    h = h * lax.rsqrt(jnp.mean(h * h, axis=-1, keepdims=True) + EPS)
    h = h.transpose(0, 2, 1, 3).reshape(B, S, V_WIDTH) * g_out.astype(f32)
    h = h * jax.nn.sigmoid(o.astype(f32))
    return h.astype(u.dtype) @ w_out, state


def pool_mixer(u, buf, pos0, w_pool, scale):
    B, S, D = u.shape
    f32 = jnp.float32
    ext = jnp.concatenate([buf.astype(f32), u.astype(f32)], axis=1)
    cs = jnp.concatenate([jnp.zeros((B, 1, D), f32), jnp.cumsum(ext, axis=1)], axis=1)
    end = cs[:, POOL_PAD + 1:]
    pos = pos0 + jnp.arange(S)
    groups = []
    for gi, w in enumerate(POOL_WINDOWS):
        sl = slice(gi * POOL_GROUP_DIM, (gi + 1) * POOL_GROUP_DIM)
        start = cs[:, POOL_PAD + 1 - w:POOL_PAD + 1 - w + S, sl]
        cnt = jnp.minimum(w, pos + 1).astype(f32)[None, :, None]
        groups.append((end[..., sl] - start) / cnt)
    pooled = jnp.concatenate(groups, axis=-1) - ext[:, POOL_PAD:]
    y = jnp.einsum('bsgc,gcd->bsgd', pooled.reshape(B, S, POOL_GROUPS, POOL_GROUP_DIM), w_pool.astype(f32))
    y = y.reshape(B, S, D) * scale.astype(f32)
    return y.astype(u.dtype), ext[:, -POOL_PAD:].astype(u.dtype)


def conv_ffn(u, buf, w_up, conv_w, conv_b, w_down):
    S = u.shape[1]
    g, a = jnp.split(u @ w_up, 2, axis=-1)
    ext = jnp.concatenate([buf.astype(g.dtype), g], axis=1)
    gc = conv_b + sum(conv_w[j] * ext[:, j:j + S] for j in range(CONV_WIDTH))
    h = jax.nn.silu(gc) * a
    return h @ w_down, ext[:, -CONV_PAD:]


def trunk(x, st_C, st_n, st_m, st_pool, st_conv, lead, pos0, norm_mix, norm_ffn, norm_final,
          w_mlstm_in, b_mlstm_gate, g_mlstm_out, w_mlstm_out, w_pool, pool_scale, w_up, conv_w, conv_b, w_down):
    new_C, new_n, new_m, new_pool, new_conv = [], [], [], [], []
    for layer in range(DEPTH):
        j = layer // 2
        u = rms_norm(x, norm_mix[layer])
        if layer % 2 == 0:
            y, (C, n, m) = mlstm_mixer(u, w_mlstm_in[j], b_mlstm_gate[j], g_mlstm_out[j], w_mlstm_out[j],
                                       (st_C[j], st_n[j], st_m[j]), lead)
            new_C.append(C)
            new_n.append(n)
            new_m.append(m)
        else:
            y, pbuf = pool_mixer(u, st_pool[j], pos0, w_pool[j], pool_scale[j])
            new_pool.append(pbuf)
        x = x + y.astype(x.dtype)
        u = rms_norm(x, norm_ffn[layer])
        y, cbuf = conv_ffn(u, st_conv[layer], w_up[layer], conv_w[layer], conv_b[layer], w_down[layer])
        new_conv.append(cbuf)
        x = x + y.astype(x.dtype)
    out = rms_norm(x, norm_final)
    return (out, jnp.stack(new_C), jnp.stack(new_n), jnp.stack(new_m), jnp.stack(new_pool), jnp.stack(new_conv))


def setup_inputs(seed: int = 0) -> dict:
    key = jax.random.key(seed)
    ks = jax.random.split(key, 24)
    nrm = jax.random.normal
    H = MLSTM_HEADS
    x_prompt = nrm(ks[0], (BATCH, SEQ, D_MODEL), jnp.float32)
    x_sample = nrm(ks[1], (DEC_BATCH, DEC_SEQ, D_MODEL), jnp.float32)
    state_mlstm_C = 0.5 * nrm(ks[2], (N_MLSTM_LAYERS, DEC_BATCH, H, MLSTM_DK, MLSTM_DV), jnp.float32)
    state_mlstm_n = nrm(ks[3], (N_MLSTM_LAYERS, DEC_BATCH, H, MLSTM_DK), jnp.float32)
    state_mlstm_m = nrm(ks[4], (N_MLSTM_LAYERS, DEC_BATCH, H), jnp.float32)
    state_pool = nrm(ks[5], (N_POOL_LAYERS, DEC_BATCH, POOL_PAD, D_MODEL), jnp.float32)
    state_ffn_conv = nrm(ks[6], (DEPTH, DEC_BATCH, CONV_PAD, D_FF), jnp.float32)
    meta_tokens = nrm(ks[7], (N_META, D_MODEL), jnp.float32)
    norm_mix = 1.0 + 0.02 * nrm(ks[8], (DEPTH, D_MODEL), jnp.float32)
    norm_ffn = 1.0 + 0.02 * nrm(ks[9], (DEPTH, D_MODEL), jnp.float32)
    norm_final = 1.0 + 0.02 * nrm(ks[10], (D_MODEL,), jnp.float32)
    w_mlstm_in = nrm(ks[11], (N_MLSTM_LAYERS, D_MODEL, MLSTM_IN_WIDTH), jnp.float32) * D_MODEL ** -0.5
    f_bias = jnp.linspace(3.0, 6.0, H, dtype=jnp.float32)
    b_mlstm_gate = jnp.concatenate([0.1 * nrm(ks[12], (N_MLSTM_LAYERS, H), jnp.float32),
                                    f_bias + 0.1 * nrm(ks[13], (N_MLSTM_LAYERS, H), jnp.float32)], axis=-1)
    g_mlstm_out = 1.0 + 0.02 * nrm(ks[14], (N_MLSTM_LAYERS, V_WIDTH), jnp.float32)
    w_mlstm_out = nrm(ks[15], (N_MLSTM_LAYERS, V_WIDTH, D_MODEL), jnp.float32) * V_WIDTH ** -0.5
    w_pool = nrm(ks[16], (N_POOL_LAYERS, POOL_GROUPS, POOL_GROUP_DIM, POOL_GROUP_DIM), jnp.float32) * POOL_GROUP_DIM ** -0.5
    pool_scale = 1.0 + 0.02 * nrm(ks[17], (N_POOL_LAYERS, D_MODEL), jnp.float32)
    w_up = nrm(ks[18], (DEPTH, D_MODEL, 2 * D_FF), jnp.float32) * D_MODEL ** -0.5
    conv_w = nrm(ks[19], (DEPTH, CONV_WIDTH, D_FF), jnp.float32) * CONV_WIDTH ** -0.5
    conv_b = 0.01 * nrm(ks[20], (DEPTH, D_FF), jnp.float32)
    w_down = nrm(ks[21], (DEPTH, D_FF, D_MODEL), jnp.float32) * D_FF ** -0.5
    return {'x_prompt': x_prompt, 'x_sample': x_sample,
            'state_mlstm_C': state_mlstm_C, 'state_mlstm_n': state_mlstm_n, 'state_mlstm_m': state_mlstm_m,
            'state_pool': state_pool, 'state_ffn_conv': state_ffn_conv,
            'meta_tokens': meta_tokens, 'norm_mix': norm_mix, 'norm_ffn': norm_ffn, 'norm_final': norm_final,
            'w_mlstm_in': w_mlstm_in, 'b_mlstm_gate': b_mlstm_gate, 'g_mlstm_out': g_mlstm_out,
            'w_mlstm_out': w_mlstm_out, 'w_pool': w_pool, 'pool_scale': pool_scale,
            'w_up': w_up, 'conv_w': conv_w, 'conv_b': conv_b, 'w_down': w_down}


def reference(x_prompt, x_sample, state_mlstm_C, state_mlstm_n, state_mlstm_m, state_pool, state_ffn_conv,
              meta_tokens, norm_mix, norm_ffn, norm_final, w_mlstm_in, b_mlstm_gate, g_mlstm_out, w_mlstm_out,
              w_pool, pool_scale, w_up, conv_w, conv_b, w_down):
    f32 = jnp.float32
    B = x_prompt.shape[0]
    H = MLSTM_HEADS
    weights = (norm_mix, norm_ffn, norm_final, w_mlstm_in, b_mlstm_gate, g_mlstm_out, w_mlstm_out,
               w_pool, pool_scale, w_up, conv_w, conv_b, w_down)
    meta = jnp.broadcast_to(meta_tokens.astype(x_prompt.dtype)[None], (B, N_META, D_MODEL))
    xp = jnp.concatenate([meta, x_prompt], axis=1)
    zC = jnp.zeros((N_MLSTM_LAYERS, B, H, MLSTM_DK, MLSTM_DV), f32)
    zn = jnp.zeros((N_MLSTM_LAYERS, B, H, MLSTM_DK), f32)
    zm = jnp.zeros((N_MLSTM_LAYERS, B, H), f32)
    zpool = jnp.zeros((N_POOL_LAYERS, B, POOL_PAD, D_MODEL), x_prompt.dtype)
    zconv = jnp.zeros((DEPTH, B, CONV_PAD, D_FF), x_prompt.dtype)
    yp, C_p, n_p, m_p, pool_p, conv_p = trunk(xp, zC, zn, zm, zpool, zconv, N_META, 0, *weights)
    y_prompt = yp[:, N_META:]
    y_sample, C_s, n_s, m_s, pool_s, conv_s = trunk(x_sample, state_mlstm_C, state_mlstm_n, state_mlstm_m,
                                                     state_pool, state_ffn_conv, 0, PAST_LEN, *weights)
    return (y_prompt, y_sample, C_p, n_p, m_p, pool_p, conv_p, C_s, n_s, m_s, pool_s, conv_s)
```

```python
import functools

import jax
import jax.numpy as jnp
from jax import lax
from jax.experimental import pallas as pl
from jax.experimental.pallas import tpu as pltpu

F32 = jnp.float32
BF16 = jnp.bfloat16

EPS = 1e-6
D = 2048
H = 4
DK = 256
DV = 512
QKW = H * DK
VW = H * DV
PW = 2 * QKW + 2 * VW
GATE_LANES = 128
SCALE = DK ** -0.5
POOL_WINDOWS = (2, 4, 8, 16)
PG = D // len(POOL_WINDOWS)
POOL_HALO = 16
D_FF = 5504
TF = 512
FP = -(-D_FF // TF) * TF
NF = FP // TF
N_META = 16
PAST_LEN = 16384
SCAN_CHUNK = 256
SAMPLE_GROUP = 16
SAMPLE_TILE_SEQS = 64
VMEM_LIMIT = 56 * 1024 * 1024


def _params(n_axes):
    return pltpu.CompilerParams(dimension_semantics=("arbitrary",) * n_axes,
                                vmem_limit_bytes=VMEM_LIMIT)


def _rms(x, g):
    return x * lax.rsqrt(jnp.mean(x * x, axis=-1, keepdims=True) + EPS) * g


def _log_sigmoid(x):
    return jnp.minimum(x, 0.0) - jnp.log(1.0 + jnp.exp(-jnp.abs(x)))


def _proj_kernel(x_ref, g_ref, w_ref, wg_ref, p_ref, gate_ref, u_scr):
    @pl.when(pl.program_id(1) == 0)
    def _():
        ub = _rms(x_ref[...], g_ref[...]).astype(BF16)
        u_scr[...] = ub
        gate_ref[...] = jnp.dot(ub, wg_ref[...], preferred_element_type=F32)

    p_ref[...] = jnp.dot(u_scr[...], w_ref[...], preferred_element_type=F32).astype(p_ref.dtype)


def _proj(x, gamma, w, wg, *, tm, tn, out_dtype):
    m = x.shape[0]
    return pl.pallas_call(
        _proj_kernel,
        grid=(m // tm, PW // tn),
        in_specs=[pl.BlockSpec((tm, D), lambda i, j: (i, 0)),
                  pl.BlockSpec((1, D), lambda i, j: (0, 0)),
                  pl.BlockSpec((D, tn), lambda i, j: (0, j)),
                  pl.BlockSpec((D, GATE_LANES), lambda i, j: (0, 0))],
        out_specs=[pl.BlockSpec((tm, tn), lambda i, j: (i, j)),
                   pl.BlockSpec((tm, GATE_LANES), lambda i, j: (i, 0))],
        out_shape=[jax.ShapeDtypeStruct((m, PW), out_dtype),
                   jax.ShapeDtypeStruct((m, GATE_LANES), F32)],
        scratch_shapes=[pltpu.VMEM((tm, D), BF16)],
        compiler_params=_params(2),
        name="proj",
    )(x, gamma, w, wg)


def _head_output(num, den, m_t, gout, o):
    hv = num / jnp.maximum(jnp.abs(den), jnp.exp(-m_t))
    hv = hv * lax.rsqrt(jnp.mean(hv * hv, axis=1, keepdims=True) + EPS)
    return hv * gout * jax.nn.sigmoid(o.astype(F32))


def _scan_kernel(q_ref, k_ref, v_ref, o_ref, gt_ref, bias_ref, gout_ref, c0_ref, n0_ref, m0_ref,
                 hg_ref, cout_ref, nout_ref, mout_ref, c_scr, n_scr, m_scr, *, L, lead_pad, nc):
    c = pl.program_id(1)

    @pl.when(c == 0)
    def _():
        c_scr[...] = c0_ref[...]
        n_scr[...] = n0_ref[...]
        m_scr[...] = m0_ref[...]

    gates = gt_ref[...] + bias_ref[...]
    gates_t = gates.T
    row = lax.broadcasted_iota(jnp.int32, (L, L), 0)
    col = lax.broadcasted_iota(jnp.int32, (L, L), 1)
    causal = row >= col
    causal_t = row <= col
    if lead_pad:
        live_col = lax.broadcasted_iota(jnp.int32, (L, 1), 0) >= lead_pad
        live_row = lax.broadcasted_iota(jnp.int32, (1, L), 1) >= lead_pad

    for hh in range(H):
        ig_col = gates[:, hh:hh + 1]
        ig_row = gates_t[hh:hh + 1, :]
        lf_col = _log_sigmoid(gates[:, H + hh:H + hh + 1])
        lf_row = _log_sigmoid(gates_t[H + hh:H + hh + 1, :])
        if lead_pad:
            ig_col = jnp.where(live_col, ig_col, -jnp.inf)
            ig_row = jnp.where(live_row, ig_row, -jnp.inf)
            lf_col = jnp.where(live_col, lf_col, 0.0)
            lf_row = jnp.where(live_row, lf_row, 0.0)
        b_col = jnp.sum(jnp.where(causal, lf_row, 0.0), axis=1, keepdims=True)
        b_row = jnp.sum(jnp.where(causal_t, lf_col, 0.0), axis=0, keepdims=True)
        m_prev = m_scr[hh:hh + 1, 0:1]
        d = jnp.where(causal, b_col - b_row + ig_row, -jnp.inf)
        inter = b_col + m_prev
        m_t = jnp.maximum(inter, jnp.max(d, axis=1, keepdims=True))
        w_inter = jnp.exp(inter - m_t) * SCALE

        q = q_ref[:, hh * DK:(hh + 1) * DK]
        k = k_ref[:, hh * DK:(hh + 1) * DK]
        v = v_ref[:, hh * DV:(hh + 1) * DV]
        qk = lax.dot_general(q, k, (((1,), (1,)), ((), ())), preferred_element_type=F32)
        s = qk * (jnp.exp(d - m_t) * SCALE)
        cmat = c_scr[hh]
        nvec = n_scr[hh:hh + 1, :]
        num = w_inter * jnp.dot(q, cmat.astype(BF16), preferred_element_type=F32) \
            + jnp.dot(s.astype(BF16), v, preferred_element_type=F32)
        den = w_inter * jnp.sum(q.astype(F32) * nvec, axis=1, keepdims=True) \
            + jnp.sum(s, axis=1, keepdims=True)
        hout = _head_output(num, den, m_t, gout_ref[:, hh * DV:(hh + 1) * DV],
                            o_ref[:, hh * DV:(hh + 1) * DV])
        hg_ref[:, hh * DV:(hh + 1) * DV] = hout.astype(hg_ref.dtype)

        m_new = m_t[L - 1:L, :]
        b_last = b_col[L - 1:L, :]
        decay = jnp.exp(b_last + m_prev - m_new)
        wk = jnp.exp(b_last - b_col + ig_col - m_new) * k.astype(F32)
        c_scr[hh] = decay * cmat + jnp.dot(wk.T.astype(BF16), v, preferred_element_type=F32)
        n_scr[hh:hh + 1, :] = decay * nvec + jnp.sum(wk, axis=0, keepdims=True)
        m_scr[hh:hh + 1, :] = jnp.broadcast_to(m_new, (1, GATE_LANES))

    @pl.when(c == nc - 1)
    def _():
        cout_ref[...] = c_scr[...]
        nout_ref[...] = n_scr[...]
        mout_ref[...] = m_scr[...]


def _scan(p, gates, bias, gout, c0, n0, m0, *, L, lead_pad=0):
    b, s, _ = p.shape
    nc = s // L
    kern = functools.partial(_scan_kernel, L=L, lead_pad=lead_pad, nc=nc)
    return pl.pallas_call(
        kern,
        grid=(b, nc),
        in_specs=[pl.BlockSpec((None, L, QKW), lambda i, c: (i, c, 0)),
                  pl.BlockSpec((None, L, QKW), lambda i, c: (i, c, 1)),
                  pl.BlockSpec((None, L, VW), lambda i, c: (i, c, 1)),
                  pl.BlockSpec((None, L, VW), lambda i, c: (i, c, 2)),
                  pl.BlockSpec((None, L, GATE_LANES), lambda i, c: (i, c, 0)),
                  pl.BlockSpec((1, GATE_LANES), lambda i, c: (0, 0)),
                  pl.BlockSpec((1, VW), lambda i, c: (0, 0)),
                  pl.BlockSpec((H, DK, DV), lambda i, c: (0, 0, 0)),
                  pl.BlockSpec((8, DK), lambda i, c: (0, 0)),
                  pl.BlockSpec((8, GATE_LANES), lambda i, c: (0, 0))],
        out_specs=[pl.BlockSpec((None, L, VW), lambda i, c: (i, c, 0)),
                   pl.BlockSpec((None, H, DK, DV), lambda i, c: (i, 0, 0, 0)),
                   pl.BlockSpec((None, 8, DK), lambda i, c: (i, 0, 0)),
                   pl.BlockSpec((None, 8, GATE_LANES), lambda i, c: (i, 0, 0))],
        out_shape=[jax.ShapeDtypeStruct((b, s, VW), BF16),
                   jax.ShapeDtypeStruct((b, H, DK, DV), F32),
                   jax.ShapeDtypeStruct((b, 8, DK), F32),
                   jax.ShapeDtypeStruct((b, 8, GATE_LANES), F32)],
        scratch_shapes=[pltpu.VMEM((H, DK, DV), F32),
                        pltpu.VMEM((8, DK), F32),
                        pltpu.VMEM((8, GATE_LANES), F32)],
        compiler_params=_params(2),
        name="scan",
    )(p, p, p, p, gates, bias, gout, c0, n0, m0)


def _scan_s_kernel(q_ref, k_ref, v_ref, o_ref, gt_ref, bias_ref, gout_ref, mtok_ref, c_ref, n_ref,
                   hg_ref, cout_ref, nout_ref, mout_ref, qc_scr, ntok_scr, *, T, NB):
    hh = pl.program_id(1)
    LT = NB * T
    gates = gt_ref[...] + bias_ref[...]
    gates_t = gates.T
    lane = lax.broadcasted_iota(jnp.int32, (LT, GATE_LANES), 1)
    sub = lax.broadcasted_iota(jnp.int32, (GATE_LANES, LT), 0)

    def pick_col(a, idx):
        return jnp.sum(jnp.where(lane == idx, a, 0.0), axis=1, keepdims=True)

    def pick_row(a, idx):
        return jnp.sum(jnp.where(sub == idx, a, 0.0), axis=0, keepdims=True)

    ig_col = pick_col(gates, hh)
    ig_row = pick_row(gates_t, hh)
    lf_col = _log_sigmoid(pick_col(gates, hh + H))
    lf_row = _log_sigmoid(pick_row(gates_t, hh + H))
    m_prev = pick_col(mtok_ref[...], hh)

    row = lax.broadcasted_iota(jnp.int32, (LT, LT), 0)
    col = lax.broadcasted_iota(jnp.int32, (LT, LT), 1)
    same = (row // T) == (col // T)
    causal = jnp.logical_and(same, row >= col)
    causal_t = jnp.logical_and(same, row <= col)
    b_col = jnp.sum(jnp.where(causal, lf_row, 0.0), axis=1, keepdims=True)
    b_row = jnp.sum(jnp.where(causal_t, lf_col, 0.0), axis=0, keepdims=True)
    b_end = jnp.sum(jnp.where(same, lf_row, 0.0), axis=1, keepdims=True)
    d = jnp.where(causal, b_col - b_row + ig_row, -jnp.inf)
    inter = b_col + m_prev
    m_t = jnp.maximum(inter, jnp.max(d, axis=1, keepdims=True))
    d_end = jnp.where(same, b_end - b_row + ig_row, -jnp.inf)
    m_new = jnp.maximum(b_end + m_prev, jnp.max(d_end, axis=1, keepdims=True))
    w_inter = jnp.exp(inter - m_t) * SCALE

    q32 = q_ref[...]
    k32 = k_ref[...]
    q = q32.astype(BF16)
    v = v_ref[...].astype(BF16)
    qk = lax.dot_general(q, k32.astype(BF16), (((1,), (1,)), ((), ())), preferred_element_type=F32)
    s = qk * (jnp.exp(d - m_t) * SCALE)
    num_intra = jnp.dot(s.astype(BF16), v, preferred_element_type=F32)
    den_intra = jnp.sum(s, axis=1, keepdims=True)

    decay = jnp.exp(b_end + m_prev - m_new)
    wk = jnp.exp(b_end - b_col + ig_col - m_new) * k32
    wk_t = wk.T
    col_seq = lax.broadcasted_iota(jnp.int32, (DK, LT), 1) // T

    for bb in range(NB):
        r0 = bb * T
        cmat = c_ref[bb]
        nvec = n_ref[bb:bb + 1, :]
        qc_scr[r0:r0 + T, :] = jnp.dot(q32[r0:r0 + T, :].astype(BF16), cmat.astype(BF16),
                                       preferred_element_type=F32)
        ntok_scr[r0:r0 + T, :] = jnp.broadcast_to(nvec, (T, DK))
        upd = jnp.dot(jnp.where(col_seq == bb, wk_t, 0.0).astype(BF16), v, preferred_element_type=F32)
        dec = decay[r0:r0 + 1, :]
        cout_ref[bb] = dec * cmat + upd
        nout_ref[bb:bb + 1, :] = dec * nvec + jnp.sum(wk[r0:r0 + T, :], axis=0, keepdims=True)
        mout_ref[bb:bb + 1, :] = jnp.broadcast_to(m_new[r0:r0 + 1, :], (1, GATE_LANES))

    num = w_inter * qc_scr[...] + num_intra
    den = w_inter * jnp.sum(q32 * ntok_scr[...], axis=1, keepdims=True) + den_intra
    hg_ref[...] = _head_output(num, den, m_t, gout_ref[...], o_ref[...]).astype(hg_ref.dtype)


def _scan_s(p, gates, bias, gout, mtok, c, n_hm, *, T):
    nseq = c.shape[0]
    nb = SAMPLE_GROUP
    lt = nb * T
    kern = functools.partial(_scan_s_kernel, T=T, NB=nb)
    return pl.pallas_call(
        kern,
        grid=(nseq // nb, H),
        in_specs=[pl.BlockSpec((lt, DK), lambda g, h: (g, h)),
                  pl.BlockSpec((lt, DK), lambda g, h: (g, H + h)),
                  pl.BlockSpec((lt, DV), lambda g, h: (g, H + h)),
                  pl.BlockSpec((lt, DV), lambda g, h: (g, 2 * H + h)),
                  pl.BlockSpec((lt, GATE_LANES), lambda g, h: (g, 0)),
                  pl.BlockSpec((1, GATE_LANES), lambda g, h: (0, 0)),
                  pl.BlockSpec((1, DV), lambda g, h: (0, h)),
                  pl.BlockSpec((lt, GATE_LANES), lambda g, h: (g, 0)),
                  pl.BlockSpec((nb, None, DK, DV), lambda g, h: (g, h, 0, 0)),
                  pl.BlockSpec((None, nb, DK), lambda g, h: (h, g, 0))],
        out_specs=[pl.BlockSpec((lt, DV), lambda g, h: (g, h)),
                   pl.BlockSpec((nb, None, DK, DV), lambda g, h: (g, h, 0, 0)),
                   pl.BlockSpec((None, nb, DK), lambda g, h: (h, g, 0)),
                   pl.BlockSpec((None, nb, GATE_LANES), lambda g, h: (h, g, 0))],
        out_shape=[jax.ShapeDtypeStruct((nseq * T, VW), BF16),
                   jax.ShapeDtypeStruct((nseq, H, DK, DV), F32),
                   jax.ShapeDtypeStruct((H, nseq, DK), F32),
                   jax.ShapeDtypeStruct((H, nseq, GATE_LANES), F32)],
        scratch_shapes=[pltpu.VMEM((lt, DV), F32), pltpu.VMEM((lt, DK), F32)],
        compiler_params=_params(2),
        name="scan_sample",
    )(p, p, p, p, gates, bias, gout, mtok, c, n_hm)


def _mmres_kernel(a_ref, w_ref, x_ref, o_ref):
    o_ref[...] = x_ref[...] + jnp.dot(a_ref[...], w_ref[...], preferred_element_type=F32)


def _mmres(a, w, x, *, tm, tn):
    m, kdim = a.shape
    n = w.shape[1]
    return pl.pallas_call(
        _mmres_kernel,
        grid=(m // tm, n // tn),
        in_specs=[pl.BlockSpec((tm, kdim), lambda i, j: (i, 0)),
                  pl.BlockSpec((kdim, tn), lambda i, j: (0, j)),
                  pl.BlockSpec((tm, tn), lambda i, j: (i, j))],
        out_specs=pl.BlockSpec((tm, tn), lambda i, j: (i, j)),
        out_shape=jax.ShapeDtypeStruct((m, n), F32),
        compiler_params=_params(2),
        name="out_proj",
    )(a, w, x)


def _ffn_kernel(x_ref, gam_ref, wg_ref, wa_ref, cw_ref, cb_ref, wd_ref, st_ref, gfin_ref,
                o_ref, so_ref, u_scr, gext_scr, carry_scr, *, tm, sr, shift, tps, final_norm):
    i = pl.program_id(0)
    f = pl.program_id(1)

    @pl.when(f == 0)
    def _():
        x = x_ref[...]
        u_scr[...] = _rms(x, gam_ref[...]).astype(BF16)
        o_ref[...] = x

    u = u_scr[...]
    g = jnp.dot(u, wg_ref[...], preferred_element_type=F32)
    a = jnp.dot(u, wa_ref[...], preferred_element_type=F32)
    g_tail = g[tm - sr:tm, :]
    if tps == 1:
        gext_scr[0:sr, :] = st_ref[...]
    else:
        seq_start = (i % tps) == 0

        @pl.when(seq_start)
        def _():
            gext_scr[0:sr, :] = st_ref[...]

        @pl.when(jnp.logical_not(seq_start))
        def _():
            gext_scr[0:sr, :] = carry_scr[f]

        carry_scr[f] = g_tail
    gext_scr[sr:sr + tm, :] = g
    so_ref[...] = g_tail

    cw = cw_ref[...]
    g_m2 = gext_scr[sr - 2 * shift:sr - 2 * shift + tm, :]
    g_m1 = gext_scr[sr - shift:sr - shift + tm, :]
    gc = cb_ref[...] + ((cw[0:1, :] * g_m2 + cw[1:2, :] * g_m1) + cw[2:3, :] * g)
    hmid = (gc * jax.nn.sigmoid(gc)) * a
    o_ref[...] += jnp.dot(hmid.astype(BF16), wd_ref[...], preferred_element_type=F32)

    if final_norm:
        @pl.when(f == pl.num_programs(1) - 1)
        def _():
            o_ref[...] = _rms(o_ref[...], gfin_ref[...])


def _ffn(x, gamma, wup, cw, cb, wdn, st, gfin, *, tm, sr, shift, tps, st_per_tile, final_norm):
    m = x.shape[0]
    nt = m // tm
    kern = functools.partial(_ffn_kernel, tm=tm, sr=sr, shift=shift, tps=tps, final_norm=final_norm)
    st_map = (lambda i, f: (i, f)) if st_per_tile else (lambda i, f: (0, f))
    return pl.pallas_call(
        kern,
        grid=(nt, NF),
        in_specs=[pl.BlockSpec((tm, D), lambda i, f: (i, 0)),
                  pl.BlockSpec((1, D), lambda i, f: (0, 0)),
                  pl.BlockSpec((D, TF), lambda i, f: (0, f)),
                  pl.BlockSpec((D, TF), lambda i, f: (0, NF + f)),
                  pl.BlockSpec((3, TF), lambda i, f: (0, f)),
                  pl.BlockSpec((1, TF), lambda i, f: (0, f)),
                  pl.BlockSpec((TF, D), lambda i, f: (f, 0)),
                  pl.BlockSpec((sr, TF), st_map),
                  pl.BlockSpec((1, D), lambda i, f: (0, 0))],
        out_specs=[pl.BlockSpec((tm, D), lambda i, f: (i, 0)),
                   pl.BlockSpec((sr, TF), lambda i, f: (i, f))],
        out_shape=[jax.ShapeDtypeStruct((m, D), F32),
                   jax.ShapeDtypeStruct((nt * sr, FP), F32)],
        scratch_shapes=[pltpu.VMEM((tm, D), BF16),
                        pltpu.VMEM((sr + tm, TF), F32),
                        pltpu.VMEM((NF, sr, TF), F32)],
        compiler_params=_params(2),
        name="conv_ffn",
    )(x, gamma, wup, wup, cw, cb, wdn, st, gfin)


def _pool_kernel(x_ref, gam_ref, wp_ref, sc_ref, st_ref, o_ref, ut_ref, rinv_scr, uext_scr, carry_scr,
                 *, tm, shift, tps, pos0, tr):
    i = pl.program_id(0)
    grp = pl.program_id(1)
    hr = POOL_HALO * shift

    @pl.when(grp == 0)
    def _():
        x = x_ref[...]
        rinv_scr[...] = lax.rsqrt(jnp.mean(x * x, axis=1, keepdims=True) + EPS)

    for kk, w in enumerate(POOL_WINDOWS):
        @pl.when(grp == kk)
        def _(kk=kk, w=w):
            cs = slice(kk * PG, (kk + 1) * PG)
            xg = x_ref[:, cs]
            ug = xg * rinv_scr[...] * gam_ref[:, cs]
            if tps == 1:
                uext_scr[0:hr, :] = st_ref[...]
            else:
                seq_start = (i % tps) == 0

                @pl.when(seq_start)
                def _():
                    uext_scr[0:hr, :] = st_ref[...]

                @pl.when(jnp.logical_not(seq_start))
                def _():
                    uext_scr[0:hr, :] = carry_scr[kk]

            uext_scr[hr:hr + tm, :] = ug
            acc = ug
            for j in range(1, w):
                acc = acc + uext_scr[hr - j * shift:hr - j * shift + tm, :]
            if pos0 + 1 >= w:
                pooled = acc / float(w) - ug
            else:
                step = (i % tps) * (tm // shift) + lax.broadcasted_iota(jnp.int32, (tm, 1), 0) // shift
                cnt = jnp.minimum(w, pos0 + step + 1).astype(F32)
                pooled = acc / cnt - ug
            y = jnp.dot(pooled.astype(BF16), wp_ref[kk], preferred_element_type=F32)
            o_ref[...] = xg + y * sc_ref[:, cs]
            ut_ref[...] = ug[tm - tr:tm, :]
            if tps > 1:
                carry_scr[kk] = uext_scr[tm:tm + hr, :]


def _pool(x, gamma, wp, sc, st, *, tm, shift, tps, pos0, tr, st_per_tile):
    m = x.shape[0]
    nt = m // tm
    hr = POOL_HALO * shift
    ng = len(POOL_WINDOWS)
    kern = functools.partial(_pool_kernel, tm=tm, shift=shift, tps=tps, pos0=pos0, tr=tr)
    st_map = (lambda i, g: (i, g)) if st_per_tile else (lambda i, g: (0, g))
    return pl.pallas_call(
        kern,
        grid=(nt, ng),
        in_specs=[pl.BlockSpec((tm, D), lambda i, g: (i, 0)),
                  pl.BlockSpec((1, D), lambda i, g: (0, 0)),
                  pl.BlockSpec((ng, PG, PG), lambda i, g: (0, 0, 0)),
                  pl.BlockSpec((1, D), lambda i, g: (0, 0)),
                  pl.BlockSpec((hr, PG), st_map)],
        out_specs=[pl.BlockSpec((tm, PG), lambda i, g: (i, g)),
                   pl.BlockSpec((tr, PG), lambda i, g: (i, g))],
        out_shape=[jax.ShapeDtypeStruct((m, D), F32),
                   jax.ShapeDtypeStruct((nt * tr, D), F32)],
        scratch_shapes=[pltpu.VMEM((tm, 1), F32),
                        pltpu.VMEM((hr + tm, PG), F32),
                        pltpu.VMEM((ng, hr, PG), F32)],
        compiler_params=_params(2),
        name="pool_mixer",
    )(x, gamma, wp, sc, st)


def _pad_cols(a, n):
    return jnp.pad(a, ((0, 0), (0, n - a.shape[1])))


def _to_time_major(a, t):
    nseq = a.shape[0]
    a = a.reshape((nseq // SAMPLE_TILE_SEQS, SAMPLE_TILE_SEQS, t) + a.shape[2:])
    a = jnp.swapaxes(a, 1, 2)
    return a.reshape((nseq * t,) + a.shape[3:])


def _from_time_major(a, nseq, t):
    a = a.reshape((nseq // SAMPLE_TILE_SEQS, t, SAMPLE_TILE_SEQS) + a.shape[1:])
    a = jnp.swapaxes(a, 1, 2)
    return a.reshape((nseq, t) + a.shape[3:])


def kernel(x_prompt, x_sample, state_mlstm_C, state_mlstm_n, state_mlstm_m, state_pool, state_ffn_conv,
           meta_tokens, norm_mix, norm_ffn, norm_final, w_mlstm_in, b_mlstm_gate, g_mlstm_out, w_mlstm_out,
           w_pool, pool_scale, w_up, conv_w, conv_b, w_down):
    bsz, seq, _ = x_prompt.shape
    nseq, t_dec, _ = x_sample.shape

    w_in = w_mlstm_in[0]
    w_main = w_in[:, :PW].astype(BF16)
    w_gate = _pad_cols(w_in[:, PW:], GATE_LANES).astype(BF16)
    bias = _pad_cols(b_mlstm_gate[0][None, :], GATE_LANES)
    w_out = w_mlstm_out[0].astype(BF16)
    gout = g_mlstm_out[0][None, :]
    wp = w_pool[0].astype(BF16)
    psc = pool_scale[0][None, :]
    gfin = norm_final[None, :]
    ffn_w = []
    for layer in range(2):
        wu = w_up[layer]
        wup = jnp.concatenate([_pad_cols(wu[:, :D_FF], FP), _pad_cols(wu[:, D_FF:], FP)], axis=1).astype(BF16)
        wdn = jnp.pad(w_down[layer], ((0, FP - D_FF), (0, 0))).astype(BF16)
        ffn_w.append((norm_ffn[layer][None, :], wup, _pad_cols(conv_w[layer], FP),
                      _pad_cols(conv_b[layer][None, :], FP), wdn))
    g_mix0 = norm_mix[0][None, :]
    g_mix1 = norm_mix[1][None, :]

    def long_stream(x, nb, s, st, *, tm, chunk, lead_pad, pos0):
        c0, n0, m0, conv0, pool0, conv1 = st
        tm_ffn = min(tm, 512)
        p, gates = _proj(x, g_mix0, w_main, w_gate, tm=tm, tn=512, out_dtype=BF16)
        p = p.reshape(nb, s, PW)
        gates = gates.reshape(nb, s, GATE_LANES)
        if lead_pad:
            p = jnp.pad(p, ((0, 0), (lead_pad, 0), (0, 0)))
            gates = jnp.pad(gates, ((0, 0), (lead_pad, 0), (0, 0)))
        hg, c_new, n_new, m_new = _scan(p, gates, bias, gout, c0, n0, m0, L=chunk, lead_pad=lead_pad)
        hg = hg[:, lead_pad:].reshape(nb * s, VW)
        x1 = _mmres(hg, w_out, x, tm=tm, tn=512)
        x2, cs0 = _ffn(x1, *ffn_w[0], conv0, gfin, tm=tm_ffn, sr=8, shift=1, tps=s // tm_ffn,
                       st_per_tile=False, final_norm=False)
        x3, ut = _pool(x2, g_mix1, wp, psc, pool0, tm=tm, shift=1, tps=s // tm, pos0=pos0,
                       tr=POOL_HALO, st_per_tile=False)
        y, cs1 = _ffn(x3, *ffn_w[1], conv1, gfin, tm=tm_ffn, sr=8, shift=1, tps=s // tm_ffn,
                      st_per_tile=False, final_norm=True)
        cs0 = cs0.reshape(nb, s // tm_ffn, 8, FP)[:, -1]
        cs1 = cs1.reshape(nb, s // tm_ffn, 8, FP)[:, -1]
        ut = ut.reshape(nb, s // tm, POOL_HALO, D)[:, -1]
        return y, (c_new, n_new, m_new, cs0, ut, cs1)

    zero_state = (jnp.zeros((H, DK, DV), F32), jnp.zeros((8, DK), F32), jnp.zeros((8, GATE_LANES), F32),
                  jnp.zeros((8, FP), F32), jnp.zeros((POOL_HALO, D), F32), jnp.zeros((8, FP), F32))
    _, (c_m, n_m, m_m, cs0_m, ut_m, cs1_m) = long_stream(
        meta_tokens, 1, N_META, zero_state, tm=N_META, chunk=128, lead_pad=128 - N_META, pos0=0)

    y_p, (c_p, n_p, m_p, cs0_p, ut_p, cs1_p) = long_stream(
        x_prompt.reshape(bsz * seq, D), bsz, seq, (c_m[0], n_m[0], m_m[0], cs0_m[0], ut_m[0], cs1_m[0]),
        tm=1024, chunk=SCAN_CHUNK, lead_pad=0, pos0=N_META)
    y_prompt = y_p.reshape(bsz, seq, D)
    C_p = c_p[None]
    n_p = n_p[:, :H][None]
    m_p = m_p[:, :H, 0][None]
    pool_p = ut_p[:, 1:][None]
    conv_p = jnp.stack([cs0_p[:, 6:, :D_FF], cs1_p[:, 6:, :D_FF]])

    xs = x_sample.reshape(nseq * t_dec, D)
    p_s, gates_s = _proj(xs, g_mix0, w_main, w_gate, tm=nseq * t_dec, tn=512, out_dtype=F32)
    mtok = _pad_cols(jnp.repeat(state_mlstm_m[0], t_dec, axis=0), GATE_LANES)
    n_hm = jnp.swapaxes(state_mlstm_n[0], 0, 1)
    hg_s, C_s, n_s_hm, m_s_hm = _scan_s(p_s, gates_s, bias, gout, mtok, state_mlstm_C[0], n_hm, T=t_dec)
    tm_s = SAMPLE_TILE_SEQS * t_dec
    hg_t = _to_time_major(hg_s.reshape(nseq, t_dec, VW), t_dec)
    x_t = _to_time_major(x_sample, t_dec)
    x1 = _mmres(hg_t, w_out, x_t, tm=tm_s, tn=512)

    def conv_state_in(cs):
        return _to_time_major(jnp.pad(cs, ((0, 0), (0, 0), (0, FP - D_FF))), 2)

    def conv_state_out(cs):
        return _from_time_major(cs, nseq, 2)[:, :, :D_FF]

    sr_s = 2 * SAMPLE_TILE_SEQS
    x2, cs0_s = _ffn(x1, *ffn_w[0], conv_state_in(state_ffn_conv[0]), gfin, tm=tm_s, sr=sr_s,
                     shift=SAMPLE_TILE_SEQS, tps=1, st_per_tile=True, final_norm=False)
    pool_in = jnp.pad(state_pool[0], ((0, 0), (1, 0), (0, 0)))
    x3, ut_s = _pool(x2, g_mix1, wp, psc, _to_time_major(pool_in, POOL_HALO), tm=tm_s,
                     shift=SAMPLE_TILE_SEQS, tps=1, pos0=PAST_LEN, tr=tm_s, st_per_tile=True)
    y_s, cs1_s = _ffn(x3, *ffn_w[1], conv_state_in(state_ffn_conv[1]), gfin, tm=tm_s, sr=sr_s,
                      shift=SAMPLE_TILE_SEQS, tps=1, st_per_tile=True, final_norm=True)
    y_sample = _from_time_major(y_s, nseq, t_dec)
    n_s = jnp.swapaxes(n_s_hm, 0, 1)[None]
    m_s = jnp.swapaxes(m_s_hm[:, :, 0], 0, 1)[None]
    pool_s = jnp.concatenate([state_pool[0][:, t_dec:], _from_time_major(ut_s, nseq, t_dec)], axis=1)[None]
    conv_s = jnp.stack([conv_state_out(cs0_s), conv_state_out(cs1_s)])

    return (y_prompt, y_sample, C_p, n_p, m_p, pool_p, conv_p,
            C_s[None], n_s, m_s, pool_s, conv_s)
```

```python
import functools

import jax
import jax.numpy as jnp
from jax import lax
from jax.experimental import pallas as pl
from jax.experimental.pallas import tpu as pltpu

F32 = jnp.float32
BF16 = jnp.bfloat16

EPS = 1e-6
D = 2048
H = 4
DK = 256
DV = 512
QKW = H * DK
VW = H * DV
PW = 2 * QKW + 2 * VW
GATE_LANES = 128
SCALE = DK ** -0.5
POOL_WINDOWS = (2, 4, 8, 16)
PG = D // len(POOL_WINDOWS)
POOL_HALO = 16
D_FF = 5504
TF = 256
NF = -(-D_FF // TF)
LANES = 128
FF_LANE_BLOCKS = D_FF // LANES
N_META = 16
PAST_LEN = 16384
SCAN_CHUNK = 256
SAMPLE_GROUP = 16
SAMPLE_TILE_SEQS = 128
VMEM_LIMIT = 56 * 1024 * 1024


def _params(n_axes):
    return pltpu.CompilerParams(dimension_semantics=("arbitrary",) * n_axes,
                                vmem_limit_bytes=VMEM_LIMIT)


def _rms(x, g):
    return x * lax.rsqrt(jnp.mean(x * x, axis=-1, keepdims=True) + EPS) * g


def _log_sigmoid(x):
    return jnp.minimum(x, 0.0) - jnp.log(1.0 + jnp.exp(-jnp.abs(x)))


def _proj_kernel(x_ref, g_ref, w_ref, wg_ref, p_ref, gate_ref, u_scr):
    @pl.when(pl.program_id(1) == 0)
    def _():
        ub = _rms(x_ref[...], g_ref[...]).astype(BF16)
        u_scr[...] = ub
        gate_ref[...] = jnp.dot(ub, wg_ref[...], preferred_element_type=F32)

    p_ref[...] = jnp.dot(u_scr[...], w_ref[...].astype(BF16),
                         preferred_element_type=F32).astype(p_ref.dtype)


def _proj(x, gamma, w, wg, *, tm, tn, out_dtype):
    m = x.shape[0]
    return pl.pallas_call(
        _proj_kernel,
        grid=(m // tm, PW // tn),
        in_specs=[pl.BlockSpec((tm, D), lambda i, j: (i, 0)),
                  pl.BlockSpec((1, D), lambda i, j: (0, 0)),
                  pl.BlockSpec((None, D, tn), lambda i, j: (0, 0, j)),
                  pl.BlockSpec((D, GATE_LANES), lambda i, j: (0, 0))],
        out_specs=[pl.BlockSpec((tm, tn), lambda i, j: (i, j)),
                   pl.BlockSpec((tm, GATE_LANES), lambda i, j: (i, 0))],
        out_shape=[jax.ShapeDtypeStruct((m, PW), out_dtype),
                   jax.ShapeDtypeStruct((m, GATE_LANES), F32)],
        scratch_shapes=[pltpu.VMEM((tm, D), BF16)],
        compiler_params=_params(2),
        name="proj",
    )(x, gamma, w, wg)


def _head_output(num, den, m_t, gout, o):
    hv = num / jnp.maximum(jnp.abs(den), jnp.exp(-m_t))
    hv = hv * lax.rsqrt(jnp.mean(hv * hv, axis=1, keepdims=True) + EPS)
    return hv * gout * jax.nn.sigmoid(o.astype(F32))


def _scan_kernel(q_ref, k_ref, v_ref, o_ref, gt_ref, bias_ref, gout_ref, c0_ref, n0_ref, m0_ref,
                 hg_ref, cout_ref, nout_ref, mout_ref, c_scr, n_scr, m_scr, *, L, lead_pad, nc):
    c = pl.program_id(1)

    @pl.when(c == 0)
    def _():
        c_scr[...] = c0_ref[...]
        n_scr[...] = n0_ref[...]
        m_scr[...] = m0_ref[...]

    gates = gt_ref[...] + bias_ref[...]
    gates_t = gates.T
    row = lax.broadcasted_iota(jnp.int32, (L, L), 0)
    col = lax.broadcasted_iota(jnp.int32, (L, L), 1)
    causal = row >= col
    causal_t = row <= col
    if lead_pad:
        live_col = lax.broadcasted_iota(jnp.int32, (L, 1), 0) >= lead_pad
        live_row = lax.broadcasted_iota(jnp.int32, (1, L), 1) >= lead_pad

    for hh in range(H):
        ig_col = gates[:, hh:hh + 1]
        ig_row = gates_t[hh:hh + 1, :]
        lf_col = _log_sigmoid(gates[:, H + hh:H + hh + 1])
        lf_row = _log_sigmoid(gates_t[H + hh:H + hh + 1, :])
        if lead_pad:
            ig_col = jnp.where(live_col, ig_col, -jnp.inf)
            ig_row = jnp.where(live_row, ig_row, -jnp.inf)
            lf_col = jnp.where(live_col, lf_col, 0.0)
            lf_row = jnp.where(live_row, lf_row, 0.0)
        b_col = jnp.sum(jnp.where(causal, lf_row, 0.0), axis=1, keepdims=True)
        b_row = jnp.sum(jnp.where(causal_t, lf_col, 0.0), axis=0, keepdims=True)
        m_prev = m_scr[hh:hh + 1, 0:1]
        d = jnp.where(causal, b_col - b_row + ig_row, -jnp.inf)
        inter = b_col + m_prev
        m_t = jnp.maximum(inter, jnp.max(d, axis=1, keepdims=True))
        w_inter = jnp.exp(inter - m_t) * SCALE

        q = q_ref[:, hh * DK:(hh + 1) * DK]
        k = k_ref[:, hh * DK:(hh + 1) * DK]
        v = v_ref[:, hh * DV:(hh + 1) * DV]
        qk = lax.dot_general(q, k, (((1,), (1,)), ((), ())), preferred_element_type=F32)
        s = qk * (jnp.exp(d - m_t) * SCALE)
        cmat = c_scr[hh]
        nvec = n_scr[hh:hh + 1, :]
        num = w_inter * jnp.dot(q, cmat.astype(BF16), preferred_element_type=F32) \
            + jnp.dot(s.astype(BF16), v, preferred_element_type=F32)
        den = w_inter * jnp.sum(q.astype(F32) * nvec, axis=1, keepdims=True) \
            + jnp.sum(s, axis=1, keepdims=True)
        hout = _head_output(num, den, m_t, gout_ref[:, hh * DV:(hh + 1) * DV],
                            o_ref[:, hh * DV:(hh + 1) * DV])
        hg_ref[:, hh * DV:(hh + 1) * DV] = hout.astype(hg_ref.dtype)

        m_new = m_t[L - 1:L, :]
        b_last = b_col[L - 1:L, :]
        decay = jnp.exp(b_last + m_prev - m_new)
        wk = jnp.exp(b_last - b_col + ig_col - m_new) * k.astype(F32)
        c_scr[hh] = decay * cmat + jnp.dot(wk.T.astype(BF16), v, preferred_element_type=F32)
        n_scr[hh:hh + 1, :] = decay * nvec + jnp.sum(wk, axis=0, keepdims=True)
        m_scr[hh:hh + 1, :] = jnp.broadcast_to(m_new, (1, GATE_LANES))

    @pl.when(c == nc - 1)
    def _():
        cout_ref[...] = c_scr[...]
        nout_ref[...] = n_scr[...]
        mout_ref[...] = m_scr[...]


def _scan(p, gates, bias, gout, c0, n0, m0, *, L, lead_pad=0):
    b, s, _ = p.shape
    nc = s // L
    kern = functools.partial(_scan_kernel, L=L, lead_pad=lead_pad, nc=nc)
    return pl.pallas_call(
        kern,
        grid=(b, nc),
        in_specs=[pl.BlockSpec((None, L, QKW), lambda i, c: (i, c, 0)),
                  pl.BlockSpec((None, L, QKW), lambda i, c: (i, c, 1)),
                  pl.BlockSpec((None, L, VW), lambda i, c: (i, c, 1)),
                  pl.BlockSpec((None, L, VW), lambda i, c: (i, c, 2)),
                  pl.BlockSpec((None, L, GATE_LANES), lambda i, c: (i, c, 0)),
                  pl.BlockSpec((1, GATE_LANES), lambda i, c: (0, 0)),
                  pl.BlockSpec((1, VW), lambda i, c: (0, 0)),
                  pl.BlockSpec((H, DK, DV), lambda i, c: (0, 0, 0)),
                  pl.BlockSpec((8, DK), lambda i, c: (0, 0)),
                  pl.BlockSpec((8, GATE_LANES), lambda i, c: (0, 0))],
        out_specs=[pl.BlockSpec((None, L, VW), lambda i, c: (i, c, 0)),
                   pl.BlockSpec((None, H, DK, DV), lambda i, c: (i, 0, 0, 0)),
                   pl.BlockSpec((None, 8, DK), lambda i, c: (i, 0, 0)),
                   pl.BlockSpec((None, 8, GATE_LANES), lambda i, c: (i, 0, 0))],
        out_shape=[jax.ShapeDtypeStruct((b, s, VW), BF16),
                   jax.ShapeDtypeStruct((b, H, DK, DV), F32),
                   jax.ShapeDtypeStruct((b, 8, DK), F32),
                   jax.ShapeDtypeStruct((b, 8, GATE_LANES), F32)],
        scratch_shapes=[pltpu.VMEM((H, DK, DV), F32),
                        pltpu.VMEM((8, DK), F32),
                        pltpu.VMEM((8, GATE_LANES), F32)],
        compiler_params=_params(2),
        name="scan",
    )(p, p, p, p, gates, bias, gout, c0, n0, m0)


def _scan_s_kernel(q_ref, k_ref, v_ref, o_ref, gt_ref, bias_ref, gout_ref, mtok_ref, c_ref, n_ref,
                   hg_ref, cout_ref, nout_ref, mout_ref, qc_scr, ntok_scr, *, T, NB):
    hh = pl.program_id(1)
    LT = NB * T
    gates = gt_ref[...] + bias_ref[...]
    gates_t = gates.T
    lane = lax.broadcasted_iota(jnp.int32, (LT, GATE_LANES), 1)
    sub = lax.broadcasted_iota(jnp.int32, (GATE_LANES, LT), 0)

    def pick_col(a, idx):
        return jnp.sum(jnp.where(lane == idx, a, 0.0), axis=1, keepdims=True)

    def pick_row(a, idx):
        return jnp.sum(jnp.where(sub == idx, a, 0.0), axis=0, keepdims=True)

    ig_col = pick_col(gates, hh)
    ig_row = pick_row(gates_t, hh)
    lf_col = _log_sigmoid(pick_col(gates, hh + H))
    lf_row = _log_sigmoid(pick_row(gates_t, hh + H))
    m_prev = pick_col(mtok_ref[...], hh)

    row = lax.broadcasted_iota(jnp.int32, (LT, LT), 0)
    col = lax.broadcasted_iota(jnp.int32, (LT, LT), 1)
    same = (row // T) == (col // T)
    causal = jnp.logical_and(same, row >= col)
    causal_t = jnp.logical_and(same, row <= col)
    b_col = jnp.sum(jnp.where(causal, lf_row, 0.0), axis=1, keepdims=True)
    b_row = jnp.sum(jnp.where(causal_t, lf_col, 0.0), axis=0, keepdims=True)
    b_end = jnp.sum(jnp.where(same, lf_row, 0.0), axis=1, keepdims=True)
    d = jnp.where(causal, b_col - b_row + ig_row, -jnp.inf)
    inter = b_col + m_prev
    m_t = jnp.maximum(inter, jnp.max(d, axis=1, keepdims=True))
    d_end = jnp.where(same, b_end - b_row + ig_row, -jnp.inf)
    m_new = jnp.maximum(b_end + m_prev, jnp.max(d_end, axis=1, keepdims=True))
    w_inter = jnp.exp(inter - m_t) * SCALE

    q32 = q_ref[...]
    k32 = k_ref[...]
    q = q32.astype(BF16)
    v = v_ref[...].astype(BF16)
    qk = lax.dot_general(q, k32.astype(BF16), (((1,), (1,)), ((), ())), preferred_element_type=F32)
    s = qk * (jnp.exp(d - m_t) * SCALE)
    num_intra = jnp.dot(s.astype(BF16), v, preferred_element_type=F32)
    den_intra = jnp.sum(s, axis=1, keepdims=True)

    decay = jnp.exp(b_end + m_prev - m_new)
    wk = jnp.exp(b_end - b_col + ig_col - m_new) * k32
    wk_t = wk.T
    col_seq = lax.broadcasted_iota(jnp.int32, (DK, LT), 1) // T

    for bb in range(NB):
        r0 = bb * T
        cmat = c_ref[bb]
        nvec = n_ref[bb:bb + 1, :]
        qc_scr[r0:r0 + T, :] = jnp.dot(q32[r0:r0 + T, :].astype(BF16), cmat.astype(BF16),
                                       preferred_element_type=F32)
        ntok_scr[r0:r0 + T, :] = jnp.broadcast_to(nvec, (T, DK))
        upd = jnp.dot(jnp.where(col_seq == bb, wk_t, 0.0).astype(BF16), v, preferred_element_type=F32)
        dec = decay[r0:r0 + 1, :]
        cout_ref[bb] = dec * cmat + upd
        nout_ref[bb:bb + 1, :] = dec * nvec + jnp.sum(wk[r0:r0 + T, :], axis=0, keepdims=True)
        mout_ref[bb:bb + 1, :] = jnp.broadcast_to(m_new[r0:r0 + 1, :], (1, GATE_LANES))

    num = w_inter * qc_scr[...] + num_intra
    den = w_inter * jnp.sum(q32 * ntok_scr[...], axis=1, keepdims=True) + den_intra
    hg_ref[...] = _head_output(num, den, m_t, gout_ref[...], o_ref[...]).astype(hg_ref.dtype)


def _scan_s(p, gates, bias, gout, mtok, c, n_hm, *, T):
    nseq = c.shape[0]
    nb = SAMPLE_GROUP
    lt = nb * T
    kern = functools.partial(_scan_s_kernel, T=T, NB=nb)
    return pl.pallas_call(
        kern,
        grid=(nseq // nb, H),
        in_specs=[pl.BlockSpec((lt, DK), lambda g, h: (g, h)),
                  pl.BlockSpec((lt, DK), lambda g, h: (g, H + h)),
                  pl.BlockSpec((lt, DV), lambda g, h: (g, H + h)),
                  pl.BlockSpec((lt, DV), lambda g, h: (g, 2 * H + h)),
                  pl.BlockSpec((lt, GATE_LANES), lambda g, h: (g, 0)),
                  pl.BlockSpec((1, GATE_LANES), lambda g, h: (0, 0)),
                  pl.BlockSpec((1, DV), lambda g, h: (0, h)),
                  pl.BlockSpec((lt, GATE_LANES), lambda g, h: (g, 0)),
                  pl.BlockSpec((nb, None, DK, DV), lambda g, h: (g, h, 0, 0)),
                  pl.BlockSpec((None, nb, DK), lambda g, h: (h, g, 0))],
        out_specs=[pl.BlockSpec((lt, DV), lambda g, h: (g, h)),
                   pl.BlockSpec((nb, None, DK, DV), lambda g, h: (g, h, 0, 0)),
                   pl.BlockSpec((None, nb, DK), lambda g, h: (h, g, 0)),
                   pl.BlockSpec((None, nb, GATE_LANES), lambda g, h: (h, g, 0))],
        out_shape=[jax.ShapeDtypeStruct((nseq * T, VW), BF16),
                   jax.ShapeDtypeStruct((nseq, H, DK, DV), F32),
                   jax.ShapeDtypeStruct((H, nseq, DK), F32),
                   jax.ShapeDtypeStruct((H, nseq, GATE_LANES), F32)],
        scratch_shapes=[pltpu.VMEM((lt, DV), F32), pltpu.VMEM((lt, DK), F32)],
        compiler_params=_params(2),
        name="scan_sample",
    )(p, p, p, p, gates, bias, gout, mtok, c, n_hm)


def _mmres_kernel(a_ref, w_ref, x_ref, o_ref):
    o_ref[...] = x_ref[...] + jnp.dot(a_ref[...], w_ref[...].astype(BF16), preferred_element_type=F32)


def _mmres(a, w, x, *, tm, tn):
    m, kdim = a.shape
    n = w.shape[2]
    return pl.pallas_call(
        _mmres_kernel,
        grid=(m // tm, n // tn),
        in_specs=[pl.BlockSpec((tm, kdim), lambda i, j: (i, 0)),
                  pl.BlockSpec((None, kdim, tn), lambda i, j: (0, 0, j)),
                  pl.BlockSpec((tm, tn), lambda i, j: (i, j))],
        out_specs=pl.BlockSpec((tm, tn), lambda i, j: (i, j)),
        out_shape=jax.ShapeDtypeStruct((m, n), F32),
        compiler_params=_params(2),
        name="out_proj",
    )(a, w, x)


def _ffn_kernel(x_ref, gam_ref, wg_ref, wa0_ref, wa1_ref, cw_ref, cb_ref, wd_ref, st_ref, gfin_ref,
                o_ref, so_ref, u_scr, gext_scr, carry_scr, *, tm, sr, shift, tps, nsub, final_norm):
    i = pl.program_id(0)
    f = pl.program_id(1)

    @pl.when(f == 0)
    def _():
        x = x_ref[...]
        u_scr[...] = _rms(x, gam_ref[...]).astype(BF16)
        o_ref[...] = x

    valid = D_FF - f * TF
    col_ok = lax.broadcasted_iota(jnp.int32, (1, TF), 1) < valid
    row_ok = lax.broadcasted_iota(jnp.int32, (TF, 1), 0) < valid
    wg = wg_ref[...].astype(BF16)
    wa = jnp.concatenate([wa0_ref[...], wa1_ref[...]], axis=1).astype(BF16)
    wd = jnp.where(row_ok, wd_ref[...], 0.0).astype(BF16)
    cw = cw_ref[...]
    cb = cb_ref[...]

    if tps == 1:
        gext_scr[0:sr, :] = st_ref[...]
    else:
        seq_start = (i % tps) == 0

        @pl.when(seq_start)
        def _():
            gext_scr[0:sr, :] = st_ref[...]

        @pl.when(jnp.logical_not(seq_start))
        def _():
            gext_scr[0:sr, :] = carry_scr[f]

    ts = tm // nsub
    for h in range(nsub):
        r0 = h * ts
        u = u_scr[r0:r0 + ts, :]
        g = jnp.dot(u, wg, preferred_element_type=F32)
        a = jnp.dot(u, wa, preferred_element_type=F32)
        gext_scr[sr + r0:sr + r0 + ts, :] = g
        g_m2 = gext_scr[sr + r0 - 2 * shift:sr + r0 - 2 * shift + ts, :]
        g_m1 = gext_scr[sr + r0 - shift:sr + r0 - shift + ts, :]
        gc = cb + ((cw[0:1, :] * g_m2 + cw[1:2, :] * g_m1) + cw[2:3, :] * g)
        hmid = jnp.where(col_ok, (gc * jax.nn.sigmoid(gc)) * a, 0.0)
        o_ref[r0:r0 + ts, :] += jnp.dot(hmid.astype(BF16), wd, preferred_element_type=F32)

    g_tail = gext_scr[tm:tm + sr, :]
    so_ref[...] = g_tail
    if tps > 1:
        carry_scr[f] = g_tail

    if final_norm:
        @pl.when(f == pl.num_programs(1) - 1)
        def _():
            o_ref[...] = _rms(o_ref[...], gfin_ref[...])


def _ffn(x, layer, gamma, w_up, conv_w, conv_b, w_down, st, gfin, *, tm, sr, shift, tps, st_per_tile,
         final_norm):
    assert TF == 2 * LANES
    m = x.shape[0]
    nt = m // tm
    nsub = 2 if tm >= 64 else 1
    kern = functools.partial(_ffn_kernel, tm=tm, sr=sr, shift=shift, tps=tps, nsub=nsub,
                             final_norm=final_norm)
    st_map = (lambda i, f: (i, f)) if st_per_tile else (lambda i, f: (0, f))
    last_a = 2 * FF_LANE_BLOCKS - 1
    return pl.pallas_call(
        kern,
        grid=(nt, NF),
        in_specs=[pl.BlockSpec((tm, D), lambda i, f: (i, 0), pipeline_mode=pl.Buffered(1)),
                  pl.BlockSpec((1, D), lambda i, f: (0, 0)),
                  pl.BlockSpec((None, D, TF), lambda i, f: (layer, 0, f)),
                  pl.BlockSpec((None, D, LANES), lambda i, f: (layer, 0, FF_LANE_BLOCKS + 2 * f)),
                  pl.BlockSpec((None, D, LANES),
                               lambda i, f: (layer, 0, jnp.minimum(FF_LANE_BLOCKS + 2 * f + 1, last_a))),
                  pl.BlockSpec((None, 3, TF), lambda i, f: (layer, 0, f)),
                  pl.BlockSpec((None, 1, TF), lambda i, f: (layer, 0, f)),
                  pl.BlockSpec((None, TF, D), lambda i, f: (layer, f, 0)),
                  pl.BlockSpec((sr, TF), st_map),
                  pl.BlockSpec((1, D), lambda i, f: (0, 0))],
        out_specs=[pl.BlockSpec((tm, D), lambda i, f: (i, 0)),
                   pl.BlockSpec((sr, TF), lambda i, f: (i, f))],
        out_shape=[jax.ShapeDtypeStruct((m, D), F32),
                   jax.ShapeDtypeStruct((nt * sr, D_FF), F32)],
        scratch_shapes=[pltpu.VMEM((tm, D), BF16),
                        pltpu.VMEM((sr + tm, TF), F32),
                        pltpu.VMEM((NF, sr, TF), F32)],
        compiler_params=_params(2),
        name="conv_ffn",
    )(x, gamma, w_up, w_up, w_up, conv_w, conv_b, w_down, st, gfin)


def _pool_kernel(x_ref, gam_ref, wp_ref, sc_ref, st_ref, o_ref, ut_ref, rinv_scr, uext_scr, carry_scr,
                 *, tm, shift, tps, pos0, tr):
    i = pl.program_id(0)
    grp = pl.program_id(1)
    hr = POOL_HALO * shift

    @pl.when(grp == 0)
    def _():
        x = x_ref[...]
        rinv_scr[...] = lax.rsqrt(jnp.mean(x * x, axis=1, keepdims=True) + EPS)

    for kk, w in enumerate(POOL_WINDOWS):
        @pl.when(grp == kk)
        def _(kk=kk, w=w):
            cs = slice(kk * PG, (kk + 1) * PG)
            xg = x_ref[:, cs]
            ug = xg * rinv_scr[...] * gam_ref[:, cs]
            if tps == 1:
                uext_scr[0:hr, :] = st_ref[...]
            else:
                seq_start = (i % tps) == 0

                @pl.when(seq_start)
                def _():
                    uext_scr[0:hr, :] = st_ref[...]

                @pl.when(jnp.logical_not(seq_start))
                def _():
                    uext_scr[0:hr, :] = carry_scr[kk]

            uext_scr[hr:hr + tm, :] = ug
            acc = ug
            for j in range(1, w):
                acc = acc + uext_scr[hr - j * shift:hr - j * shift + tm, :]
            if pos0 + 1 >= w:
                pooled = acc / float(w) - ug
            else:
                step = (i % tps) * (tm // shift) + lax.broadcasted_iota(jnp.int32, (tm, 1), 0) // shift
                cnt = jnp.minimum(w, pos0 + step + 1).astype(F32)
                pooled = acc / cnt - ug
            y = jnp.dot(pooled.astype(BF16), wp_ref[kk].astype(BF16), preferred_element_type=F32)
            o_ref[...] = xg + y * sc_ref[:, cs]
            ut_ref[...] = ug[tm - tr:tm, :]
            if tps > 1:
                carry_scr[kk] = uext_scr[tm:tm + hr, :]


def _pool(x, gamma, wp, sc, st, *, tm, shift, tps, pos0, tr, st_per_tile):
    m = x.shape[0]
    nt = m // tm
    hr = POOL_HALO * shift
    ng = len(POOL_WINDOWS)
    kern = functools.partial(_pool_kernel, tm=tm, shift=shift, tps=tps, pos0=pos0, tr=tr)
    st_map = (lambda i, g: (i, g)) if st_per_tile else (lambda i, g: (0, g))
    return pl.pallas_call(
        kern,
        grid=(nt, ng),
        in_specs=[pl.BlockSpec((tm, D), lambda i, g: (i, 0)),
                  pl.BlockSpec((1, D), lambda i, g: (0, 0)),
                  pl.BlockSpec((None, ng, PG, PG), lambda i, g: (0, 0, 0, 0)),
                  pl.BlockSpec((1, D), lambda i, g: (0, 0)),
                  pl.BlockSpec((hr, PG), st_map)],
        out_specs=[pl.BlockSpec((tm, PG), lambda i, g: (i, g)),
                   pl.BlockSpec((tr, PG), lambda i, g: (i, g))],
        out_shape=[jax.ShapeDtypeStruct((m, D), F32),
                   jax.ShapeDtypeStruct((nt * tr, D), F32)],
        scratch_shapes=[pltpu.VMEM((tm, 1), F32),
                        pltpu.VMEM((hr + tm, PG), F32),
                        pltpu.VMEM((ng, hr, PG), F32)],
        compiler_params=_params(2),
        name="pool_mixer",
    )(x, gamma, wp, sc, st)


def _pad_cols(a, n):
    return jnp.pad(a, ((0, 0), (0, n - a.shape[1])))


def _to_time_major(a, t):
    nseq = a.shape[0]
    a = a.reshape((nseq // SAMPLE_TILE_SEQS, SAMPLE_TILE_SEQS, t) + a.shape[2:])
    a = jnp.swapaxes(a, 1, 2)
    return a.reshape((nseq * t,) + a.shape[3:])


def _from_time_major(a, nseq, t):
    a = a.reshape((nseq // SAMPLE_TILE_SEQS, t, SAMPLE_TILE_SEQS) + a.shape[1:])
    a = jnp.swapaxes(a, 1, 2)
    return a.reshape((nseq, t) + a.shape[3:])


def kernel(x_prompt, x_sample, state_mlstm_C, state_mlstm_n, state_mlstm_m, state_pool, state_ffn_conv,
           meta_tokens, norm_mix, norm_ffn, norm_final, w_mlstm_in, b_mlstm_gate, g_mlstm_out, w_mlstm_out,
           w_pool, pool_scale, w_up, conv_w, conv_b, w_down):
    bsz, seq, _ = x_prompt.shape
    nseq, t_dec, _ = x_sample.shape

    w_main = w_mlstm_in
    w_gate = _pad_cols(w_mlstm_in[0][:, PW:], GATE_LANES).astype(BF16)
    bias = _pad_cols(b_mlstm_gate[0][None, :], GATE_LANES)
    w_out = w_mlstm_out
    gout = g_mlstm_out[0][None, :]
    wp = w_pool
    psc = pool_scale[0][None, :]
    gfin = norm_final[None, :]
    conv_b3 = conv_b[:, None, :]
    ffn_w = [(layer, norm_ffn[layer][None, :], w_up, conv_w, conv_b3, w_down) for layer in range(2)]
    g_mix0 = norm_mix[0][None, :]
    g_mix1 = norm_mix[1][None, :]

    def long_stream(x, nb, s, st, *, tm, chunk, lead_pad, pos0):
        c0, n0, m0, conv0, pool0, conv1 = st
        tm_ffn = tm
        p, gates = _proj(x, g_mix0, w_main, w_gate, tm=tm, tn=512, out_dtype=BF16)
        p = p.reshape(nb, s, PW)
        gates = gates.reshape(nb, s, GATE_LANES)
        if lead_pad:
            p = jnp.pad(p, ((0, 0), (lead_pad, 0), (0, 0)))
            gates = jnp.pad(gates, ((0, 0), (lead_pad, 0), (0, 0)))
        hg, c_new, n_new, m_new = _scan(p, gates, bias, gout, c0, n0, m0, L=chunk, lead_pad=lead_pad)
        hg = hg[:, lead_pad:].reshape(nb * s, VW)
        x1 = _mmres(hg, w_out, x, tm=tm, tn=512)
        x2, cs0 = _ffn(x1, *ffn_w[0], conv0, gfin, tm=tm_ffn, sr=8, shift=1, tps=s // tm_ffn,
                       st_per_tile=False, final_norm=False)
        x3, ut = _pool(x2, g_mix1, wp, psc, pool0, tm=tm, shift=1, tps=s // tm, pos0=pos0,
                       tr=POOL_HALO, st_per_tile=False)
        y, cs1 = _ffn(x3, *ffn_w[1], conv1, gfin, tm=tm_ffn, sr=8, shift=1, tps=s // tm_ffn,
                      st_per_tile=False, final_norm=True)
        cs0 = cs0.reshape(nb, s // tm_ffn, 8, D_FF)[:, -1]
        cs1 = cs1.reshape(nb, s // tm_ffn, 8, D_FF)[:, -1]
        ut = ut.reshape(nb, s // tm, POOL_HALO, D)[:, -1]
        return y, (c_new, n_new, m_new, cs0, ut, cs1)

    zero_state = (jnp.zeros((H, DK, DV), F32), jnp.zeros((8, DK), F32), jnp.zeros((8, GATE_LANES), F32),
                  jnp.zeros((8, D_FF), F32), jnp.zeros((POOL_HALO, D), F32), jnp.zeros((8, D_FF), F32))
    _, (c_m, n_m, m_m, cs0_m, ut_m, cs1_m) = long_stream(
        meta_tokens, 1, N_META, zero_state, tm=N_META, chunk=128, lead_pad=128 - N_META, pos0=0)

    y_p, (c_p, n_p, m_p, cs0_p, ut_p, cs1_p) = long_stream(
        x_prompt.reshape(bsz * seq, D), bsz, seq, (c_m[0], n_m[0], m_m[0], cs0_m[0], ut_m[0], cs1_m[0]),
        tm=1024, chunk=SCAN_CHUNK, lead_pad=0, pos0=N_META)
    y_prompt = y_p.reshape(bsz, seq, D)
    C_p = c_p[None]
    n_p = n_p[:, :H][None]
    m_p = m_p[:, :H, 0][None]
    pool_p = ut_p[:, 1:][None]
    conv_p = jnp.stack([cs0_p[:, 6:], cs1_p[:, 6:]])

    xs = x_sample.reshape(nseq * t_dec, D)
    p_s, gates_s = _proj(xs, g_mix0, w_main, w_gate, tm=nseq * t_dec, tn=512, out_dtype=F32)
    mtok = _pad_cols(jnp.repeat(state_mlstm_m[0], t_dec, axis=0), GATE_LANES)
    n_hm = jnp.swapaxes(state_mlstm_n[0], 0, 1)
    hg_s, C_s, n_s_hm, m_s_hm = _scan_s(p_s, gates_s, bias, gout, mtok, state_mlstm_C[0], n_hm, T=t_dec)
    tm_s = SAMPLE_TILE_SEQS * t_dec
    hg_t = _to_time_major(hg_s.reshape(nseq, t_dec, VW), t_dec)
    x_t = _to_time_major(x_sample, t_dec)
    x1 = _mmres(hg_t, w_out, x_t, tm=tm_s, tn=512)

    def conv_state_in(cs):
        return _to_time_major(cs, 2)

    def conv_state_out(cs):
        return _from_time_major(cs, nseq, 2)

    sr_s = 2 * SAMPLE_TILE_SEQS
    x2, cs0_s = _ffn(x1, *ffn_w[0], conv_state_in(state_ffn_conv[0]), gfin, tm=tm_s, sr=sr_s,
                     shift=SAMPLE_TILE_SEQS, tps=1, st_per_tile=True, final_norm=False)
    pool_in = jnp.pad(state_pool[0], ((0, 0), (1, 0), (0, 0)))
    x3, ut_s = _pool(x2, g_mix1, wp, psc, _to_time_major(pool_in, POOL_HALO), tm=tm_s,
                     shift=SAMPLE_TILE_SEQS, tps=1, pos0=PAST_LEN, tr=tm_s, st_per_tile=True)
    y_s, cs1_s = _ffn(x3, *ffn_w[1], conv_state_in(state_ffn_conv[1]), gfin, tm=tm_s, sr=sr_s,
                      shift=SAMPLE_TILE_SEQS, tps=1, st_per_tile=True, final_norm=True)
    y_sample = _from_time_major(y_s, nseq, t_dec)
    n_s = jnp.swapaxes(n_s_hm, 0, 1)[None]
    m_s = jnp.swapaxes(m_s_hm[:, :, 0], 0, 1)[None]
    pool_s = jnp.concatenate([state_pool[0][:, t_dec:], _from_time_major(ut_s, nseq, t_dec)], axis=1)[None]
    conv_s = jnp.stack([conv_state_out(cs0_s), conv_state_out(cs1_s)])

    return (y_prompt, y_sample, C_p, n_p, m_p, pool_p, conv_p,
            C_s[None], n_s, m_s, pool_s, conv_s)
```

```python
import functools

import jax
import jax.numpy as jnp
from jax import lax
from jax.experimental import pallas as pl
from jax.experimental.pallas import tpu as pltpu

F32 = jnp.float32
BF16 = jnp.bfloat16

EPS = 1e-6
D = 2048
H = 4
DK = 256
DV = 512
QKW = H * DK
VW = H * DV
PW = 2 * QKW + 2 * VW
GATE_LANES = 128
SCALE = DK ** -0.5
POOL_WINDOWS = (2, 4, 8, 16)
PG = D // len(POOL_WINDOWS)
POOL_HALO = 16
D_FF = 5504
TF = 256
NF = -(-D_FF // TF)
LANES = 128
FF_LANE_BLOCKS = D_FF // LANES
N_META = 16
PAST_LEN = 16384
SCAN_CHUNK = 256
SAMPLE_GROUP = 16
SAMPLE_TILE_SEQS = 128
VMEM_LIMIT = 60 * 1024 * 1024


def _params(n_axes):
    return pltpu.CompilerParams(dimension_semantics=("arbitrary",) * n_axes,
                                vmem_limit_bytes=VMEM_LIMIT)


def _rms(x, g):
    return x * lax.rsqrt(jnp.mean(x * x, axis=-1, keepdims=True) + EPS) * g


def _log_sigmoid(x):
    return jnp.minimum(x, 0.0) - jnp.log(1.0 + jnp.exp(-jnp.abs(x)))


def _proj_kernel(x_ref, g_ref, w_ref, wg_ref, p_ref, gate_ref, u_scr):
    @pl.when(pl.program_id(1) == 0)
    def _():
        ub = _rms(x_ref[...], g_ref[...]).astype(BF16)
        u_scr[...] = ub
        gate_ref[...] = jnp.dot(ub, wg_ref[...], preferred_element_type=F32)

    p_ref[...] = jnp.dot(u_scr[...], w_ref[...].astype(BF16),
                         preferred_element_type=F32).astype(p_ref.dtype)


def _proj(x, gamma, w, wg, *, tm, tn, out_dtype):
    m = x.shape[0]
    return pl.pallas_call(
        _proj_kernel,
        grid=(m // tm, PW // tn),
        in_specs=[pl.BlockSpec((tm, D), lambda i, j: (i, 0)),
                  pl.BlockSpec((1, D), lambda i, j: (0, 0)),
                  pl.BlockSpec((None, D, tn), lambda i, j: (0, 0, j)),
                  pl.BlockSpec((D, GATE_LANES), lambda i, j: (0, 0))],
        out_specs=[pl.BlockSpec((tm, tn), lambda i, j: (i, j)),
                   pl.BlockSpec((tm, GATE_LANES), lambda i, j: (i, 0))],
        out_shape=[jax.ShapeDtypeStruct((m, PW), out_dtype),
                   jax.ShapeDtypeStruct((m, GATE_LANES), F32)],
        scratch_shapes=[pltpu.VMEM((tm, D), BF16)],
        compiler_params=_params(2),
        name="proj",
    )(x, gamma, w, wg)


def _head_output(num, den, m_t, gout, o):
    hv = num / jnp.maximum(jnp.abs(den), jnp.exp(-m_t))
    hv = hv * lax.rsqrt(jnp.mean(hv * hv, axis=1, keepdims=True) + EPS)
    return hv * gout * jax.nn.sigmoid(o.astype(F32))


def _scan_kernel(q_ref, k_ref, v_ref, o_ref, gt_ref, bias_ref, gout_ref, c0_ref, n0_ref, m0_ref,
                 hg_ref, cout_ref, nout_ref, mout_ref, c_scr, n_scr, m_scr, *, L, lead_pad, nc):
    c = pl.program_id(1)

    @pl.when(c == 0)
    def _():
        c_scr[...] = c0_ref[...]
        n_scr[...] = n0_ref[...]
        m_scr[...] = m0_ref[...]

    gates = gt_ref[...] + bias_ref[...]
    gates_t = gates.T
    row = lax.broadcasted_iota(jnp.int32, (L, L), 0)
    col = lax.broadcasted_iota(jnp.int32, (L, L), 1)
    causal = row >= col
    causal_t = row <= col
    if lead_pad:
        live_col = lax.broadcasted_iota(jnp.int32, (L, 1), 0) >= lead_pad
        live_row = lax.broadcasted_iota(jnp.int32, (1, L), 1) >= lead_pad

    for hh in range(H):
        ig_col = gates[:, hh:hh + 1]
        ig_row = gates_t[hh:hh + 1, :]
        lf_col = _log_sigmoid(gates[:, H + hh:H + hh + 1])
        lf_row = _log_sigmoid(gates_t[H + hh:H + hh + 1, :])
        if lead_pad:
            ig_col = jnp.where(live_col, ig_col, -jnp.inf)
            ig_row = jnp.where(live_row, ig_row, -jnp.inf)
            lf_col = jnp.where(live_col, lf_col, 0.0)
            lf_row = jnp.where(live_row, lf_row, 0.0)
        b_col = jnp.sum(jnp.where(causal, lf_row, 0.0), axis=1, keepdims=True)
        b_row = jnp.sum(jnp.where(causal_t, lf_col, 0.0), axis=0, keepdims=True)
        m_prev = m_scr[hh:hh + 1, 0:1]
        d = jnp.where(causal, b_col - b_row + ig_row, -jnp.inf)
        inter = b_col + m_prev
        m_t = jnp.maximum(inter, jnp.max(d, axis=1, keepdims=True))
        w_inter = jnp.exp(inter - m_t) * SCALE

        q = q_ref[:, hh * DK:(hh + 1) * DK]
        k = k_ref[:, hh * DK:(hh + 1) * DK]
        v = v_ref[:, hh * DV:(hh + 1) * DV]
        qk = lax.dot_general(q, k, (((1,), (1,)), ((), ())), preferred_element_type=F32)
        s = qk * (jnp.exp(d - m_t) * SCALE)
        cmat = c_scr[hh]
        nvec = n_scr[hh:hh + 1, :]
        num = w_inter * jnp.dot(q, cmat.astype(BF16), preferred_element_type=F32) \
            + jnp.dot(s.astype(BF16), v, preferred_element_type=F32)
        den = w_inter * jnp.sum(q.astype(F32) * nvec, axis=1, keepdims=True) \
            + jnp.sum(s, axis=1, keepdims=True)
        hout = _head_output(num, den, m_t, gout_ref[:, hh * DV:(hh + 1) * DV],
                            o_ref[:, hh * DV:(hh + 1) * DV])
        hg_ref[:, hh * DV:(hh + 1) * DV] = hout.astype(hg_ref.dtype)

        m_new = m_t[L - 1:L, :]
        b_last = b_col[L - 1:L, :]
        decay = jnp.exp(b_last + m_prev - m_new)
        wk = jnp.exp(b_last - b_col + ig_col - m_new) * k.astype(F32)
        c_scr[hh] = decay * cmat + jnp.dot(wk.T.astype(BF16), v, preferred_element_type=F32)
        n_scr[hh:hh + 1, :] = decay * nvec + jnp.sum(wk, axis=0, keepdims=True)
        m_scr[hh:hh + 1, :] = jnp.broadcast_to(m_new, (1, GATE_LANES))

    @pl.when(c == nc - 1)
    def _():
        cout_ref[...] = c_scr[...]
        nout_ref[...] = n_scr[...]
        mout_ref[...] = m_scr[...]


def _scan(p, gates, bias, gout, c0, n0, m0, *, L, lead_pad=0):
    b, s, _ = p.shape
    nc = s // L
    kern = functools.partial(_scan_kernel, L=L, lead_pad=lead_pad, nc=nc)
    return pl.pallas_call(
        kern,
        grid=(b, nc),
        in_specs=[pl.BlockSpec((None, L, QKW), lambda i, c: (i, c, 0)),
                  pl.BlockSpec((None, L, QKW), lambda i, c: (i, c, 1)),
                  pl.BlockSpec((None, L, VW), lambda i, c: (i, c, 1)),
                  pl.BlockSpec((None, L, VW), lambda i, c: (i, c, 2)),
                  pl.BlockSpec((None, L, GATE_LANES), lambda i, c: (i, c, 0)),
                  pl.BlockSpec((1, GATE_LANES), lambda i, c: (0, 0)),
                  pl.BlockSpec((1, VW), lambda i, c: (0, 0)),
                  pl.BlockSpec((H, DK, DV), lambda i, c: (0, 0, 0)),
                  pl.BlockSpec((8, DK), lambda i, c: (0, 0)),
                  pl.BlockSpec((8, GATE_LANES), lambda i, c: (0, 0))],
        out_specs=[pl.BlockSpec((None, L, VW), lambda i, c: (i, c, 0)),
                   pl.BlockSpec((None, H, DK, DV), lambda i, c: (i, 0, 0, 0)),
                   pl.BlockSpec((None, 8, DK), lambda i, c: (i, 0, 0)),
                   pl.BlockSpec((None, 8, GATE_LANES), lambda i, c: (i, 0, 0))],
        out_shape=[jax.ShapeDtypeStruct((b, s, VW), BF16),
                   jax.ShapeDtypeStruct((b, H, DK, DV), F32),
                   jax.ShapeDtypeStruct((b, 8, DK), F32),
                   jax.ShapeDtypeStruct((b, 8, GATE_LANES), F32)],
        scratch_shapes=[pltpu.VMEM((H, DK, DV), F32),
                        pltpu.VMEM((8, DK), F32),
                        pltpu.VMEM((8, GATE_LANES), F32)],
        compiler_params=_params(2),
        name="scan",
    )(p, p, p, p, gates, bias, gout, c0, n0, m0)


def _scan_s_kernel(q_ref, k_ref, v_ref, o_ref, gt_ref, bias_ref, gout_ref, mtok_ref, c_ref, n_ref,
                   hg_ref, cout_ref, nout_ref, mout_ref, qc_scr, ntok_scr, *, T, NB):
    hh = pl.program_id(1)
    LT = NB * T
    gates = gt_ref[...] + bias_ref[...]
    gates_t = gates.T
    lane = lax.broadcasted_iota(jnp.int32, (LT, GATE_LANES), 1)
    sub = lax.broadcasted_iota(jnp.int32, (GATE_LANES, LT), 0)

    def pick_col(a, idx):
        return jnp.sum(jnp.where(lane == idx, a, 0.0), axis=1, keepdims=True)

    def pick_row(a, idx):
        return jnp.sum(jnp.where(sub == idx, a, 0.0), axis=0, keepdims=True)

    ig_col = pick_col(gates, hh)
    ig_row = pick_row(gates_t, hh)
    lf_col = _log_sigmoid(pick_col(gates, hh + H))
    lf_row = _log_sigmoid(pick_row(gates_t, hh + H))
    m_prev = pick_col(mtok_ref[...], hh)

    row = lax.broadcasted_iota(jnp.int32, (LT, LT), 0)
    col = lax.broadcasted_iota(jnp.int32, (LT, LT), 1)
    same = (row // T) == (col // T)
    causal = jnp.logical_and(same, row >= col)
    causal_t = jnp.logical_and(same, row <= col)
    b_col = jnp.sum(jnp.where(causal, lf_row, 0.0), axis=1, keepdims=True)
    b_row = jnp.sum(jnp.where(causal_t, lf_col, 0.0), axis=0, keepdims=True)
    b_end = jnp.sum(jnp.where(same, lf_row, 0.0), axis=1, keepdims=True)
    d = jnp.where(causal, b_col - b_row + ig_row, -jnp.inf)
    inter = b_col + m_prev
    m_t = jnp.maximum(inter, jnp.max(d, axis=1, keepdims=True))
    d_end = jnp.where(same, b_end - b_row + ig_row, -jnp.inf)
    m_new = jnp.maximum(b_end + m_prev, jnp.max(d_end, axis=1, keepdims=True))
    w_inter = jnp.exp(inter - m_t) * SCALE

    q32 = q_ref[...]
    k32 = k_ref[...]
    q = q32.astype(BF16)
    v = v_ref[...].astype(BF16)
    qk = lax.dot_general(q, k32.astype(BF16), (((1,), (1,)), ((), ())), preferred_element_type=F32)
    s = qk * (jnp.exp(d - m_t) * SCALE)
    num_intra = jnp.dot(s.astype(BF16), v, preferred_element_type=F32)
    den_intra = jnp.sum(s, axis=1, keepdims=True)

    decay = jnp.exp(b_end + m_prev - m_new)
    wk = jnp.exp(b_end - b_col + ig_col - m_new) * k32
    wk_t = wk.T
    col_seq = lax.broadcasted_iota(jnp.int32, (DK, LT), 1) // T

    for bb in range(NB):
        r0 = bb * T
        cmat = c_ref[bb]
        nvec = n_ref[bb:bb + 1, :]
        qc_scr[r0:r0 + T, :] = jnp.dot(q32[r0:r0 + T, :].astype(BF16), cmat.astype(BF16),
                                       preferred_element_type=F32)
        ntok_scr[r0:r0 + T, :] = jnp.broadcast_to(nvec, (T, DK))
        upd = jnp.dot(jnp.where(col_seq == bb, wk_t, 0.0).astype(BF16), v, preferred_element_type=F32)
        dec = decay[r0:r0 + 1, :]
        cout_ref[bb] = dec * cmat + upd
        nout_ref[bb:bb + 1, :] = dec * nvec + jnp.sum(wk[r0:r0 + T, :], axis=0, keepdims=True)
        mout_ref[bb:bb + 1, :] = jnp.broadcast_to(m_new[r0:r0 + 1, :], (1, GATE_LANES))

    num = w_inter * qc_scr[...] + num_intra
    den = w_inter * jnp.sum(q32 * ntok_scr[...], axis=1, keepdims=True) + den_intra
    hg_ref[...] = _head_output(num, den, m_t, gout_ref[...], o_ref[...]).astype(hg_ref.dtype)


def _scan_s(p, gates, bias, gout, mtok, c, n_hm, *, T):
    nseq = c.shape[0]
    nb = SAMPLE_GROUP
    lt = nb * T
    kern = functools.partial(_scan_s_kernel, T=T, NB=nb)
    return pl.pallas_call(
        kern,
        grid=(nseq // nb, H),
        in_specs=[pl.BlockSpec((lt, DK), lambda g, h: (g, h)),
                  pl.BlockSpec((lt, DK), lambda g, h: (g, H + h)),
                  pl.BlockSpec((lt, DV), lambda g, h: (g, H + h)),
                  pl.BlockSpec((lt, DV), lambda g, h: (g, 2 * H + h)),
                  pl.BlockSpec((lt, GATE_LANES), lambda g, h: (g, 0)),
                  pl.BlockSpec((1, GATE_LANES), lambda g, h: (0, 0)),
                  pl.BlockSpec((1, DV), lambda g, h: (0, h)),
                  pl.BlockSpec((lt, GATE_LANES), lambda g, h: (g, 0)),
                  pl.BlockSpec((nb, None, DK, DV), lambda g, h: (g, h, 0, 0)),
                  pl.BlockSpec((None, nb, DK), lambda g, h: (h, g, 0))],
        out_specs=[pl.BlockSpec((lt, DV), lambda g, h: (g, h)),
                   pl.BlockSpec((nb, None, DK, DV), lambda g, h: (g, h, 0, 0)),
                   pl.BlockSpec((None, nb, DK), lambda g, h: (h, g, 0)),
                   pl.BlockSpec((None, nb, GATE_LANES), lambda g, h: (h, g, 0))],
        out_shape=[jax.ShapeDtypeStruct((nseq * T, VW), BF16),
                   jax.ShapeDtypeStruct((nseq, H, DK, DV), F32),
                   jax.ShapeDtypeStruct((H, nseq, DK), F32),
                   jax.ShapeDtypeStruct((H, nseq, GATE_LANES), F32)],
        scratch_shapes=[pltpu.VMEM((lt, DV), F32), pltpu.VMEM((lt, DK), F32)],
        compiler_params=_params(2),
        name="scan_sample",
    )(p, p, p, p, gates, bias, gout, mtok, c, n_hm)


def _mmres_kernel(a_ref, w_ref, x_ref, o_ref):
    o_ref[...] = x_ref[...] + jnp.dot(a_ref[...], w_ref[...].astype(BF16), preferred_element_type=F32)


def _mmres(a, w, x, *, tm, tn):
    m, kdim = a.shape
    n = w.shape[2]
    return pl.pallas_call(
        _mmres_kernel,
        grid=(m // tm, n // tn),
        in_specs=[pl.BlockSpec((tm, kdim), lambda i, j: (i, 0)),
                  pl.BlockSpec((None, kdim, tn), lambda i, j: (0, 0, j)),
                  pl.BlockSpec((tm, tn), lambda i, j: (i, j))],
        out_specs=pl.BlockSpec((tm, tn), lambda i, j: (i, j)),
        out_shape=jax.ShapeDtypeStruct((m, n), F32),
        compiler_params=_params(2),
        name="out_proj",
    )(a, w, x)


def _ffn_kernel(x_ref, gam_ref, wg_ref, wa0_ref, wa1_ref, cw_ref, cb_ref, wd_ref, st_ref, gfin_ref,
                o_ref, so_ref, u_scr, gext_scr, a_scr, carry_scr, wup0_scr, wup1_scr, wdn0_scr, wdn1_scr,
                *, tm, sr, shift, tps, nsub, final_norm):
    i = pl.program_id(0)
    s = pl.program_id(1)
    wup_slots = (wup0_scr, wup1_scr)
    wdn_slots = (wdn0_scr, wdn1_scr)

    def cast_steps(slot):
        wup_scr = wup_slots[slot]
        wdn_scr = wdn_slots[slot]

        def cast_gate():
            wup_scr[:, 0:TF] = wg_ref[...].astype(BF16)

        def cast_value():
            wup_scr[:, TF:TF + LANES] = wa0_ref[...].astype(BF16)
            wup_scr[:, TF + LANES:2 * TF] = wa1_ref[...].astype(BF16)

        def cast_down(r0, rows):
            valid = D_FF - jnp.minimum(s, NF - 1) * TF
            row_ok = r0 + lax.broadcasted_iota(jnp.int32, (rows, 1), 0) < valid
            wdn_scr[r0:r0 + rows, :] = jnp.where(row_ok, wd_ref[r0:r0 + rows, :], 0.0).astype(BF16)

        half = TF // 2
        return [cast_gate, cast_value, functools.partial(cast_down, 0, half),
                functools.partial(cast_down, half, half)]

    def run_tile(slot, fillers=()):
        fillers = list(fillers)

        def fill():
            if fillers:
                fillers.pop(0)()

        wup_scr = wup_slots[slot]
        wdn_scr = wdn_slots[slot]
        f = s - 1
        col_ok = lax.broadcasted_iota(jnp.int32, (1, TF), 1) < D_FF - f * TF
        cw = cw_ref[...]
        cb = cb_ref[...]
        if tps == 1:
            gext_scr[0:sr, :] = st_ref[...]
        else:
            gext_scr[0:sr, :] = jnp.where((i % tps) == 0, st_ref[...], carry_scr[f])
        ts = tm // nsub
        for h in range(nsub):
            r0 = h * ts
            ga = jnp.dot(u_scr[r0:r0 + ts, :], wup_scr[...], preferred_element_type=F32)
            gext_scr[sr + r0:sr + r0 + ts, :] = ga[:, 0:TF]
            a_scr[r0:r0 + ts, :] = ga[:, TF:2 * TF]
            fill()
        for h in range(nsub):
            r0 = h * ts
            g = gext_scr[sr + r0:sr + r0 + ts, :]
            g_m2 = gext_scr[sr + r0 - 2 * shift:sr + r0 - 2 * shift + ts, :]
            g_m1 = gext_scr[sr + r0 - shift:sr + r0 - shift + ts, :]
            gc = cb + ((cw[0:1, :] * g_m2 + cw[1:2, :] * g_m1) + cw[2:3, :] * g)
            hmid = jnp.where(col_ok, (gc * jax.nn.sigmoid(gc)) * a_scr[r0:r0 + ts, :], 0.0)
            o_ref[r0:r0 + ts, :] += jnp.dot(hmid.astype(BF16), wdn_scr[...], preferred_element_type=F32)
            fill()
        while fillers:
            fill()
        g_tail = gext_scr[tm:tm + sr, :]
        so_ref[...] = g_tail
        if tps > 1:
            carry_scr[f] = g_tail

    @pl.when(s == 0)
    def _():
        x = x_ref[...]
        u_scr[...] = _rms(x, gam_ref[...]).astype(BF16)
        o_ref[...] = x
        for step in cast_steps(0):
            step()
        if tps > 1:
            @pl.when(i == 0)
            def _():
                carry_scr[...] = jnp.zeros_like(carry_scr)

    for parity in range(2):
        @pl.when(jnp.logical_and(s > 0, s % 2 == parity))
        def _(parity=parity):
            run_tile(1 - parity, cast_steps(parity))

    if final_norm:
        @pl.when(s == NF)
        def _():
            o_ref[...] = _rms(o_ref[...], gfin_ref[...])


def _ffn(x, layer, gamma, w_up, conv_w, conv_b, w_down, st, gfin, *, tm, sr, shift, tps, st_per_tile,
         final_norm):
    assert TF == 2 * LANES
    m = x.shape[0]
    nt = m // tm
    nsub = 2 if tm >= 64 else 1
    kern = functools.partial(_ffn_kernel, tm=tm, sr=sr, shift=shift, tps=tps, nsub=nsub,
                             final_norm=final_norm)
    last_a = 2 * FF_LANE_BLOCKS - 1

    def wt(s):
        return jnp.minimum(s, NF - 1)

    def ft(s):
        return jnp.maximum(s - 1, 0)

    st_map = (lambda i, s: (i, ft(s))) if st_per_tile else (lambda i, s: (0, ft(s)))
    return pl.pallas_call(
        kern,
        grid=(nt, NF + 1),
        in_specs=[pl.BlockSpec((tm, D), lambda i, s: (i, 0), pipeline_mode=pl.Buffered(1)),
                  pl.BlockSpec((1, D), lambda i, s: (0, 0)),
                  pl.BlockSpec((None, D, TF), lambda i, s: (layer, 0, wt(s))),
                  pl.BlockSpec((None, D, LANES), lambda i, s: (layer, 0, FF_LANE_BLOCKS + 2 * wt(s))),
                  pl.BlockSpec((None, D, LANES),
                               lambda i, s: (layer, 0, jnp.minimum(FF_LANE_BLOCKS + 2 * wt(s) + 1, last_a))),
                  pl.BlockSpec((None, 3, TF), lambda i, s: (layer, 0, ft(s))),
                  pl.BlockSpec((None, 1, TF), lambda i, s: (layer, 0, ft(s))),
                  pl.BlockSpec((None, TF, D), lambda i, s: (layer, wt(s), 0)),
                  pl.BlockSpec((sr, TF), st_map),
                  pl.BlockSpec((1, D), lambda i, s: (0, 0))],
        out_specs=[pl.BlockSpec((tm, D), lambda i, s: (i, 0)),
                   pl.BlockSpec((sr, TF), lambda i, s: (i, ft(s)))],
        out_shape=[jax.ShapeDtypeStruct((m, D), F32),
                   jax.ShapeDtypeStruct((nt * sr, D_FF), F32)],
        scratch_shapes=[pltpu.VMEM((tm, D), BF16),
                        pltpu.VMEM((sr + tm, TF), F32),
                        pltpu.VMEM((tm, TF), F32),
                        pltpu.VMEM((NF, sr, TF), F32),
                        pltpu.VMEM((D, 2 * TF), BF16),
                        pltpu.VMEM((D, 2 * TF), BF16),
                        pltpu.VMEM((TF, D), BF16),
                        pltpu.VMEM((TF, D), BF16)],
        compiler_params=_params(2),
        name="conv_ffn",
    )(x, gamma, w_up, w_up, w_up, conv_w, conv_b, w_down, st, gfin)


def _pool_kernel(x_ref, gam_ref, wp_ref, sc_ref, st_ref, o_ref, ut_ref, rinv_scr, uext_scr, carry_scr,
                 *, tm, shift, tps, pos0, tr):
    i = pl.program_id(0)
    grp = pl.program_id(1)
    hr = POOL_HALO * shift

    @pl.when(grp == 0)
    def _():
        x = x_ref[...]
        rinv_scr[...] = lax.rsqrt(jnp.mean(x * x, axis=1, keepdims=True) + EPS)

    for kk, w in enumerate(POOL_WINDOWS):
        @pl.when(grp == kk)
        def _(kk=kk, w=w):
            cs = slice(kk * PG, (kk + 1) * PG)
            xg = x_ref[:, cs]
            ug = xg * rinv_scr[...] * gam_ref[:, cs]
            if tps == 1:
                uext_scr[0:hr, :] = st_ref[...]
            else:
                seq_start = (i % tps) == 0

                @pl.when(seq_start)
                def _():
                    uext_scr[0:hr, :] = st_ref[...]

                @pl.when(jnp.logical_not(seq_start))
                def _():
                    uext_scr[0:hr, :] = carry_scr[kk]

            uext_scr[hr:hr + tm, :] = ug
            acc = ug
            for j in range(1, w):
                acc = acc + uext_scr[hr - j * shift:hr - j * shift + tm, :]
            if pos0 + 1 >= w:
                pooled = acc / float(w) - ug
            else:
                step = (i % tps) * (tm // shift) + lax.broadcasted_iota(jnp.int32, (tm, 1), 0) // shift
                cnt = jnp.minimum(w, pos0 + step + 1).astype(F32)
                pooled = acc / cnt - ug
            y = jnp.dot(pooled.astype(BF16), wp_ref[kk].astype(BF16), preferred_element_type=F32)
            o_ref[...] = xg + y * sc_ref[:, cs]
            ut_ref[...] = ug[tm - tr:tm, :]
            if tps > 1:
                carry_scr[kk] = uext_scr[tm:tm + hr, :]


def _pool(x, gamma, wp, sc, st, *, tm, shift, tps, pos0, tr, st_per_tile):
    m = x.shape[0]
    nt = m // tm
    hr = POOL_HALO * shift
    ng = len(POOL_WINDOWS)
    kern = functools.partial(_pool_kernel, tm=tm, shift=shift, tps=tps, pos0=pos0, tr=tr)
    st_map = (lambda i, g: (i, g)) if st_per_tile else (lambda i, g: (0, g))
    return pl.pallas_call(
        kern,
        grid=(nt, ng),
        in_specs=[pl.BlockSpec((tm, D), lambda i, g: (i, 0)),
                  pl.BlockSpec((1, D), lambda i, g: (0, 0)),
                  pl.BlockSpec((None, ng, PG, PG), lambda i, g: (0, 0, 0, 0)),
                  pl.BlockSpec((1, D), lambda i, g: (0, 0)),
                  pl.BlockSpec((hr, PG), st_map)],
        out_specs=[pl.BlockSpec((tm, PG), lambda i, g: (i, g)),
                   pl.BlockSpec((tr, PG), lambda i, g: (i, g))],
        out_shape=[jax.ShapeDtypeStruct((m, D), F32),
                   jax.ShapeDtypeStruct((nt * tr, D), F32)],
        scratch_shapes=[pltpu.VMEM((tm, 1), F32),
                        pltpu.VMEM((hr + tm, PG), F32),
                        pltpu.VMEM((ng, hr, PG), F32)],
        compiler_params=_params(2),
        name="pool_mixer",
    )(x, gamma, wp, sc, st)


def _pad_cols(a, n):
    return jnp.pad(a, ((0, 0), (0, n - a.shape[1])))


def _to_time_major(a, t):
    nseq = a.shape[0]
    a = a.reshape((nseq // SAMPLE_TILE_SEQS, SAMPLE_TILE_SEQS, t) + a.shape[2:])
    a = jnp.swapaxes(a, 1, 2)
    return a.reshape((nseq * t,) + a.shape[3:])


def _from_time_major(a, nseq, t):
    a = a.reshape((nseq // SAMPLE_TILE_SEQS, t, SAMPLE_TILE_SEQS) + a.shape[1:])
    a = jnp.swapaxes(a, 1, 2)
    return a.reshape((nseq, t) + a.shape[3:])


def kernel(x_prompt, x_sample, state_mlstm_C, state_mlstm_n, state_mlstm_m, state_pool, state_ffn_conv,
           meta_tokens, norm_mix, norm_ffn, norm_final, w_mlstm_in, b_mlstm_gate, g_mlstm_out, w_mlstm_out,
           w_pool, pool_scale, w_up, conv_w, conv_b, w_down):
    bsz, seq, _ = x_prompt.shape
    nseq, t_dec, _ = x_sample.shape

    w_main = w_mlstm_in
    w_gate = _pad_cols(w_mlstm_in[0][:, PW:], GATE_LANES).astype(BF16)
    bias = _pad_cols(b_mlstm_gate[0][None, :], GATE_LANES)
    w_out = w_mlstm_out
    gout = g_mlstm_out[0][None, :]
    wp = w_pool
    psc = pool_scale[0][None, :]
    gfin = norm_final[None, :]
    conv_b3 = conv_b[:, None, :]
    ffn_w = [(layer, norm_ffn[layer][None, :], w_up, conv_w, conv_b3, w_down) for layer in range(2)]
    g_mix0 = norm_mix[0][None, :]
    g_mix1 = norm_mix[1][None, :]

    def long_stream(x, nb, s, st, *, tm, chunk, lead_pad, pos0):
        c0, n0, m0, conv0, pool0, conv1 = st
        tm_ffn = tm
        p, gates = _proj(x, g_mix0, w_main, w_gate, tm=tm, tn=512, out_dtype=BF16)
        p = p.reshape(nb, s, PW)
        gates = gates.reshape(nb, s, GATE_LANES)
        if lead_pad:
            p = jnp.pad(p, ((0, 0), (lead_pad, 0), (0, 0)))
            gates = jnp.pad(gates, ((0, 0), (lead_pad, 0), (0, 0)))
        hg, c_new, n_new, m_new = _scan(p, gates, bias, gout, c0, n0, m0, L=chunk, lead_pad=lead_pad)
        hg = hg[:, lead_pad:].reshape(nb * s, VW)
        x1 = _mmres(hg, w_out, x, tm=tm, tn=512)
        x2, cs0 = _ffn(x1, *ffn_w[0], conv0, gfin, tm=tm_ffn, sr=8, shift=1, tps=s // tm_ffn,
                       st_per_tile=False, final_norm=False)
        x3, ut = _pool(x2, g_mix1, wp, psc, pool0, tm=tm, shift=1, tps=s // tm, pos0=pos0,
                       tr=POOL_HALO, st_per_tile=False)
        y, cs1 = _ffn(x3, *ffn_w[1], conv1, gfin, tm=tm_ffn, sr=8, shift=1, tps=s // tm_ffn,
                      st_per_tile=False, final_norm=True)
        cs0 = cs0.reshape(nb, s // tm_ffn, 8, D_FF)[:, -1]
        cs1 = cs1.reshape(nb, s // tm_ffn, 8, D_FF)[:, -1]
        ut = ut.reshape(nb, s // tm, POOL_HALO, D)[:, -1]
        return y, (c_new, n_new, m_new, cs0, ut, cs1)

    zero_state = (jnp.zeros((H, DK, DV), F32), jnp.zeros((8, DK), F32), jnp.zeros((8, GATE_LANES), F32),
                  jnp.zeros((8, D_FF), F32), jnp.zeros((POOL_HALO, D), F32), jnp.zeros((8, D_FF), F32))
    _, (c_m, n_m, m_m, cs0_m, ut_m, cs1_m) = long_stream(
        meta_tokens, 1, N_META, zero_state, tm=N_META, chunk=128, lead_pad=128 - N_META, pos0=0)

    y_p, (c_p, n_p, m_p, cs0_p, ut_p, cs1_p) = long_stream(
        x_prompt.reshape(bsz * seq, D), bsz, seq, (c_m[0], n_m[0], m_m[0], cs0_m[0], ut_m[0], cs1_m[0]),
        tm=1024, chunk=SCAN_CHUNK, lead_pad=0, pos0=N_META)
    y_prompt = y_p.reshape(bsz, seq, D)
    C_p = c_p[None]
    n_p = n_p[:, :H][None]
    m_p = m_p[:, :H, 0][None]
    pool_p = ut_p[:, 1:][None]
    conv_p = jnp.stack([cs0_p[:, 6:], cs1_p[:, 6:]])

    xs = x_sample.reshape(nseq * t_dec, D)
    p_s, gates_s = _proj(xs, g_mix0, w_main, w_gate, tm=nseq * t_dec, tn=512, out_dtype=F32)
    mtok = _pad_cols(jnp.repeat(state_mlstm_m[0], t_dec, axis=0), GATE_LANES)
    n_hm = jnp.swapaxes(state_mlstm_n[0], 0, 1)
    hg_s, C_s, n_s_hm, m_s_hm = _scan_s(p_s, gates_s, bias, gout, mtok, state_mlstm_C[0], n_hm, T=t_dec)
    tm_s = SAMPLE_TILE_SEQS * t_dec
    hg_t = _to_time_major(hg_s.reshape(nseq, t_dec, VW), t_dec)
    x_t = _to_time_major(x_sample, t_dec)
    x1 = _mmres(hg_t, w_out, x_t, tm=tm_s, tn=512)

    def conv_state_in(cs):
        return _to_time_major(cs, 2)

    def conv_state_out(cs):
        return _from_time_major(cs, nseq, 2)

    sr_s = 2 * SAMPLE_TILE_SEQS
    x2, cs0_s = _ffn(x1, *ffn_w[0], conv_state_in(state_ffn_conv[0]), gfin, tm=tm_s, sr=sr_s,
                     shift=SAMPLE_TILE_SEQS, tps=1, st_per_tile=True, final_norm=False)
    pool_in = jnp.pad(state_pool[0], ((0, 0), (1, 0), (0, 0)))
    x3, ut_s = _pool(x2, g_mix1, wp, psc, _to_time_major(pool_in, POOL_HALO), tm=tm_s,
                     shift=SAMPLE_TILE_SEQS, tps=1, pos0=PAST_LEN, tr=tm_s, st_per_tile=True)
    y_s, cs1_s = _ffn(x3, *ffn_w[1], conv_state_in(state_ffn_conv[1]), gfin, tm=tm_s, sr=sr_s,
                      shift=SAMPLE_TILE_SEQS, tps=1, st_per_tile=True, final_norm=True)
    y_sample = _from_time_major(y_s, nseq, t_dec)
    n_s = jnp.swapaxes(n_s_hm, 0, 1)[None]
    m_s = jnp.swapaxes(m_s_hm[:, :, 0], 0, 1)[None]
    pool_s = jnp.concatenate([state_pool[0][:, t_dec:], _from_time_major(ut_s, nseq, t_dec)], axis=1)[None]
    conv_s = jnp.stack([conv_state_out(cs0_s), conv_state_out(cs1_s)])

    return (y_prompt, y_sample, C_p, n_p, m_p, pool_p, conv_p,
            C_s[None], n_s, m_s, pool_s, conv_s)
```

```python
import functools

import jax
import jax.numpy as jnp
from jax import lax
from jax.experimental import pallas as pl
from jax.experimental.pallas import tpu as pltpu

F32 = jnp.float32
BF16 = jnp.bfloat16

EPS = 1e-6
D = 2048
H = 4
DK = 256
DV = 512
QKW = H * DK
VW = H * DV
PW = 2 * QKW + 2 * VW
GATE_LANES = 128
SCALE = DK ** -0.5
POOL_WINDOWS = (2, 4, 8, 16)
PG = D // len(POOL_WINDOWS)
POOL_HALO = 16
D_FF = 5504
TF = 256
NF = -(-D_FF // TF)
LANES = 128
FF_LANE_BLOCKS = D_FF // LANES
N_META = 16
PAST_LEN = 16384
SCAN_CHUNK = 256
SAMPLE_GROUP = 16
SAMPLE_TILE_SEQS = 128
VMEM_LIMIT = 60 * 1024 * 1024


def _params(n_axes):
    return pltpu.CompilerParams(dimension_semantics=("arbitrary",) * n_axes,
                                vmem_limit_bytes=VMEM_LIMIT)


def _rms(x, g):
    return x * lax.rsqrt(jnp.mean(x * x, axis=-1, keepdims=True) + EPS) * g


def _log_sigmoid(x):
    return jnp.minimum(x, 0.0) - jnp.log(1.0 + jnp.exp(-jnp.abs(x)))


def _proj_kernel(x_ref, g_ref, w_ref, wg_ref, p_ref, gate_ref, *rest, emit_bf16):
    u_scr = rest[-1]

    @pl.when(pl.program_id(1) == 0)
    def _():
        ub = _rms(x_ref[...], g_ref[...]).astype(BF16)
        u_scr[...] = ub
        gate_ref[...] = jnp.dot(ub, wg_ref[...], preferred_element_type=F32)

    w = w_ref[...].astype(BF16)
    if emit_bf16:
        rest[0][...] = w
    p_ref[...] = jnp.dot(u_scr[...], w, preferred_element_type=F32).astype(p_ref.dtype)


def _proj(x, gamma, w, wg, *, tm, tn, out_dtype):
    m = x.shape[0]
    emit = w.dtype != BF16
    w_spec = (pl.BlockSpec((None, D, tn), lambda i, j: (0, 0, j)) if emit
              else pl.BlockSpec((D, tn), lambda i, j: (0, j)))
    out_specs = [pl.BlockSpec((tm, tn), lambda i, j: (i, j)),
                 pl.BlockSpec((tm, GATE_LANES), lambda i, j: (i, 0))]
    out_shape = [jax.ShapeDtypeStruct((m, PW), out_dtype),
                 jax.ShapeDtypeStruct((m, GATE_LANES), F32)]
    if emit:
        out_specs.append(pl.BlockSpec((D, tn), lambda i, j: (0, j)))
        out_shape.append(jax.ShapeDtypeStruct((D, PW), BF16))
    return pl.pallas_call(
        functools.partial(_proj_kernel, emit_bf16=emit),
        grid=(m // tm, PW // tn),
        in_specs=[pl.BlockSpec((tm, D), lambda i, j: (i, 0)),
                  pl.BlockSpec((1, D), lambda i, j: (0, 0)),
                  w_spec,
                  pl.BlockSpec((D, GATE_LANES), lambda i, j: (0, 0))],
        out_specs=out_specs,
        out_shape=out_shape,
        scratch_shapes=[pltpu.VMEM((tm, D), BF16)],
        compiler_params=_params(2),
        name="proj_cast" if emit else "proj",
    )(x, gamma, w, wg)


def _head_output(num, den, m_t, gout, o):
    hv = num / jnp.maximum(jnp.abs(den), jnp.exp(-m_t))
    hv = hv * lax.rsqrt(jnp.mean(hv * hv, axis=1, keepdims=True) + EPS)
    return hv * gout * jax.nn.sigmoid(o.astype(F32))


def _scan_kernel(q_ref, k_ref, v_ref, o_ref, gt_ref, bias_ref, gout_ref, c0_ref, n0_ref, m0_ref,
                 hg_ref, cout_ref, nout_ref, mout_ref, c_scr, n_scr, m_scr, *, L, lead_pad, nc):
    c = pl.program_id(1)

    @pl.when(c == 0)
    def _():
        c_scr[...] = c0_ref[...]
        n_scr[...] = n0_ref[...]
        m_scr[...] = m0_ref[...]

    gates = gt_ref[...] + bias_ref[...]
    gates_t = gates.T
    row = lax.broadcasted_iota(jnp.int32, (L, L), 0)
    col = lax.broadcasted_iota(jnp.int32, (L, L), 1)
    causal = row >= col
    causal_t = row <= col
    if lead_pad:
        live_col = lax.broadcasted_iota(jnp.int32, (L, 1), 0) >= lead_pad
        live_row = lax.broadcasted_iota(jnp.int32, (1, L), 1) >= lead_pad

    for hh in range(H):
        ig_col = gates[:, hh:hh + 1]
        ig_row = gates_t[hh:hh + 1, :]
        lf_col = _log_sigmoid(gates[:, H + hh:H + hh + 1])
        lf_row = _log_sigmoid(gates_t[H + hh:H + hh + 1, :])
        if lead_pad:
            ig_col = jnp.where(live_col, ig_col, -jnp.inf)
            ig_row = jnp.where(live_row, ig_row, -jnp.inf)
            lf_col = jnp.where(live_col, lf_col, 0.0)
            lf_row = jnp.where(live_row, lf_row, 0.0)
        b_col = jnp.sum(jnp.where(causal, lf_row, 0.0), axis=1, keepdims=True)
        b_row = jnp.sum(jnp.where(causal_t, lf_col, 0.0), axis=0, keepdims=True)
        m_prev = m_scr[hh:hh + 1, 0:1]
        d = jnp.where(causal, b_col - b_row + ig_row, -jnp.inf)
        inter = b_col + m_prev
        m_t = jnp.maximum(inter, jnp.max(d, axis=1, keepdims=True))
        w_inter = jnp.exp(inter - m_t) * SCALE

        q = q_ref[:, hh * DK:(hh + 1) * DK]
        k = k_ref[:, hh * DK:(hh + 1) * DK]
        v = v_ref[:, hh * DV:(hh + 1) * DV]
        qk = lax.dot_general(q, k, (((1,), (1,)), ((), ())), preferred_element_type=F32)
        s = qk * (jnp.exp(d - m_t) * SCALE)
        cmat = c_scr[hh]
        nvec = n_scr[hh:hh + 1, :]
        num = w_inter * jnp.dot(q, cmat.astype(BF16), preferred_element_type=F32) \
            + jnp.dot(s.astype(BF16), v, preferred_element_type=F32)
        den = w_inter * jnp.sum(q.astype(F32) * nvec, axis=1, keepdims=True) \
            + jnp.sum(s, axis=1, keepdims=True)
        hout = _head_output(num, den, m_t, gout_ref[:, hh * DV:(hh + 1) * DV],
                            o_ref[:, hh * DV:(hh + 1) * DV])
        hg_ref[:, hh * DV:(hh + 1) * DV] = hout.astype(hg_ref.dtype)

        m_new = m_t[L - 1:L, :]
        b_last = b_col[L - 1:L, :]
        decay = jnp.exp(b_last + m_prev - m_new)
        wk = jnp.exp(b_last - b_col + ig_col - m_new) * k.astype(F32)
        c_scr[hh] = decay * cmat + jnp.dot(wk.T.astype(BF16), v, preferred_element_type=F32)
        n_scr[hh:hh + 1, :] = decay * nvec + jnp.sum(wk, axis=0, keepdims=True)
        m_scr[hh:hh + 1, :] = jnp.broadcast_to(m_new, (1, GATE_LANES))

    @pl.when(c == nc - 1)
    def _():
        cout_ref[...] = c_scr[...]
        nout_ref[...] = n_scr[...]
        mout_ref[...] = m_scr[...]


def _scan(p, gates, bias, gout, c0, n0, m0, *, L, lead_pad=0):
    b, s, _ = p.shape
    nc = s // L
    kern = functools.partial(_scan_kernel, L=L, lead_pad=lead_pad, nc=nc)
    return pl.pallas_call(
        kern,
        grid=(b, nc),
        in_specs=[pl.BlockSpec((None, L, QKW), lambda i, c: (i, c, 0)),
                  pl.BlockSpec((None, L, QKW), lambda i, c: (i, c, 1)),
                  pl.BlockSpec((None, L, VW), lambda i, c: (i, c, 1)),
                  pl.BlockSpec((None, L, VW), lambda i, c: (i, c, 2)),
                  pl.BlockSpec((None, L, GATE_LANES), lambda i, c: (i, c, 0)),
                  pl.BlockSpec((1, GATE_LANES), lambda i, c: (0, 0)),
                  pl.BlockSpec((1, VW), lambda i, c: (0, 0)),
                  pl.BlockSpec((H, DK, DV), lambda i, c: (0, 0, 0)),
                  pl.BlockSpec((8, DK), lambda i, c: (0, 0)),
                  pl.BlockSpec((8, GATE_LANES), lambda i, c: (0, 0))],
        out_specs=[pl.BlockSpec((None, L, VW), lambda i, c: (i, c, 0)),
                   pl.BlockSpec((None, H, DK, DV), lambda i, c: (i, 0, 0, 0)),
                   pl.BlockSpec((None, 8, DK), lambda i, c: (i, 0, 0)),
                   pl.BlockSpec((None, 8, GATE_LANES), lambda i, c: (i, 0, 0))],
        out_shape=[jax.ShapeDtypeStruct((b, s, VW), BF16),
                   jax.ShapeDtypeStruct((b, H, DK, DV), F32),
                   jax.ShapeDtypeStruct((b, 8, DK), F32),
                   jax.ShapeDtypeStruct((b, 8, GATE_LANES), F32)],
        scratch_shapes=[pltpu.VMEM((H, DK, DV), F32),
                        pltpu.VMEM((8, DK), F32),
                        pltpu.VMEM((8, GATE_LANES), F32)],
        compiler_params=_params(2),
        name="scan",
    )(p, p, p, p, gates, bias, gout, c0, n0, m0)


def _scan_s_kernel(q_ref, k_ref, v_ref, o_ref, gt_ref, bias_ref, gout_ref, mtok_ref, c_ref, n_ref,
                   hg_ref, cout_ref, nout_ref, mout_ref, qc_scr, ntok_scr, *, T, NB):
    hh = pl.program_id(1)
    LT = NB * T
    gates = gt_ref[...] + bias_ref[...]
    gates_t = gates.T
    lane = lax.broadcasted_iota(jnp.int32, (LT, GATE_LANES), 1)
    sub = lax.broadcasted_iota(jnp.int32, (GATE_LANES, LT), 0)

    def pick_col(a, idx):
        return jnp.sum(jnp.where(lane == idx, a, 0.0), axis=1, keepdims=True)

    def pick_row(a, idx):
        return jnp.sum(jnp.where(sub == idx, a, 0.0), axis=0, keepdims=True)

    ig_col = pick_col(gates, hh)
    ig_row = pick_row(gates_t, hh)
    lf_col = _log_sigmoid(pick_col(gates, hh + H))
    lf_row = _log_sigmoid(pick_row(gates_t, hh + H))
    m_prev = pick_col(mtok_ref[...], hh)

    row = lax.broadcasted_iota(jnp.int32, (LT, LT), 0)
    col = lax.broadcasted_iota(jnp.int32, (LT, LT), 1)
    same = (row // T) == (col // T)
    causal = jnp.logical_and(same, row >= col)
    causal_t = jnp.logical_and(same, row <= col)
    b_col = jnp.sum(jnp.where(causal, lf_row, 0.0), axis=1, keepdims=True)
    b_row = jnp.sum(jnp.where(causal_t, lf_col, 0.0), axis=0, keepdims=True)
    b_end = jnp.sum(jnp.where(same, lf_row, 0.0), axis=1, keepdims=True)
    d = jnp.where(causal, b_col - b_row + ig_row, -jnp.inf)
    inter = b_col + m_prev
    m_t = jnp.maximum(inter, jnp.max(d, axis=1, keepdims=True))
    d_end = jnp.where(same, b_end - b_row + ig_row, -jnp.inf)
    m_new = jnp.maximum(b_end + m_prev, jnp.max(d_end, axis=1, keepdims=True))
    w_inter = jnp.exp(inter - m_t) * SCALE

    q32 = q_ref[...]
    k32 = k_ref[...]
    q = q32.astype(BF16)
    v = v_ref[...].astype(BF16)
    qk = lax.dot_general(q, k32.astype(BF16), (((1,), (1,)), ((), ())), preferred_element_type=F32)
    s = qk * (jnp.exp(d - m_t) * SCALE)
    num_intra = jnp.dot(s.astype(BF16), v, preferred_element_type=F32)
    den_intra = jnp.sum(s, axis=1, keepdims=True)

    decay = jnp.exp(b_end + m_prev - m_new)
    wk = jnp.exp(b_end - b_col + ig_col - m_new) * k32
    wk_t = wk.T
    col_seq = lax.broadcasted_iota(jnp.int32, (DK, LT), 1) // T

    for bb in range(NB):
        r0 = bb * T
        cmat = c_ref[bb]
        nvec = n_ref[bb:bb + 1, :]
        qc_scr[r0:r0 + T, :] = jnp.dot(q32[r0:r0 + T, :].astype(BF16), cmat.astype(BF16),
                                       preferred_element_type=F32)
        ntok_scr[r0:r0 + T, :] = jnp.broadcast_to(nvec, (T, DK))
        upd = jnp.dot(jnp.where(col_seq == bb, wk_t, 0.0).astype(BF16), v, preferred_element_type=F32)
        dec = decay[r0:r0 + 1, :]
        cout_ref[bb] = dec * cmat + upd
        nout_ref[bb:bb + 1, :] = dec * nvec + jnp.sum(wk[r0:r0 + T, :], axis=0, keepdims=True)
        mout_ref[bb:bb + 1, :] = jnp.broadcast_to(m_new[r0:r0 + 1, :], (1, GATE_LANES))

    num = w_inter * qc_scr[...] + num_intra
    den = w_inter * jnp.sum(q32 * ntok_scr[...], axis=1, keepdims=True) + den_intra
    hg_ref[...] = _head_output(num, den, m_t, gout_ref[...], o_ref[...]).astype(hg_ref.dtype)


def _scan_s(p, gates, bias, gout, mtok, c, n_hm, *, T):
    nseq = c.shape[0]
    nb = SAMPLE_GROUP
    lt = nb * T
    kern = functools.partial(_scan_s_kernel, T=T, NB=nb)
    return pl.pallas_call(
        kern,
        grid=(nseq // nb, H),
        in_specs=[pl.BlockSpec((lt, DK), lambda g, h: (g, h)),
                  pl.BlockSpec((lt, DK), lambda g, h: (g, H + h)),
                  pl.BlockSpec((lt, DV), lambda g, h: (g, H + h)),
                  pl.BlockSpec((lt, DV), lambda g, h: (g, 2 * H + h)),
                  pl.BlockSpec((lt, GATE_LANES), lambda g, h: (g, 0)),
                  pl.BlockSpec((1, GATE_LANES), lambda g, h: (0, 0)),
                  pl.BlockSpec((1, DV), lambda g, h: (0, h)),
                  pl.BlockSpec((lt, GATE_LANES), lambda g, h: (g, 0)),
                  pl.BlockSpec((nb, None, DK, DV), lambda g, h: (g, h, 0, 0)),
                  pl.BlockSpec((None, nb, DK), lambda g, h: (h, g, 0))],
        out_specs=[pl.BlockSpec((lt, DV), lambda g, h: (g, h)),
                   pl.BlockSpec((nb, None, DK, DV), lambda g, h: (g, h, 0, 0)),
                   pl.BlockSpec((None, nb, DK), lambda g, h: (h, g, 0)),
                   pl.BlockSpec((None, nb, GATE_LANES), lambda g, h: (h, g, 0))],
        out_shape=[jax.ShapeDtypeStruct((nseq * T, VW), BF16),
                   jax.ShapeDtypeStruct((nseq, H, DK, DV), F32),
                   jax.ShapeDtypeStruct((H, nseq, DK), F32),
                   jax.ShapeDtypeStruct((H, nseq, GATE_LANES), F32)],
        scratch_shapes=[pltpu.VMEM((lt, DV), F32), pltpu.VMEM((lt, DK), F32)],
        compiler_params=_params(2),
        name="scan_sample",
    )(p, p, p, p, gates, bias, gout, mtok, c, n_hm)


def _mmres_cast_kernel(a_ref, w_ref, x_ref, o_ref, wc_ref):
    w = w_ref[...].astype(BF16)
    wc_ref[...] = w
    o_ref[...] = x_ref[...] + jnp.dot(a_ref[...], w, preferred_element_type=F32)


def _mmres_cast(a, w, x, *, tn):
    m, kdim = a.shape
    n = w.shape[2]
    return pl.pallas_call(
        _mmres_cast_kernel,
        grid=(n // tn,),
        in_specs=[pl.BlockSpec((m, kdim), lambda j: (0, 0)),
                  pl.BlockSpec((None, kdim, tn), lambda j: (0, 0, j)),
                  pl.BlockSpec((m, tn), lambda j: (0, j))],
        out_specs=[pl.BlockSpec((m, tn), lambda j: (0, j)),
                   pl.BlockSpec((kdim, tn), lambda j: (0, j))],
        out_shape=[jax.ShapeDtypeStruct((m, n), F32),
                   jax.ShapeDtypeStruct((kdim, n), BF16)],
        compiler_params=_params(1),
        name="out_proj_cast",
    )(a, w, x)


def _mmres_kernel(a_ref, w_ref, x_ref, o_ref):
    o_ref[...] = x_ref[...] + jnp.dot(a_ref[...], w_ref[...], preferred_element_type=F32)


def _mmres(a, w, x, *, tm):
    m, kdim = a.shape
    n = w.shape[1]
    return pl.pallas_call(
        _mmres_kernel,
        grid=(m // tm,),
        in_specs=[pl.BlockSpec((tm, kdim), lambda i: (i, 0)),
                  pl.BlockSpec((kdim, n), lambda i: (0, 0)),
                  pl.BlockSpec((tm, n), lambda i: (i, 0))],
        out_specs=pl.BlockSpec((tm, n), lambda i: (i, 0)),
        out_shape=jax.ShapeDtypeStruct((m, n), F32),
        compiler_params=_params(1),
        name="out_proj",
    )(a, w, x)


def _ffn_kernel(x_ref, gam_ref, wg_ref, wa0_ref, wa1_ref, cw_ref, cb_ref, wd_ref, st_ref, gfin_ref,
                o_ref, so_ref, wupc_ref, wdnc_ref,
                u_scr, gext_scr, a_scr, carry_scr, wup0_scr, wup1_scr, wdn0_scr, wdn1_scr,
                *, tm, sr, shift, tps, nsub, final_norm):
    i = pl.program_id(0)
    s = pl.program_id(1)
    wup_slots = (wup0_scr, wup1_scr)
    wdn_slots = (wdn0_scr, wdn1_scr)

    def cast_steps(slot):
        wup_scr = wup_slots[slot]
        wdn_scr = wdn_slots[slot]
        valid = D_FF - jnp.minimum(s, NF - 1) * TF

        def cast_gate():
            ok = lax.broadcasted_iota(jnp.int32, (1, TF), 1) < valid
            w = jnp.where(ok, wg_ref[...], 0.0).astype(BF16)
            wup_scr[:, 0:TF] = w
            wupc_ref[:, 0:TF] = w

        def cast_value():
            lane = lax.broadcasted_iota(jnp.int32, (1, LANES), 1)
            w0 = jnp.where(lane < valid, wa0_ref[...], 0.0).astype(BF16)
            w1 = jnp.where(lane + LANES < valid, wa1_ref[...], 0.0).astype(BF16)
            wup_scr[:, TF:TF + LANES] = w0
            wup_scr[:, TF + LANES:2 * TF] = w1
            wupc_ref[:, TF:TF + LANES] = w0
            wupc_ref[:, TF + LANES:2 * TF] = w1

        def cast_down(r0, rows):
            row_ok = r0 + lax.broadcasted_iota(jnp.int32, (rows, 1), 0) < valid
            w = jnp.where(row_ok, wd_ref[r0:r0 + rows, :], 0.0).astype(BF16)
            wdn_scr[r0:r0 + rows, :] = w
            wdnc_ref[r0:r0 + rows, :] = w

        half = TF // 2
        return [cast_gate, cast_value, functools.partial(cast_down, 0, half),
                functools.partial(cast_down, half, half)]

    def run_tile(slot, fillers=()):
        fillers = list(fillers)

        def fill():
            if fillers:
                fillers.pop(0)()

        wup_scr = wup_slots[slot]
        wdn_scr = wdn_slots[slot]
        f = s - 1
        col_ok = lax.broadcasted_iota(jnp.int32, (1, TF), 1) < D_FF - f * TF
        cw = cw_ref[...]
        cb = cb_ref[...]
        if tps == 1:
            gext_scr[0:sr, :] = st_ref[...]
        else:
            gext_scr[0:sr, :] = jnp.where((i % tps) == 0, st_ref[...], carry_scr[f])
        ts = tm // nsub
        for h in range(nsub):
            r0 = h * ts
            ga = jnp.dot(u_scr[r0:r0 + ts, :], wup_scr[...], preferred_element_type=F32)
            gext_scr[sr + r0:sr + r0 + ts, :] = ga[:, 0:TF]
            a_scr[r0:r0 + ts, :] = ga[:, TF:2 * TF]
            fill()
        for h in range(nsub):
            r0 = h * ts
            g = gext_scr[sr + r0:sr + r0 + ts, :]
            g_m2 = gext_scr[sr + r0 - 2 * shift:sr + r0 - 2 * shift + ts, :]
            g_m1 = gext_scr[sr + r0 - shift:sr + r0 - shift + ts, :]
            gc = cb + ((cw[0:1, :] * g_m2 + cw[1:2, :] * g_m1) + cw[2:3, :] * g)
            hmid = jnp.where(col_ok, (gc * jax.nn.sigmoid(gc)) * a_scr[r0:r0 + ts, :], 0.0)
            o_ref[r0:r0 + ts, :] += jnp.dot(hmid.astype(BF16), wdn_scr[...], preferred_element_type=F32)
            fill()
        while fillers:
            fill()
        g_tail = gext_scr[tm:tm + sr, :]
        so_ref[...] = g_tail
        if tps > 1:
            carry_scr[f] = g_tail

    @pl.when(s == 0)
    def _():
        x = x_ref[...]
        u_scr[...] = _rms(x, gam_ref[...]).astype(BF16)
        o_ref[...] = x
        for step in cast_steps(0):
            step()
        if tps > 1:
            @pl.when(i == 0)
            def _():
                carry_scr[...] = jnp.zeros_like(carry_scr)

    for parity in range(2):
        @pl.when(jnp.logical_and(s > 0, s % 2 == parity))
        def _(parity=parity):
            run_tile(1 - parity, cast_steps(parity))

    if final_norm:
        @pl.when(s == NF)
        def _():
            o_ref[...] = _rms(o_ref[...], gfin_ref[...])


def _ffn(x, layer, gamma, w_up, conv_w, conv_b, w_down, st, gfin, *, tm, sr, shift, tps, st_per_tile,
         final_norm):
    assert TF == 2 * LANES
    m = x.shape[0]
    nt = m // tm
    nsub = 2 if tm >= 64 else 1
    kern = functools.partial(_ffn_kernel, tm=tm, sr=sr, shift=shift, tps=tps, nsub=nsub,
                             final_norm=final_norm)
    last_a = 2 * FF_LANE_BLOCKS - 1

    def wt(s):
        return jnp.minimum(s, NF - 1)

    def ft(s):
        return jnp.maximum(s - 1, 0)

    st_map = (lambda i, s: (i, ft(s))) if st_per_tile else (lambda i, s: (0, ft(s)))
    return pl.pallas_call(
        kern,
        grid=(nt, NF + 1),
        in_specs=[pl.BlockSpec((tm, D), lambda i, s: (i, 0), pipeline_mode=pl.Buffered(1)),
                  pl.BlockSpec((1, D), lambda i, s: (0, 0)),
                  pl.BlockSpec((None, D, TF), lambda i, s: (layer, 0, wt(s))),
                  pl.BlockSpec((None, D, LANES), lambda i, s: (layer, 0, FF_LANE_BLOCKS + 2 * wt(s))),
                  pl.BlockSpec((None, D, LANES),
                               lambda i, s: (layer, 0, jnp.minimum(FF_LANE_BLOCKS + 2 * wt(s) + 1, last_a))),
                  pl.BlockSpec((None, 3, TF), lambda i, s: (layer, 0, ft(s))),
                  pl.BlockSpec((None, 1, TF), lambda i, s: (layer, 0, ft(s))),
                  pl.BlockSpec((None, TF, D), lambda i, s: (layer, wt(s), 0)),
                  pl.BlockSpec((sr, TF), st_map),
                  pl.BlockSpec((1, D), lambda i, s: (0, 0))],
        out_specs=[pl.BlockSpec((tm, D), lambda i, s: (i, 0)),
                   pl.BlockSpec((sr, TF), lambda i, s: (i, ft(s))),
                   pl.BlockSpec((D, 2 * TF), lambda i, s: (0, wt(s))),
                   pl.BlockSpec((TF, D), lambda i, s: (wt(s), 0))],
        out_shape=[jax.ShapeDtypeStruct((m, D), F32),
                   jax.ShapeDtypeStruct((nt * sr, D_FF), F32),
                   jax.ShapeDtypeStruct((D, NF * 2 * TF), BF16),
                   jax.ShapeDtypeStruct((NF * TF, D), BF16)],
        scratch_shapes=[pltpu.VMEM((tm, D), BF16),
                        pltpu.VMEM((sr + tm, TF), F32),
                        pltpu.VMEM((tm, TF), F32),
                        pltpu.VMEM((NF, sr, TF), F32),
                        pltpu.VMEM((D, 2 * TF), BF16),
                        pltpu.VMEM((D, 2 * TF), BF16),
                        pltpu.VMEM((TF, D), BF16),
                        pltpu.VMEM((TF, D), BF16)],
        compiler_params=_params(2),
        name="conv_ffn_cast",
    )(x, gamma, w_up, w_up, w_up, conv_w, conv_b, w_down, st, gfin)


def _ffn_bf16_kernel(x_ref, gam_ref, wup_ref, cw_ref, cb_ref, wdn_ref, st_ref, gfin_ref,
                     o_ref, so_ref, u_scr, gext_scr, a_scr, carry_scr,
                     *, tm, sr, shift, tps, nsub, final_norm):
    i = pl.program_id(0)
    f = pl.program_id(1)
    tf2 = 2 * TF

    @pl.when(f == 0)
    def _():
        x = x_ref[...]
        u_scr[...] = _rms(x, gam_ref[...]).astype(BF16)
        o_ref[...] = x
        if tps > 1:
            @pl.when(i == 0)
            def _():
                carry_scr[...] = jnp.zeros_like(carry_scr)

    col_ok = lax.broadcasted_iota(jnp.int32, (1, tf2), 1) < D_FF - f * tf2
    cw = cw_ref[...]
    cb = cb_ref[...]
    if tps == 1:
        gext_scr[0:sr, :] = st_ref[...]
    else:
        gext_scr[0:sr, :] = jnp.where((i % tps) == 0, st_ref[...], carry_scr[f])
    ts = tm // nsub
    for h in range(nsub):
        r0 = h * ts
        ga = jnp.dot(u_scr[r0:r0 + ts, :], wup_ref[...], preferred_element_type=F32)
        gext_scr[sr + r0:sr + r0 + ts, 0:TF] = ga[:, 0:TF]
        gext_scr[sr + r0:sr + r0 + ts, TF:tf2] = ga[:, 2 * TF:3 * TF]
        a_scr[r0:r0 + ts, 0:TF] = ga[:, TF:2 * TF]
        a_scr[r0:r0 + ts, TF:tf2] = ga[:, 3 * TF:4 * TF]
    for h in range(nsub):
        r0 = h * ts
        g = gext_scr[sr + r0:sr + r0 + ts, :]
        g_m2 = gext_scr[sr + r0 - 2 * shift:sr + r0 - 2 * shift + ts, :]
        g_m1 = gext_scr[sr + r0 - shift:sr + r0 - shift + ts, :]
        gc = cb + ((cw[0:1, :] * g_m2 + cw[1:2, :] * g_m1) + cw[2:3, :] * g)
        hmid = jnp.where(col_ok, (gc * jax.nn.sigmoid(gc)) * a_scr[r0:r0 + ts, :], 0.0)
        o_ref[r0:r0 + ts, :] += jnp.dot(hmid.astype(BF16), wdn_ref[...], preferred_element_type=F32)
    g_tail = gext_scr[tm:tm + sr, :]
    so_ref[...] = g_tail
    if tps > 1:
        carry_scr[f] = g_tail

    if final_norm:
        @pl.when(f == pl.num_programs(1) - 1)
        def _():
            o_ref[...] = _rms(o_ref[...], gfin_ref[...])


def _ffn_bf16(x, layer, gamma, wupc, conv_w, conv_b, wdnc, st, gfin, *, tm, sr, shift, tps, st_per_tile,
              final_norm):
    assert NF % 2 == 0
    m = x.shape[0]
    nt = m // tm
    tf2 = 2 * TF
    nf2 = NF // 2
    kern = functools.partial(_ffn_bf16_kernel, tm=tm, sr=sr, shift=shift, tps=tps, nsub=2,
                             final_norm=final_norm)
    st_map = (lambda i, f: (i, f)) if st_per_tile else (lambda i, f: (0, f))
    return pl.pallas_call(
        kern,
        grid=(nt, nf2),
        in_specs=[pl.BlockSpec((tm, D), lambda i, f: (i, 0), pipeline_mode=pl.Buffered(1)),
                  pl.BlockSpec((1, D), lambda i, f: (0, 0)),
                  pl.BlockSpec((D, 2 * tf2), lambda i, f: (0, f)),
                  pl.BlockSpec((None, 3, tf2), lambda i, f: (layer, 0, f)),
                  pl.BlockSpec((None, 1, tf2), lambda i, f: (layer, 0, f)),
                  pl.BlockSpec((tf2, D), lambda i, f: (f, 0)),
                  pl.BlockSpec((sr, tf2), st_map),
                  pl.BlockSpec((1, D), lambda i, f: (0, 0))],
        out_specs=[pl.BlockSpec((tm, D), lambda i, f: (i, 0)),
                   pl.BlockSpec((sr, tf2), lambda i, f: (i, f))],
        out_shape=[jax.ShapeDtypeStruct((m, D), F32),
                   jax.ShapeDtypeStruct((nt * sr, D_FF), F32)],
        scratch_shapes=[pltpu.VMEM((tm, D), BF16),
                        pltpu.VMEM((sr + tm, tf2), F32),
                        pltpu.VMEM((tm, tf2), F32),
                        pltpu.VMEM((nf2, sr, tf2), F32)],
        compiler_params=_params(2),
        name="conv_ffn",
    )(x, gamma, wupc, conv_w, conv_b, wdnc, st, gfin)


def _pool_kernel(x_ref, gam_ref, wp_ref, sc_ref, st_ref, o_ref, ut_ref, rinv_scr, uext_scr, carry_scr,
                 *, tm, shift, tps, pos0, tr):
    i = pl.program_id(0)
    grp = pl.program_id(1)
    hr = POOL_HALO * shift

    @pl.when(grp == 0)
    def _():
        x = x_ref[...]
        rinv_scr[...] = lax.rsqrt(jnp.mean(x * x, axis=1, keepdims=True) + EPS)

    for kk, w in enumerate(POOL_WINDOWS):
        @pl.when(grp == kk)
        def _(kk=kk, w=w):
            cs = slice(kk * PG, (kk + 1) * PG)
            xg = x_ref[:, cs]
            ug = xg * rinv_scr[...] * gam_ref[:, cs]
            if tps == 1:
                uext_scr[0:hr, :] = st_ref[...]
            else:
                seq_start = (i % tps) == 0

                @pl.when(seq_start)
                def _():
                    uext_scr[0:hr, :] = st_ref[...]

                @pl.when(jnp.logical_not(seq_start))
                def _():
                    uext_scr[0:hr, :] = carry_scr[kk]

            uext_scr[hr:hr + tm, :] = ug
            acc = ug
            for j in range(1, w):
                acc = acc + uext_scr[hr - j * shift:hr - j * shift + tm, :]
            if pos0 + 1 >= w:
                pooled = acc / float(w) - ug
            else:
                step = (i % tps) * (tm // shift) + lax.broadcasted_iota(jnp.int32, (tm, 1), 0) // shift
                cnt = jnp.minimum(w, pos0 + step + 1).astype(F32)
                pooled = acc / cnt - ug
            y = jnp.dot(pooled.astype(BF16), wp_ref[kk].astype(BF16), preferred_element_type=F32)
            o_ref[...] = xg + y * sc_ref[:, cs]
            ut_ref[...] = ug[tm - tr:tm, :]
            if tps > 1:
                carry_scr[kk] = uext_scr[tm:tm + hr, :]


def _pool(x, gamma, wp, sc, st, *, tm, shift, tps, pos0, tr, st_per_tile):
    m = x.shape[0]
    nt = m // tm
    hr = POOL_HALO * shift
    ng = len(POOL_WINDOWS)
    kern = functools.partial(_pool_kernel, tm=tm, shift=shift, tps=tps, pos0=pos0, tr=tr)
    st_map = (lambda i, g: (i, g)) if st_per_tile else (lambda i, g: (0, g))
    return pl.pallas_call(
        kern,
        grid=(nt, ng),
        in_specs=[pl.BlockSpec((tm, D), lambda i, g: (i, 0)),
                  pl.BlockSpec((1, D), lambda i, g: (0, 0)),
                  pl.BlockSpec((None, ng, PG, PG), lambda i, g: (0, 0, 0, 0)),
                  pl.BlockSpec((1, D), lambda i, g: (0, 0)),
                  pl.BlockSpec((hr, PG), st_map)],
        out_specs=[pl.BlockSpec((tm, PG), lambda i, g: (i, g)),
                   pl.BlockSpec((tr, PG), lambda i, g: (i, g))],
        out_shape=[jax.ShapeDtypeStruct((m, D), F32),
                   jax.ShapeDtypeStruct((nt * tr, D), F32)],
        scratch_shapes=[pltpu.VMEM((tm, 1), F32),
                        pltpu.VMEM((hr + tm, PG), F32),
                        pltpu.VMEM((ng, hr, PG), F32)],
        compiler_params=_params(2),
        name="pool_mixer",
    )(x, gamma, wp, sc, st)


def _pad_cols(a, n):
    return jnp.pad(a, ((0, 0), (0, n - a.shape[1])))


def _to_time_major(a, t):
    nseq = a.shape[0]
    a = a.reshape((nseq // SAMPLE_TILE_SEQS, SAMPLE_TILE_SEQS, t) + a.shape[2:])
    a = jnp.swapaxes(a, 1, 2)
    return a.reshape((nseq * t,) + a.shape[3:])


def _from_time_major(a, nseq, t):
    a = a.reshape((nseq // SAMPLE_TILE_SEQS, t, SAMPLE_TILE_SEQS) + a.shape[1:])
    a = jnp.swapaxes(a, 1, 2)
    return a.reshape((nseq, t) + a.shape[3:])


def kernel(x_prompt, x_sample, state_mlstm_C, state_mlstm_n, state_mlstm_m, state_pool, state_ffn_conv,
           meta_tokens, norm_mix, norm_ffn, norm_final, w_mlstm_in, b_mlstm_gate, g_mlstm_out, w_mlstm_out,
           w_pool, pool_scale, w_up, conv_w, conv_b, w_down):
    bsz, seq, _ = x_prompt.shape
    nseq, t_dec, _ = x_sample.shape

    w_main = w_mlstm_in
    w_gate = _pad_cols(w_mlstm_in[0][:, PW:], GATE_LANES).astype(BF16)
    bias = _pad_cols(b_mlstm_gate[0][None, :], GATE_LANES)
    w_out = w_mlstm_out
    gout = g_mlstm_out[0][None, :]
    wp = w_pool
    psc = pool_scale[0][None, :]
    gfin = norm_final[None, :]
    conv_b3 = conv_b[:, None, :]
    ffn_w = [(layer, norm_ffn[layer][None, :], w_up, conv_w, conv_b3, w_down) for layer in range(2)]
    g_mix0 = norm_mix[0][None, :]
    g_mix1 = norm_mix[1][None, :]

    ffn_cache = {}
    mix_cache = {}

    def proj(x, **kw):
        if "w_in" not in mix_cache:
            p, gates, mix_cache["w_in"] = _proj(x, g_mix0, w_main, w_gate, **kw)
            return p, gates
        return _proj(x, g_mix0, mix_cache["w_in"], w_gate, **kw)

    def out_proj(a, x, *, tm):
        if "w_out" not in mix_cache:
            y, mix_cache["w_out"] = _mmres_cast(a, w_out, x, tn=512)
            return y
        return _mmres(a, mix_cache["w_out"], x, tm=tm)

    def ffn(x, layer, st, **kw):
        lyr, gamma, wu, cwt, cbs, wd = ffn_w[layer]
        if layer not in ffn_cache:
            y, cs, wupc, wdnc = _ffn(x, lyr, gamma, wu, cwt, cbs, wd, st, gfin, **kw)
            ffn_cache[layer] = (wupc, wdnc)
            return y, cs
        wupc, wdnc = ffn_cache[layer]
        return _ffn_bf16(x, lyr, gamma, wupc, cwt, cbs, wdnc, st, gfin, **kw)

    def long_stream(x, nb, s, st, *, tm, chunk, lead_pad, pos0):
        c0, n0, m0, conv0, pool0, conv1 = st
        tm_ffn = tm
        p, gates = proj(x, tm=tm, tn=512, out_dtype=BF16)
        p = p.reshape(nb, s, PW)
        gates = gates.reshape(nb, s, GATE_LANES)
        if lead_pad:
            p = jnp.pad(p, ((0, 0), (lead_pad, 0), (0, 0)))
            gates = jnp.pad(gates, ((0, 0), (lead_pad, 0), (0, 0)))
        hg, c_new, n_new, m_new = _scan(p, gates, bias, gout, c0, n0, m0, L=chunk, lead_pad=lead_pad)
        hg = hg[:, lead_pad:].reshape(nb * s, VW)
        x1 = out_proj(hg, x, tm=min(tm, 512))
        x2, cs0 = ffn(x1, 0, conv0, tm=tm_ffn, sr=8, shift=1, tps=s // tm_ffn,
                      st_per_tile=False, final_norm=False)
        x3, ut = _pool(x2, g_mix1, wp, psc, pool0, tm=tm, shift=1, tps=s // tm, pos0=pos0,
                       tr=POOL_HALO, st_per_tile=False)
        y, cs1 = ffn(x3, 1, conv1, tm=tm_ffn, sr=8, shift=1, tps=s // tm_ffn,
                     st_per_tile=False, final_norm=True)
        cs0 = cs0.reshape(nb, s // tm_ffn, 8, D_FF)[:, -1]
        cs1 = cs1.reshape(nb, s // tm_ffn, 8, D_FF)[:, -1]
        ut = ut.reshape(nb, s // tm, POOL_HALO, D)[:, -1]
        return y, (c_new, n_new, m_new, cs0, ut, cs1)

    zero_state = (jnp.zeros((H, DK, DV), F32), jnp.zeros((8, DK), F32), jnp.zeros((8, GATE_LANES), F32),
                  jnp.zeros((8, D_FF), F32), jnp.zeros((POOL_HALO, D), F32), jnp.zeros((8, D_FF), F32))
    _, (c_m, n_m, m_m, cs0_m, ut_m, cs1_m) = long_stream(
        meta_tokens, 1, N_META, zero_state, tm=N_META, chunk=128, lead_pad=128 - N_META, pos0=0)

    y_p, (c_p, n_p, m_p, cs0_p, ut_p, cs1_p) = long_stream(
        x_prompt.reshape(bsz * seq, D), bsz, seq, (c_m[0], n_m[0], m_m[0], cs0_m[0], ut_m[0], cs1_m[0]),
        tm=1024, chunk=SCAN_CHUNK, lead_pad=0, pos0=N_META)
    y_prompt = y_p.reshape(bsz, seq, D)
    C_p = c_p[None]
    n_p = n_p[:, :H][None]
    m_p = m_p[:, :H, 0][None]
    pool_p = ut_p[:, 1:][None]
    conv_p = jnp.stack([cs0_p[:, 6:], cs1_p[:, 6:]])

    xs = x_sample.reshape(nseq * t_dec, D)
    p_s, gates_s = proj(xs, tm=nseq * t_dec, tn=512, out_dtype=F32)
    mtok = _pad_cols(jnp.repeat(state_mlstm_m[0], t_dec, axis=0), GATE_LANES)
    n_hm = jnp.swapaxes(state_mlstm_n[0], 0, 1)
    hg_s, C_s, n_s_hm, m_s_hm = _scan_s(p_s, gates_s, bias, gout, mtok, state_mlstm_C[0], n_hm, T=t_dec)
    tm_s = SAMPLE_TILE_SEQS * t_dec
    hg_t = _to_time_major(hg_s.reshape(nseq, t_dec, VW), t_dec)
    x_t = _to_time_major(x_sample, t_dec)
    x1 = out_proj(hg_t, x_t, tm=512)

    def conv_state_in(cs):
        return _to_time_major(cs, 2)

    def conv_state_out(cs):
        return _from_time_major(cs, nseq, 2)

    sr_s = 2 * SAMPLE_TILE_SEQS
    x2, cs0_s = ffn(x1, 0, conv_state_in(state_ffn_conv[0]), tm=tm_s, sr=sr_s,
                    shift=SAMPLE_TILE_SEQS, tps=1, st_per_tile=True, final_norm=False)
    pool_in = jnp.pad(state_pool[0], ((0, 0), (1, 0), (0, 0)))
    x3, ut_s = _pool(x2, g_mix1, wp, psc, _to_time_major(pool_in, POOL_HALO), tm=tm_s,
                     shift=SAMPLE_TILE_SEQS, tps=1, pos0=PAST_LEN, tr=tm_s, st_per_tile=True)
    y_s, cs1_s = ffn(x3, 1, conv_state_in(state_ffn_conv[1]), tm=tm_s, sr=sr_s,
                     shift=SAMPLE_TILE_SEQS, tps=1, st_per_tile=True, final_norm=True)
    y_sample = _from_time_major(y_s, nseq, t_dec)
    n_s = jnp.swapaxes(n_s_hm, 0, 1)[None]
    m_s = jnp.swapaxes(m_s_hm[:, :, 0], 0, 1)[None]
    pool_s = jnp.concatenate([state_pool[0][:, t_dec:], _from_time_major(ut_s, nseq, t_dec)], axis=1)[None]
    conv_s = jnp.stack([conv_state_out(cs0_s), conv_state_out(cs1_s)])

    return (y_prompt, y_sample, C_p, n_p, m_p, pool_p, conv_p,
            C_s[None], n_s, m_s, pool_s, conv_s)
```

```python
import functools

import jax
import jax.numpy as jnp
from jax import lax
from jax.experimental import pallas as pl
from jax.experimental.pallas import tpu as pltpu

F32 = jnp.float32
BF16 = jnp.bfloat16

EPS = 1e-6
D = 2048
H = 4
DK = 256
DV = 512
QKW = H * DK
VW = H * DV
PW = 2 * QKW + 2 * VW
GATE_LANES = 128
SCALE = DK ** -0.5
POOL_WINDOWS = (2, 4, 8, 16)
PG = D // len(POOL_WINDOWS)
POOL_HALO = 16
D_FF = 5504
TF = 256
NF = -(-D_FF // TF)
LANES = 128
FF_LANE_BLOCKS = D_FF // LANES
N_META = 16
PAST_LEN = 16384
SCAN_CHUNK = 256
SAMPLE_GROUP = 16
SAMPLE_TILE_SEQS = 128
VMEM_LIMIT = 60 * 1024 * 1024


def _params(n_axes):
    return pltpu.CompilerParams(dimension_semantics=("arbitrary",) * n_axes,
                                vmem_limit_bytes=VMEM_LIMIT)


def _rms(x, g):
    return x * lax.rsqrt(jnp.mean(x * x, axis=-1, keepdims=True) + EPS) * g


def _log_sigmoid(x):
    return jnp.minimum(x, 0.0) - jnp.log(1.0 + jnp.exp(-jnp.abs(x)))


def _proj_kernel(x_ref, g_ref, w_ref, wg_ref, p_ref, gate_ref, *rest, emit_bf16):
    u_scr = rest[-1]

    @pl.when(pl.program_id(1) == 0)
    def _():
        ub = _rms(x_ref[...], g_ref[...]).astype(BF16)
        u_scr[...] = ub
        gate_ref[...] = jnp.dot(ub, wg_ref[...], preferred_element_type=F32)

    w = w_ref[...].astype(BF16)
    if emit_bf16:
        rest[0][...] = w
    p_ref[...] = jnp.dot(u_scr[...], w, preferred_element_type=F32).astype(p_ref.dtype)


def _proj(x, gamma, w, wg, *, tm, tn, out_dtype):
    m = x.shape[0]
    emit = w.dtype != BF16
    w_spec = (pl.BlockSpec((None, D, tn), lambda i, j: (0, 0, j)) if emit
              else pl.BlockSpec((D, tn), lambda i, j: (0, j)))
    out_specs = [pl.BlockSpec((tm, tn), lambda i, j: (i, j)),
                 pl.BlockSpec((tm, GATE_LANES), lambda i, j: (i, 0))]
    out_shape = [jax.ShapeDtypeStruct((m, PW), out_dtype),
                 jax.ShapeDtypeStruct((m, GATE_LANES), F32)]
    if emit:
        out_specs.append(pl.BlockSpec((D, tn), lambda i, j: (0, j)))
        out_shape.append(jax.ShapeDtypeStruct((D, PW), BF16))
    return pl.pallas_call(
        functools.partial(_proj_kernel, emit_bf16=emit),
        grid=(m // tm, PW // tn),
        in_specs=[pl.BlockSpec((tm, D), lambda i, j: (i, 0)),
                  pl.BlockSpec((1, D), lambda i, j: (0, 0)),
                  w_spec,
                  pl.BlockSpec((D, GATE_LANES), lambda i, j: (0, 0))],
        out_specs=out_specs,
        out_shape=out_shape,
        scratch_shapes=[pltpu.VMEM((tm, D), BF16)],
        compiler_params=_params(2),
        name="proj_cast" if emit else "proj",
    )(x, gamma, w, wg)


def _head_output(num, den, m_t, gout, o):
    hv = num / jnp.maximum(jnp.abs(den), jnp.exp(-m_t))
    hv = hv * lax.rsqrt(jnp.mean(hv * hv, axis=1, keepdims=True) + EPS)
    return hv * gout * jax.nn.sigmoid(o.astype(F32))


def _scan_kernel(q_ref, k_ref, v_ref, o_ref, gt_ref, bias_ref, gout_ref, c0_ref, n0_ref, m0_ref,
                 hg_ref, cout_ref, nout_ref, mout_ref, c_scr, n_scr, m_scr, *, L, lead_pad, nc):
    c = pl.program_id(1)

    @pl.when(c == 0)
    def _():
        c_scr[...] = c0_ref[...]
        n_scr[...] = n0_ref[...]
        m_scr[...] = m0_ref[...]

    gates = gt_ref[...] + bias_ref[...]
    gates_t = gates.T
    row = lax.broadcasted_iota(jnp.int32, (L, L), 0)
    col = lax.broadcasted_iota(jnp.int32, (L, L), 1)
    causal = row >= col
    causal_t = row <= col
    if lead_pad:
        live_col = lax.broadcasted_iota(jnp.int32, (L, 1), 0) >= lead_pad
        live_row = lax.broadcasted_iota(jnp.int32, (1, L), 1) >= lead_pad

    for hh in range(H):
        ig_col = gates[:, hh:hh + 1]
        ig_row = gates_t[hh:hh + 1, :]
        lf_col = _log_sigmoid(gates[:, H + hh:H + hh + 1])
        lf_row = _log_sigmoid(gates_t[H + hh:H + hh + 1, :])
        if lead_pad:
            ig_col = jnp.where(live_col, ig_col, -jnp.inf)
            ig_row = jnp.where(live_row, ig_row, -jnp.inf)
            lf_col = jnp.where(live_col, lf_col, 0.0)
            lf_row = jnp.where(live_row, lf_row, 0.0)
        b_col = jnp.sum(jnp.where(causal, lf_row, 0.0), axis=1, keepdims=True)
        b_row = jnp.sum(jnp.where(causal_t, lf_col, 0.0), axis=0, keepdims=True)
        m_prev = m_scr[hh:hh + 1, 0:1]
        d = jnp.where(causal, b_col - b_row + ig_row, -jnp.inf)
        inter = b_col + m_prev
        m_t = jnp.maximum(inter, jnp.max(d, axis=1, keepdims=True))
        w_inter = jnp.exp(inter - m_t) * SCALE

        q = q_ref[:, hh * DK:(hh + 1) * DK]
        k = k_ref[:, hh * DK:(hh + 1) * DK]
        v = v_ref[:, hh * DV:(hh + 1) * DV]
        qk = lax.dot_general(q, k, (((1,), (1,)), ((), ())), preferred_element_type=F32)
        s = qk * (jnp.exp(d - m_t) * SCALE)
        cmat = c_scr[hh]
        nvec = n_scr[hh:hh + 1, :]
        num = w_inter * jnp.dot(q, cmat.astype(BF16), preferred_element_type=F32) \
            + jnp.dot(s.astype(BF16), v, preferred_element_type=F32)
        den = w_inter * jnp.sum(q.astype(F32) * nvec, axis=1, keepdims=True) \
            + jnp.sum(s, axis=1, keepdims=True)
        hout = _head_output(num, den, m_t, gout_ref[:, hh * DV:(hh + 1) * DV],
                            o_ref[:, hh * DV:(hh + 1) * DV])
        hg_ref[:, hh * DV:(hh + 1) * DV] = hout.astype(hg_ref.dtype)

        m_new = m_t[L - 1:L, :]
        b_last = b_col[L - 1:L, :]
        decay = jnp.exp(b_last + m_prev - m_new)
        wk = jnp.exp(b_last - b_col + ig_col - m_new) * k.astype(F32)
        c_scr[hh] = decay * cmat + jnp.dot(wk.T.astype(BF16), v, preferred_element_type=F32)
        n_scr[hh:hh + 1, :] = decay * nvec + jnp.sum(wk, axis=0, keepdims=True)
        m_scr[hh:hh + 1, :] = jnp.broadcast_to(m_new, (1, GATE_LANES))

    @pl.when(c == nc - 1)
    def _():
        cout_ref[...] = c_scr[...]
        nout_ref[...] = n_scr[...]
        mout_ref[...] = m_scr[...]


def _scan(p, gates, bias, gout, c0, n0, m0, *, L, lead_pad=0):
    b, s, _ = p.shape
    nc = s // L
    kern = functools.partial(_scan_kernel, L=L, lead_pad=lead_pad, nc=nc)
    return pl.pallas_call(
        kern,
        grid=(b, nc),
        in_specs=[pl.BlockSpec((None, L, QKW), lambda i, c: (i, c, 0)),
                  pl.BlockSpec((None, L, QKW), lambda i, c: (i, c, 1)),
                  pl.BlockSpec((None, L, VW), lambda i, c: (i, c, 1)),
                  pl.BlockSpec((None, L, VW), lambda i, c: (i, c, 2)),
                  pl.BlockSpec((None, L, GATE_LANES), lambda i, c: (i, c, 0)),
                  pl.BlockSpec((1, GATE_LANES), lambda i, c: (0, 0)),
                  pl.BlockSpec((1, VW), lambda i, c: (0, 0)),
                  pl.BlockSpec((H, DK, DV), lambda i, c: (0, 0, 0)),
                  pl.BlockSpec((8, DK), lambda i, c: (0, 0)),
                  pl.BlockSpec((8, GATE_LANES), lambda i, c: (0, 0))],
        out_specs=[pl.BlockSpec((None, L, VW), lambda i, c: (i, c, 0)),
                   pl.BlockSpec((None, H, DK, DV), lambda i, c: (i, 0, 0, 0)),
                   pl.BlockSpec((None, 8, DK), lambda i, c: (i, 0, 0)),
                   pl.BlockSpec((None, 8, GATE_LANES), lambda i, c: (i, 0, 0))],
        out_shape=[jax.ShapeDtypeStruct((b, s, VW), BF16),
                   jax.ShapeDtypeStruct((b, H, DK, DV), F32),
                   jax.ShapeDtypeStruct((b, 8, DK), F32),
                   jax.ShapeDtypeStruct((b, 8, GATE_LANES), F32)],
        scratch_shapes=[pltpu.VMEM((H, DK, DV), F32),
                        pltpu.VMEM((8, DK), F32),
                        pltpu.VMEM((8, GATE_LANES), F32)],
        compiler_params=_params(2),
        name="scan",
    )(p, p, p, p, gates, bias, gout, c0, n0, m0)


def _scan_s_kernel(q_ref, k_ref, v_ref, o_ref, gt_ref, bias_ref, gout_ref, mtok_ref, c_ref, n_ref,
                   hg_ref, cout_ref, nout_ref, mout_ref, qc_scr, ntok_scr, *, T, NB):
    hh = pl.program_id(1)
    LT = NB * T
    gates = gt_ref[...] + bias_ref[...]
    gates_t = gates.T
    lane = lax.broadcasted_iota(jnp.int32, (LT, GATE_LANES), 1)
    sub = lax.broadcasted_iota(jnp.int32, (GATE_LANES, LT), 0)

    def pick_col(a, idx):
        return jnp.sum(jnp.where(lane == idx, a, 0.0), axis=1, keepdims=True)

    def pick_row(a, idx):
        return jnp.sum(jnp.where(sub == idx, a, 0.0), axis=0, keepdims=True)

    ig_col = pick_col(gates, hh)
    ig_row = pick_row(gates_t, hh)
    lf_col = _log_sigmoid(pick_col(gates, hh + H))
    lf_row = _log_sigmoid(pick_row(gates_t, hh + H))
    m_prev = pick_col(mtok_ref[...], hh)

    row = lax.broadcasted_iota(jnp.int32, (LT, LT), 0)
    col = lax.broadcasted_iota(jnp.int32, (LT, LT), 1)
    same = (row // T) == (col // T)
    causal = jnp.logical_and(same, row >= col)
    causal_t = jnp.logical_and(same, row <= col)
    b_col = jnp.sum(jnp.where(causal, lf_row, 0.0), axis=1, keepdims=True)
    b_row = jnp.sum(jnp.where(causal_t, lf_col, 0.0), axis=0, keepdims=True)
    b_end = jnp.sum(jnp.where(same, lf_row, 0.0), axis=1, keepdims=True)
    d = jnp.where(causal, b_col - b_row + ig_row, -jnp.inf)
    inter = b_col + m_prev
    m_t = jnp.maximum(inter, jnp.max(d, axis=1, keepdims=True))
    d_end = jnp.where(same, b_end - b_row + ig_row, -jnp.inf)
    m_new = jnp.maximum(b_end + m_prev, jnp.max(d_end, axis=1, keepdims=True))
    w_inter = jnp.exp(inter - m_t) * SCALE

    q32 = q_ref[...]
    k32 = k_ref[...]
    q = q32.astype(BF16)
    v = v_ref[...].astype(BF16)
    qk = lax.dot_general(q, k32.astype(BF16), (((1,), (1,)), ((), ())), preferred_element_type=F32)
    s = qk * (jnp.exp(d - m_t) * SCALE)
    num_intra = jnp.dot(s.astype(BF16), v, preferred_element_type=F32)
    den_intra = jnp.sum(s, axis=1, keepdims=True)

    decay = jnp.exp(b_end + m_prev - m_new)
    wk = jnp.exp(b_end - b_col + ig_col - m_new) * k32
    wk_t = wk.T
    col_seq = lax.broadcasted_iota(jnp.int32, (DK, LT), 1) // T

    for bb in range(NB):
        r0 = bb * T
        cmat = c_ref[bb]
        nvec = n_ref[bb:bb + 1, :]
        qc_scr[r0:r0 + T, :] = jnp.dot(q32[r0:r0 + T, :].astype(BF16), cmat.astype(BF16),
                                       preferred_element_type=F32)
        ntok_scr[r0:r0 + T, :] = jnp.broadcast_to(nvec, (T, DK))
        upd = jnp.dot(jnp.where(col_seq == bb, wk_t, 0.0).astype(BF16), v, preferred_element_type=F32)
        dec = decay[r0:r0 + 1, :]
        cout_ref[bb] = dec * cmat + upd
        nout_ref[bb:bb + 1, :] = dec * nvec + jnp.sum(wk[r0:r0 + T, :], axis=0, keepdims=True)
        mout_ref[bb:bb + 1, :] = jnp.broadcast_to(m_new[r0:r0 + 1, :], (1, GATE_LANES))

    num = w_inter * qc_scr[...] + num_intra
    den = w_inter * jnp.sum(q32 * ntok_scr[...], axis=1, keepdims=True) + den_intra
    hg_ref[...] = _head_output(num, den, m_t, gout_ref[...], o_ref[...]).astype(hg_ref.dtype)


def _scan_s(p, gates, bias, gout, mtok, c, n_hm, *, T):
    nseq = c.shape[0]
    nb = SAMPLE_GROUP
    lt = nb * T
    kern = functools.partial(_scan_s_kernel, T=T, NB=nb)
    return pl.pallas_call(
        kern,
        grid=(nseq // nb, H),
        in_specs=[pl.BlockSpec((lt, DK), lambda g, h: (g, h)),
                  pl.BlockSpec((lt, DK), lambda g, h: (g, H + h)),
                  pl.BlockSpec((lt, DV), lambda g, h: (g, H + h)),
                  pl.BlockSpec((lt, DV), lambda g, h: (g, 2 * H + h)),
                  pl.BlockSpec((lt, GATE_LANES), lambda g, h: (g, 0)),
                  pl.BlockSpec((1, GATE_LANES), lambda g, h: (0, 0)),
                  pl.BlockSpec((1, DV), lambda g, h: (0, h)),
                  pl.BlockSpec((lt, GATE_LANES), lambda g, h: (g, 0)),
                  pl.BlockSpec((nb, None, DK, DV), lambda g, h: (g, h, 0, 0)),
                  pl.BlockSpec((None, nb, DK), lambda g, h: (h, g, 0))],
        out_specs=[pl.BlockSpec((lt, DV), lambda g, h: (g, h)),
                   pl.BlockSpec((nb, None, DK, DV), lambda g, h: (g, h, 0, 0)),
                   pl.BlockSpec((None, nb, DK), lambda g, h: (h, g, 0)),
                   pl.BlockSpec((None, nb, GATE_LANES), lambda g, h: (h, g, 0))],
        out_shape=[jax.ShapeDtypeStruct((nseq * T, VW), BF16),
                   jax.ShapeDtypeStruct((nseq, H, DK, DV), F32),
                   jax.ShapeDtypeStruct((H, nseq, DK), F32),
                   jax.ShapeDtypeStruct((H, nseq, GATE_LANES), F32)],
        scratch_shapes=[pltpu.VMEM((lt, DV), F32), pltpu.VMEM((lt, DK), F32)],
        compiler_params=_params(2),
        name="scan_sample",
    )(p, p, p, p, gates, bias, gout, mtok, c, n_hm)


def _mmres_cast_kernel(a_ref, w_ref, x_ref, o_ref, wc_ref):
    w = w_ref[...].astype(BF16)
    wc_ref[...] = w
    o_ref[...] = x_ref[...] + jnp.dot(a_ref[...], w, preferred_element_type=F32)


def _mmres_cast(a, w, x, *, tn):
    m, kdim = a.shape
    n = w.shape[2]
    return pl.pallas_call(
        _mmres_cast_kernel,
        grid=(n // tn,),
        in_specs=[pl.BlockSpec((m, kdim), lambda j: (0, 0)),
                  pl.BlockSpec((None, kdim, tn), lambda j: (0, 0, j)),
                  pl.BlockSpec((m, tn), lambda j: (0, j))],
        out_specs=[pl.BlockSpec((m, tn), lambda j: (0, j)),
                   pl.BlockSpec((kdim, tn), lambda j: (0, j))],
        out_shape=[jax.ShapeDtypeStruct((m, n), F32),
                   jax.ShapeDtypeStruct((kdim, n), BF16)],
        compiler_params=_params(1),
        name="out_proj_cast",
    )(a, w, x)


def _mmres_kernel(a_ref, w_ref, x_ref, o_ref):
    o_ref[...] = x_ref[...] + jnp.dot(a_ref[...], w_ref[...], preferred_element_type=F32)


def _mmres(a, w, x, *, tm):
    m, kdim = a.shape
    n = w.shape[1]
    return pl.pallas_call(
        _mmres_kernel,
        grid=(m // tm,),
        in_specs=[pl.BlockSpec((tm, kdim), lambda i: (i, 0)),
                  pl.BlockSpec((kdim, n), lambda i: (0, 0)),
                  pl.BlockSpec((tm, n), lambda i: (i, 0))],
        out_specs=pl.BlockSpec((tm, n), lambda i: (i, 0)),
        out_shape=jax.ShapeDtypeStruct((m, n), F32),
        compiler_params=_params(1),
        name="out_proj",
    )(a, w, x)


def _ffn_kernel(x_ref, gam_ref, wg_ref, wa0_ref, wa1_ref, cw_ref, cb_ref, wd_ref, st_ref, gfin_ref,
                o_ref, so_ref, wupc_ref, wdnc_ref,
                u_scr, gext_scr, a_scr, carry_scr, wup0_scr, wup1_scr, wdn0_scr, wdn1_scr,
                *, tm, sr, shift, tps, nsub, final_norm):
    i = pl.program_id(0)
    s = pl.program_id(1)
    wup_slots = (wup0_scr, wup1_scr)
    wdn_slots = (wdn0_scr, wdn1_scr)

    def cast_steps(slot):
        wup_scr = wup_slots[slot]
        wdn_scr = wdn_slots[slot]
        valid = D_FF - jnp.minimum(s, NF - 1) * TF

        def cast_gate():
            ok = lax.broadcasted_iota(jnp.int32, (1, TF), 1) < valid
            w = jnp.where(ok, wg_ref[...], 0.0).astype(BF16)
            wup_scr[:, 0:TF] = w
            wupc_ref[:, 0:TF] = w

        def cast_value():
            lane = lax.broadcasted_iota(jnp.int32, (1, LANES), 1)
            w0 = jnp.where(lane < valid, wa0_ref[...], 0.0).astype(BF16)
            w1 = jnp.where(lane + LANES < valid, wa1_ref[...], 0.0).astype(BF16)
            wup_scr[:, TF:TF + LANES] = w0
            wup_scr[:, TF + LANES:2 * TF] = w1
            wupc_ref[:, TF:TF + LANES] = w0
            wupc_ref[:, TF + LANES:2 * TF] = w1

        def cast_down(r0, rows):
            row_ok = r0 + lax.broadcasted_iota(jnp.int32, (rows, 1), 0) < valid
            w = jnp.where(row_ok, wd_ref[r0:r0 + rows, :], 0.0).astype(BF16)
            wdn_scr[r0:r0 + rows, :] = w
            wdnc_ref[r0:r0 + rows, :] = w

        half = TF // 2
        return [cast_gate, cast_value, functools.partial(cast_down, 0, half),
                functools.partial(cast_down, half, half)]

    def run_tile(slot, fillers=()):
        fillers = list(fillers)

        def fill():
            if fillers:
                fillers.pop(0)()

        wup_scr = wup_slots[slot]
        wdn_scr = wdn_slots[slot]
        f = s - 1
        col_ok = lax.broadcasted_iota(jnp.int32, (1, TF), 1) < D_FF - f * TF
        cw = cw_ref[...]
        cb = cb_ref[...]
        if tps == 1:
            gext_scr[0:sr, :] = st_ref[...]
        else:
            gext_scr[0:sr, :] = jnp.where((i % tps) == 0, st_ref[...], carry_scr[f])
        ts = tm // nsub
        for h in range(nsub):
            r0 = h * ts
            ga = jnp.dot(u_scr[r0:r0 + ts, :], wup_scr[...], preferred_element_type=F32)
            gext_scr[sr + r0:sr + r0 + ts, :] = ga[:, 0:TF]
            a_scr[r0:r0 + ts, :] = ga[:, TF:2 * TF]
            fill()
        for h in range(nsub):
            r0 = h * ts
            g = gext_scr[sr + r0:sr + r0 + ts, :]
            g_m2 = gext_scr[sr + r0 - 2 * shift:sr + r0 - 2 * shift + ts, :]
            g_m1 = gext_scr[sr + r0 - shift:sr + r0 - shift + ts, :]
            gc = cb + ((cw[0:1, :] * g_m2 + cw[1:2, :] * g_m1) + cw[2:3, :] * g)
            hmid = jnp.where(col_ok, (gc * jax.nn.sigmoid(gc)) * a_scr[r0:r0 + ts, :], 0.0)
            o_ref[r0:r0 + ts, :] += jnp.dot(hmid.astype(BF16), wdn_scr[...], preferred_element_type=F32)
            fill()
        while fillers:
            fill()
        g_tail = gext_scr[tm:tm + sr, :]
        so_ref[...] = g_tail
        if tps > 1:
            carry_scr[f] = g_tail

    @pl.when(s == 0)
    def _():
        x = x_ref[...]
        u_scr[...] = _rms(x, gam_ref[...]).astype(BF16)
        o_ref[...] = x
        for step in cast_steps(0):
            step()
        if tps > 1:
            @pl.when(i == 0)
            def _():
                carry_scr[...] = jnp.zeros_like(carry_scr)

    for parity in range(2):
        @pl.when(jnp.logical_and(s > 0, s % 2 == parity))
        def _(parity=parity):
            run_tile(1 - parity, cast_steps(parity))

    if final_norm:
        @pl.when(s == NF)
        def _():
            o_ref[...] = _rms(o_ref[...], gfin_ref[...])


def _ffn(x, layer, gamma, w_up, conv_w, conv_b, w_down, st, gfin, *, tm, sr, shift, tps, st_per_tile,
         final_norm):
    assert TF == 2 * LANES
    m = x.shape[0]
    nt = m // tm
    nsub = 2 if tm >= 64 else 1
    kern = functools.partial(_ffn_kernel, tm=tm, sr=sr, shift=shift, tps=tps, nsub=nsub,
                             final_norm=final_norm)
    last_a = 2 * FF_LANE_BLOCKS - 1

    def wt(s):
        return jnp.minimum(s, NF - 1)

    def ft(s):
        return jnp.maximum(s - 1, 0)

    st_map = (lambda i, s: (i, ft(s))) if st_per_tile else (lambda i, s: (0, ft(s)))
    return pl.pallas_call(
        kern,
        grid=(nt, NF + 1),
        in_specs=[pl.BlockSpec((tm, D), lambda i, s: (i, 0), pipeline_mode=pl.Buffered(1)),
                  pl.BlockSpec((1, D), lambda i, s: (0, 0)),
                  pl.BlockSpec((None, D, TF), lambda i, s: (layer, 0, wt(s))),
                  pl.BlockSpec((None, D, LANES), lambda i, s: (layer, 0, FF_LANE_BLOCKS + 2 * wt(s))),
                  pl.BlockSpec((None, D, LANES),
                               lambda i, s: (layer, 0, jnp.minimum(FF_LANE_BLOCKS + 2 * wt(s) + 1, last_a))),
                  pl.BlockSpec((None, 3, TF), lambda i, s: (layer, 0, ft(s))),
                  pl.BlockSpec((None, 1, TF), lambda i, s: (layer, 0, ft(s))),
                  pl.BlockSpec((None, TF, D), lambda i, s: (layer, wt(s), 0)),
                  pl.BlockSpec((sr, TF), st_map),
                  pl.BlockSpec((1, D), lambda i, s: (0, 0))],
        out_specs=[pl.BlockSpec((tm, D), lambda i, s: (i, 0)),
                   pl.BlockSpec((sr, TF), lambda i, s: (i, ft(s))),
                   pl.BlockSpec((D, 2 * TF), lambda i, s: (0, wt(s))),
                   pl.BlockSpec((TF, D), lambda i, s: (wt(s), 0))],
        out_shape=[jax.ShapeDtypeStruct((m, D), F32),
                   jax.ShapeDtypeStruct((nt * sr, D_FF), F32),
                   jax.ShapeDtypeStruct((D, NF * 2 * TF), BF16),
                   jax.ShapeDtypeStruct((NF * TF, D), BF16)],
        scratch_shapes=[pltpu.VMEM((tm, D), BF16),
                        pltpu.VMEM((sr + tm, TF), F32),
                        pltpu.VMEM((tm, TF), F32),
                        pltpu.VMEM((NF, sr, TF), F32),
                        pltpu.VMEM((D, 2 * TF), BF16),
                        pltpu.VMEM((D, 2 * TF), BF16),
                        pltpu.VMEM((TF, D), BF16),
                        pltpu.VMEM((TF, D), BF16)],
        compiler_params=_params(2),
        name="conv_ffn_cast",
    )(x, gamma, w_up, w_up, w_up, conv_w, conv_b, w_down, st, gfin)


def _ffn_bf16_kernel(x_ref, gam_ref, wup_ref, cw_ref, cb_ref, wdn_ref, st_ref, gfin_ref,
                     o_ref, so_ref, u_scr, gext_scr, a_scr, carry_scr,
                     *, tm, sr, shift, tps, nsub, final_norm):
    i = pl.program_id(0)
    f = pl.program_id(1)
    tf2 = 2 * TF

    @pl.when(f == 0)
    def _():
        x = x_ref[...]
        u_scr[...] = _rms(x, gam_ref[...]).astype(BF16)
        o_ref[...] = x
        if tps > 1:
            @pl.when(i == 0)
            def _():
                carry_scr[...] = jnp.zeros_like(carry_scr)

    col_ok = lax.broadcasted_iota(jnp.int32, (1, tf2), 1) < D_FF - f * tf2
    cw = cw_ref[...]
    cb = cb_ref[...]
    if tps == 1:
        gext_scr[0:sr, :] = st_ref[...]
    else:
        gext_scr[0:sr, :] = jnp.where((i % tps) == 0, st_ref[...], carry_scr[f])
    ts = tm // nsub
    for h in range(nsub):
        r0 = h * ts
        ga = jnp.dot(u_scr[r0:r0 + ts, :], wup_ref[...], preferred_element_type=F32)
        gext_scr[sr + r0:sr + r0 + ts, 0:TF] = ga[:, 0:TF]
        gext_scr[sr + r0:sr + r0 + ts, TF:tf2] = ga[:, 2 * TF:3 * TF]
        a_scr[r0:r0 + ts, 0:TF] = ga[:, TF:2 * TF]
        a_scr[r0:r0 + ts, TF:tf2] = ga[:, 3 * TF:4 * TF]
    for h in range(nsub):
        r0 = h * ts
        g = gext_scr[sr + r0:sr + r0 + ts, :]
        g_m2 = gext_scr[sr + r0 - 2 * shift:sr + r0 - 2 * shift + ts, :]
        g_m1 = gext_scr[sr + r0 - shift:sr + r0 - shift + ts, :]
        gc = cb + ((cw[0:1, :] * g_m2 + cw[1:2, :] * g_m1) + cw[2:3, :] * g)
        hmid = jnp.where(col_ok, (gc * jax.nn.sigmoid(gc)) * a_scr[r0:r0 + ts, :], 0.0)
        o_ref[r0:r0 + ts, :] += jnp.dot(hmid.astype(BF16), wdn_ref[...], preferred_element_type=F32)
    g_tail = gext_scr[tm:tm + sr, :]
    so_ref[...] = g_tail
    if tps > 1:
        carry_scr[f] = g_tail

    if final_norm:
        @pl.when(f == pl.num_programs(1) - 1)
        def _():
            o_ref[...] = _rms(o_ref[...], gfin_ref[...])


def _ffn_bf16(x, layer, gamma, wupc, conv_w, conv_b, wdnc, st, gfin, *, tm, sr, shift, tps, st_per_tile,
              final_norm):
    assert NF % 2 == 0
    m = x.shape[0]
    nt = m // tm
    tf2 = 2 * TF
    nf2 = NF // 2
    kern = functools.partial(_ffn_bf16_kernel, tm=tm, sr=sr, shift=shift, tps=tps,
                             nsub=2 if tm >= 64 else 1, final_norm=final_norm)
    st_map = (lambda i, f: (i, f)) if st_per_tile else (lambda i, f: (0, f))
    return pl.pallas_call(
        kern,
        grid=(nt, nf2),
        in_specs=[pl.BlockSpec((tm, D), lambda i, f: (i, 0), pipeline_mode=pl.Buffered(1)),
                  pl.BlockSpec((1, D), lambda i, f: (0, 0)),
                  pl.BlockSpec((D, 2 * tf2), lambda i, f: (0, f)),
                  pl.BlockSpec((None, 3, tf2), lambda i, f: (layer, 0, f)),
                  pl.BlockSpec((None, 1, tf2), lambda i, f: (layer, 0, f)),
                  pl.BlockSpec((tf2, D), lambda i, f: (f, 0)),
                  pl.BlockSpec((sr, tf2), st_map),
                  pl.BlockSpec((1, D), lambda i, f: (0, 0))],
        out_specs=[pl.BlockSpec((tm, D), lambda i, f: (i, 0)),
                   pl.BlockSpec((sr, tf2), lambda i, f: (i, f))],
        out_shape=[jax.ShapeDtypeStruct((m, D), F32),
                   jax.ShapeDtypeStruct((nt * sr, D_FF), F32)],
        scratch_shapes=[pltpu.VMEM((tm, D), BF16),
                        pltpu.VMEM((sr + tm, tf2), F32),
                        pltpu.VMEM((tm, tf2), F32),
                        pltpu.VMEM((nf2, sr, tf2), F32)],
        compiler_params=_params(2),
        name="conv_ffn",
    )(x, gamma, wupc, conv_w, conv_b, wdnc, st, gfin)


def _pool_kernel(x_ref, gam_ref, wp_ref, sc_ref, st_ref, o_ref, ut_ref, rinv_scr, uext_scr, carry_scr,
                 *, tm, shift, tps, pos0, tr):
    i = pl.program_id(0)
    grp = pl.program_id(1)
    hr = POOL_HALO * shift

    @pl.when(grp == 0)
    def _():
        x = x_ref[...]
        rinv_scr[...] = lax.rsqrt(jnp.mean(x * x, axis=1, keepdims=True) + EPS)

    for kk, w in enumerate(POOL_WINDOWS):
        @pl.when(grp == kk)
        def _(kk=kk, w=w):
            cs = slice(kk * PG, (kk + 1) * PG)
            xg = x_ref[:, cs]
            ug = xg * rinv_scr[...] * gam_ref[:, cs]
            if tps == 1:
                uext_scr[0:hr, :] = st_ref[...]
            else:
                seq_start = (i % tps) == 0

                @pl.when(seq_start)
                def _():
                    uext_scr[0:hr, :] = st_ref[...]

                @pl.when(jnp.logical_not(seq_start))
                def _():
                    uext_scr[0:hr, :] = carry_scr[kk]

            uext_scr[hr:hr + tm, :] = ug
            acc = ug
            for j in range(1, w):
                acc = acc + uext_scr[hr - j * shift:hr - j * shift + tm, :]
            if pos0 + 1 >= w:
                pooled = acc / float(w) - ug
            else:
                step = (i % tps) * (tm // shift) + lax.broadcasted_iota(jnp.int32, (tm, 1), 0) // shift
                cnt = jnp.minimum(w, pos0 + step + 1).astype(F32)
                pooled = acc / cnt - ug
            y = jnp.dot(pooled.astype(BF16), wp_ref[kk].astype(BF16), preferred_element_type=F32)
            o_ref[...] = xg + y * sc_ref[:, cs]
            ut_ref[...] = ug[tm - tr:tm, :]
            if tps > 1:
                carry_scr[kk] = uext_scr[tm:tm + hr, :]


def _pool(x, gamma, wp, sc, st, *, tm, shift, tps, pos0, tr, st_per_tile):
    m = x.shape[0]
    nt = m // tm
    hr = POOL_HALO * shift
    ng = len(POOL_WINDOWS)
    kern = functools.partial(_pool_kernel, tm=tm, shift=shift, tps=tps, pos0=pos0, tr=tr)
    st_map = (lambda i, g: (i, g)) if st_per_tile else (lambda i, g: (0, g))
    return pl.pallas_call(
        kern,
        grid=(nt, ng),
        in_specs=[pl.BlockSpec((tm, D), lambda i, g: (i, 0)),
                  pl.BlockSpec((1, D), lambda i, g: (0, 0)),
                  pl.BlockSpec((None, ng, PG, PG), lambda i, g: (0, 0, 0, 0)),
                  pl.BlockSpec((1, D), lambda i, g: (0, 0)),
                  pl.BlockSpec((hr, PG), st_map)],
        out_specs=[pl.BlockSpec((tm, PG), lambda i, g: (i, g)),
                   pl.BlockSpec((tr, PG), lambda i, g: (i, g))],
        out_shape=[jax.ShapeDtypeStruct((m, D), F32),
                   jax.ShapeDtypeStruct((nt * tr, D), F32)],
        scratch_shapes=[pltpu.VMEM((tm, 1), F32),
                        pltpu.VMEM((hr + tm, PG), F32),
                        pltpu.VMEM((ng, hr, PG), F32)],
        compiler_params=_params(2),
        name="pool_mixer",
    )(x, gamma, wp, sc, st)


def _pool_sample_kernel(*refs, nseq, t_dec):
    nh = POOL_HALO - 1
    x_ref, gam_ref, wp_ref, sc_ref = refs[0:4]
    st_refs = refs[4:4 + nh]
    o_ref, ut_ref, rinv_scr, uext_scr = refs[4 + nh:]
    grp = pl.program_id(0)
    tm = nseq * t_dec
    hr = POOL_HALO * nseq

    @pl.when(grp == 0)
    def _():
        x = x_ref[...]
        rinv_scr[...] = lax.rsqrt(jnp.mean(x * x, axis=1, keepdims=True) + EPS)

    for kk, w in enumerate(POOL_WINDOWS):
        @pl.when(grp == kk)
        def _(kk=kk, w=w):
            cs = slice(kk * PG, (kk + 1) * PG)
            xg = x_ref[:, cs]
            ug = xg * rinv_scr[...] * gam_ref[:, cs]
            for t in range(nh):
                uext_scr[(t + 1) * nseq:(t + 2) * nseq, :] = st_refs[t][...]
            uext_scr[hr:hr + tm, :] = ug
            acc = ug
            for j in range(1, w):
                acc = acc + uext_scr[hr - j * nseq:hr - j * nseq + tm, :]
            pooled = acc / float(w) - ug
            y = jnp.dot(pooled.astype(BF16), wp_ref[kk].astype(BF16), preferred_element_type=F32)
            o_ref[...] = xg + y * sc_ref[:, cs]
            ut_ref[...] = ug


def _pool_sample(x, gamma, wp, sc, state, *, nseq, t_dec):
    assert PAST_LEN + 1 >= max(POOL_WINDOWS)
    nh = POOL_HALO - 1
    ng = len(POOL_WINDOWS)
    tm = nseq * t_dec
    hr = POOL_HALO * nseq
    kern = functools.partial(_pool_sample_kernel, nseq=nseq, t_dec=t_dec)
    hist_specs = [pl.BlockSpec((nseq, PG), lambda g, t=t: (0, t * ng + g)) for t in range(nh)]
    return pl.pallas_call(
        kern,
        grid=(ng,),
        in_specs=[pl.BlockSpec((tm, D), lambda g: (0, 0)),
                  pl.BlockSpec((1, D), lambda g: (0, 0)),
                  pl.BlockSpec((None, ng, PG, PG), lambda g: (0, 0, 0, 0)),
                  pl.BlockSpec((1, D), lambda g: (0, 0))] + hist_specs,
        out_specs=[pl.BlockSpec((tm, PG), lambda g: (0, g)),
                   pl.BlockSpec((tm, PG), lambda g: (0, g))],
        out_shape=[jax.ShapeDtypeStruct((tm, D), F32),
                   jax.ShapeDtypeStruct((tm, D), F32)],
        scratch_shapes=[pltpu.VMEM((tm, 1), F32),
                        pltpu.VMEM((hr + tm, PG), F32)],
        compiler_params=_params(1),
        name="pool_mixer_sample",
    )(x, gamma, wp, sc, *([state] * nh))


def _pad_cols(a, n):
    return jnp.pad(a, ((0, 0), (0, n - a.shape[1])))


def _to_time_major(a, t):
    nseq = a.shape[0]
    a = a.reshape((nseq // SAMPLE_TILE_SEQS, SAMPLE_TILE_SEQS, t) + a.shape[2:])
    a = jnp.swapaxes(a, 1, 2)
    return a.reshape((nseq * t,) + a.shape[3:])


def _from_time_major(a, nseq, t):
    a = a.reshape((nseq // SAMPLE_TILE_SEQS, t, SAMPLE_TILE_SEQS) + a.shape[1:])
    a = jnp.swapaxes(a, 1, 2)
    return a.reshape((nseq, t) + a.shape[3:])


def kernel(x_prompt, x_sample, state_mlstm_C, state_mlstm_n, state_mlstm_m, state_pool, state_ffn_conv,
           meta_tokens, norm_mix, norm_ffn, norm_final, w_mlstm_in, b_mlstm_gate, g_mlstm_out, w_mlstm_out,
           w_pool, pool_scale, w_up, conv_w, conv_b, w_down):
    bsz, seq, _ = x_prompt.shape
    nseq, t_dec, _ = x_sample.shape

    w_main = w_mlstm_in
    w_gate = _pad_cols(w_mlstm_in[0][:, PW:], GATE_LANES).astype(BF16)
    bias = _pad_cols(b_mlstm_gate[0][None, :], GATE_LANES)
    w_out = w_mlstm_out
    gout = g_mlstm_out[0][None, :]
    wp = w_pool
    psc = pool_scale[0][None, :]
    gfin = norm_final[None, :]
    conv_b3 = conv_b[:, None, :]
    ffn_w = [(layer, norm_ffn[layer][None, :], w_up, conv_w, conv_b3, w_down) for layer in range(2)]
    g_mix0 = norm_mix[0][None, :]
    g_mix1 = norm_mix[1][None, :]

    ffn_cache = {}
    mix_cache = {}

    def proj(x, **kw):
        if "w_in" not in mix_cache:
            p, gates, mix_cache["w_in"] = _proj(x, g_mix0, w_main, w_gate, **kw)
            return p, gates
        return _proj(x, g_mix0, mix_cache["w_in"], w_gate, **kw)

    def out_proj(a, x, *, tm):
        if "w_out" not in mix_cache:
            y, mix_cache["w_out"] = _mmres_cast(a, w_out, x, tn=512)
            return y
        return _mmres(a, mix_cache["w_out"], x, tm=tm)

    def ffn(x, layer, st, **kw):
        lyr, gamma, wu, cwt, cbs, wd = ffn_w[layer]
        if layer not in ffn_cache:
            y, cs, wupc, wdnc = _ffn(x, lyr, gamma, wu, cwt, cbs, wd, st, gfin, **kw)
            ffn_cache[layer] = (wupc, wdnc)
            return y, cs
        wupc, wdnc = ffn_cache[layer]
        return _ffn_bf16(x, lyr, gamma, wupc, cwt, cbs, wdnc, st, gfin, **kw)

    def long_stream(x, nb, s, st, *, tm, chunk, lead_pad, pos0):
        c0, n0, m0, conv0, pool0, conv1 = st
        tm_ffn = tm
        p, gates = proj(x, tm=tm, tn=1024, out_dtype=BF16)
        p = p.reshape(nb, s, PW)
        gates = gates.reshape(nb, s, GATE_LANES)
        if lead_pad:
            p = jnp.pad(p, ((0, 0), (lead_pad, 0), (0, 0)))
            gates = jnp.pad(gates, ((0, 0), (lead_pad, 0), (0, 0)))
        hg, c_new, n_new, m_new = _scan(p, gates, bias, gout, c0, n0, m0, L=chunk, lead_pad=lead_pad)
        hg = hg[:, lead_pad:].reshape(nb * s, VW)
        x1 = out_proj(hg, x, tm=min(tm, 512))
        x2, cs0 = ffn(x1, 0, conv0, tm=tm_ffn, sr=8, shift=1, tps=s // tm_ffn,
                      st_per_tile=False, final_norm=False)
        x3, ut = _pool(x2, g_mix1, wp, psc, pool0, tm=tm, shift=1, tps=s // tm, pos0=pos0,
                       tr=POOL_HALO, st_per_tile=False)
        y, cs1 = ffn(x3, 1, conv1, tm=tm_ffn, sr=8, shift=1, tps=s // tm_ffn,
                     st_per_tile=False, final_norm=True)
        cs0 = cs0.reshape(nb, s // tm_ffn, 8, D_FF)[:, -1]
        cs1 = cs1.reshape(nb, s // tm_ffn, 8, D_FF)[:, -1]
        ut = ut.reshape(nb, s // tm, POOL_HALO, D)[:, -1]
        return y, (c_new, n_new, m_new, cs0, ut, cs1)

    def prompt_streams():
        zero_state = (jnp.zeros((H, DK, DV), F32), jnp.zeros((8, DK), F32), jnp.zeros((8, GATE_LANES), F32),
                      jnp.zeros((8, D_FF), F32), jnp.zeros((POOL_HALO, D), F32), jnp.zeros((8, D_FF), F32))
        _, (c_m, n_m, m_m, cs0_m, ut_m, cs1_m) = long_stream(
            meta_tokens, 1, N_META, zero_state, tm=N_META, chunk=128, lead_pad=128 - N_META, pos0=0)

        y_p, (c_p, n_p, m_p, cs0_p, ut_p, cs1_p) = long_stream(
            x_prompt.reshape(bsz * seq, D), bsz, seq, (c_m[0], n_m[0], m_m[0], cs0_m[0], ut_m[0], cs1_m[0]),
            tm=1024, chunk=SCAN_CHUNK, lead_pad=0, pos0=N_META)
        return (y_p.reshape(bsz, seq, D), c_p[None], n_p[:, :H][None], m_p[:, :H, 0][None],
                ut_p[:, 1:][None], jnp.stack([cs0_p[:, 6:], cs1_p[:, 6:]]))

    xs = x_sample.reshape(nseq * t_dec, D)
    p_s, gates_s = proj(xs, tm=nseq * t_dec, tn=512, out_dtype=F32)
    mtok = _pad_cols(jnp.repeat(state_mlstm_m[0], t_dec, axis=0), GATE_LANES)
    n_hm = jnp.swapaxes(state_mlstm_n[0], 0, 1)
    hg_s, C_s, n_s_hm, m_s_hm = _scan_s(p_s, gates_s, bias, gout, mtok, state_mlstm_C[0], n_hm, T=t_dec)
    assert nseq == SAMPLE_TILE_SEQS
    tm_s = SAMPLE_TILE_SEQS * t_dec
    hg_t = _to_time_major(hg_s.reshape(nseq, t_dec, VW), t_dec)
    x_t = _to_time_major(x_sample, t_dec)
    x1 = out_proj(hg_t, x_t, tm=512)

    def conv_state_in(cs):
        return _to_time_major(cs, 2)

    def conv_state_out(cs):
        return _from_time_major(cs, nseq, 2)

    sr_s = 2 * SAMPLE_TILE_SEQS
    x2, cs0_s = ffn(x1, 0, conv_state_in(state_ffn_conv[0]), tm=tm_s, sr=sr_s,
                    shift=SAMPLE_TILE_SEQS, tps=1, st_per_tile=True, final_norm=False)
    x3, ut_s = _pool_sample(x2, g_mix1, wp, psc, state_pool[0].reshape(nseq, (POOL_HALO - 1) * D),
                            nseq=nseq, t_dec=t_dec)
    y_s, cs1_s = ffn(x3, 1, conv_state_in(state_ffn_conv[1]), tm=tm_s, sr=sr_s,
                     shift=SAMPLE_TILE_SEQS, tps=1, st_per_tile=True, final_norm=True)
    y_sample = _from_time_major(y_s, nseq, t_dec)
    n_s = jnp.swapaxes(n_s_hm, 0, 1)[None]
    m_s = jnp.swapaxes(m_s_hm[:, :, 0], 0, 1)[None]
    pool_s = jnp.concatenate([state_pool[0][:, t_dec:], _from_time_major(ut_s, nseq, t_dec)], axis=1)[None]
    conv_s = jnp.stack([conv_state_out(cs0_s), conv_state_out(cs1_s)])

    y_prompt, C_p, n_p, m_p, pool_p, conv_p = prompt_streams()
    return (y_prompt, y_sample, C_p, n_p, m_p, pool_p, conv_p,
            C_s[None], n_s, m_s, pool_s, conv_s)
```

```python
import functools

import jax
import jax.numpy as jnp
from jax import lax
from jax.experimental import pallas as pl
from jax.experimental.pallas import tpu as pltpu

F32 = jnp.float32
BF16 = jnp.bfloat16

EPS = 1e-6
D = 2048
H = 4
DK = 256
DV = 512
QKW = H * DK
VW = H * DV
PW = 2 * QKW + 2 * VW
GATE_LANES = 128
SCALE = DK ** -0.5
POOL_WINDOWS = (2, 4, 8, 16)
PG = D // len(POOL_WINDOWS)
POOL_HALO = 16
POOL_PAD_ROWS = 16
D_FF = 5504
TF = 256
NF = -(-D_FF // TF)
LANES = 128
FF_LANE_BLOCKS = D_FF // LANES
N_META = 16
PAST_LEN = 16384
SCAN_CHUNK = 256
SAMPLE_GROUP = 16
SAMPLE_TILE_SEQS = 128
VMEM_LIMIT = 60 * 1024 * 1024


def _params(n_axes):
    return pltpu.CompilerParams(dimension_semantics=("arbitrary",) * n_axes,
                                vmem_limit_bytes=VMEM_LIMIT)


def _rms(x, g):
    return x * lax.rsqrt(jnp.mean(x * x, axis=-1, keepdims=True) + EPS) * g


def _log_sigmoid(x):
    return jnp.minimum(x, 0.0) - jnp.log(1.0 + jnp.exp(-jnp.abs(x)))


def _proj_kernel(x_ref, g_ref, w_ref, wg_ref, p_ref, gate_ref, u_scr):
    @pl.when(pl.program_id(1) == 0)
    def _():
        ub = _rms(x_ref[...], g_ref[...]).astype(BF16)
        u_scr[...] = ub
        gate_ref[...] = jnp.dot(ub, wg_ref[...], preferred_element_type=F32)

    p_ref[...] = jnp.dot(u_scr[...], w_ref[...], preferred_element_type=F32).astype(p_ref.dtype)


def _proj(x, gamma, w, wg, *, tm, tn, out_dtype):
    m = x.shape[0]
    return pl.pallas_call(
        _proj_kernel,
        grid=(m // tm, PW // tn),
        in_specs=[pl.BlockSpec((tm, D), lambda i, j: (i, 0)),
                  pl.BlockSpec((1, D), lambda i, j: (0, 0)),
                  pl.BlockSpec((D, tn), lambda i, j: (0, j)),
                  pl.BlockSpec((D, GATE_LANES), lambda i, j: (0, 0))],
        out_specs=[pl.BlockSpec((tm, tn), lambda i, j: (i, j)),
                   pl.BlockSpec((tm, GATE_LANES), lambda i, j: (i, 0))],
        out_shape=[jax.ShapeDtypeStruct((m, PW), out_dtype),
                   jax.ShapeDtypeStruct((m, GATE_LANES), F32)],
        scratch_shapes=[pltpu.VMEM((tm, D), BF16)],
        compiler_params=_params(2),
        name="proj",
    )(x, gamma, w, wg)


def _head_output(num, den, m_t, gout, o):
    hv = num * (1.0 / jnp.maximum(jnp.abs(den), jnp.exp(-m_t)))
    hv = hv * lax.rsqrt(jnp.mean(hv * hv, axis=1, keepdims=True) + EPS)
    return hv * gout * jax.nn.sigmoid(o.astype(F32))


def _scan_kernel(q_ref, k_ref, v_ref, o_ref, gt_ref, bias_ref, gout_ref, c0_ref, n0_ref, m0_ref,
                 hg_ref, cout_ref, nout_ref, mout_ref, c_scr, n_scr, m_scr, *, L, lead_pad, nc):
    c = pl.program_id(1)

    @pl.when(c == 0)
    def _():
        c_scr[...] = c0_ref[...]
        n_scr[...] = n0_ref[...]
        m_scr[...] = m0_ref[...]

    gates = gt_ref[...] + bias_ref[...]
    gates_t = gates.T
    row = lax.broadcasted_iota(jnp.int32, (L, L), 0)
    col = lax.broadcasted_iota(jnp.int32, (L, L), 1)
    causal = row >= col
    causal_t = row <= col
    if lead_pad:
        live_col = lax.broadcasted_iota(jnp.int32, (L, 1), 0) >= lead_pad
        live_row = lax.broadcasted_iota(jnp.int32, (1, L), 1) >= lead_pad

    for hh in range(H):
        ig_col = gates[:, hh:hh + 1]
        ig_row = gates_t[hh:hh + 1, :]
        lf_col = _log_sigmoid(gates[:, H + hh:H + hh + 1])
        lf_row = _log_sigmoid(gates_t[H + hh:H + hh + 1, :])
        if lead_pad:
            ig_col = jnp.where(live_col, ig_col, -jnp.inf)
            ig_row = jnp.where(live_row, ig_row, -jnp.inf)
            lf_col = jnp.where(live_col, lf_col, 0.0)
            lf_row = jnp.where(live_row, lf_row, 0.0)
        b_col = jnp.sum(jnp.where(causal, lf_row, 0.0), axis=1, keepdims=True)
        b_row = jnp.sum(jnp.where(causal_t, lf_col, 0.0), axis=0, keepdims=True)
        m_prev = m_scr[hh, 0:1, 0:1]
        d = jnp.where(causal, b_col - b_row + ig_row, -jnp.inf)
        inter = b_col + m_prev
        m_t = jnp.maximum(inter, jnp.max(d, axis=1, keepdims=True))
        w_inter = jnp.exp(inter - m_t) * SCALE

        q = q_ref[:, hh * DK:(hh + 1) * DK]
        k = k_ref[:, hh * DK:(hh + 1) * DK]
        v = v_ref[:, hh * DV:(hh + 1) * DV]
        qk = lax.dot_general(q, k, (((1,), (1,)), ((), ())), preferred_element_type=F32)
        s = qk * (jnp.exp(d - m_t) * SCALE)
        cmat = c_scr[hh]
        nvec = n_scr[hh, 0:1, :]
        num = w_inter * jnp.dot(q, cmat.astype(BF16), preferred_element_type=F32) \
            + jnp.dot(s.astype(BF16), v, preferred_element_type=F32)
        den = w_inter * jnp.sum(q.astype(F32) * nvec, axis=1, keepdims=True) \
            + jnp.sum(s, axis=1, keepdims=True)
        hout = _head_output(num, den, m_t, gout_ref[:, hh * DV:(hh + 1) * DV],
                            o_ref[:, hh * DV:(hh + 1) * DV])
        hg_ref[:, hh * DV:(hh + 1) * DV] = hout.astype(hg_ref.dtype)

        m_new = m_t[L - 1:L, :]
        b_last = b_col[L - 1:L, :]
        decay = jnp.exp(b_last + m_prev - m_new)
        wk = jnp.exp(b_last - b_col + ig_col - m_new) * k.astype(F32)
        c_scr[hh] = decay * cmat + jnp.dot(wk.T.astype(BF16), v, preferred_element_type=F32)
        n_scr[hh] = jnp.broadcast_to(decay * nvec + jnp.sum(wk, axis=0, keepdims=True), (8, DK))
        m_scr[hh] = jnp.broadcast_to(m_new, (8, GATE_LANES))

    @pl.when(c == nc - 1)
    def _():
        cout_ref[...] = c_scr[...]
        nout_ref[...] = n_scr[...]
        mout_ref[...] = m_scr[...]


def _scan(p, gates, bias, gout, c0, n0, m0, *, L, lead_pad=0):
    b, s, _ = p.shape
    nc = s // L
    kern = functools.partial(_scan_kernel, L=L, lead_pad=lead_pad, nc=nc)
    return pl.pallas_call(
        kern,
        grid=(b, nc),
        in_specs=[pl.BlockSpec((None, L, QKW), lambda i, c: (i, c, 0)),
                  pl.BlockSpec((None, L, QKW), lambda i, c: (i, c, 1)),
                  pl.BlockSpec((None, L, VW), lambda i, c: (i, c, 1)),
                  pl.BlockSpec((None, L, VW), lambda i, c: (i, c, 2)),
                  pl.BlockSpec((None, L, GATE_LANES), lambda i, c: (i, c, 0)),
                  pl.BlockSpec((1, GATE_LANES), lambda i, c: (0, 0)),
                  pl.BlockSpec((1, VW), lambda i, c: (0, 0)),
                  pl.BlockSpec((H, DK, DV), lambda i, c: (0, 0, 0)),
                  pl.BlockSpec((H, 8, DK), lambda i, c: (0, 0, 0)),
                  pl.BlockSpec((H, 8, GATE_LANES), lambda i, c: (0, 0, 0))],
        out_specs=[pl.BlockSpec((None, L, VW), lambda i, c: (i, c, 0)),
                   pl.BlockSpec((None, H, DK, DV), lambda i, c: (i, 0, 0, 0)),
                   pl.BlockSpec((None, H, 8, DK), lambda i, c: (i, 0, 0, 0)),
                   pl.BlockSpec((None, H, 8, GATE_LANES), lambda i, c: (i, 0, 0, 0))],
        out_shape=[jax.ShapeDtypeStruct((b, s, VW), BF16),
                   jax.ShapeDtypeStruct((b, H, DK, DV), F32),
                   jax.ShapeDtypeStruct((b, H, 8, DK), F32),
                   jax.ShapeDtypeStruct((b, H, 8, GATE_LANES), F32)],
        scratch_shapes=[pltpu.VMEM((H, DK, DV), F32),
                        pltpu.VMEM((H, 8, DK), F32),
                        pltpu.VMEM((H, 8, GATE_LANES), F32)],
        compiler_params=_params(2),
        name="scan",
    )(p, p, p, p, gates, bias, gout, c0, n0, m0)


def _scan_s_kernel(q_ref, k_ref, v_ref, o_ref, gt_ref, bias_ref, gout_ref, mtok_ref, c_ref, n_ref,
                   hg_ref, cout_ref, nout_ref, mout_ref, qc_scr, ntok_scr, *, T, NB):
    hh = pl.program_id(1)
    LT = NB * T
    gates = gt_ref[...] + bias_ref[...]
    gates_t = gates.T
    lane = lax.broadcasted_iota(jnp.int32, (LT, GATE_LANES), 1)
    sub = lax.broadcasted_iota(jnp.int32, (GATE_LANES, LT), 0)

    def pick_col(a, idx):
        return jnp.sum(jnp.where(lane == idx, a, 0.0), axis=1, keepdims=True)

    def pick_row(a, idx):
        return jnp.sum(jnp.where(sub == idx, a, 0.0), axis=0, keepdims=True)

    ig_col = pick_col(gates, hh)
    ig_row = pick_row(gates_t, hh)
    lf_col = _log_sigmoid(pick_col(gates, hh + H))
    lf_row = _log_sigmoid(pick_row(gates_t, hh + H))
    m_prev = pick_col(mtok_ref[...], hh)

    row = lax.broadcasted_iota(jnp.int32, (LT, LT), 0)
    col = lax.broadcasted_iota(jnp.int32, (LT, LT), 1)
    same = (row // T) == (col // T)
    causal = jnp.logical_and(same, row >= col)
    causal_t = jnp.logical_and(same, row <= col)
    b_col = jnp.sum(jnp.where(causal, lf_row, 0.0), axis=1, keepdims=True)
    b_row = jnp.sum(jnp.where(causal_t, lf_col, 0.0), axis=0, keepdims=True)
    b_end = jnp.sum(jnp.where(same, lf_row, 0.0), axis=1, keepdims=True)
    d = jnp.where(causal, b_col - b_row + ig_row, -jnp.inf)
    inter = b_col + m_prev
    m_t = jnp.maximum(inter, jnp.max(d, axis=1, keepdims=True))
    d_end = jnp.where(same, b_end - b_row + ig_row, -jnp.inf)
    m_new = jnp.maximum(b_end + m_prev, jnp.max(d_end, axis=1, keepdims=True))
    w_inter = jnp.exp(inter - m_t) * SCALE

    q32 = q_ref[...]
    k32 = k_ref[...]
    q = q32.astype(BF16)
    v = v_ref[...].astype(BF16)
    qk = lax.dot_general(q, k32.astype(BF16), (((1,), (1,)), ((), ())), preferred_element_type=F32)
    s = qk * (jnp.exp(d - m_t) * SCALE)
    num_intra = jnp.dot(s.astype(BF16), v, preferred_element_type=F32)
    den_intra = jnp.sum(s, axis=1, keepdims=True)

    decay = jnp.exp(b_end + m_prev - m_new)
    wk = jnp.exp(b_end - b_col + ig_col - m_new) * k32
    wk_t = wk.T
    col_seq = lax.broadcasted_iota(jnp.int32, (DK, LT), 1) // T

    for bb in range(NB):
        r0 = bb * T
        cmat = c_ref[bb]
        nvec = n_ref[bb:bb + 1, :]
        qc_scr[r0:r0 + T, :] = jnp.dot(q32[r0:r0 + T, :].astype(BF16), cmat.astype(BF16),
                                       preferred_element_type=F32)
        ntok_scr[r0:r0 + T, :] = jnp.broadcast_to(nvec, (T, DK))
        upd = jnp.dot(jnp.where(col_seq == bb, wk_t, 0.0).astype(BF16), v, preferred_element_type=F32)
        dec = decay[r0:r0 + 1, :]
        cout_ref[bb] = dec * cmat + upd
        nout_ref[bb:bb + 1, :] = dec * nvec + jnp.sum(wk[r0:r0 + T, :], axis=0, keepdims=True)
        mout_ref[bb:bb + 1, :] = jnp.broadcast_to(m_new[r0:r0 + 1, :], (1, GATE_LANES))

    num = w_inter * qc_scr[...] + num_intra
    den = w_inter * jnp.sum(q32 * ntok_scr[...], axis=1, keepdims=True) + den_intra
    hg_ref[...] = _head_output(num, den, m_t, gout_ref[...], o_ref[...]).astype(hg_ref.dtype)


def _scan_s(p, gates, bias, gout, mtok, c, n_hm, *, T):
    nseq = c.shape[0]
    nb = SAMPLE_GROUP
    lt = nb * T
    kern = functools.partial(_scan_s_kernel, T=T, NB=nb)
    return pl.pallas_call(
        kern,
        grid=(nseq // nb, H),
        in_specs=[pl.BlockSpec((lt, DK), lambda g, h: (g, h)),
                  pl.BlockSpec((lt, DK), lambda g, h: (g, H + h)),
                  pl.BlockSpec((lt, DV), lambda g, h: (g, H + h)),
                  pl.BlockSpec((lt, DV), lambda g, h: (g, 2 * H + h)),
                  pl.BlockSpec((lt, GATE_LANES), lambda g, h: (g, 0)),
                  pl.BlockSpec((1, GATE_LANES), lambda g, h: (0, 0)),
                  pl.BlockSpec((1, DV), lambda g, h: (0, h)),
                  pl.BlockSpec((lt, GATE_LANES), lambda g, h: (g, 0)),
                  pl.BlockSpec((nb, None, DK, DV), lambda g, h: (g, h, 0, 0)),
                  pl.BlockSpec((None, nb, DK), lambda g, h: (h, g, 0))],
        out_specs=[pl.BlockSpec((lt, DV), lambda g, h: (g, h)),
                   pl.BlockSpec((nb, None, DK, DV), lambda g, h: (g, h, 0, 0)),
                   pl.BlockSpec((None, nb, DK), lambda g, h: (h, g, 0)),
                   pl.BlockSpec((None, nb, GATE_LANES), lambda g, h: (h, g, 0))],
        out_shape=[jax.ShapeDtypeStruct((nseq * T, VW), BF16),
                   jax.ShapeDtypeStruct((nseq, H, DK, DV), F32),
                   jax.ShapeDtypeStruct((H, nseq, DK), F32),
                   jax.ShapeDtypeStruct((H, nseq, GATE_LANES), F32)],
        scratch_shapes=[pltpu.VMEM((lt, DV), F32), pltpu.VMEM((lt, DK), F32)],
        compiler_params=_params(2),
        name="scan_sample",
    )(p, p, p, p, gates, bias, gout, mtok, c, n_hm)


def _mmres_cast_kernel(a_ref, w_ref, x_ref, o_ref, wc_ref):
    w = w_ref[...].astype(BF16)
    wc_ref[...] = w
    o_ref[...] = x_ref[...] + jnp.dot(a_ref[...], w, preferred_element_type=F32)


def _mmres_cast(a, w, x, *, tn):
    m, kdim = a.shape
    n = w.shape[2]
    return pl.pallas_call(
        _mmres_cast_kernel,
        grid=(n // tn,),
        in_specs=[pl.BlockSpec((m, kdim), lambda j: (0, 0)),
                  pl.BlockSpec((None, kdim, tn), lambda j: (0, 0, j)),
                  pl.BlockSpec((m, tn), lambda j: (0, j))],
        out_specs=[pl.BlockSpec((m, tn), lambda j: (0, j)),
                   pl.BlockSpec((kdim, tn), lambda j: (0, j))],
        out_shape=[jax.ShapeDtypeStruct((m, n), F32),
                   jax.ShapeDtypeStruct((kdim, n), BF16)],
        compiler_params=_params(1),
        name="out_proj_cast",
    )(a, w, x)


def _mmres_kernel(a_ref, w_ref, x_ref, o_ref):
    o_ref[...] = x_ref[...] + jnp.dot(a_ref[...], w_ref[...], preferred_element_type=F32)


def _mmres(a, w, x, *, tm):
    m, kdim = a.shape
    n = w.shape[1]
    return pl.pallas_call(
        _mmres_kernel,
        grid=(m // tm,),
        in_specs=[pl.BlockSpec((tm, kdim), lambda i: (i, 0)),
                  pl.BlockSpec((kdim, n), lambda i: (0, 0)),
                  pl.BlockSpec((tm, n), lambda i: (i, 0))],
        out_specs=pl.BlockSpec((tm, n), lambda i: (i, 0)),
        out_shape=jax.ShapeDtypeStruct((m, n), F32),
        compiler_params=_params(1),
        name="out_proj",
    )(a, w, x)


def _ffn_kernel(x_ref, gam_ref, wg_ref, wa0_ref, wa1_ref, cw_ref, cb_ref, wd_ref, st_ref, gfin_ref,
                o_ref, so_ref, wupc_ref, wdnc_ref,
                u_scr, gext_scr, a_scr, carry_scr, wup0_scr, wup1_scr, wdn0_scr, wdn1_scr,
                *, tm, sr, shift, tps, nsub, final_norm):
    i = pl.program_id(0)
    s = pl.program_id(1)
    wup_slots = (wup0_scr, wup1_scr)
    wdn_slots = (wdn0_scr, wdn1_scr)

    def cast_steps(slot):
        wup_scr = wup_slots[slot]
        wdn_scr = wdn_slots[slot]
        valid = D_FF - jnp.minimum(s, NF - 1) * TF

        def cast_gate():
            ok = lax.broadcasted_iota(jnp.int32, (1, TF), 1) < valid
            w = jnp.where(ok, wg_ref[...], 0.0).astype(BF16)
            wup_scr[:, 0:TF] = w
            wupc_ref[:, 0:TF] = w

        def cast_value():
            lane = lax.broadcasted_iota(jnp.int32, (1, LANES), 1)
            w0 = jnp.where(lane < valid, wa0_ref[...], 0.0).astype(BF16)
            w1 = jnp.where(lane + LANES < valid, wa1_ref[...], 0.0).astype(BF16)
            wup_scr[:, TF:TF + LANES] = w0
            wup_scr[:, TF + LANES:2 * TF] = w1
            wupc_ref[:, TF:TF + LANES] = w0
            wupc_ref[:, TF + LANES:2 * TF] = w1

        def cast_down(r0, rows):
            row_ok = r0 + lax.broadcasted_iota(jnp.int32, (rows, 1), 0) < valid
            w = jnp.where(row_ok, wd_ref[r0:r0 + rows, :], 0.0).astype(BF16)
            wdn_scr[r0:r0 + rows, :] = w
            wdnc_ref[r0:r0 + rows, :] = w

        half = TF // 2
        return [cast_gate, cast_value, functools.partial(cast_down, 0, half),
                functools.partial(cast_down, half, half)]

    def run_tile(slot, fillers=()):
        fillers = list(fillers)

        def fill():
            if fillers:
                fillers.pop(0)()

        wup_scr = wup_slots[slot]
        wdn_scr = wdn_slots[slot]
        f = s - 1
        col_ok = lax.broadcasted_iota(jnp.int32, (1, TF), 1) < D_FF - f * TF
        cw = cw_ref[...]
        cb = cb_ref[...]
        if tps == 1:
            gext_scr[0:sr, :] = st_ref[...]
        else:
            gext_scr[0:sr, :] = jnp.where((i % tps) == 0, st_ref[...], carry_scr[f])
        ts = tm // nsub
        for h in range(nsub):
            r0 = h * ts
            ga = jnp.dot(u_scr[r0:r0 + ts, :], wup_scr[...], preferred_element_type=F32)
            gext_scr[sr + r0:sr + r0 + ts, :] = ga[:, 0:TF]
            a_scr[r0:r0 + ts, :] = ga[:, TF:2 * TF]
            fill()
        for h in range(nsub):
            r0 = h * ts
            g = gext_scr[sr + r0:sr + r0 + ts, :]
            g_m2 = gext_scr[sr + r0 - 2 * shift:sr + r0 - 2 * shift + ts, :]
            g_m1 = gext_scr[sr + r0 - shift:sr + r0 - shift + ts, :]
            gc = cb + ((cw[0:1, :] * g_m2 + cw[1:2, :] * g_m1) + cw[2:3, :] * g)
            hmid = jnp.where(col_ok, (gc * jax.nn.sigmoid(gc)) * a_scr[r0:r0 + ts, :], 0.0)
            o_ref[r0:r0 + ts, :] += jnp.dot(hmid.astype(BF16), wdn_scr[...], preferred_element_type=F32)
            fill()
        while fillers:
            fill()
        g_tail = gext_scr[tm:tm + sr, :]
        so_ref[...] = g_tail
        if tps > 1:
            carry_scr[f] = g_tail

    @pl.when(s == 0)
    def _():
        x = x_ref[...]
        u_scr[...] = _rms(x, gam_ref[...]).astype(BF16)
        o_ref[...] = x
        for step in cast_steps(0):
            step()
        if tps > 1:
            @pl.when(i == 0)
            def _():
                carry_scr[...] = jnp.zeros_like(carry_scr)

    for parity in range(2):
        @pl.when(jnp.logical_and(s > 0, s % 2 == parity))
        def _(parity=parity):
            run_tile(1 - parity, cast_steps(parity))

    if final_norm:
        @pl.when(s == NF)
        def _():
            o_ref[...] = _rms(o_ref[...], gfin_ref[...])


def _ffn(x, layer, gamma, w_up, conv_w, conv_b, w_down, st, gfin, *, tm, sr, shift, tps, st_per_tile,
         final_norm):
    assert TF == 2 * LANES
    m = x.shape[0]
    nt = m // tm
    nsub = 2 if tm >= 64 else 1
    kern = functools.partial(_ffn_kernel, tm=tm, sr=sr, shift=shift, tps=tps, nsub=nsub,
                             final_norm=final_norm)
    last_a = 2 * FF_LANE_BLOCKS - 1

    def wt(s):
        return jnp.minimum(s, NF - 1)

    def ft(s):
        return jnp.maximum(s - 1, 0)

    st_map = (lambda i, s: (i, ft(s))) if st_per_tile else (lambda i, s: (0, ft(s)))
    return pl.pallas_call(
        kern,
        grid=(nt, NF + 1),
        in_specs=[pl.BlockSpec((tm, D), lambda i, s: (i, 0), pipeline_mode=pl.Buffered(1)),
                  pl.BlockSpec((1, D), lambda i, s: (0, 0)),
                  pl.BlockSpec((None, D, TF), lambda i, s: (layer, 0, wt(s))),
                  pl.BlockSpec((None, D, LANES), lambda i, s: (layer, 0, FF_LANE_BLOCKS + 2 * wt(s))),
                  pl.BlockSpec((None, D, LANES),
                               lambda i, s: (layer, 0, jnp.minimum(FF_LANE_BLOCKS + 2 * wt(s) + 1, last_a))),
                  pl.BlockSpec((None, 3, TF), lambda i, s: (layer, 0, ft(s))),
                  pl.BlockSpec((None, 1, TF), lambda i, s: (layer, 0, ft(s))),
                  pl.BlockSpec((None, TF, D), lambda i, s: (layer, wt(s), 0)),
                  pl.BlockSpec((sr, TF), st_map),
                  pl.BlockSpec((1, D), lambda i, s: (0, 0))],
        out_specs=[pl.BlockSpec((tm, D), lambda i, s: (i, 0)),
                   pl.BlockSpec((sr, TF), lambda i, s: (i, ft(s))),
                   pl.BlockSpec((D, 2 * TF), lambda i, s: (0, wt(s))),
                   pl.BlockSpec((TF, D), lambda i, s: (wt(s), 0))],
        out_shape=[jax.ShapeDtypeStruct((m, D), F32),
                   jax.ShapeDtypeStruct((nt * sr, D_FF), F32),
                   jax.ShapeDtypeStruct((D, NF * 2 * TF), BF16),
                   jax.ShapeDtypeStruct((NF * TF, D), BF16)],
        scratch_shapes=[pltpu.VMEM((tm, D), BF16),
                        pltpu.VMEM((sr + tm, TF), F32),
                        pltpu.VMEM((tm, TF), F32),
                        pltpu.VMEM((NF, sr, TF), F32),
                        pltpu.VMEM((D, 2 * TF), BF16),
                        pltpu.VMEM((D, 2 * TF), BF16),
                        pltpu.VMEM((TF, D), BF16),
                        pltpu.VMEM((TF, D), BF16)],
        compiler_params=_params(2),
        name="conv_ffn_cast",
    )(x, gamma, w_up, w_up, w_up, conv_w, conv_b, w_down, st, gfin)


def _ffn_bf16_kernel(x_ref, gam_ref, wup_ref, cw_ref, cb_ref, wdn_ref, st_ref, gfin_ref,
                     o_ref, so_ref, u_scr, gext_scr, a_scr, carry_scr,
                     *, tm, sr, shift, tps, nsub, final_norm):
    i = pl.program_id(0)
    f = pl.program_id(1)
    tf2 = 2 * TF

    @pl.when(f == 0)
    def _():
        x = x_ref[...]
        u_scr[...] = _rms(x, gam_ref[...]).astype(BF16)
        o_ref[...] = x
        if tps > 1:
            @pl.when(i == 0)
            def _():
                carry_scr[...] = jnp.zeros_like(carry_scr)

    col_ok = lax.broadcasted_iota(jnp.int32, (1, tf2), 1) < D_FF - f * tf2
    cw = cw_ref[...]
    cb = cb_ref[...]
    if tps == 1:
        gext_scr[0:sr, :] = st_ref[...]
    else:
        gext_scr[0:sr, :] = jnp.where((i % tps) == 0, st_ref[...], carry_scr[f])
    ts = tm // nsub
    for h in range(nsub):
        r0 = h * ts
        ga = jnp.dot(u_scr[r0:r0 + ts, :], wup_ref[...], preferred_element_type=F32)
        gext_scr[sr + r0:sr + r0 + ts, 0:TF] = ga[:, 0:TF]
        gext_scr[sr + r0:sr + r0 + ts, TF:tf2] = ga[:, 2 * TF:3 * TF]
        a_scr[r0:r0 + ts, 0:TF] = ga[:, TF:2 * TF]
        a_scr[r0:r0 + ts, TF:tf2] = ga[:, 3 * TF:4 * TF]
    for h in range(nsub):
        r0 = h * ts
        g = gext_scr[sr + r0:sr + r0 + ts, :]
        g_m2 = gext_scr[sr + r0 - 2 * shift:sr + r0 - 2 * shift + ts, :]
        g_m1 = gext_scr[sr + r0 - shift:sr + r0 - shift + ts, :]
        gc = cb + ((cw[0:1, :] * g_m2 + cw[1:2, :] * g_m1) + cw[2:3, :] * g)
        hmid = jnp.where(col_ok, (gc * jax.nn.sigmoid(gc)) * a_scr[r0:r0 + ts, :], 0.0)
        o_ref[r0:r0 + ts, :] += jnp.dot(hmid.astype(BF16), wdn_ref[...], preferred_element_type=F32)
    g_tail = gext_scr[tm:tm + sr, :]
    so_ref[...] = g_tail
    if tps > 1:
        carry_scr[f] = g_tail

    if final_norm:
        @pl.when(f == pl.num_programs(1) - 1)
        def _():
            o_ref[...] = _rms(o_ref[...], gfin_ref[...])


def _ffn_bf16(x, layer, gamma, wupc, conv_w, conv_b, wdnc, st, gfin, *, tm, sr, shift, tps, st_per_tile,
              final_norm):
    assert NF % 2 == 0
    m = x.shape[0]
    nt = m // tm
    tf2 = 2 * TF
    nf2 = NF // 2
    kern = functools.partial(_ffn_bf16_kernel, tm=tm, sr=sr, shift=shift, tps=tps,
                             nsub=2 if tm >= 64 else 1, final_norm=final_norm)
    st_map = (lambda i, f: (i, f)) if st_per_tile else (lambda i, f: (0, f))
    return pl.pallas_call(
        kern,
        grid=(nt, nf2),
        in_specs=[pl.BlockSpec((tm, D), lambda i, f: (i, 0)),
                  pl.BlockSpec((1, D), lambda i, f: (0, 0)),
                  pl.BlockSpec((D, 2 * tf2), lambda i, f: (0, f)),
                  pl.BlockSpec((None, 3, tf2), lambda i, f: (layer, 0, f)),
                  pl.BlockSpec((None, 1, tf2), lambda i, f: (layer, 0, f)),
                  pl.BlockSpec((tf2, D), lambda i, f: (f, 0)),
                  pl.BlockSpec((sr, tf2), st_map),
                  pl.BlockSpec((1, D), lambda i, f: (0, 0))],
        out_specs=[pl.BlockSpec((tm, D), lambda i, f: (i, 0)),
                   pl.BlockSpec((sr, tf2), lambda i, f: (i, f))],
        out_shape=[jax.ShapeDtypeStruct((m, D), F32),
                   jax.ShapeDtypeStruct((nt * sr, D_FF), F32)],
        scratch_shapes=[pltpu.VMEM((tm, D), BF16),
                        pltpu.VMEM((sr + tm, tf2), F32),
                        pltpu.VMEM((tm, tf2), F32),
                        pltpu.VMEM((nf2, sr, tf2), F32)],
        compiler_params=_params(2),
        name="conv_ffn",
    )(x, gamma, wupc, conv_w, conv_b, wdnc, st, gfin)


def _window_sum(ext_scr, tmp_scrs, base, tm, shift, w):
    starts = {w: base}
    v = w
    while v > 2:
        starts[v // 2] = (starts[v] - (v // 2) * shift) // 8 * 8
        v //= 2
    src = ext_scr
    v = 1
    k = 0
    while True:
        lo = starts[2 * v]
        n = base + tm - lo
        val = src[lo:lo + n, :] + src[lo - v * shift:lo - v * shift + n, :]
        v *= 2
        if v == w:
            return val
        dst = tmp_scrs[k % 2]
        dst[lo:lo + n, :] = val
        src = dst
        k += 1


def _pool_kernel(x_ref, gam_ref, wp_ref, sc_ref, st_ref, o_ref, ut_ref, rinv_scr, uext_scr, tmp0_scr, tmp1_scr,
                 carry_scr,
                 *, tm, shift, tps, pos0, tr):
    i = pl.program_id(0)
    grp = pl.program_id(1)
    hr = POOL_HALO * shift
    base = POOL_PAD_ROWS + hr

    @pl.when(grp == 0)
    def _():
        x = x_ref[...]
        rinv_scr[...] = lax.rsqrt(jnp.mean(x * x, axis=1, keepdims=True) + EPS)
        if tps > 1:
            @pl.when(i == 0)
            def _():
                carry_scr[...] = jnp.zeros_like(carry_scr)

    for kk, w in enumerate(POOL_WINDOWS):
        @pl.when(grp == kk)
        def _(kk=kk, w=w):
            cs = slice(kk * PG, (kk + 1) * PG)
            xg = x_ref[:, cs]
            ug = xg * rinv_scr[...] * gam_ref[:, cs]
            uext_scr[0:POOL_PAD_ROWS, :] = jnp.zeros((POOL_PAD_ROWS, PG), F32)
            if tps == 1:
                uext_scr[POOL_PAD_ROWS:base, :] = st_ref[...]
            else:
                uext_scr[POOL_PAD_ROWS:base, :] = jnp.where((i % tps) == 0, st_ref[...], carry_scr[kk])
            uext_scr[base:base + tm, :] = ug
            acc = _window_sum(uext_scr, (tmp0_scr, tmp1_scr), base, tm, shift, w)
            if pos0 + 1 >= w:
                pooled = acc / float(w) - ug
            else:
                step = (i % tps) * (tm // shift) + lax.broadcasted_iota(jnp.int32, (tm, 1), 0) // shift
                cnt = jnp.minimum(w, pos0 + step + 1).astype(F32)
                pooled = acc / cnt - ug
            y = jnp.dot(pooled.astype(BF16), wp_ref[kk].astype(BF16), preferred_element_type=F32)
            o_ref[...] = xg + y * sc_ref[:, cs]
            ut_ref[...] = ug[tm - tr:tm, :]
            if tps > 1:
                carry_scr[kk] = ug[tm - hr:tm, :]


def _pool(x, gamma, wp, sc, st, *, tm, shift, tps, pos0, tr, st_per_tile):
    m = x.shape[0]
    nt = m // tm
    hr = POOL_HALO * shift
    ng = len(POOL_WINDOWS)
    kern = functools.partial(_pool_kernel, tm=tm, shift=shift, tps=tps, pos0=pos0, tr=tr)
    st_map = (lambda i, g: (i, g)) if st_per_tile else (lambda i, g: (0, g))
    return pl.pallas_call(
        kern,
        grid=(nt, ng),
        in_specs=[pl.BlockSpec((tm, D), lambda i, g: (i, 0)),
                  pl.BlockSpec((1, D), lambda i, g: (0, 0)),
                  pl.BlockSpec((None, ng, PG, PG), lambda i, g: (0, 0, 0, 0)),
                  pl.BlockSpec((1, D), lambda i, g: (0, 0)),
                  pl.BlockSpec((hr, PG), st_map)],
        out_specs=[pl.BlockSpec((tm, PG), lambda i, g: (i, g)),
                   pl.BlockSpec((tr, PG), lambda i, g: (i, g))],
        out_shape=[jax.ShapeDtypeStruct((m, D), F32),
                   jax.ShapeDtypeStruct((nt * tr, D), F32)],
        scratch_shapes=[pltpu.VMEM((tm, 1), F32)]
        + [pltpu.VMEM((POOL_PAD_ROWS + hr + tm, PG), F32)] * 3
        + [pltpu.VMEM((ng, hr, PG), F32)],
        compiler_params=_params(2),
        name="pool_mixer",
    )(x, gamma, wp, sc, st)


def _pool_sample_kernel(*refs, nseq, t_dec):
    nh = POOL_HALO - 1
    x_ref, gam_ref, wp_ref, sc_ref = refs[0:4]
    st_refs = refs[4:4 + nh]
    o_ref, ut_ref, rinv_scr, uext_scr = refs[4 + nh:]
    grp = pl.program_id(0)
    tm = nseq * t_dec
    hr = POOL_HALO * nseq

    @pl.when(grp == 0)
    def _():
        x = x_ref[...]
        rinv_scr[...] = lax.rsqrt(jnp.mean(x * x, axis=1, keepdims=True) + EPS)

    for kk, w in enumerate(POOL_WINDOWS):
        @pl.when(grp == kk)
        def _(kk=kk, w=w):
            cs = slice(kk * PG, (kk + 1) * PG)
            xg = x_ref[:, cs]
            ug = xg * rinv_scr[...] * gam_ref[:, cs]
            for t in range(nh):
                uext_scr[(t + 1) * nseq:(t + 2) * nseq, :] = st_refs[t][...]
            uext_scr[hr:hr + tm, :] = ug
            acc = ug
            for j in range(1, w):
                acc = acc + uext_scr[hr - j * nseq:hr - j * nseq + tm, :]
            pooled = acc / float(w) - ug
            y = jnp.dot(pooled.astype(BF16), wp_ref[kk].astype(BF16), preferred_element_type=F32)
            o_ref[...] = xg + y * sc_ref[:, cs]
            ut_ref[...] = ug


def _pool_sample(x, gamma, wp, sc, state, *, nseq, t_dec):
    assert PAST_LEN + 1 >= max(POOL_WINDOWS)
    nh = POOL_HALO - 1
    ng = len(POOL_WINDOWS)
    tm = nseq * t_dec
    hr = POOL_HALO * nseq
    kern = functools.partial(_pool_sample_kernel, nseq=nseq, t_dec=t_dec)
    hist_specs = [pl.BlockSpec((nseq, PG), lambda g, t=t: (0, t * ng + g)) for t in range(nh)]
    return pl.pallas_call(
        kern,
        grid=(ng,),
        in_specs=[pl.BlockSpec((tm, D), lambda g: (0, 0)),
                  pl.BlockSpec((1, D), lambda g: (0, 0)),
                  pl.BlockSpec((None, ng, PG, PG), lambda g: (0, 0, 0, 0)),
                  pl.BlockSpec((1, D), lambda g: (0, 0))] + hist_specs,
        out_specs=[pl.BlockSpec((tm, PG), lambda g: (0, g)),
                   pl.BlockSpec((tm, PG), lambda g: (0, g))],
        out_shape=[jax.ShapeDtypeStruct((tm, D), F32),
                   jax.ShapeDtypeStruct((tm, D), F32)],
        scratch_shapes=[pltpu.VMEM((tm, 1), F32),
                        pltpu.VMEM((hr + tm, PG), F32)],
        compiler_params=_params(1),
        name="pool_mixer_sample",
    )(x, gamma, wp, sc, *([state] * nh))


def _pad_cols(a, n):
    return jnp.pad(a, ((0, 0), (0, n - a.shape[1])))


def _to_time_major(a, t):
    nseq = a.shape[0]
    a = a.reshape((nseq // SAMPLE_TILE_SEQS, SAMPLE_TILE_SEQS, t) + a.shape[2:])
    a = jnp.swapaxes(a, 1, 2)
    return a.reshape((nseq * t,) + a.shape[3:])


def _from_time_major(a, nseq, t):
    a = a.reshape((nseq // SAMPLE_TILE_SEQS, t, SAMPLE_TILE_SEQS) + a.shape[1:])
    a = jnp.swapaxes(a, 1, 2)
    return a.reshape((nseq, t) + a.shape[3:])


def kernel(x_prompt, x_sample, state_mlstm_C, state_mlstm_n, state_mlstm_m, state_pool, state_ffn_conv,
           meta_tokens, norm_mix, norm_ffn, norm_final, w_mlstm_in, b_mlstm_gate, g_mlstm_out, w_mlstm_out,
           w_pool, pool_scale, w_up, conv_w, conv_b, w_down):
    bsz, seq, _ = x_prompt.shape
    nseq, t_dec, _ = x_sample.shape

    w_main = w_mlstm_in[0][:, :PW].astype(BF16)
    w_gate = _pad_cols(w_mlstm_in[0][:, PW:].astype(BF16), GATE_LANES)
    bias = _pad_cols(b_mlstm_gate[0][None, :], GATE_LANES)
    w_out = w_mlstm_out
    gout = g_mlstm_out[0][None, :]
    wp = w_pool
    psc = pool_scale[0][None, :]
    gfin = norm_final[None, :]
    conv_b3 = conv_b[:, None, :]
    ffn_w = [(layer, norm_ffn[layer][None, :], w_up, conv_w, conv_b3, w_down) for layer in range(2)]
    g_mix0 = norm_mix[0][None, :]
    g_mix1 = norm_mix[1][None, :]

    ffn_cache = {}
    mix_cache = {}

    def proj(x, **kw):
        return _proj(x, g_mix0, w_main, w_gate, **kw)

    def out_proj(a, x, *, tm):
        if "w_out" not in mix_cache:
            y, mix_cache["w_out"] = _mmres_cast(a, w_out, x, tn=512)
            return y
        return _mmres(a, mix_cache["w_out"], x, tm=tm)

    def ffn(x, layer, st, **kw):
        lyr, gamma, wu, cwt, cbs, wd = ffn_w[layer]
        if layer not in ffn_cache:
            y, cs, wupc, wdnc = _ffn(x, lyr, gamma, wu, cwt, cbs, wd, st, gfin, **kw)
            ffn_cache[layer] = (wupc, wdnc)
            return y, cs
        wupc, wdnc = ffn_cache[layer]
        return _ffn_bf16(x, lyr, gamma, wupc, cwt, cbs, wdnc, st, gfin, **kw)

    def long_stream(x, nb, s, st, *, tm, chunk, lead_pad, pos0):
        c0, n0, m0, conv0, pool0, conv1 = st
        tm_ffn = tm
        p, gates = proj(x, tm=tm, tn=1024, out_dtype=BF16)
        p = p.reshape(nb, s, PW)
        gates = gates.reshape(nb, s, GATE_LANES)
        if lead_pad:
            p = jnp.pad(p, ((0, 0), (lead_pad, 0), (0, 0)))
            gates = jnp.pad(gates, ((0, 0), (lead_pad, 0), (0, 0)))
        hg, c_new, n_new, m_new = _scan(p, gates, bias, gout, c0, n0, m0, L=chunk, lead_pad=lead_pad)
        hg = hg[:, lead_pad:].reshape(nb * s, VW)
        x1 = out_proj(hg, x, tm=min(tm, 512))
        x2, cs0 = ffn(x1, 0, conv0, tm=tm_ffn, sr=8, shift=1, tps=s // tm_ffn,
                      st_per_tile=False, final_norm=False)
        x3, ut = _pool(x2, g_mix1, wp, psc, pool0, tm=tm, shift=1, tps=s // tm, pos0=pos0,
                       tr=POOL_HALO, st_per_tile=False)
        y, cs1 = ffn(x3, 1, conv1, tm=tm_ffn, sr=8, shift=1, tps=s // tm_ffn,
                     st_per_tile=False, final_norm=True)
        cs0 = cs0.reshape(nb, s // tm_ffn, 8, D_FF)[:, -1]
        cs1 = cs1.reshape(nb, s // tm_ffn, 8, D_FF)[:, -1]
        ut = ut.reshape(nb, s // tm, POOL_HALO, D)[:, -1]
        return y, (c_new, n_new, m_new, cs0, ut, cs1)

    def prompt_streams():
        zero_state = (jnp.zeros((H, DK, DV), F32), jnp.zeros((H, 8, DK), F32), jnp.zeros((H, 8, GATE_LANES), F32),
                      jnp.zeros((8, D_FF), F32), jnp.zeros((POOL_HALO, D), F32), jnp.zeros((8, D_FF), F32))
        _, (c_m, n_m, m_m, cs0_m, ut_m, cs1_m) = long_stream(
            meta_tokens, 1, N_META, zero_state, tm=N_META, chunk=128, lead_pad=128 - N_META, pos0=0)

        y_p, (c_p, n_p, m_p, cs0_p, ut_p, cs1_p) = long_stream(
            x_prompt.reshape(bsz * seq, D), bsz, seq, (c_m[0], n_m[0], m_m[0], cs0_m[0], ut_m[0], cs1_m[0]),
            tm=1024, chunk=SCAN_CHUNK, lead_pad=0, pos0=N_META)
        return (y_p.reshape(bsz, seq, D), c_p[None], n_p[:, :, 0][None], m_p[:, :, 0, 0][None],
                ut_p[:, 1:][None], jnp.stack([cs0_p[:, 6:], cs1_p[:, 6:]]))

    xs = x_sample.reshape(nseq * t_dec, D)
    p_s, gates_s = proj(xs, tm=nseq * t_dec, tn=512, out_dtype=F32)
    mtok = _pad_cols(jnp.repeat(state_mlstm_m[0], t_dec, axis=0), GATE_LANES)
    n_hm = jnp.swapaxes(state_mlstm_n[0], 0, 1)
    hg_s, C_s, n_s_hm, m_s_hm = _scan_s(p_s, gates_s, bias, gout, mtok, state_mlstm_C[0], n_hm, T=t_dec)
    assert nseq == SAMPLE_TILE_SEQS
    tm_s = SAMPLE_TILE_SEQS * t_dec
    hg_t = _to_time_major(hg_s.reshape(nseq, t_dec, VW), t_dec)
    x_t = _to_time_major(x_sample, t_dec)
    x1 = out_proj(hg_t, x_t, tm=512)

    def conv_state_in(cs):
        return _to_time_major(cs, 2)

    def conv_state_out(cs):
        return _from_time_major(cs, nseq, 2)

    sr_s = 2 * SAMPLE_TILE_SEQS
    x2, cs0_s = ffn(x1, 0, conv_state_in(state_ffn_conv[0]), tm=tm_s, sr=sr_s,
                    shift=SAMPLE_TILE_SEQS, tps=1, st_per_tile=True, final_norm=False)
    x3, ut_s = _pool_sample(x2, g_mix1, wp, psc, state_pool[0].reshape(nseq, (POOL_HALO - 1) * D),
                            nseq=nseq, t_dec=t_dec)
    y_s, cs1_s = ffn(x3, 1, conv_state_in(state_ffn_conv[1]), tm=tm_s, sr=sr_s,
                     shift=SAMPLE_TILE_SEQS, tps=1, st_per_tile=True, final_norm=True)
    y_sample = _from_time_major(y_s, nseq, t_dec)
    n_s = jnp.swapaxes(n_s_hm, 0, 1)[None]
    m_s = jnp.swapaxes(m_s_hm[:, :, 0], 0, 1)[None]
    pool_s = jnp.concatenate([state_pool[0][:, t_dec:], _from_time_major(ut_s, nseq, t_dec)], axis=1)[None]
    conv_s = jnp.stack([conv_state_out(cs0_s), conv_state_out(cs1_s)])

    y_prompt, C_p, n_p, m_p, pool_p, conv_p = prompt_streams()
    return (y_prompt, y_sample, C_p, n_p, m_p, pool_p, conv_p,
            C_s[None], n_s, m_s, pool_s, conv_s)
```

```python
import functools

import jax
import jax.numpy as jnp
from jax import lax
from jax.experimental import pallas as pl
from jax.experimental.pallas import tpu as pltpu

F32 = jnp.float32
BF16 = jnp.bfloat16

EPS = 1e-6
D = 2048
H = 4
DK = 256
DV = 512
QKW = H * DK
VW = H * DV
PW = 2 * QKW + 2 * VW
GATE_LANES = 128
SCALE = DK ** -0.5
POOL_WINDOWS = (2, 4, 8, 16)
PG = D // len(POOL_WINDOWS)
POOL_HALO = 16
POOL_PAD_ROWS = 16
D_FF = 5504
TF = 256
NF = -(-D_FF // TF)
LANES = 128
FF_LANE_BLOCKS = D_FF // LANES
CONV_TAIL = 2
N_META = 16
PAST_LEN = 16384
SCAN_CHUNK = 256
SAMPLE_GROUP = 16
SAMPLE_TILE_SEQS = 128
VMEM_LIMIT = 60 * 1024 * 1024


def _params(n_axes):
    return pltpu.CompilerParams(dimension_semantics=("arbitrary",) * n_axes,
                                vmem_limit_bytes=VMEM_LIMIT)


def _rms(x, g):
    return x * lax.rsqrt(jnp.mean(x * x, axis=-1, keepdims=True) + EPS) * g


def _log_sigmoid(x):
    return jnp.minimum(x, 0.0) - jnp.log(1.0 + jnp.exp(-jnp.abs(x)))


_NT = (((1,), (1,)), ((), ()))


def _proj_kernel(x_ref, g_ref, wt_ref, wgt_ref, p_ref, gate_ref, *rest, emit_bf16):
    u_scr = rest[-1]

    @pl.when(pl.program_id(1) == 0)
    def _():
        ub = _rms(x_ref[...], g_ref[...]).astype(BF16)
        u_scr[...] = ub
        wg = wgt_ref[...]
        if emit_bf16:
            row_ok = lax.broadcasted_iota(jnp.int32, (GATE_LANES, 1), 0) < 2 * H
            wg = jnp.where(row_ok, wg, 0.0).astype(BF16)
            rest[1][...] = wg
        gate_ref[...] = lax.dot_general(ub, wg, _NT, preferred_element_type=F32)

    w = wt_ref[...].astype(BF16)
    if emit_bf16:
        rest[0][...] = w
    p_ref[...] = lax.dot_general(u_scr[...], w, _NT, preferred_element_type=F32).astype(p_ref.dtype)


def _proj(x, gamma, wt, wgt, *, tm, tn, out_dtype):
    m = x.shape[0]
    emit = wt.dtype != BF16
    gate_spec = pl.BlockSpec((GATE_LANES, D), (lambda i, j: (PW // GATE_LANES, 0)) if emit
                             else (lambda i, j: (0, 0)))
    out_specs = [pl.BlockSpec((tm, tn), lambda i, j: (i, j)),
                 pl.BlockSpec((tm, GATE_LANES), lambda i, j: (i, 0))]
    out_shape = [jax.ShapeDtypeStruct((m, PW), out_dtype),
                 jax.ShapeDtypeStruct((m, GATE_LANES), F32)]
    if emit:
        out_specs += [pl.BlockSpec((tn, D), lambda i, j: (j, 0)),
                      pl.BlockSpec((GATE_LANES, D), lambda i, j: (0, 0))]
        out_shape += [jax.ShapeDtypeStruct((PW, D), BF16),
                      jax.ShapeDtypeStruct((GATE_LANES, D), BF16)]
    return pl.pallas_call(
        functools.partial(_proj_kernel, emit_bf16=emit),
        grid=(m // tm, PW // tn),
        in_specs=[pl.BlockSpec((tm, D), lambda i, j: (i, 0)),
                  pl.BlockSpec((1, D), lambda i, j: (0, 0)),
                  pl.BlockSpec((tn, D), lambda i, j: (j, 0)),
                  gate_spec],
        out_specs=out_specs,
        out_shape=out_shape,
        scratch_shapes=[pltpu.VMEM((tm, D), BF16)],
        compiler_params=_params(2),
        name="proj_cast" if emit else "proj",
    )(x, gamma, wt, wgt)


def _head_output(num, den, m_t, gout, o):
    hv = num * (1.0 / jnp.maximum(jnp.abs(den), jnp.exp(-m_t)))
    hv = hv * lax.rsqrt(jnp.mean(hv * hv, axis=1, keepdims=True) + EPS)
    return hv * gout * jax.nn.sigmoid(o.astype(F32))


def _scan_kernel(q_ref, k_ref, v_ref, o_ref, gt_ref, bias_ref, gout_ref, c0_ref, n0_ref, m0_ref,
                 hg_ref, cout_ref, nout_ref, mout_ref, c_scr, n_scr, m_scr, *, L, lead_pad, nc):
    c = pl.program_id(1)

    @pl.when(c == 0)
    def _():
        c_scr[...] = c0_ref[...]
        n_scr[...] = n0_ref[...]
        m_scr[...] = m0_ref[...]

    gates = gt_ref[...] + bias_ref[...]
    gates_t = gates.T
    row = lax.broadcasted_iota(jnp.int32, (L, L), 0)
    col = lax.broadcasted_iota(jnp.int32, (L, L), 1)
    causal = row >= col
    causal_t = row <= col
    if lead_pad:
        live_col = lax.broadcasted_iota(jnp.int32, (L, 1), 0) >= lead_pad
        live_row = lax.broadcasted_iota(jnp.int32, (1, L), 1) >= lead_pad

    for hh in range(H):
        ig_col = gates[:, hh:hh + 1]
        ig_row = gates_t[hh:hh + 1, :]
        lf_col = _log_sigmoid(gates[:, H + hh:H + hh + 1])
        lf_row = _log_sigmoid(gates_t[H + hh:H + hh + 1, :])
        if lead_pad:
            ig_col = jnp.where(live_col, ig_col, -jnp.inf)
            ig_row = jnp.where(live_row, ig_row, -jnp.inf)
            lf_col = jnp.where(live_col, lf_col, 0.0)
            lf_row = jnp.where(live_row, lf_row, 0.0)
        b_col = jnp.sum(jnp.where(causal, lf_row, 0.0), axis=1, keepdims=True)
        b_row = jnp.sum(jnp.where(causal_t, lf_col, 0.0), axis=0, keepdims=True)
        m_prev = m_scr[hh, 0:1, 0:1]
        d = jnp.where(causal, b_col - b_row + ig_row, -jnp.inf)
        inter = b_col + m_prev
        m_t = jnp.maximum(inter, jnp.max(d, axis=1, keepdims=True))
        w_inter = jnp.exp(inter - m_t) * SCALE

        q = q_ref[:, hh * DK:(hh + 1) * DK]
        k = k_ref[:, hh * DK:(hh + 1) * DK]
        v = v_ref[:, hh * DV:(hh + 1) * DV]
        qk = lax.dot_general(q, k, (((1,), (1,)), ((), ())), preferred_element_type=F32)
        s = qk * (jnp.exp(d - m_t) * SCALE)
        cmat = c_scr[hh]
        nvec = n_scr[hh, 0:1, :]
        num = w_inter * jnp.dot(q, cmat.astype(BF16), preferred_element_type=F32) \
            + jnp.dot(s.astype(BF16), v, preferred_element_type=F32)
        den = w_inter * jnp.sum(q.astype(F32) * nvec, axis=1, keepdims=True) \
            + jnp.sum(s, axis=1, keepdims=True)
        hout = _head_output(num, den, m_t, gout_ref[:, hh * DV:(hh + 1) * DV],
                            o_ref[:, hh * DV:(hh + 1) * DV])
        hg_ref[:, hh * DV:(hh + 1) * DV] = hout.astype(hg_ref.dtype)

        m_new = m_t[L - 1:L, :]
        b_last = b_col[L - 1:L, :]
        decay = jnp.exp(b_last + m_prev - m_new)
        wk = jnp.exp(b_last - b_col + ig_col - m_new) * k.astype(F32)
        c_scr[hh] = decay * cmat + jnp.dot(wk.T.astype(BF16), v, preferred_element_type=F32)
        n_scr[hh] = jnp.broadcast_to(decay * nvec + jnp.sum(wk, axis=0, keepdims=True), (8, DK))
        m_scr[hh] = jnp.broadcast_to(m_new, (8, GATE_LANES))

    @pl.when(c == nc - 1)
    def _():
        cout_ref[...] = c_scr[...]
        nout_ref[...] = n_scr[...]
        mout_ref[...] = m_scr[...]


def _scan(p, gates, bias, gout, c0, n0, m0, *, L, lead_pad=0):
    b, s, _ = p.shape
    nc = s // L
    kern = functools.partial(_scan_kernel, L=L, lead_pad=lead_pad, nc=nc)
    return pl.pallas_call(
        kern,
        grid=(b, nc),
        in_specs=[pl.BlockSpec((None, L, QKW), lambda i, c: (i, c, 0)),
                  pl.BlockSpec((None, L, QKW), lambda i, c: (i, c, 1)),
                  pl.BlockSpec((None, L, VW), lambda i, c: (i, c, 1)),
                  pl.BlockSpec((None, L, VW), lambda i, c: (i, c, 2)),
                  pl.BlockSpec((None, L, GATE_LANES), lambda i, c: (i, c, 0)),
                  pl.BlockSpec((1, GATE_LANES), lambda i, c: (0, 0)),
                  pl.BlockSpec((1, VW), lambda i, c: (0, 0)),
                  pl.BlockSpec((H, DK, DV), lambda i, c: (0, 0, 0)),
                  pl.BlockSpec((H, 8, DK), lambda i, c: (0, 0, 0)),
                  pl.BlockSpec((H, 8, GATE_LANES), lambda i, c: (0, 0, 0))],
        out_specs=[pl.BlockSpec((None, L, VW), lambda i, c: (i, c, 0)),
                   pl.BlockSpec((None, H, DK, DV), lambda i, c: (i, 0, 0, 0)),
                   pl.BlockSpec((None, H, 8, DK), lambda i, c: (i, 0, 0, 0)),
                   pl.BlockSpec((None, H, 8, GATE_LANES), lambda i, c: (i, 0, 0, 0))],
        out_shape=[jax.ShapeDtypeStruct((b, s, VW), BF16),
                   jax.ShapeDtypeStruct((b, H, DK, DV), F32),
                   jax.ShapeDtypeStruct((b, H, 8, DK), F32),
                   jax.ShapeDtypeStruct((b, H, 8, GATE_LANES), F32)],
        scratch_shapes=[pltpu.VMEM((H, DK, DV), F32),
                        pltpu.VMEM((H, 8, DK), F32),
                        pltpu.VMEM((H, 8, GATE_LANES), F32)],
        compiler_params=_params(2),
        name="scan",
    )(p, p, p, p, gates, bias, gout, c0, n0, m0)


def _scan_s_kernel(q_ref, k_ref, v_ref, o_ref, gt_ref, bias_ref, gout_ref, mtok_ref, c_ref, n_ref,
                   hg_ref, cout_ref, nout_ref, mout_ref, qc_scr, ntok_scr, ho_scr, *, T, NB):
    hh = pl.program_id(1)
    LT = NB * T
    gates = gt_ref[...] + bias_ref[...]
    gates_t = gates.T
    lane = lax.broadcasted_iota(jnp.int32, (LT, GATE_LANES), 1)
    sub = lax.broadcasted_iota(jnp.int32, (GATE_LANES, LT), 0)

    def pick_col(a, idx):
        return jnp.sum(jnp.where(lane == idx, a, 0.0), axis=1, keepdims=True)

    def pick_row(a, idx):
        return jnp.sum(jnp.where(sub == idx, a, 0.0), axis=0, keepdims=True)

    ig_col = pick_col(gates, hh)
    ig_row = pick_row(gates_t, hh)
    lf_col = _log_sigmoid(pick_col(gates, hh + H))
    lf_row = _log_sigmoid(pick_row(gates_t, hh + H))
    m_prev = pick_col(mtok_ref[...], hh)

    row = lax.broadcasted_iota(jnp.int32, (LT, LT), 0)
    col = lax.broadcasted_iota(jnp.int32, (LT, LT), 1)
    same = (row // T) == (col // T)
    causal = jnp.logical_and(same, row >= col)
    causal_t = jnp.logical_and(same, row <= col)
    b_col = jnp.sum(jnp.where(causal, lf_row, 0.0), axis=1, keepdims=True)
    b_row = jnp.sum(jnp.where(causal_t, lf_col, 0.0), axis=0, keepdims=True)
    b_end = jnp.sum(jnp.where(same, lf_row, 0.0), axis=1, keepdims=True)
    d = jnp.where(causal, b_col - b_row + ig_row, -jnp.inf)
    inter = b_col + m_prev
    m_t = jnp.maximum(inter, jnp.max(d, axis=1, keepdims=True))
    d_end = jnp.where(same, b_end - b_row + ig_row, -jnp.inf)
    m_new = jnp.maximum(b_end + m_prev, jnp.max(d_end, axis=1, keepdims=True))
    w_inter = jnp.exp(inter - m_t) * SCALE

    q32 = q_ref[...]
    k32 = k_ref[...]
    q = q32.astype(BF16)
    v = v_ref[...].astype(BF16)
    qk = lax.dot_general(q, k32.astype(BF16), (((1,), (1,)), ((), ())), preferred_element_type=F32)
    s = qk * (jnp.exp(d - m_t) * SCALE)
    num_intra = jnp.dot(s.astype(BF16), v, preferred_element_type=F32)
    den_intra = jnp.sum(s, axis=1, keepdims=True)

    decay = jnp.exp(b_end + m_prev - m_new)
    wk = jnp.exp(b_end - b_col + ig_col - m_new) * k32
    wk_t = wk.T
    col_seq = lax.broadcasted_iota(jnp.int32, (DK, LT), 1) // T

    for bb in range(NB):
        r0 = bb * T
        cmat = c_ref[bb]
        nvec = n_ref[bb:bb + 1, :]
        qc_scr[r0:r0 + T, :] = jnp.dot(q32[r0:r0 + T, :].astype(BF16), cmat.astype(BF16),
                                       preferred_element_type=F32)
        ntok_scr[r0:r0 + T, :] = jnp.broadcast_to(nvec, (T, DK))
        upd = jnp.dot(jnp.where(col_seq == bb, wk_t, 0.0).astype(BF16), v, preferred_element_type=F32)
        dec = decay[r0:r0 + 1, :]
        cout_ref[bb] = dec * cmat + upd
        nout_ref[bb:bb + 1, :] = dec * nvec + jnp.sum(wk[r0:r0 + T, :], axis=0, keepdims=True)
        mout_ref[bb:bb + 1, :] = jnp.broadcast_to(m_new[r0:r0 + 1, :], (1, GATE_LANES))

    num = w_inter * qc_scr[...] + num_intra
    den = w_inter * jnp.sum(q32 * ntok_scr[...], axis=1, keepdims=True) + den_intra
    ho_scr[...] = _head_output(num, den, m_t, gout_ref[...], o_ref[...]).reshape(NB, T, DV)
    for t in range(T):
        hg_ref[t] = ho_scr[:, t, :].astype(hg_ref.dtype)


def _scan_s(p, gates, bias, gout, mtok, c, n_hm, *, T):
    nseq = c.shape[0]
    nb = SAMPLE_GROUP
    lt = nb * T
    kern = functools.partial(_scan_s_kernel, T=T, NB=nb)
    return pl.pallas_call(
        kern,
        grid=(nseq // nb, H),
        in_specs=[pl.BlockSpec((lt, DK), lambda g, h: (g, h)),
                  pl.BlockSpec((lt, DK), lambda g, h: (g, H + h)),
                  pl.BlockSpec((lt, DV), lambda g, h: (g, H + h)),
                  pl.BlockSpec((lt, DV), lambda g, h: (g, 2 * H + h)),
                  pl.BlockSpec((lt, GATE_LANES), lambda g, h: (g, 0)),
                  pl.BlockSpec((1, GATE_LANES), lambda g, h: (0, 0)),
                  pl.BlockSpec((1, DV), lambda g, h: (0, h)),
                  pl.BlockSpec((lt, GATE_LANES), lambda g, h: (g, 0)),
                  pl.BlockSpec((nb, None, DK, DV), lambda g, h: (g, h, 0, 0)),
                  pl.BlockSpec((None, nb, DK), lambda g, h: (h, g, 0))],
        out_specs=[pl.BlockSpec((T, nb, DV), lambda g, h: (0, g, h)),
                   pl.BlockSpec((nb, None, DK, DV), lambda g, h: (g, h, 0, 0)),
                   pl.BlockSpec((None, nb, DK), lambda g, h: (h, g, 0)),
                   pl.BlockSpec((None, nb, GATE_LANES), lambda g, h: (h, g, 0))],
        out_shape=[jax.ShapeDtypeStruct((T, nseq, VW), BF16),
                   jax.ShapeDtypeStruct((nseq, H, DK, DV), F32),
                   jax.ShapeDtypeStruct((H, nseq, DK), F32),
                   jax.ShapeDtypeStruct((H, nseq, GATE_LANES), F32)],
        scratch_shapes=[pltpu.VMEM((lt, DV), F32), pltpu.VMEM((lt, DK), F32), pltpu.VMEM((nb, T, DV), F32)],
        compiler_params=_params(2),
        name="scan_sample",
    )(p, p, p, p, gates, bias, gout, mtok, c, n_hm)


def _mmres_cast_kernel(a_ref, w_ref, x_ref, o_ref, wc_ref):
    w = w_ref[...].astype(BF16)
    wc_ref[...] = w
    y = jnp.dot(a_ref[...], w, preferred_element_type=F32)
    nseq, t_dec, _ = x_ref.shape
    for t in range(t_dec):
        o_ref[t * nseq:(t + 1) * nseq, :] = x_ref[:, t, :] + y[t * nseq:(t + 1) * nseq, :]


def _mmres_cast(a, w, x, *, tn):
    m, kdim = a.shape
    n = w.shape[2]
    nseq, t_dec, _ = x.shape
    return pl.pallas_call(
        _mmres_cast_kernel,
        grid=(n // tn,),
        in_specs=[pl.BlockSpec((m, kdim), lambda j: (0, 0)),
                  pl.BlockSpec((None, kdim, tn), lambda j: (0, 0, j)),
                  pl.BlockSpec((nseq, t_dec, tn), lambda j: (0, 0, j))],
        out_specs=[pl.BlockSpec((m, tn), lambda j: (0, j)),
                   pl.BlockSpec((kdim, tn), lambda j: (0, j))],
        out_shape=[jax.ShapeDtypeStruct((m, n), F32),
                   jax.ShapeDtypeStruct((kdim, n), BF16)],
        compiler_params=_params(1),
        name="out_proj_cast",
    )(a, w, x)


def _mmres_kernel(a_ref, w_ref, x_ref, o_ref):
    o_ref[...] = x_ref[...] + jnp.dot(a_ref[...], w_ref[...], preferred_element_type=F32)


def _mmres(a, w, x, *, tm):
    m, kdim = a.shape
    n = w.shape[1]
    return pl.pallas_call(
        _mmres_kernel,
        grid=(m // tm,),
        in_specs=[pl.BlockSpec((tm, kdim), lambda i: (i, 0)),
                  pl.BlockSpec((kdim, n), lambda i: (0, 0)),
                  pl.BlockSpec((tm, n), lambda i: (i, 0))],
        out_specs=pl.BlockSpec((tm, n), lambda i: (i, 0)),
        out_shape=jax.ShapeDtypeStruct((m, n), F32),
        compiler_params=_params(1),
        name="out_proj",
    )(a, w, x)


def _ffn_kernel(x_ref, gam_ref, wg_ref, wa0_ref, wa1_ref, cw_ref, cb_ref, wd_ref, st_ref, gfin_ref,
                o_ref, so_ref, wupc_ref, wdnc_ref,
                u_scr, gext_scr, a_scr, wup0_scr, wup1_scr, wdn0_scr, wdn1_scr,
                *, tm, shift, nsub, final_norm):
    s = pl.program_id(1)
    sr = CONV_TAIL * shift
    wup_slots = (wup0_scr, wup1_scr)
    wdn_slots = (wdn0_scr, wdn1_scr)

    def cast_steps(slot):
        wup_scr = wup_slots[slot]
        wdn_scr = wdn_slots[slot]
        valid = D_FF - jnp.minimum(s, NF - 1) * TF

        def cast_gate():
            ok = lax.broadcasted_iota(jnp.int32, (1, TF), 1) < valid
            w = jnp.where(ok, wg_ref[...], 0.0).astype(BF16)
            wup_scr[:, 0:TF] = w
            wupc_ref[:, 0:TF] = w

        def cast_value():
            lane = lax.broadcasted_iota(jnp.int32, (1, LANES), 1)
            w0 = jnp.where(lane < valid, wa0_ref[...], 0.0).astype(BF16)
            w1 = jnp.where(lane + LANES < valid, wa1_ref[...], 0.0).astype(BF16)
            wup_scr[:, TF:TF + LANES] = w0
            wup_scr[:, TF + LANES:2 * TF] = w1
            wupc_ref[:, TF:TF + LANES] = w0
            wupc_ref[:, TF + LANES:2 * TF] = w1

        def cast_down(r0, rows):
            row_ok = r0 + lax.broadcasted_iota(jnp.int32, (rows, 1), 0) < valid
            w = jnp.where(row_ok, wd_ref[r0:r0 + rows, :], 0.0).astype(BF16)
            wdn_scr[r0:r0 + rows, :] = w
            wdnc_ref[r0:r0 + rows, :] = w

        half = TF // 2
        return [cast_gate, cast_value, functools.partial(cast_down, 0, half),
                functools.partial(cast_down, half, half)]

    def run_tile(slot, fillers=()):
        fillers = list(fillers)

        def fill():
            if fillers:
                fillers.pop(0)()

        wup_scr = wup_slots[slot]
        wdn_scr = wdn_slots[slot]
        f = s - 1
        col_ok = lax.broadcasted_iota(jnp.int32, (1, TF), 1) < D_FF - f * TF
        cw = cw_ref[...]
        cb = cb_ref[...]
        for t in range(CONV_TAIL):
            gext_scr[t * shift:(t + 1) * shift, :] = st_ref[:, t, :]
        ts = tm // nsub
        for h in range(nsub):
            r0 = h * ts
            ga = jnp.dot(u_scr[r0:r0 + ts, :], wup_scr[...], preferred_element_type=F32)
            gext_scr[sr + r0:sr + r0 + ts, :] = ga[:, 0:TF]
            a_scr[r0:r0 + ts, :] = ga[:, TF:2 * TF]
            fill()
        for h in range(nsub):
            r0 = h * ts
            g = gext_scr[sr + r0:sr + r0 + ts, :]
            g_m2 = gext_scr[sr + r0 - 2 * shift:sr + r0 - 2 * shift + ts, :]
            g_m1 = gext_scr[sr + r0 - shift:sr + r0 - shift + ts, :]
            gc = cb + ((cw[0:1, :] * g_m2 + cw[1:2, :] * g_m1) + cw[2:3, :] * g)
            hmid = jnp.where(col_ok, (gc * jax.nn.sigmoid(gc)) * a_scr[r0:r0 + ts, :], 0.0)
            o_ref[r0:r0 + ts, :] += jnp.dot(hmid.astype(BF16), wdn_scr[...], preferred_element_type=F32)
            fill()
        while fillers:
            fill()
        for t in range(CONV_TAIL):
            so_ref[:, t, :] = gext_scr[tm + t * shift:tm + (t + 1) * shift, :]

    @pl.when(s == 0)
    def _():
        x = x_ref[...]
        u_scr[...] = _rms(x, gam_ref[...]).astype(BF16)
        o_ref[...] = x
        for step in cast_steps(0):
            step()

    for parity in range(2):
        @pl.when(jnp.logical_and(s > 0, s % 2 == parity))
        def _(parity=parity):
            run_tile(1 - parity, cast_steps(parity))

    if final_norm:
        @pl.when(s == NF)
        def _():
            o_ref[...] = _rms(o_ref[...], gfin_ref[...])


def _ffn(x, layer, gamma, w_up, conv_w, conv_b, w_down, st, gfin, *, final_norm):
    assert TF == 2 * LANES
    tm = x.shape[0]
    shift = st.shape[0]
    sr = CONV_TAIL * shift
    kern = functools.partial(_ffn_kernel, tm=tm, shift=shift, nsub=2, final_norm=final_norm)
    last_a = 2 * FF_LANE_BLOCKS - 1

    def wt(s):
        return jnp.minimum(s, NF - 1)

    def ft(s):
        return jnp.maximum(s - 1, 0)

    st_spec = pl.BlockSpec((shift, CONV_TAIL, TF), lambda i, s: (0, 0, ft(s)))
    return pl.pallas_call(
        kern,
        grid=(1, NF + 1),
        in_specs=[pl.BlockSpec((tm, D), lambda i, s: (i, 0), pipeline_mode=pl.Buffered(1)),
                  pl.BlockSpec((1, D), lambda i, s: (0, 0)),
                  pl.BlockSpec((None, D, TF), lambda i, s: (layer, 0, wt(s))),
                  pl.BlockSpec((None, D, LANES), lambda i, s: (layer, 0, FF_LANE_BLOCKS + 2 * wt(s))),
                  pl.BlockSpec((None, D, LANES),
                               lambda i, s: (layer, 0, jnp.minimum(FF_LANE_BLOCKS + 2 * wt(s) + 1, last_a))),
                  pl.BlockSpec((None, 3, TF), lambda i, s: (layer, 0, ft(s))),
                  pl.BlockSpec((None, 1, TF), lambda i, s: (layer, 0, ft(s))),
                  pl.BlockSpec((None, TF, D), lambda i, s: (layer, wt(s), 0)),
                  st_spec,
                  pl.BlockSpec((1, D), lambda i, s: (0, 0))],
        out_specs=[pl.BlockSpec((tm, D), lambda i, s: (i, 0)),
                   st_spec,
                   pl.BlockSpec((D, 2 * TF), lambda i, s: (0, wt(s))),
                   pl.BlockSpec((TF, D), lambda i, s: (wt(s), 0))],
        out_shape=[jax.ShapeDtypeStruct((tm, D), F32),
                   jax.ShapeDtypeStruct(st.shape, F32),
                   jax.ShapeDtypeStruct((D, NF * 2 * TF), BF16),
                   jax.ShapeDtypeStruct((NF * TF, D), BF16)],
        scratch_shapes=[pltpu.VMEM((tm, D), BF16),
                        pltpu.VMEM((sr + tm, TF), F32),
                        pltpu.VMEM((tm, TF), F32),
                        pltpu.VMEM((D, 2 * TF), BF16),
                        pltpu.VMEM((D, 2 * TF), BF16),
                        pltpu.VMEM((TF, D), BF16),
                        pltpu.VMEM((TF, D), BF16)],
        compiler_params=_params(2),
        name="conv_ffn_cast",
    )(x, gamma, w_up, w_up, w_up, conv_w, conv_b, w_down, st, gfin)


def _ffn_bf16_kernel(x_ref, gam_ref, wup_ref, cw_ref, cb_ref, wdn_ref, st_ref, gfin_ref,
                     o_ref, so_ref, u_scr, gext_scr, a_scr, carry_scr,
                     *, tm, sr, shift, tps, nsub, final_norm):
    i = pl.program_id(0)
    f = pl.program_id(1)
    tf2 = 2 * TF

    @pl.when(f == 0)
    def _():
        x = x_ref[...]
        u_scr[...] = _rms(x, gam_ref[...]).astype(BF16)
        o_ref[...] = x
        if tps > 1:
            @pl.when(i == 0)
            def _():
                carry_scr[...] = jnp.zeros_like(carry_scr)

    col_ok = lax.broadcasted_iota(jnp.int32, (1, tf2), 1) < D_FF - f * tf2
    cw = cw_ref[...]
    cb = cb_ref[...]
    if tps == 1:
        gext_scr[0:sr, :] = st_ref[...]
    else:
        gext_scr[0:sr, :] = jnp.where((i % tps) == 0, st_ref[...], carry_scr[f])
    ts = tm // nsub
    for h in range(nsub):
        r0 = h * ts
        ga = jnp.dot(u_scr[r0:r0 + ts, :], wup_ref[...], preferred_element_type=F32)
        gext_scr[sr + r0:sr + r0 + ts, 0:TF] = ga[:, 0:TF]
        gext_scr[sr + r0:sr + r0 + ts, TF:tf2] = ga[:, 2 * TF:3 * TF]
        a_scr[r0:r0 + ts, 0:TF] = ga[:, TF:2 * TF]
        a_scr[r0:r0 + ts, TF:tf2] = ga[:, 3 * TF:4 * TF]
    for h in range(nsub):
        r0 = h * ts
        g = gext_scr[sr + r0:sr + r0 + ts, :]
        g_m2 = gext_scr[sr + r0 - 2 * shift:sr + r0 - 2 * shift + ts, :]
        g_m1 = gext_scr[sr + r0 - shift:sr + r0 - shift + ts, :]
        gc = cb + ((cw[0:1, :] * g_m2 + cw[1:2, :] * g_m1) + cw[2:3, :] * g)
        hmid = jnp.where(col_ok, (gc * jax.nn.sigmoid(gc)) * a_scr[r0:r0 + ts, :], 0.0)
        o_ref[r0:r0 + ts, :] += jnp.dot(hmid.astype(BF16), wdn_ref[...], preferred_element_type=F32)
    g_tail = gext_scr[tm:tm + sr, :]
    so_ref[...] = g_tail
    if tps > 1:
        carry_scr[f] = g_tail

    if final_norm:
        @pl.when(f == pl.num_programs(1) - 1)
        def _():
            o_ref[...] = _rms(o_ref[...], gfin_ref[...])


def _ffn_bf16(x, layer, gamma, wupc, conv_w, conv_b, wdnc, st, gfin, *, tm, sr, shift, tps, st_per_tile,
              final_norm):
    assert NF % 2 == 0
    m = x.shape[0]
    nt = m // tm
    tf2 = 2 * TF
    nf2 = NF // 2
    kern = functools.partial(_ffn_bf16_kernel, tm=tm, sr=sr, shift=shift, tps=tps,
                             nsub=2 if tm >= 64 else 1, final_norm=final_norm)
    st_map = (lambda i, f: (i, f)) if st_per_tile else (lambda i, f: (0, f))
    return pl.pallas_call(
        kern,
        grid=(nt, nf2),
        in_specs=[pl.BlockSpec((tm, D), lambda i, f: (i, 0)),
                  pl.BlockSpec((1, D), lambda i, f: (0, 0)),
                  pl.BlockSpec((D, 2 * tf2), lambda i, f: (0, f)),
                  pl.BlockSpec((None, 3, tf2), lambda i, f: (layer, 0, f)),
                  pl.BlockSpec((None, 1, tf2), lambda i, f: (layer, 0, f)),
                  pl.BlockSpec((tf2, D), lambda i, f: (f, 0)),
                  pl.BlockSpec((sr, tf2), st_map),
                  pl.BlockSpec((1, D), lambda i, f: (0, 0))],
        out_specs=[pl.BlockSpec((tm, D), lambda i, f: (i, 0)),
                   pl.BlockSpec((sr, tf2), lambda i, f: (i, f))],
        out_shape=[jax.ShapeDtypeStruct((m, D), F32),
                   jax.ShapeDtypeStruct((nt * sr, D_FF), F32)],
        scratch_shapes=[pltpu.VMEM((tm, D), BF16),
                        pltpu.VMEM((sr + tm, tf2), F32),
                        pltpu.VMEM((tm, tf2), F32),
                        pltpu.VMEM((nf2, sr, tf2), F32)],
        compiler_params=_params(2),
        name="conv_ffn",
    )(x, gamma, wupc, conv_w, conv_b, wdnc, st, gfin)


def _window_sum(ext_scr, tmp_scrs, base, tm, shift, w):
    starts = {w: base}
    v = w
    while v > 2:
        starts[v // 2] = (starts[v] - (v // 2) * shift) // 8 * 8
        v //= 2
    src = ext_scr
    v = 1
    k = 0
    while True:
        lo = starts[2 * v]
        n = base + tm - lo
        val = src[lo:lo + n, :] + src[lo - v * shift:lo - v * shift + n, :]
        v *= 2
        if v == w:
            return val
        dst = tmp_scrs[k % 2]
        dst[lo:lo + n, :] = val
        src = dst
        k += 1


def _pool_kernel(x_ref, gam_ref, wp_ref, sc_ref, st_ref, o_ref, ut_ref, rinv_scr, uext_scr, tmp0_scr, tmp1_scr,
                 carry_scr,
                 *, tm, shift, tps, pos0, tr):
    i = pl.program_id(0)
    grp = pl.program_id(1)
    hr = POOL_HALO * shift
    base = POOL_PAD_ROWS + hr

    @pl.when(grp == 0)
    def _():
        x = x_ref[...]
        rinv_scr[...] = lax.rsqrt(jnp.mean(x * x, axis=1, keepdims=True) + EPS)
        if tps > 1:
            @pl.when(i == 0)
            def _():
                carry_scr[...] = jnp.zeros_like(carry_scr)

    for kk, w in enumerate(POOL_WINDOWS):
        @pl.when(grp == kk)
        def _(kk=kk, w=w):
            cs = slice(kk * PG, (kk + 1) * PG)
            xg = x_ref[:, cs]
            ug = xg * rinv_scr[...] * gam_ref[:, cs]
            uext_scr[0:POOL_PAD_ROWS, :] = jnp.zeros((POOL_PAD_ROWS, PG), F32)
            if tps == 1:
                uext_scr[POOL_PAD_ROWS:base, :] = st_ref[...]
            else:
                uext_scr[POOL_PAD_ROWS:base, :] = jnp.where((i % tps) == 0, st_ref[...], carry_scr[kk])
            uext_scr[base:base + tm, :] = ug
            acc = _window_sum(uext_scr, (tmp0_scr, tmp1_scr), base, tm, shift, w)
            if pos0 + 1 >= w:
                pooled = acc / float(w) - ug
            else:
                step = (i % tps) * (tm // shift) + lax.broadcasted_iota(jnp.int32, (tm, 1), 0) // shift
                cnt = jnp.minimum(w, pos0 + step + 1).astype(F32)
                pooled = acc / cnt - ug
            y = jnp.dot(pooled.astype(BF16), wp_ref[kk].astype(BF16), preferred_element_type=F32)
            o_ref[...] = xg + y * sc_ref[:, cs]
            ut_ref[...] = ug[tm - tr:tm, :]
            if tps > 1:
                carry_scr[kk] = ug[tm - hr:tm, :]


def _pool(x, gamma, wp, sc, st, *, tm, shift, tps, pos0, tr, st_per_tile):
    m = x.shape[0]
    nt = m // tm
    hr = POOL_HALO * shift
    ng = len(POOL_WINDOWS)
    kern = functools.partial(_pool_kernel, tm=tm, shift=shift, tps=tps, pos0=pos0, tr=tr)
    st_map = (lambda i, g: (i, g)) if st_per_tile else (lambda i, g: (0, g))
    return pl.pallas_call(
        kern,
        grid=(nt, ng),
        in_specs=[pl.BlockSpec((tm, D), lambda i, g: (i, 0)),
                  pl.BlockSpec((1, D), lambda i, g: (0, 0)),
                  pl.BlockSpec((None, ng, PG, PG), lambda i, g: (0, 0, 0, 0)),
                  pl.BlockSpec((1, D), lambda i, g: (0, 0)),
                  pl.BlockSpec((hr, PG), st_map)],
        out_specs=[pl.BlockSpec((tm, PG), lambda i, g: (i, g)),
                   pl.BlockSpec((tr, PG), lambda i, g: (i, g))],
        out_shape=[jax.ShapeDtypeStruct((m, D), F32),
                   jax.ShapeDtypeStruct((nt * tr, D), F32)],
        scratch_shapes=[pltpu.VMEM((tm, 1), F32)]
        + [pltpu.VMEM((POOL_PAD_ROWS + hr + tm, PG), F32)] * 3
        + [pltpu.VMEM((ng, hr, PG), F32)],
        compiler_params=_params(2),
        name="pool_mixer",
    )(x, gamma, wp, sc, st)


def _pool_sample_kernel(x_ref, gam_ref, wp_ref, sc_ref, st_ref, o_ref, new_ref, rinv_scr, uext_scr,
                        *, nseq, t_dec):
    nh = POOL_HALO - 1
    grp = pl.program_id(0)
    tm = nseq * t_dec
    hr = POOL_HALO * nseq

    @pl.when(grp == 0)
    def _():
        x = x_ref[...]
        rinv_scr[...] = lax.rsqrt(jnp.mean(x * x, axis=1, keepdims=True) + EPS)

    for kk, w in enumerate(POOL_WINDOWS):
        @pl.when(grp == kk)
        def _(kk=kk, w=w):
            cs = slice(kk * PG, (kk + 1) * PG)
            xg = x_ref[:, cs]
            ug = xg * rinv_scr[...] * gam_ref[:, cs]
            for t in range(nh):
                uext_scr[(t + 1) * nseq:(t + 2) * nseq, :] = st_ref[:, t, :]
            uext_scr[hr:hr + tm, :] = ug
            acc = ug
            for j in range(1, w):
                acc = acc + uext_scr[hr - j * nseq:hr - j * nseq + tm, :]
            pooled = acc / float(w) - ug
            y = jnp.dot(pooled.astype(BF16), wp_ref[kk].astype(BF16), preferred_element_type=F32)
            o_ref[...] = xg + y * sc_ref[:, cs]
            for t in range(nh):
                src = (t + t_dec + 1) * nseq
                new_ref[:, t, :] = uext_scr[src:src + nseq, :]


def _pool_sample(x, gamma, wp, sc, state, *, nseq, t_dec):
    assert PAST_LEN + 1 >= max(POOL_WINDOWS)
    nh = POOL_HALO - 1
    ng = len(POOL_WINDOWS)
    tm = nseq * t_dec
    hr = POOL_HALO * nseq
    kern = functools.partial(_pool_sample_kernel, nseq=nseq, t_dec=t_dec)
    hist_spec = pl.BlockSpec((nseq, nh, PG), lambda g: (0, 0, g))
    return pl.pallas_call(
        kern,
        grid=(ng,),
        in_specs=[pl.BlockSpec((tm, D), lambda g: (0, 0)),
                  pl.BlockSpec((1, D), lambda g: (0, 0)),
                  pl.BlockSpec((None, ng, PG, PG), lambda g: (0, 0, 0, 0)),
                  pl.BlockSpec((1, D), lambda g: (0, 0)),
                  hist_spec],
        out_specs=[pl.BlockSpec((tm, PG), lambda g: (0, g)),
                   hist_spec],
        out_shape=[jax.ShapeDtypeStruct((tm, D), F32),
                   jax.ShapeDtypeStruct((nseq, nh, D), F32)],
        scratch_shapes=[pltpu.VMEM((tm, 1), F32),
                        pltpu.VMEM((hr + tm, PG), F32)],
        compiler_params=_params(1),
        name="pool_mixer_sample",
    )(x, gamma, wp, sc, state)


def _pad_cols(a, n):
    return jnp.pad(a, ((0, 0), (0, n - a.shape[1])))


def _to_time_major(a, t):
    nseq = a.shape[0]
    a = a.reshape((nseq // SAMPLE_TILE_SEQS, SAMPLE_TILE_SEQS, t) + a.shape[2:])
    a = jnp.swapaxes(a, 1, 2)
    return a.reshape((nseq * t,) + a.shape[3:])


def _from_time_major(a, nseq, t):
    a = a.reshape((nseq // SAMPLE_TILE_SEQS, t, SAMPLE_TILE_SEQS) + a.shape[1:])
    a = jnp.swapaxes(a, 1, 2)
    return a.reshape((nseq, t) + a.shape[3:])


def kernel(x_prompt, x_sample, state_mlstm_C, state_mlstm_n, state_mlstm_m, state_pool, state_ffn_conv,
           meta_tokens, norm_mix, norm_ffn, norm_final, w_mlstm_in, b_mlstm_gate, g_mlstm_out, w_mlstm_out,
           w_pool, pool_scale, w_up, conv_w, conv_b, w_down):
    bsz, seq, _ = x_prompt.shape
    nseq, t_dec, _ = x_sample.shape

    w_in_t = jnp.swapaxes(w_mlstm_in[0], 0, 1)
    bias = _pad_cols(b_mlstm_gate[0][None, :], GATE_LANES)
    w_out = w_mlstm_out
    gout = g_mlstm_out[0][None, :]
    wp = w_pool
    psc = pool_scale[0][None, :]
    gfin = norm_final[None, :]
    conv_b3 = conv_b[:, None, :]
    ffn_w = [(layer, norm_ffn[layer][None, :], w_up, conv_w, conv_b3, w_down) for layer in range(2)]
    g_mix0 = norm_mix[0][None, :]
    g_mix1 = norm_mix[1][None, :]

    ffn_cache = {}
    mix_cache = {}

    def proj(x, **kw):
        if "w_in" not in mix_cache:
            p, gates, wtc, wgc = _proj(x, g_mix0, w_in_t, w_in_t, **kw)
            mix_cache["w_in"] = (wtc, wgc)
            return p, gates
        return _proj(x, g_mix0, *mix_cache["w_in"], **kw)

    def out_proj(a, x, *, tm):
        if "w_out" not in mix_cache:
            y, mix_cache["w_out"] = _mmres_cast(a, w_out, x, tn=512)
            return y
        return _mmres(a, mix_cache["w_out"], x, tm=tm)

    def ffn_first(x, layer, st, *, final_norm):
        lyr, gamma, wu, cwt, cbs, wd = ffn_w[layer]
        y, cs, wupc, wdnc = _ffn(x, lyr, gamma, wu, cwt, cbs, wd, st, gfin, final_norm=final_norm)
        ffn_cache[layer] = (wupc, wdnc)
        return y, cs

    def ffn(x, layer, st, **kw):
        lyr, gamma, _, cwt, cbs, _ = ffn_w[layer]
        wupc, wdnc = ffn_cache[layer]
        return _ffn_bf16(x, lyr, gamma, wupc, cwt, cbs, wdnc, st, gfin, **kw)

    def long_stream(x, nb, s, st, *, tm, chunk, lead_pad, pos0):
        c0, n0, m0, conv0, pool0, conv1 = st
        tm_ffn = tm
        p, gates = proj(x, tm=tm, tn=1024, out_dtype=BF16)
        p = p.reshape(nb, s, PW)
        gates = gates.reshape(nb, s, GATE_LANES)
        if lead_pad:
            p = jnp.pad(p, ((0, 0), (lead_pad, 0), (0, 0)))
            gates = jnp.pad(gates, ((0, 0), (lead_pad, 0), (0, 0)))
        hg, c_new, n_new, m_new = _scan(p, gates, bias, gout, c0, n0, m0, L=chunk, lead_pad=lead_pad)
        hg = hg[:, lead_pad:].reshape(nb * s, VW)
        x1 = out_proj(hg, x, tm=min(tm, 512))
        x2, cs0 = ffn(x1, 0, conv0, tm=tm_ffn, sr=8, shift=1, tps=s // tm_ffn,
                      st_per_tile=False, final_norm=False)
        x3, ut = _pool(x2, g_mix1, wp, psc, pool0, tm=tm, shift=1, tps=s // tm, pos0=pos0,
                       tr=POOL_HALO, st_per_tile=False)
        y, cs1 = ffn(x3, 1, conv1, tm=tm_ffn, sr=8, shift=1, tps=s // tm_ffn,
                     st_per_tile=False, final_norm=True)
        cs0 = cs0.reshape(nb, s // tm_ffn, 8, D_FF)[:, -1]
        cs1 = cs1.reshape(nb, s // tm_ffn, 8, D_FF)[:, -1]
        ut = ut.reshape(nb, s // tm, POOL_HALO, D)[:, -1]
        return y, (c_new, n_new, m_new, cs0, ut, cs1)

    def prompt_streams():
        zero_state = (jnp.zeros((H, DK, DV), F32), jnp.zeros((H, 8, DK), F32), jnp.zeros((H, 8, GATE_LANES), F32),
                      jnp.zeros((8, D_FF), F32), jnp.zeros((POOL_HALO, D), F32), jnp.zeros((8, D_FF), F32))
        _, (c_m, n_m, m_m, cs0_m, ut_m, cs1_m) = long_stream(
            meta_tokens, 1, N_META, zero_state, tm=N_META, chunk=128, lead_pad=128 - N_META, pos0=0)

        y_p, (c_p, n_p, m_p, cs0_p, ut_p, cs1_p) = long_stream(
            x_prompt.reshape(bsz * seq, D), bsz, seq, (c_m[0], n_m[0], m_m[0], cs0_m[0], ut_m[0], cs1_m[0]),
            tm=1024, chunk=SCAN_CHUNK, lead_pad=0, pos0=N_META)
        return (y_p.reshape(bsz, seq, D), c_p[None], n_p[:, :, 0][None], m_p[:, :, 0, 0][None],
                ut_p[:, 1:][None], jnp.stack([cs0_p[:, 6:], cs1_p[:, 6:]]))

    xs = x_sample.reshape(nseq * t_dec, D)
    p_s, gates_s = proj(xs, tm=nseq * t_dec, tn=512, out_dtype=F32)
    mtok = _pad_cols(jnp.repeat(state_mlstm_m[0], t_dec, axis=0), GATE_LANES)
    n_hm = jnp.swapaxes(state_mlstm_n[0], 0, 1)
    hg_s, C_s, n_s_hm, m_s_hm = _scan_s(p_s, gates_s, bias, gout, mtok, state_mlstm_C[0], n_hm, T=t_dec)
    assert nseq == SAMPLE_TILE_SEQS
    x1 = out_proj(hg_s.reshape(t_dec * nseq, VW), x_sample, tm=512)
    x2, cs0_s = ffn_first(x1, 0, state_ffn_conv[0], final_norm=False)
    x3, pool_new = _pool_sample(x2, g_mix1, wp, psc, state_pool[0], nseq=nseq, t_dec=t_dec)
    y_s, cs1_s = ffn_first(x3, 1, state_ffn_conv[1], final_norm=True)
    y_sample = _from_time_major(y_s, nseq, t_dec)
    n_s = jnp.swapaxes(n_s_hm, 0, 1)[None]
    m_s = jnp.swapaxes(m_s_hm[:, :, 0], 0, 1)[None]
    pool_s = pool_new[None]
    conv_s = jnp.stack([cs0_s, cs1_s])

    y_prompt, C_p, n_p, m_p, pool_p, conv_p = prompt_streams()
    return (y_prompt, y_sample, C_p, n_p, m_p, pool_p, conv_p,
            C_s[None], n_s, m_s, pool_s, conv_s)
```

```python
import functools

import jax
import jax.numpy as jnp
from jax import lax
from jax.experimental import pallas as pl
from jax.experimental.pallas import tpu as pltpu

F32 = jnp.float32
BF16 = jnp.bfloat16

EPS = 1e-6
D = 2048
H = 4
DK = 256
DV = 512
QKW = H * DK
VW = H * DV
PW = 2 * QKW + 2 * VW
GATE_LANES = 128
SCALE = DK ** -0.5
POOL_WINDOWS = (2, 4, 8, 16)
PG = D // len(POOL_WINDOWS)
POOL_HALO = 16
POOL_PAD_ROWS = 16
D_FF = 5504
TF = 256
NF = -(-D_FF // TF)
LANES = 128
FF_LANE_BLOCKS = D_FF // LANES
CONV_TAIL = 2
N_META = 16
PAST_LEN = 16384
SCAN_CHUNK = 256
SAMPLE_GROUP = 16
SAMPLE_TILE_SEQS = 128
VMEM_LIMIT = 60 * 1024 * 1024


def _params(n_axes):
    return pltpu.CompilerParams(dimension_semantics=("arbitrary",) * n_axes,
                                vmem_limit_bytes=VMEM_LIMIT)


def _rms(x, g):
    return x * lax.rsqrt(jnp.mean(x * x, axis=-1, keepdims=True) + EPS) * g


def _log_sigmoid(x):
    return jnp.minimum(x, 0.0) - jnp.log(1.0 + jnp.exp(-jnp.abs(x)))


_NT = (((1,), (1,)), ((), ()))


def _proj_kernel(x_ref, g_ref, wt_ref, wgt_ref, p_ref, gate_ref, *rest, emit_bf16):
    u_scr = rest[-1]

    @pl.when(pl.program_id(1) == 0)
    def _():
        ub = _rms(x_ref[...], g_ref[...]).astype(BF16)
        u_scr[...] = ub
        wg = wgt_ref[...]
        if emit_bf16:
            row_ok = lax.broadcasted_iota(jnp.int32, (GATE_LANES, 1), 0) < 2 * H
            wg = jnp.where(row_ok, wg, 0.0).astype(BF16)
            rest[1][...] = wg
        gate_ref[...] = lax.dot_general(ub, wg, _NT, preferred_element_type=F32)

    w = wt_ref[...].astype(BF16)
    if emit_bf16:
        rest[0][...] = w
    p_ref[...] = lax.dot_general(u_scr[...], w, _NT, preferred_element_type=F32).astype(p_ref.dtype)


def _proj(x, gamma, wt, wgt, *, tm, tn, out_dtype):
    m = x.shape[0]
    emit = wt.dtype != BF16
    gate_spec = pl.BlockSpec((GATE_LANES, D), (lambda i, j: (PW // GATE_LANES, 0)) if emit
                             else (lambda i, j: (0, 0)))
    out_specs = [pl.BlockSpec((tm, tn), lambda i, j: (i, j)),
                 pl.BlockSpec((tm, GATE_LANES), lambda i, j: (i, 0))]
    out_shape = [jax.ShapeDtypeStruct((m, PW), out_dtype),
                 jax.ShapeDtypeStruct((m, GATE_LANES), F32)]
    if emit:
        out_specs += [pl.BlockSpec((tn, D), lambda i, j: (j, 0)),
                      pl.BlockSpec((GATE_LANES, D), lambda i, j: (0, 0))]
        out_shape += [jax.ShapeDtypeStruct((PW, D), BF16),
                      jax.ShapeDtypeStruct((GATE_LANES, D), BF16)]
    return pl.pallas_call(
        functools.partial(_proj_kernel, emit_bf16=emit),
        grid=(m // tm, PW // tn),
        in_specs=[pl.BlockSpec((tm, D), lambda i, j: (i, 0)),
                  pl.BlockSpec((1, D), lambda i, j: (0, 0)),
                  pl.BlockSpec((tn, D), lambda i, j: (j, 0)),
                  gate_spec],
        out_specs=out_specs,
        out_shape=out_shape,
        scratch_shapes=[pltpu.VMEM((tm, D), BF16)],
        compiler_params=_params(2),
        name="proj_cast" if emit else "proj",
    )(x, gamma, wt, wgt)


def _head_output(num, den, m_t, gout, o):
    hv = num * (1.0 / jnp.maximum(jnp.abs(den), jnp.exp(-m_t)))
    hv = hv * lax.rsqrt(jnp.mean(hv * hv, axis=1, keepdims=True) + EPS)
    return hv * gout * jax.nn.sigmoid(o.astype(F32))


def _scan_kernel(q_ref, k_ref, v_ref, o_ref, gt_ref, bias_ref, gout_ref, c0_ref, n0_ref, m0_ref,
                 hg_ref, cout_ref, nout_ref, mout_ref, c_scr, n_scr, m_scr, *, L, lead_pad, nc):
    c = pl.program_id(1)

    @pl.when(c == 0)
    def _():
        c_scr[...] = c0_ref[...]
        n_scr[...] = n0_ref[...]
        m_scr[...] = m0_ref[...]

    gates = gt_ref[...] + bias_ref[...]
    gates_t = gates.T
    row = lax.broadcasted_iota(jnp.int32, (L, L), 0)
    col = lax.broadcasted_iota(jnp.int32, (L, L), 1)
    causal = row >= col
    causal_t = row <= col
    if lead_pad:
        live_col = lax.broadcasted_iota(jnp.int32, (L, 1), 0) >= lead_pad
        live_row = lax.broadcasted_iota(jnp.int32, (1, L), 1) >= lead_pad

    for hh in range(H):
        ig_col = gates[:, hh:hh + 1]
        ig_row = gates_t[hh:hh + 1, :]
        lf_col = _log_sigmoid(gates[:, H + hh:H + hh + 1])
        lf_row = _log_sigmoid(gates_t[H + hh:H + hh + 1, :])
        if lead_pad:
            ig_col = jnp.where(live_col, ig_col, -jnp.inf)
            ig_row = jnp.where(live_row, ig_row, -jnp.inf)
            lf_col = jnp.where(live_col, lf_col, 0.0)
            lf_row = jnp.where(live_row, lf_row, 0.0)
        b_col = jnp.sum(jnp.where(causal, lf_row, 0.0), axis=1, keepdims=True)
        b_row = jnp.sum(jnp.where(causal_t, lf_col, 0.0), axis=0, keepdims=True)
        m_prev = m_scr[hh, 0:1, 0:1]
        d = jnp.where(causal, b_col - b_row + ig_row, -jnp.inf)
        inter = b_col + m_prev
        m_t = jnp.maximum(inter, jnp.max(d, axis=1, keepdims=True))
        w_inter = jnp.exp(inter - m_t) * SCALE

        q = q_ref[:, hh * DK:(hh + 1) * DK]
        k = k_ref[:, hh * DK:(hh + 1) * DK]
        v = v_ref[:, hh * DV:(hh + 1) * DV]
        qk = lax.dot_general(q, k, _NT, preferred_element_type=F32)
        s = qk * (jnp.exp(d - m_t) * SCALE)
        cmat = c_scr[hh]
        nvec = n_scr[hh, 0:1, :]
        num = w_inter * jnp.dot(q, cmat.astype(BF16), preferred_element_type=F32) \
            + jnp.dot(s.astype(BF16), v, preferred_element_type=F32)
        den = w_inter * jnp.sum(q.astype(F32) * nvec, axis=1, keepdims=True) \
            + jnp.sum(s, axis=1, keepdims=True)
        hout = _head_output(num, den, m_t, gout_ref[:, hh * DV:(hh + 1) * DV],
                            o_ref[:, hh * DV:(hh + 1) * DV])
        hg_ref[:, hh * DV:(hh + 1) * DV] = hout.astype(hg_ref.dtype)

        m_new = m_t[L - 1:L, :]
        b_last = b_col[L - 1:L, :]
        decay = jnp.exp(b_last + m_prev - m_new)
        wk = jnp.exp(b_last - b_col + ig_col - m_new) * k.astype(F32)
        c_scr[hh] = decay * cmat + jnp.dot(wk.T.astype(BF16), v, preferred_element_type=F32)
        n_scr[hh] = jnp.broadcast_to(decay * nvec + jnp.sum(wk, axis=0, keepdims=True), (8, DK))
        m_scr[hh] = jnp.broadcast_to(m_new, (8, GATE_LANES))

    @pl.when(c == nc - 1)
    def _():
        cout_ref[...] = c_scr[...]
        nout_ref[...] = n_scr[...]
        mout_ref[...] = m_scr[...]


def _scan(p, gates, bias, gout, c0, n0, m0, *, L, lead_pad=0):
    b, s, _ = p.shape
    nc = s // L
    kern = functools.partial(_scan_kernel, L=L, lead_pad=lead_pad, nc=nc)
    return pl.pallas_call(
        kern,
        grid=(b, nc),
        in_specs=[pl.BlockSpec((None, L, QKW), lambda i, c: (i, c, 0)),
                  pl.BlockSpec((None, L, QKW), lambda i, c: (i, c, 1)),
                  pl.BlockSpec((None, L, VW), lambda i, c: (i, c, 1)),
                  pl.BlockSpec((None, L, VW), lambda i, c: (i, c, 2)),
                  pl.BlockSpec((None, L, GATE_LANES), lambda i, c: (i, c, 0)),
                  pl.BlockSpec((1, GATE_LANES), lambda i, c: (0, 0)),
                  pl.BlockSpec((1, VW), lambda i, c: (0, 0)),
                  pl.BlockSpec((H, DK, DV), lambda i, c: (0, 0, 0)),
                  pl.BlockSpec((H, 8, DK), lambda i, c: (0, 0, 0)),
                  pl.BlockSpec((H, 8, GATE_LANES), lambda i, c: (0, 0, 0))],
        out_specs=[pl.BlockSpec((None, L, VW), lambda i, c: (i, c, 0)),
                   pl.BlockSpec((None, H, DK, DV), lambda i, c: (i, 0, 0, 0)),
                   pl.BlockSpec((None, H, 8, DK), lambda i, c: (i, 0, 0, 0)),
                   pl.BlockSpec((None, H, 8, GATE_LANES), lambda i, c: (i, 0, 0, 0))],
        out_shape=[jax.ShapeDtypeStruct((b, s, VW), BF16),
                   jax.ShapeDtypeStruct((b, H, DK, DV), F32),
                   jax.ShapeDtypeStruct((b, H, 8, DK), F32),
                   jax.ShapeDtypeStruct((b, H, 8, GATE_LANES), F32)],
        scratch_shapes=[pltpu.VMEM((H, DK, DV), F32),
                        pltpu.VMEM((H, 8, DK), F32),
                        pltpu.VMEM((H, 8, GATE_LANES), F32)],
        compiler_params=_params(2),
        name="scan",
    )(p, p, p, p, gates, bias, gout, c0, n0, m0)


def _scan_s_kernel(q_ref, k_ref, v_ref, o_ref, gt_ref, bias_ref, gout_ref, mtok_ref, c_ref, n_ref,
                   hg_ref, cout_ref, nout_ref, mout_ref, qc_scr, ntok_scr, ho_scr, *, T, NB):
    hh = pl.program_id(1)
    LT = NB * T
    gates = gt_ref[...] + bias_ref[...]
    gates_t = gates.T
    lane = lax.broadcasted_iota(jnp.int32, (LT, GATE_LANES), 1)
    sub = lax.broadcasted_iota(jnp.int32, (GATE_LANES, LT), 0)

    def pick_col(a, idx):
        return jnp.sum(jnp.where(lane == idx, a, 0.0), axis=1, keepdims=True)

    def pick_row(a, idx):
        return jnp.sum(jnp.where(sub == idx, a, 0.0), axis=0, keepdims=True)

    ig_col = pick_col(gates, hh)
    ig_row = pick_row(gates_t, hh)
    lf_col = _log_sigmoid(pick_col(gates, hh + H))
    lf_row = _log_sigmoid(pick_row(gates_t, hh + H))
    m_prev = pick_col(mtok_ref[...], hh)

    row = lax.broadcasted_iota(jnp.int32, (LT, LT), 0)
    col = lax.broadcasted_iota(jnp.int32, (LT, LT), 1)
    same = (row // T) == (col // T)
    causal = jnp.logical_and(same, row >= col)
    causal_t = jnp.logical_and(same, row <= col)
    b_col = jnp.sum(jnp.where(causal, lf_row, 0.0), axis=1, keepdims=True)
    b_row = jnp.sum(jnp.where(causal_t, lf_col, 0.0), axis=0, keepdims=True)
    b_end = jnp.sum(jnp.where(same, lf_row, 0.0), axis=1, keepdims=True)
    d = jnp.where(causal, b_col - b_row + ig_row, -jnp.inf)
    inter = b_col + m_prev
    m_t = jnp.maximum(inter, jnp.max(d, axis=1, keepdims=True))
    d_end = jnp.where(same, b_end - b_row + ig_row, -jnp.inf)
    m_new = jnp.maximum(b_end + m_prev, jnp.max(d_end, axis=1, keepdims=True))
    w_inter = jnp.exp(inter - m_t) * SCALE

    q32 = q_ref[...]
    k32 = k_ref[...]
    q = q32.astype(BF16)
    v = v_ref[...].astype(BF16)
    qk = lax.dot_general(q, k32.astype(BF16), (((1,), (1,)), ((), ())), preferred_element_type=F32)
    s = qk * (jnp.exp(d - m_t) * SCALE)
    num_intra = jnp.dot(s.astype(BF16), v, preferred_element_type=F32)
    den_intra = jnp.sum(s, axis=1, keepdims=True)

    decay = jnp.exp(b_end + m_prev - m_new)
    wk = jnp.exp(b_end - b_col + ig_col - m_new) * k32
    wk_t = wk.T
    col_seq = lax.broadcasted_iota(jnp.int32, (DK, LT), 1) // T

    for bb in range(NB):
        r0 = bb * T
        cmat = c_ref[bb]
        nvec = n_ref[bb:bb + 1, :]
        qc_scr[r0:r0 + T, :] = jnp.dot(q32[r0:r0 + T, :].astype(BF16), cmat.astype(BF16),
                                       preferred_element_type=F32)
        ntok_scr[r0:r0 + T, :] = jnp.broadcast_to(nvec, (T, DK))
        upd = jnp.dot(jnp.where(col_seq == bb, wk_t, 0.0).astype(BF16), v, preferred_element_type=F32)
        dec = decay[r0:r0 + 1, :]
        cout_ref[bb] = dec * cmat + upd
        nout_ref[bb:bb + 1, :] = dec * nvec + jnp.sum(wk[r0:r0 + T, :], axis=0, keepdims=True)
        mout_ref[bb:bb + 1, :] = jnp.broadcast_to(m_new[r0:r0 + 1, :], (1, GATE_LANES))

    num = w_inter * qc_scr[...] + num_intra
    den = w_inter * jnp.sum(q32 * ntok_scr[...], axis=1, keepdims=True) + den_intra
    ho_scr[...] = _head_output(num, den, m_t, gout_ref[...], o_ref[...]).reshape(NB, T, DV)
    for t in range(T):
        hg_ref[t] = ho_scr[:, t, :].astype(hg_ref.dtype)


def _scan_s(p, gates, bias, gout, mtok, c, n_hm, *, T):
    nseq = c.shape[0]
    nb = SAMPLE_GROUP
    lt = nb * T
    kern = functools.partial(_scan_s_kernel, T=T, NB=nb)
    return pl.pallas_call(
        kern,
        grid=(nseq // nb, H),
        in_specs=[pl.BlockSpec((lt, DK), lambda g, h: (g, h)),
                  pl.BlockSpec((lt, DK), lambda g, h: (g, H + h)),
                  pl.BlockSpec((lt, DV), lambda g, h: (g, H + h)),
                  pl.BlockSpec((lt, DV), lambda g, h: (g, 2 * H + h)),
                  pl.BlockSpec((lt, GATE_LANES), lambda g, h: (g, 0)),
                  pl.BlockSpec((1, GATE_LANES), lambda g, h: (0, 0)),
                  pl.BlockSpec((1, DV), lambda g, h: (0, h)),
                  pl.BlockSpec((lt, GATE_LANES), lambda g, h: (g, 0)),
                  pl.BlockSpec((nb, None, DK, DV), lambda g, h: (g, h, 0, 0)),
                  pl.BlockSpec((None, nb, DK), lambda g, h: (h, g, 0))],
        out_specs=[pl.BlockSpec((T, nb, DV), lambda g, h: (0, g, h)),
                   pl.BlockSpec((nb, None, DK, DV), lambda g, h: (g, h, 0, 0)),
                   pl.BlockSpec((None, nb, DK), lambda g, h: (h, g, 0)),
                   pl.BlockSpec((None, nb, GATE_LANES), lambda g, h: (h, g, 0))],
        out_shape=[jax.ShapeDtypeStruct((T, nseq, VW), BF16),
                   jax.ShapeDtypeStruct((nseq, H, DK, DV), F32),
                   jax.ShapeDtypeStruct((H, nseq, DK), F32),
                   jax.ShapeDtypeStruct((H, nseq, GATE_LANES), F32)],
        scratch_shapes=[pltpu.VMEM((lt, DV), F32), pltpu.VMEM((lt, DK), F32), pltpu.VMEM((nb, T, DV), F32)],
        compiler_params=_params(2),
        name="scan_sample",
    )(p, p, p, p, gates, bias, gout, mtok, c, n_hm)


def _mmres_cast_kernel(a_ref, w_ref, x_ref, o_ref, wc_ref):
    w = w_ref[...].astype(BF16)
    wc_ref[...] = w
    y = jnp.dot(a_ref[...], w, preferred_element_type=F32)
    nseq, t_dec, _ = x_ref.shape
    for t in range(t_dec):
        o_ref[t * nseq:(t + 1) * nseq, :] = x_ref[:, t, :] + y[t * nseq:(t + 1) * nseq, :]


def _mmres_cast(a, w, x, *, tn):
    m, kdim = a.shape
    n = w.shape[2]
    nseq, t_dec, _ = x.shape
    return pl.pallas_call(
        _mmres_cast_kernel,
        grid=(n // tn,),
        in_specs=[pl.BlockSpec((m, kdim), lambda j: (0, 0)),
                  pl.BlockSpec((None, kdim, tn), lambda j: (0, 0, j)),
                  pl.BlockSpec((nseq, t_dec, tn), lambda j: (0, 0, j))],
        out_specs=[pl.BlockSpec((m, tn), lambda j: (0, j)),
                   pl.BlockSpec((kdim, tn), lambda j: (0, j))],
        out_shape=[jax.ShapeDtypeStruct((m, n), F32),
                   jax.ShapeDtypeStruct((kdim, n), BF16)],
        compiler_params=_params(1),
        name="out_proj_cast",
    )(a, w, x)


def _mmres_kernel(a_ref, w_ref, x_ref, o_ref):
    o_ref[...] = x_ref[...] + jnp.dot(a_ref[...], w_ref[...], preferred_element_type=F32)


def _mmres(a, w, x, *, tm):
    m, kdim = a.shape
    n = w.shape[1]
    return pl.pallas_call(
        _mmres_kernel,
        grid=(m // tm,),
        in_specs=[pl.BlockSpec((tm, kdim), lambda i: (i, 0)),
                  pl.BlockSpec((kdim, n), lambda i: (0, 0)),
                  pl.BlockSpec((tm, n), lambda i: (i, 0))],
        out_specs=pl.BlockSpec((tm, n), lambda i: (i, 0)),
        out_shape=jax.ShapeDtypeStruct((m, n), F32),
        compiler_params=_params(1),
        name="out_proj",
    )(a, w, x)


def _ffn_kernel(x_ref, gam_ref, wg_ref, wa0_ref, wa1_ref, cw_ref, cb_ref, wd_ref, st_ref, gfin_ref,
                o_ref, so_ref, wupc_ref, wdnc_ref,
                u_scr, gext_scr, a_scr, wup0_scr, wup1_scr, wdn0_scr, wdn1_scr,
                *, tm, shift, nsub, final_norm):
    s = pl.program_id(1)
    sr = CONV_TAIL * shift
    wup_slots = (wup0_scr, wup1_scr)
    wdn_slots = (wdn0_scr, wdn1_scr)

    def cast_steps(slot):
        wup_scr = wup_slots[slot]
        wdn_scr = wdn_slots[slot]
        valid = D_FF - jnp.minimum(s, NF - 1) * TF

        def cast_gate():
            ok = lax.broadcasted_iota(jnp.int32, (1, TF), 1) < valid
            w = jnp.where(ok, wg_ref[...], 0.0).astype(BF16)
            wup_scr[:, 0:TF] = w
            wupc_ref[:, 0:TF] = w

        def cast_value():
            lane = lax.broadcasted_iota(jnp.int32, (1, LANES), 1)
            w0 = jnp.where(lane < valid, wa0_ref[...], 0.0).astype(BF16)
            w1 = jnp.where(lane + LANES < valid, wa1_ref[...], 0.0).astype(BF16)
            wup_scr[:, TF:TF + LANES] = w0
            wup_scr[:, TF + LANES:2 * TF] = w1
            wupc_ref[:, TF:TF + LANES] = w0
            wupc_ref[:, TF + LANES:2 * TF] = w1

        def cast_down(r0, rows):
            row_ok = r0 + lax.broadcasted_iota(jnp.int32, (rows, 1), 0) < valid
            w = jnp.where(row_ok, wd_ref[r0:r0 + rows, :], 0.0).astype(BF16)
            wdn_scr[r0:r0 + rows, :] = w
            wdnc_ref[r0:r0 + rows, :] = w

        half = TF // 2
        return [cast_gate, cast_value, functools.partial(cast_down, 0, half),
                functools.partial(cast_down, half, half)]

    def run_tile(slot, fillers=()):
        fillers = list(fillers)

        def fill():
            if fillers:
                fillers.pop(0)()

        wup_scr = wup_slots[slot]
        wdn_scr = wdn_slots[slot]
        f = s - 1
        col_ok = lax.broadcasted_iota(jnp.int32, (1, TF), 1) < D_FF - f * TF
        cw = cw_ref[...]
        cb = cb_ref[...]
        for t in range(CONV_TAIL):
            gext_scr[t * shift:(t + 1) * shift, :] = st_ref[:, t, :]
        ts = tm // nsub
        for h in range(nsub):
            r0 = h * ts
            ga = jnp.dot(u_scr[r0:r0 + ts, :], wup_scr[...], preferred_element_type=F32)
            gext_scr[sr + r0:sr + r0 + ts, :] = ga[:, 0:TF]
            a_scr[r0:r0 + ts, :] = ga[:, TF:2 * TF]
            fill()
        for h in range(nsub):
            r0 = h * ts
            g = gext_scr[sr + r0:sr + r0 + ts, :]
            g_m2 = gext_scr[sr + r0 - 2 * shift:sr + r0 - 2 * shift + ts, :]
            g_m1 = gext_scr[sr + r0 - shift:sr + r0 - shift + ts, :]
            gc = cb + ((cw[0:1, :] * g_m2 + cw[1:2, :] * g_m1) + cw[2:3, :] * g)
            hmid = jnp.where(col_ok, (gc * jax.nn.sigmoid(gc)) * a_scr[r0:r0 + ts, :], 0.0)
            o_ref[r0:r0 + ts, :] += jnp.dot(hmid.astype(BF16), wdn_scr[...], preferred_element_type=F32)
            fill()
        while fillers:
            fill()
        for t in range(CONV_TAIL):
            so_ref[:, t, :] = gext_scr[tm + t * shift:tm + (t + 1) * shift, :]

    @pl.when(s == 0)
    def _():
        x = x_ref[...]
        u_scr[...] = _rms(x, gam_ref[...]).astype(BF16)
        o_ref[...] = x
        for step in cast_steps(0):
            step()

    for parity in range(2):
        @pl.when(jnp.logical_and(s > 0, s % 2 == parity))
        def _(parity=parity):
            run_tile(1 - parity, cast_steps(parity))

    if final_norm:
        @pl.when(s == NF)
        def _():
            o_ref[...] = _rms(o_ref[...], gfin_ref[...])


def _ffn(x, layer, gamma, w_up, conv_w, conv_b, w_down, st, gfin, *, final_norm):
    assert TF == 2 * LANES
    tm = x.shape[0]
    shift = st.shape[0]
    sr = CONV_TAIL * shift
    kern = functools.partial(_ffn_kernel, tm=tm, shift=shift, nsub=2, final_norm=final_norm)
    last_a = 2 * FF_LANE_BLOCKS - 1

    def wt(s):
        return jnp.minimum(s, NF - 1)

    def ft(s):
        return jnp.maximum(s - 1, 0)

    st_spec = pl.BlockSpec((shift, CONV_TAIL, TF), lambda i, s: (0, 0, ft(s)))
    return pl.pallas_call(
        kern,
        grid=(1, NF + 1),
        in_specs=[pl.BlockSpec((tm, D), lambda i, s: (i, 0), pipeline_mode=pl.Buffered(1)),
                  pl.BlockSpec((1, D), lambda i, s: (0, 0)),
                  pl.BlockSpec((None, D, TF), lambda i, s: (layer, 0, wt(s))),
                  pl.BlockSpec((None, D, LANES), lambda i, s: (layer, 0, FF_LANE_BLOCKS + 2 * wt(s))),
                  pl.BlockSpec((None, D, LANES),
                               lambda i, s: (layer, 0, jnp.minimum(FF_LANE_BLOCKS + 2 * wt(s) + 1, last_a))),
                  pl.BlockSpec((None, 3, TF), lambda i, s: (layer, 0, ft(s))),
                  pl.BlockSpec((None, 1, TF), lambda i, s: (layer, 0, ft(s))),
                  pl.BlockSpec((None, TF, D), lambda i, s: (layer, wt(s), 0)),
                  st_spec,
                  pl.BlockSpec((1, D), lambda i, s: (0, 0))],
        out_specs=[pl.BlockSpec((tm, D), lambda i, s: (i, 0)),
                   st_spec,
                   pl.BlockSpec((D, 2 * TF), lambda i, s: (0, wt(s))),
                   pl.BlockSpec((TF, D), lambda i, s: (wt(s), 0))],
        out_shape=[jax.ShapeDtypeStruct((tm, D), F32),
                   jax.ShapeDtypeStruct(st.shape, F32),
                   jax.ShapeDtypeStruct((D, NF * 2 * TF), BF16),
                   jax.ShapeDtypeStruct((NF * TF, D), BF16)],
        scratch_shapes=[pltpu.VMEM((tm, D), BF16),
                        pltpu.VMEM((sr + tm, TF), F32),
                        pltpu.VMEM((tm, TF), F32),
                        pltpu.VMEM((D, 2 * TF), BF16),
                        pltpu.VMEM((D, 2 * TF), BF16),
                        pltpu.VMEM((TF, D), BF16),
                        pltpu.VMEM((TF, D), BF16)],
        compiler_params=_params(2),
        name="conv_ffn_cast",
    )(x, gamma, w_up, w_up, w_up, conv_w, conv_b, w_down, st, gfin)


def _ffn_bf16_kernel(x_ref, gam_ref, wup_ref, cw_ref, cb_ref, wdn_ref, st_ref, gfin_ref,
                     o_ref, so_ref, u_scr, gext_scr, a_scr, carry_scr,
                     *, tm, sr, shift, tps, nsub, final_norm):
    i = pl.program_id(0)
    f = pl.program_id(1)
    tf2 = 2 * TF

    @pl.when(f == 0)
    def _():
        x = x_ref[...]
        u_scr[...] = _rms(x, gam_ref[...]).astype(BF16)
        o_ref[...] = x
        if tps > 1:
            @pl.when(i == 0)
            def _():
                carry_scr[...] = jnp.zeros_like(carry_scr)

    col_ok = lax.broadcasted_iota(jnp.int32, (1, tf2), 1) < D_FF - f * tf2
    cw = cw_ref[...]
    cb = cb_ref[...]
    if tps == 1:
        gext_scr[0:sr, :] = st_ref[...]
    else:
        gext_scr[0:sr, :] = jnp.where((i % tps) == 0, st_ref[...], carry_scr[f])
    ts = tm // nsub
    for h in range(nsub):
        r0 = h * ts
        ga = jnp.dot(u_scr[r0:r0 + ts, :], wup_ref[...], preferred_element_type=F32)
        gext_scr[sr + r0:sr + r0 + ts, 0:TF] = ga[:, 0:TF]
        gext_scr[sr + r0:sr + r0 + ts, TF:tf2] = ga[:, 2 * TF:3 * TF]
        a_scr[r0:r0 + ts, 0:TF] = ga[:, TF:2 * TF]
        a_scr[r0:r0 + ts, TF:tf2] = ga[:, 3 * TF:4 * TF]
    for h in range(nsub):
        r0 = h * ts
        g = gext_scr[sr + r0:sr + r0 + ts, :]
        g_m2 = gext_scr[sr + r0 - 2 * shift:sr + r0 - 2 * shift + ts, :]
        g_m1 = gext_scr[sr + r0 - shift:sr + r0 - shift + ts, :]
        gc = cb + ((cw[0:1, :] * g_m2 + cw[1:2, :] * g_m1) + cw[2:3, :] * g)
        hmid = jnp.where(col_ok, (gc * jax.nn.sigmoid(gc)) * a_scr[r0:r0 + ts, :], 0.0)
        o_ref[r0:r0 + ts, :] += jnp.dot(hmid.astype(BF16), wdn_ref[...], preferred_element_type=F32)
    g_tail = gext_scr[tm:tm + sr, :]
    so_ref[...] = g_tail
    if tps > 1:
        carry_scr[f] = g_tail

    if final_norm:
        @pl.when(f == pl.num_programs(1) - 1)
        def _():
            o_ref[...] = _rms(o_ref[...], gfin_ref[...])


def _ffn_bf16(x, layer, gamma, wupc, conv_w, conv_b, wdnc, st, gfin, *, tm, sr, shift, tps, st_per_tile,
              final_norm):
    assert NF % 2 == 0
    m = x.shape[0]
    nt = m // tm
    tf2 = 2 * TF
    nf2 = NF // 2
    kern = functools.partial(_ffn_bf16_kernel, tm=tm, sr=sr, shift=shift, tps=tps,
                             nsub=2 if tm >= 64 else 1, final_norm=final_norm)
    st_map = (lambda i, f: (i, f)) if st_per_tile else (lambda i, f: (0, f))
    return pl.pallas_call(
        kern,
        grid=(nt, nf2),
        in_specs=[pl.BlockSpec((tm, D), lambda i, f: (i, 0)),
                  pl.BlockSpec((1, D), lambda i, f: (0, 0)),
                  pl.BlockSpec((D, 2 * tf2), lambda i, f: (0, f)),
                  pl.BlockSpec((None, 3, tf2), lambda i, f: (layer, 0, f)),
                  pl.BlockSpec((None, 1, tf2), lambda i, f: (layer, 0, f)),
                  pl.BlockSpec((tf2, D), lambda i, f: (f, 0)),
                  pl.BlockSpec((sr, tf2), st_map),
                  pl.BlockSpec((1, D), lambda i, f: (0, 0))],
        out_specs=[pl.BlockSpec((tm, D), lambda i, f: (i, 0)),
                   pl.BlockSpec((sr, tf2), lambda i, f: (i, f))],
        out_shape=[jax.ShapeDtypeStruct((m, D), F32),
                   jax.ShapeDtypeStruct((nt * sr, D_FF), F32)],
        scratch_shapes=[pltpu.VMEM((tm, D), BF16),
                        pltpu.VMEM((sr + tm, tf2), F32),
                        pltpu.VMEM((tm, tf2), F32),
                        pltpu.VMEM((nf2, sr, tf2), F32)],
        compiler_params=_params(2),
        name="conv_ffn",
    )(x, gamma, wupc, conv_w, conv_b, wdnc, st, gfin)


def _window_sum(ext_scr, tmp_scrs, base, tm, shift, w):
    starts = {w: base}
    v = w
    while v > 2:
        starts[v // 2] = (starts[v] - (v // 2) * shift) // 8 * 8
        v //= 2
    src = ext_scr
    v = 1
    k = 0
    while True:
        lo = starts[2 * v]
        n = base + tm - lo
        val = src[lo:lo + n, :] + src[lo - v * shift:lo - v * shift + n, :]
        v *= 2
        if v == w:
            return val
        dst = tmp_scrs[k % 2]
        dst[lo:lo + n, :] = val
        src = dst
        k += 1


def _pool_kernel(x_ref, gam_ref, wp_ref, sc_ref, st_ref, o_ref, ut_ref, *scratch, tm, shift, tps, pos0, tr):
    ng = len(POOL_WINDOWS)
    uext_scrs = scratch[0:ng]
    tmp_scrs = scratch[ng:ng + 2]
    carry_scr = scratch[ng + 2]
    i = pl.program_id(0)
    hr = POOL_HALO * shift
    base = POOL_PAD_ROWS + hr

    if tps > 1:
        @pl.when(i == 0)
        def _():
            carry_scr[...] = jnp.zeros_like(carry_scr)

    x = x_ref[...]
    rinv = lax.rsqrt(jnp.mean(x * x, axis=1, keepdims=True) + EPS)
    for kk, w in enumerate(POOL_WINDOWS):
        cs = slice(kk * PG, (kk + 1) * PG)
        uext_scr = uext_scrs[kk]
        xg = x_ref[:, cs]
        ug = xg * rinv * gam_ref[:, cs]
        uext_scr[0:POOL_PAD_ROWS, :] = jnp.zeros((POOL_PAD_ROWS, PG), F32)
        if tps == 1:
            uext_scr[POOL_PAD_ROWS:base, :] = st_ref[:, cs]
        else:
            uext_scr[POOL_PAD_ROWS:base, :] = jnp.where((i % tps) == 0, st_ref[:, cs], carry_scr[:, cs])
        uext_scr[base:base + tm, :] = ug
        acc = _window_sum(uext_scr, tmp_scrs, base, tm, shift, w)
        if pos0 + 1 >= w:
            pooled = acc / float(w) - ug
        else:
            step = (i % tps) * (tm // shift) + lax.broadcasted_iota(jnp.int32, (tm, 1), 0) // shift
            cnt = jnp.minimum(w, pos0 + step + 1).astype(F32)
            pooled = acc / cnt - ug
        y = jnp.dot(pooled.astype(BF16), wp_ref[kk].astype(BF16), preferred_element_type=F32)
        o_ref[:, cs] = xg + y * sc_ref[:, cs]
        ut_ref[:, cs] = ug[tm - tr:tm, :]
        if tps > 1:
            carry_scr[:, cs] = ug[tm - hr:tm, :]


def _pool(x, gamma, wp, sc, st, *, tm, shift, tps, pos0, tr):
    m = x.shape[0]
    nt = m // tm
    hr = POOL_HALO * shift
    ng = len(POOL_WINDOWS)
    kern = functools.partial(_pool_kernel, tm=tm, shift=shift, tps=tps, pos0=pos0, tr=tr)
    return pl.pallas_call(
        kern,
        grid=(nt,),
        in_specs=[pl.BlockSpec((tm, D), lambda i: (i, 0)),
                  pl.BlockSpec((1, D), lambda i: (0, 0)),
                  pl.BlockSpec((None, ng, PG, PG), lambda i: (0, 0, 0, 0)),
                  pl.BlockSpec((1, D), lambda i: (0, 0)),
                  pl.BlockSpec((hr, D), lambda i: (0, 0))],
        out_specs=[pl.BlockSpec((tm, D), lambda i: (i, 0)),
                   pl.BlockSpec((tr, D), lambda i: (i, 0))],
        out_shape=[jax.ShapeDtypeStruct((m, D), F32),
                   jax.ShapeDtypeStruct((nt * tr, D), F32)],
        scratch_shapes=[pltpu.VMEM((POOL_PAD_ROWS + hr + tm, PG), F32)] * (ng + 2)
        + [pltpu.VMEM((hr, D), F32)],
        compiler_params=_params(1),
        name="pool_mixer",
    )(x, gamma, wp, sc, st)


def _pool_sample_kernel(x_ref, gam_ref, wp_ref, sc_ref, st_ref, o_ref, new_ref, rinv_scr, uext_scr,
                        *, nseq, t_dec):
    nh = POOL_HALO - 1
    grp = pl.program_id(0)
    tm = nseq * t_dec
    hr = POOL_HALO * nseq

    @pl.when(grp == 0)
    def _():
        x = x_ref[...]
        rinv_scr[...] = lax.rsqrt(jnp.mean(x * x, axis=1, keepdims=True) + EPS)

    for kk, w in enumerate(POOL_WINDOWS):
        @pl.when(grp == kk)
        def _(kk=kk, w=w):
            cs = slice(kk * PG, (kk + 1) * PG)
            xg = x_ref[:, cs]
            ug = xg * rinv_scr[...] * gam_ref[:, cs]
            for t in range(nh):
                uext_scr[(t + 1) * nseq:(t + 2) * nseq, :] = st_ref[t]
            uext_scr[hr:hr + tm, :] = ug
            acc = ug
            for j in range(1, w):
                acc = acc + uext_scr[hr - j * nseq:hr - j * nseq + tm, :]
            pooled = acc / float(w) - ug
            y = jnp.dot(pooled.astype(BF16), wp_ref[kk].astype(BF16), preferred_element_type=F32)
            o_ref[...] = xg + y * sc_ref[:, cs]
            for t in range(nh):
                src = (t + t_dec + 1) * nseq
                new_ref[t] = uext_scr[src:src + nseq, :]


def _pool_sample(x, gamma, wp, sc, state, *, nseq, t_dec):
    assert PAST_LEN + 1 >= max(POOL_WINDOWS)
    nh = POOL_HALO - 1
    ng = len(POOL_WINDOWS)
    tm = nseq * t_dec
    hr = POOL_HALO * nseq
    kern = functools.partial(_pool_sample_kernel, nseq=nseq, t_dec=t_dec)
    hist_spec = pl.BlockSpec((nh, nseq, PG), lambda g: (0, 0, g))
    return pl.pallas_call(
        kern,
        grid=(ng,),
        in_specs=[pl.BlockSpec((tm, D), lambda g: (0, 0)),
                  pl.BlockSpec((1, D), lambda g: (0, 0)),
                  pl.BlockSpec((None, ng, PG, PG), lambda g: (0, 0, 0, 0)),
                  pl.BlockSpec((1, D), lambda g: (0, 0)),
                  hist_spec],
        out_specs=[pl.BlockSpec((tm, PG), lambda g: (0, g)),
                   hist_spec],
        out_shape=[jax.ShapeDtypeStruct((tm, D), F32),
                   jax.ShapeDtypeStruct((nh, nseq, D), F32)],
        scratch_shapes=[pltpu.VMEM((tm, 1), F32),
                        pltpu.VMEM((hr + tm, PG), F32)],
        compiler_params=_params(1),
        name="pool_mixer_sample",
    )(x, gamma, wp, sc, state)


def _pad_cols(a, n):
    return jnp.pad(a, ((0, 0), (0, n - a.shape[1])))


def _to_time_major(a, t):
    nseq = a.shape[0]
    a = a.reshape((nseq // SAMPLE_TILE_SEQS, SAMPLE_TILE_SEQS, t) + a.shape[2:])
    a = jnp.swapaxes(a, 1, 2)
    return a.reshape((nseq * t,) + a.shape[3:])


def _from_time_major(a, nseq, t):
    a = a.reshape((nseq // SAMPLE_TILE_SEQS, t, SAMPLE_TILE_SEQS) + a.shape[1:])
    a = jnp.swapaxes(a, 1, 2)
    return a.reshape((nseq, t) + a.shape[3:])


def kernel(x_prompt, x_sample, state_mlstm_C, state_mlstm_n, state_mlstm_m, state_pool, state_ffn_conv,
           meta_tokens, norm_mix, norm_ffn, norm_final, w_mlstm_in, b_mlstm_gate, g_mlstm_out, w_mlstm_out,
           w_pool, pool_scale, w_up, conv_w, conv_b, w_down):
    bsz, seq, _ = x_prompt.shape
    nseq, t_dec, _ = x_sample.shape

    w_in_t = jnp.swapaxes(w_mlstm_in[0], 0, 1)
    bias = _pad_cols(b_mlstm_gate[0][None, :], GATE_LANES)
    w_out = w_mlstm_out
    gout = g_mlstm_out[0][None, :]
    wp = w_pool
    psc = pool_scale[0][None, :]
    gfin = norm_final[None, :]
    conv_b3 = conv_b[:, None, :]
    ffn_w = [(layer, norm_ffn[layer][None, :], w_up, conv_w, conv_b3, w_down) for layer in range(2)]
    g_mix0 = norm_mix[0][None, :]
    g_mix1 = norm_mix[1][None, :]

    ffn_cache = {}
    mix_cache = {}

    def proj(x, **kw):
        if "w_in" not in mix_cache:
            p, gates, wtc, wgc = _proj(x, g_mix0, w_in_t, w_in_t, **kw)
            mix_cache["w_in"] = (wtc, wgc)
            return p, gates
        return _proj(x, g_mix0, *mix_cache["w_in"], **kw)

    def out_proj(a, x, *, tm):
        if "w_out" not in mix_cache:
            y, mix_cache["w_out"] = _mmres_cast(a, w_out, x, tn=512)
            return y
        return _mmres(a, mix_cache["w_out"], x, tm=tm)

    def ffn_first(x, layer, st, *, final_norm):
        lyr, gamma, wu, cwt, cbs, wd = ffn_w[layer]
        y, cs, wupc, wdnc = _ffn(x, lyr, gamma, wu, cwt, cbs, wd, st, gfin, final_norm=final_norm)
        ffn_cache[layer] = (wupc, wdnc)
        return y, cs

    def ffn(x, layer, st, **kw):
        lyr, gamma, _, cwt, cbs, _ = ffn_w[layer]
        wupc, wdnc = ffn_cache[layer]
        return _ffn_bf16(x, lyr, gamma, wupc, cwt, cbs, wdnc, st, gfin, **kw)

    def long_stream(x, nb, s, st, *, tm, chunk, lead_pad, pos0):
        c0, n0, m0, conv0, pool0, conv1 = st
        tm_ffn = tm
        p, gates = proj(x, tm=tm, tn=1024, out_dtype=BF16)
        p = p.reshape(nb, s, PW)
        gates = gates.reshape(nb, s, GATE_LANES)
        if lead_pad:
            p = jnp.pad(p, ((0, 0), (lead_pad, 0), (0, 0)))
            gates = jnp.pad(gates, ((0, 0), (lead_pad, 0), (0, 0)))
        hg, c_new, n_new, m_new = _scan(p, gates, bias, gout, c0, n0, m0, L=chunk, lead_pad=lead_pad)
        hg = hg[:, lead_pad:].reshape(nb * s, VW)
        x1 = out_proj(hg, x, tm=min(tm, 512))
        x2, cs0 = ffn(x1, 0, conv0, tm=tm_ffn, sr=8, shift=1, tps=s // tm_ffn,
                      st_per_tile=False, final_norm=False)
        x3, ut = _pool(x2, g_mix1, wp, psc, pool0, tm=tm, shift=1, tps=s // tm, pos0=pos0,
                       tr=POOL_HALO)
        y, cs1 = ffn(x3, 1, conv1, tm=tm_ffn, sr=8, shift=1, tps=s // tm_ffn,
                     st_per_tile=False, final_norm=True)
        cs0 = cs0.reshape(nb, s // tm_ffn, 8, D_FF)[:, -1]
        cs1 = cs1.reshape(nb, s // tm_ffn, 8, D_FF)[:, -1]
        ut = ut.reshape(nb, s // tm, POOL_HALO, D)[:, -1]
        return y, (c_new, n_new, m_new, cs0, ut, cs1)

    def prompt_streams():
        zero_state = (jnp.zeros((H, DK, DV), F32), jnp.zeros((H, 8, DK), F32), jnp.zeros((H, 8, GATE_LANES), F32),
                      jnp.zeros((8, D_FF), F32), jnp.zeros((POOL_HALO, D), F32), jnp.zeros((8, D_FF), F32))
        _, (c_m, n_m, m_m, cs0_m, ut_m, cs1_m) = long_stream(
            meta_tokens, 1, N_META, zero_state, tm=N_META, chunk=128, lead_pad=128 - N_META, pos0=0)

        y_p, (c_p, n_p, m_p, cs0_p, ut_p, cs1_p) = long_stream(
            x_prompt.reshape(bsz * seq, D), bsz, seq, (c_m[0], n_m[0], m_m[0], cs0_m[0], ut_m[0], cs1_m[0]),
            tm=1024, chunk=SCAN_CHUNK, lead_pad=0, pos0=N_META)
        return (y_p.reshape(bsz, seq, D), c_p[None], n_p[:, :, 0][None], m_p[:, :, 0, 0][None],
                ut_p[:, 1:][None], jnp.stack([cs0_p[:, 6:], cs1_p[:, 6:]]))

    xs = x_sample.reshape(nseq * t_dec, D)
    p_s, gates_s = proj(xs, tm=nseq * t_dec, tn=512, out_dtype=F32)
    mtok = _pad_cols(jnp.repeat(state_mlstm_m[0], t_dec, axis=0), GATE_LANES)
    n_hm = jnp.swapaxes(state_mlstm_n[0], 0, 1)
    hg_s, C_s, n_s_hm, m_s_hm = _scan_s(p_s, gates_s, bias, gout, mtok, state_mlstm_C[0], n_hm, T=t_dec)
    assert nseq == SAMPLE_TILE_SEQS
    x1 = out_proj(hg_s.reshape(t_dec * nseq, VW), x_sample, tm=512)
    x2, cs0_s = ffn_first(x1, 0, state_ffn_conv[0], final_norm=False)
    x3, pool_new = _pool_sample(x2, g_mix1, wp, psc, jnp.swapaxes(state_pool[0], 0, 1), nseq=nseq, t_dec=t_dec)
    y_s, cs1_s = ffn_first(x3, 1, state_ffn_conv[1], final_norm=True)
    y_sample = _from_time_major(y_s, nseq, t_dec)
    n_s = jnp.swapaxes(n_s_hm, 0, 1)[None]
    m_s = jnp.swapaxes(m_s_hm[:, :, 0], 0, 1)[None]
    pool_s = jnp.swapaxes(pool_new, 0, 1)[None]
    conv_s = jnp.stack([cs0_s, cs1_s])

    y_prompt, C_p, n_p, m_p, pool_p, conv_p = prompt_streams()
    return (y_prompt, y_sample, C_p, n_p, m_p, pool_p, conv_p,
            C_s[None], n_s, m_s, pool_s, conv_s)
```

```python
import functools

import jax
import jax.numpy as jnp
from jax import lax
from jax.experimental import pallas as pl
from jax.experimental.pallas import tpu as pltpu

F32 = jnp.float32
BF16 = jnp.bfloat16

EPS = 1e-6
D = 2048
H = 4
DK = 256
DV = 512
QKW = H * DK
VW = H * DV
PW = 2 * QKW + 2 * VW
GATE_LANES = 128
SCALE = DK ** -0.5
POOL_WINDOWS = (2, 4, 8, 16)
PG = D // len(POOL_WINDOWS)
POOL_HALO = 16
POOL_PAD_ROWS = 16
D_FF = 5504
TF = 256
NF = -(-D_FF // TF)
LANES = 128
FF_LANE_BLOCKS = D_FF // LANES
CONV_TAIL = 2
N_META = 16
PAST_LEN = 16384
SCAN_CHUNK = 512
SAMPLE_GROUP = 16
SAMPLE_TILE_SEQS = 128
VMEM_LIMIT = 60 * 1024 * 1024


def _params(n_axes):
    return pltpu.CompilerParams(dimension_semantics=("arbitrary",) * n_axes,
                                vmem_limit_bytes=VMEM_LIMIT)


def _rms(x, g):
    return x * lax.rsqrt(jnp.mean(x * x, axis=-1, keepdims=True) + EPS) * g


def _log_sigmoid(x):
    return jnp.minimum(x, 0.0) - jnp.log(1.0 + jnp.exp(-jnp.abs(x)))


_NT = (((1,), (1,)), ((), ()))


def _proj_kernel(x_ref, g_ref, wt_ref, wgt_ref, p_ref, gate_ref, *rest, emit_bf16):
    u_scr = rest[-1]

    @pl.when(pl.program_id(1) == 0)
    def _():
        ub = _rms(x_ref[...], g_ref[...]).astype(BF16)
        u_scr[...] = ub
        wg = wgt_ref[...]
        if emit_bf16:
            row_ok = lax.broadcasted_iota(jnp.int32, (GATE_LANES, 1), 0) < 2 * H
            wg = jnp.where(row_ok, wg, 0.0).astype(BF16)
            rest[1][...] = wg
        gate_ref[...] = lax.dot_general(ub, wg, _NT, preferred_element_type=F32)

    w = wt_ref[...].astype(BF16)
    if emit_bf16:
        rest[0][...] = w
    p_ref[...] = lax.dot_general(u_scr[...], w, _NT, preferred_element_type=F32).astype(p_ref.dtype)


def _proj(x, gamma, wt, wgt, *, tm, tn, out_dtype):
    m = x.shape[0]
    emit = wt.dtype != BF16
    gate_spec = pl.BlockSpec((GATE_LANES, D), (lambda i, j: (PW // GATE_LANES, 0)) if emit
                             else (lambda i, j: (0, 0)))
    out_specs = [pl.BlockSpec((tm, tn), lambda i, j: (i, j)),
                 pl.BlockSpec((tm, GATE_LANES), lambda i, j: (i, 0))]
    out_shape = [jax.ShapeDtypeStruct((m, PW), out_dtype),
                 jax.ShapeDtypeStruct((m, GATE_LANES), F32)]
    if emit:
        out_specs += [pl.BlockSpec((tn, D), lambda i, j: (j, 0)),
                      pl.BlockSpec((GATE_LANES, D), lambda i, j: (0, 0))]
        out_shape += [jax.ShapeDtypeStruct((PW, D), BF16),
                      jax.ShapeDtypeStruct((GATE_LANES, D), BF16)]
    return pl.pallas_call(
        functools.partial(_proj_kernel, emit_bf16=emit),
        grid=(m // tm, PW // tn),
        in_specs=[pl.BlockSpec((tm, D), lambda i, j: (i, 0)),
                  pl.BlockSpec((1, D), lambda i, j: (0, 0)),
                  pl.BlockSpec((tn, D), lambda i, j: (j, 0)),
                  gate_spec],
        out_specs=out_specs,
        out_shape=out_shape,
        scratch_shapes=[pltpu.VMEM((tm, D), BF16)],
        compiler_params=_params(2),
        name="proj_cast" if emit else "proj",
    )(x, gamma, wt, wgt)


def _head_output(num, den, m_t, gout, o):
    hv = num * (1.0 / jnp.maximum(jnp.abs(den), jnp.exp(-m_t)))
    hv = hv * lax.rsqrt(jnp.mean(hv * hv, axis=1, keepdims=True) + EPS)
    return hv * gout * jax.nn.sigmoid(o.astype(F32))


def _scan_kernel(q_ref, k_ref, v_ref, o_ref, gt_ref, bias_ref, gout_ref, c0_ref, n0_ref, m0_ref,
                 hg_ref, cout_ref, nout_ref, mout_ref, c_scr, n_scr, m_scr, *, L, lead_pad, nc):
    c = pl.program_id(1)

    @pl.when(c == 0)
    def _():
        c_scr[...] = c0_ref[...]
        n_scr[...] = n0_ref[...]
        m_scr[...] = m0_ref[...]

    gates = gt_ref[...] + bias_ref[...]
    gates_t = gates.T
    row = lax.broadcasted_iota(jnp.int32, (L, L), 0)
    col = lax.broadcasted_iota(jnp.int32, (L, L), 1)
    causal = row >= col
    causal_t = row <= col
    if lead_pad:
        live_col = lax.broadcasted_iota(jnp.int32, (L, 1), 0) >= lead_pad
        live_row = lax.broadcasted_iota(jnp.int32, (1, L), 1) >= lead_pad

    for hh in range(H):
        ig_col = gates[:, hh:hh + 1]
        ig_row = gates_t[hh:hh + 1, :]
        lf_col = _log_sigmoid(gates[:, H + hh:H + hh + 1])
        lf_row = _log_sigmoid(gates_t[H + hh:H + hh + 1, :])
        if lead_pad:
            ig_col = jnp.where(live_col, ig_col, -jnp.inf)
            ig_row = jnp.where(live_row, ig_row, -jnp.inf)
            lf_col = jnp.where(live_col, lf_col, 0.0)
            lf_row = jnp.where(live_row, lf_row, 0.0)
        b_col = jnp.sum(jnp.where(causal, lf_row, 0.0), axis=1, keepdims=True)
        b_row = jnp.sum(jnp.where(causal_t, lf_col, 0.0), axis=0, keepdims=True)
        m_prev = m_scr[hh, 0:1, 0:1]
        d = jnp.where(causal, b_col - b_row + ig_row, -jnp.inf)
        inter = b_col + m_prev
        m_t = jnp.maximum(inter, jnp.max(d, axis=1, keepdims=True))
        w_inter = jnp.exp(inter - m_t) * SCALE

        q = q_ref[:, hh * DK:(hh + 1) * DK]
        k = k_ref[:, hh * DK:(hh + 1) * DK]
        v = v_ref[:, hh * DV:(hh + 1) * DV]
        qk = lax.dot_general(q, k, _NT, preferred_element_type=F32)
        s = qk * (jnp.exp(d - m_t) * SCALE)
        cmat = c_scr[hh]
        nvec = n_scr[hh, 0:1, :]
        num = w_inter * jnp.dot(q, cmat.astype(BF16), preferred_element_type=F32) \
            + jnp.dot(s.astype(BF16), v, preferred_element_type=F32)
        den = w_inter * jnp.sum(q.astype(F32) * nvec, axis=1, keepdims=True) \
            + jnp.sum(s, axis=1, keepdims=True)
        hout = _head_output(num, den, m_t, gout_ref[:, hh * DV:(hh + 1) * DV],
                            o_ref[:, hh * DV:(hh + 1) * DV])
        hg_ref[:, hh * DV:(hh + 1) * DV] = hout.astype(hg_ref.dtype)

        m_new = m_t[L - 1:L, :]
        b_last = b_col[L - 1:L, :]
        decay = jnp.exp(b_last + m_prev - m_new)
        wk = jnp.exp(b_last - b_col + ig_col - m_new) * k.astype(F32)
        c_scr[hh] = decay * cmat + jnp.dot(wk.T.astype(BF16), v, preferred_element_type=F32)
        n_scr[hh] = jnp.broadcast_to(decay * nvec + jnp.sum(wk, axis=0, keepdims=True), (8, DK))
        m_scr[hh] = jnp.broadcast_to(m_new, (8, GATE_LANES))

    @pl.when(c == nc - 1)
    def _():
        cout_ref[...] = c_scr[...]
        nout_ref[...] = n_scr[...]
        mout_ref[...] = m_scr[...]


def _scan(p, gates, bias, gout, c0, n0, m0, *, L, lead_pad=0):
    b, s, _ = p.shape
    nc = s // L
    kern = functools.partial(_scan_kernel, L=L, lead_pad=lead_pad, nc=nc)
    return pl.pallas_call(
        kern,
        grid=(b, nc),
        in_specs=[pl.BlockSpec((None, L, QKW), lambda i, c: (i, c, 0)),
                  pl.BlockSpec((None, L, QKW), lambda i, c: (i, c, 1)),
                  pl.BlockSpec((None, L, VW), lambda i, c: (i, c, 1)),
                  pl.BlockSpec((None, L, VW), lambda i, c: (i, c, 2)),
                  pl.BlockSpec((None, L, GATE_LANES), lambda i, c: (i, c, 0)),
                  pl.BlockSpec((1, GATE_LANES), lambda i, c: (0, 0)),
                  pl.BlockSpec((1, VW), lambda i, c: (0, 0)),
                  pl.BlockSpec((H, DK, DV), lambda i, c: (0, 0, 0)),
                  pl.BlockSpec((H, 8, DK), lambda i, c: (0, 0, 0)),
                  pl.BlockSpec((H, 8, GATE_LANES), lambda i, c: (0, 0, 0))],
        out_specs=[pl.BlockSpec((None, L, VW), lambda i, c: (i, c, 0)),
                   pl.BlockSpec((None, H, DK, DV), lambda i, c: (i, 0, 0, 0)),
                   pl.BlockSpec((None, H, 8, DK), lambda i, c: (i, 0, 0, 0)),
                   pl.BlockSpec((None, H, 8, GATE_LANES), lambda i, c: (i, 0, 0, 0))],
        out_shape=[jax.ShapeDtypeStruct((b, s, VW), BF16),
                   jax.ShapeDtypeStruct((b, H, DK, DV), F32),
                   jax.ShapeDtypeStruct((b, H, 8, DK), F32),
                   jax.ShapeDtypeStruct((b, H, 8, GATE_LANES), F32)],
        scratch_shapes=[pltpu.VMEM((H, DK, DV), F32),
                        pltpu.VMEM((H, 8, DK), F32),
                        pltpu.VMEM((H, 8, GATE_LANES), F32)],
        compiler_params=_params(2),
        name="scan",
    )(p, p, p, p, gates, bias, gout, c0, n0, m0)


def _scan_s_kernel(q_ref, k_ref, v_ref, o_ref, gt_ref, bias_ref, gout_ref, mtok_ref, c_ref, n_ref,
                   hg_ref, cout_ref, nout_ref, mout_ref, qc_scr, ntok_scr, *, T, NB):
    hh = pl.program_id(1)
    LT = NB * T
    gates = gt_ref[...] + bias_ref[...]
    gates_t = gates.T
    lane = lax.broadcasted_iota(jnp.int32, (LT, GATE_LANES), 1)
    sub = lax.broadcasted_iota(jnp.int32, (GATE_LANES, LT), 0)

    def pick_col(a, idx):
        return jnp.sum(jnp.where(lane == idx, a, 0.0), axis=1, keepdims=True)

    def pick_row(a, idx):
        return jnp.sum(jnp.where(sub == idx, a, 0.0), axis=0, keepdims=True)

    ig_col = pick_col(gates, hh)
    ig_row = pick_row(gates_t, hh)
    lf_col = _log_sigmoid(pick_col(gates, hh + H))
    lf_row = _log_sigmoid(pick_row(gates_t, hh + H))
    m_prev = pick_col(mtok_ref[...], hh)

    row = lax.broadcasted_iota(jnp.int32, (LT, LT), 0)
    col = lax.broadcasted_iota(jnp.int32, (LT, LT), 1)
    same = (row // T) == (col // T)
    causal = jnp.logical_and(same, row >= col)
    causal_t = jnp.logical_and(same, row <= col)
    b_col = jnp.sum(jnp.where(causal, lf_row, 0.0), axis=1, keepdims=True)
    b_row = jnp.sum(jnp.where(causal_t, lf_col, 0.0), axis=0, keepdims=True)
    b_end = jnp.sum(jnp.where(same, lf_row, 0.0), axis=1, keepdims=True)
    d = jnp.where(causal, b_col - b_row + ig_row, -jnp.inf)
    inter = b_col + m_prev
    m_t = jnp.maximum(inter, jnp.max(d, axis=1, keepdims=True))
    d_end = jnp.where(same, b_end - b_row + ig_row, -jnp.inf)
    m_new = jnp.maximum(b_end + m_prev, jnp.max(d_end, axis=1, keepdims=True))
    w_inter = jnp.exp(inter - m_t) * SCALE

    q32 = q_ref[...]
    k32 = k_ref[...]
    q = q32.astype(BF16)
    v = v_ref[...].astype(BF16)
    qk = lax.dot_general(q, k32.astype(BF16), (((1,), (1,)), ((), ())), preferred_element_type=F32)
    s = qk * (jnp.exp(d - m_t) * SCALE)
    num_intra = jnp.dot(s.astype(BF16), v, preferred_element_type=F32)
    den_intra = jnp.sum(s, axis=1, keepdims=True)

    decay = jnp.exp(b_end + m_prev - m_new)
    wk = jnp.exp(b_end - b_col + ig_col - m_new) * k32
    wk_t = wk.T
    col_seq = lax.broadcasted_iota(jnp.int32, (DK, LT), 1) // T

    for bb in range(NB):
        r0 = bb * T
        cmat = c_ref[bb]
        nvec = n_ref[bb:bb + 1, :]
        qc_scr[r0:r0 + T, :] = jnp.dot(q32[r0:r0 + T, :].astype(BF16), cmat.astype(BF16),
                                       preferred_element_type=F32)
        ntok_scr[r0:r0 + T, :] = jnp.broadcast_to(nvec, (T, DK))
        upd = jnp.dot(jnp.where(col_seq == bb, wk_t, 0.0).astype(BF16), v, preferred_element_type=F32)
        dec = decay[r0:r0 + 1, :]
        cout_ref[bb] = dec * cmat + upd
        nout_ref[bb:bb + 1, :] = dec * nvec + jnp.sum(wk[r0:r0 + T, :], axis=0, keepdims=True)
        mout_ref[bb:bb + 1, :] = jnp.broadcast_to(m_new[r0:r0 + 1, :], (1, GATE_LANES))

    num = w_inter * qc_scr[...] + num_intra
    den = w_inter * jnp.sum(q32 * ntok_scr[...], axis=1, keepdims=True) + den_intra
    hout = _head_output(num, den, m_t, gout_ref[...], o_ref[...]).astype(hg_ref.dtype)
    src_row = (row % NB) * T + row // NB
    perm = jnp.where(col == src_row, 1.0, 0.0).astype(hg_ref.dtype)
    moved = jnp.dot(perm, hout, preferred_element_type=F32)
    hg_ref[...] = moved.reshape(T, NB, DV).astype(hg_ref.dtype)


def _scan_s(p, gates, bias, gout, mtok, c, n_hm, *, T):
    nseq = c.shape[0]
    nb = SAMPLE_GROUP
    lt = nb * T
    kern = functools.partial(_scan_s_kernel, T=T, NB=nb)
    return pl.pallas_call(
        kern,
        grid=(nseq // nb, H),
        in_specs=[pl.BlockSpec((lt, DK), lambda g, h: (g, h)),
                  pl.BlockSpec((lt, DK), lambda g, h: (g, H + h)),
                  pl.BlockSpec((lt, DV), lambda g, h: (g, H + h)),
                  pl.BlockSpec((lt, DV), lambda g, h: (g, 2 * H + h)),
                  pl.BlockSpec((lt, GATE_LANES), lambda g, h: (g, 0)),
                  pl.BlockSpec((1, GATE_LANES), lambda g, h: (0, 0)),
                  pl.BlockSpec((1, DV), lambda g, h: (0, h)),
                  pl.BlockSpec((lt, GATE_LANES), lambda g, h: (g, 0)),
                  pl.BlockSpec((nb, None, DK, DV), lambda g, h: (g, h, 0, 0)),
                  pl.BlockSpec((None, nb, DK), lambda g, h: (h, g, 0))],
        out_specs=[pl.BlockSpec((T, nb, DV), lambda g, h: (0, g, h)),
                   pl.BlockSpec((nb, None, DK, DV), lambda g, h: (g, h, 0, 0)),
                   pl.BlockSpec((None, nb, DK), lambda g, h: (h, g, 0)),
                   pl.BlockSpec((None, nb, GATE_LANES), lambda g, h: (h, g, 0))],
        out_shape=[jax.ShapeDtypeStruct((T, nseq, VW), BF16),
                   jax.ShapeDtypeStruct((nseq, H, DK, DV), F32),
                   jax.ShapeDtypeStruct((H, nseq, DK), F32),
                   jax.ShapeDtypeStruct((H, nseq, GATE_LANES), F32)],
        scratch_shapes=[pltpu.VMEM((lt, DV), F32), pltpu.VMEM((lt, DK), F32)],
        compiler_params=_params(2),
        name="scan_sample",
    )(p, p, p, p, gates, bias, gout, mtok, c, n_hm)


def _mmres_cast_kernel(a_ref, w_ref, x_ref, o_ref, wc_ref):
    w = w_ref[...].astype(BF16)
    wc_ref[...] = w
    y = jnp.dot(a_ref[...], w, preferred_element_type=F32)
    nseq, t_dec, _ = x_ref.shape
    for t in range(t_dec):
        o_ref[t * nseq:(t + 1) * nseq, :] = x_ref[:, t, :] + y[t * nseq:(t + 1) * nseq, :]


def _mmres_cast(a, w, x, *, tn):
    m, kdim = a.shape
    n = w.shape[2]
    nseq, t_dec, _ = x.shape
    return pl.pallas_call(
        _mmres_cast_kernel,
        grid=(n // tn,),
        in_specs=[pl.BlockSpec((m, kdim), lambda j: (0, 0)),
                  pl.BlockSpec((None, kdim, tn), lambda j: (0, 0, j)),
                  pl.BlockSpec((nseq, t_dec, tn), lambda j: (0, 0, j))],
        out_specs=[pl.BlockSpec((m, tn), lambda j: (0, j)),
                   pl.BlockSpec((kdim, tn), lambda j: (0, j))],
        out_shape=[jax.ShapeDtypeStruct((m, n), F32),
                   jax.ShapeDtypeStruct((kdim, n), BF16)],
        compiler_params=_params(1),
        name="out_proj_cast",
    )(a, w, x)


def _mmres_kernel(a_ref, w_ref, x_ref, o_ref):
    o_ref[...] = x_ref[...] + jnp.dot(a_ref[...], w_ref[...], preferred_element_type=F32)


def _mmres(a, w, x, *, tm):
    m, kdim = a.shape
    n = w.shape[1]
    return pl.pallas_call(
        _mmres_kernel,
        grid=(m // tm,),
        in_specs=[pl.BlockSpec((tm, kdim), lambda i: (i, 0)),
                  pl.BlockSpec((kdim, n), lambda i: (0, 0)),
                  pl.BlockSpec((tm, n), lambda i: (i, 0))],
        out_specs=pl.BlockSpec((tm, n), lambda i: (i, 0)),
        out_shape=jax.ShapeDtypeStruct((m, n), F32),
        compiler_params=_params(1),
        name="out_proj",
    )(a, w, x)


def _ffn_kernel(x_ref, gam_ref, wg_ref, wa0_ref, wa1_ref, cw_ref, cb_ref, wd_ref, st_ref, gfin_ref,
                o_ref, so_ref, wupc_ref, wdnc_ref,
                u_scr, gext_scr, a_scr, wup0_scr, wup1_scr, wdn0_scr, wdn1_scr,
                *, tm, shift, nsub, final_norm):
    s = pl.program_id(1)
    sr = CONV_TAIL * shift
    wup_slots = (wup0_scr, wup1_scr)
    wdn_slots = (wdn0_scr, wdn1_scr)

    def cast_steps(slot):
        wup_scr = wup_slots[slot]
        wdn_scr = wdn_slots[slot]
        valid = D_FF - jnp.minimum(s, NF - 1) * TF

        def cast_gate():
            ok = lax.broadcasted_iota(jnp.int32, (1, TF), 1) < valid
            w = jnp.where(ok, wg_ref[...], 0.0).astype(BF16)
            wup_scr[:, 0:TF] = w
            wupc_ref[:, 0:TF] = w

        def cast_value():
            lane = lax.broadcasted_iota(jnp.int32, (1, LANES), 1)
            w0 = jnp.where(lane < valid, wa0_ref[...], 0.0).astype(BF16)
            w1 = jnp.where(lane + LANES < valid, wa1_ref[...], 0.0).astype(BF16)
            wup_scr[:, TF:TF + LANES] = w0
            wup_scr[:, TF + LANES:2 * TF] = w1
            wupc_ref[:, TF:TF + LANES] = w0
            wupc_ref[:, TF + LANES:2 * TF] = w1

        def cast_down(r0, rows):
            row_ok = r0 + lax.broadcasted_iota(jnp.int32, (rows, 1), 0) < valid
            w = jnp.where(row_ok, wd_ref[r0:r0 + rows, :], 0.0).astype(BF16)
            wdn_scr[r0:r0 + rows, :] = w
            wdnc_ref[r0:r0 + rows, :] = w

        half = TF // 2
        return [cast_gate, cast_value, functools.partial(cast_down, 0, half),
                functools.partial(cast_down, half, half)]

    def run_tile(slot, fillers=()):
        fillers = list(fillers)

        def fill():
            if fillers:
                fillers.pop(0)()

        wup_scr = wup_slots[slot]
        wdn_scr = wdn_slots[slot]
        f = s - 1
        col_ok = lax.broadcasted_iota(jnp.int32, (1, TF), 1) < D_FF - f * TF
        cw = cw_ref[...]
        cb = cb_ref[...]
        for t in range(CONV_TAIL):
            gext_scr[t * shift:(t + 1) * shift, :] = st_ref[:, t, :]
        ts = tm // nsub
        for h in range(nsub):
            r0 = h * ts
            ga = jnp.dot(u_scr[r0:r0 + ts, :], wup_scr[...], preferred_element_type=F32)
            gext_scr[sr + r0:sr + r0 + ts, :] = ga[:, 0:TF]
            a_scr[r0:r0 + ts, :] = ga[:, TF:2 * TF]
            fill()
        for h in range(nsub):
            r0 = h * ts
            g = gext_scr[sr + r0:sr + r0 + ts, :]
            g_m2 = gext_scr[sr + r0 - 2 * shift:sr + r0 - 2 * shift + ts, :]
            g_m1 = gext_scr[sr + r0 - shift:sr + r0 - shift + ts, :]
            gc = cb + ((cw[0:1, :] * g_m2 + cw[1:2, :] * g_m1) + cw[2:3, :] * g)
            hmid = jnp.where(col_ok, (gc * jax.nn.sigmoid(gc)) * a_scr[r0:r0 + ts, :], 0.0)
            o_ref[r0:r0 + ts, :] += jnp.dot(hmid.astype(BF16), wdn_scr[...], preferred_element_type=F32)
            fill()
        while fillers:
            fill()
        for t in range(CONV_TAIL):
            so_ref[:, t, :] = gext_scr[tm + t * shift:tm + (t + 1) * shift, :]

    @pl.when(s == 0)
    def _():
        x = x_ref[...]
        u_scr[...] = _rms(x, gam_ref[...]).astype(BF16)
        o_ref[...] = x
        for step in cast_steps(0):
            step()

    for parity in range(2):
        @pl.when(jnp.logical_and(s > 0, s % 2 == parity))
        def _(parity=parity):
            run_tile(1 - parity, cast_steps(parity))

    if final_norm:
        @pl.when(s == NF)
        def _():
            o_ref[...] = _rms(o_ref[...], gfin_ref[...])


def _ffn(x, layer, gamma, w_up, conv_w, conv_b, w_down, st, gfin, *, final_norm):
    assert TF == 2 * LANES
    tm = x.shape[0]
    shift = st.shape[0]
    sr = CONV_TAIL * shift
    kern = functools.partial(_ffn_kernel, tm=tm, shift=shift, nsub=2, final_norm=final_norm)
    last_a = 2 * FF_LANE_BLOCKS - 1

    def wt(s):
        return jnp.minimum(s, NF - 1)

    def ft(s):
        return jnp.maximum(s - 1, 0)

    st_spec = pl.BlockSpec((shift, CONV_TAIL, TF), lambda i, s: (0, 0, ft(s)))
    return pl.pallas_call(
        kern,
        grid=(1, NF + 1),
        in_specs=[pl.BlockSpec((tm, D), lambda i, s: (i, 0), pipeline_mode=pl.Buffered(1)),
                  pl.BlockSpec((1, D), lambda i, s: (0, 0)),
                  pl.BlockSpec((None, D, TF), lambda i, s: (layer, 0, wt(s))),
                  pl.BlockSpec((None, D, LANES), lambda i, s: (layer, 0, FF_LANE_BLOCKS + 2 * wt(s))),
                  pl.BlockSpec((None, D, LANES),
                               lambda i, s: (layer, 0, jnp.minimum(FF_LANE_BLOCKS + 2 * wt(s) + 1, last_a))),
                  pl.BlockSpec((None, 3, TF), lambda i, s: (layer, 0, ft(s))),
                  pl.BlockSpec((None, 1, TF), lambda i, s: (layer, 0, ft(s))),
                  pl.BlockSpec((None, TF, D), lambda i, s: (layer, wt(s), 0)),
                  st_spec,
                  pl.BlockSpec((1, D), lambda i, s: (0, 0))],
        out_specs=[pl.BlockSpec((tm, D), lambda i, s: (i, 0)),
                   st_spec,
                   pl.BlockSpec((D, 2 * TF), lambda i, s: (0, wt(s))),
                   pl.BlockSpec((TF, D), lambda i, s: (wt(s), 0))],
        out_shape=[jax.ShapeDtypeStruct((tm, D), F32),
                   jax.ShapeDtypeStruct(st.shape, F32),
                   jax.ShapeDtypeStruct((D, NF * 2 * TF), BF16),
                   jax.ShapeDtypeStruct((NF * TF, D), BF16)],
        scratch_shapes=[pltpu.VMEM((tm, D), BF16),
                        pltpu.VMEM((sr + tm, TF), F32),
                        pltpu.VMEM((tm, TF), F32),
                        pltpu.VMEM((D, 2 * TF), BF16),
                        pltpu.VMEM((D, 2 * TF), BF16),
                        pltpu.VMEM((TF, D), BF16),
                        pltpu.VMEM((TF, D), BF16)],
        compiler_params=_params(2),
        name="conv_ffn_cast",
    )(x, gamma, w_up, w_up, w_up, conv_w, conv_b, w_down, st, gfin)


def _ffn_bf16_kernel(x_ref, gam_ref, wup_ref, cw_ref, cb_ref, wdn_ref, st_ref, gfin_ref,
                     o_ref, so_ref, u_scr, gext_scr, a_scr, carry_scr,
                     *, tm, sr, shift, tps, nsub, final_norm):
    i = pl.program_id(0)
    f = pl.program_id(1)
    tf2 = 2 * TF

    @pl.when(f == 0)
    def _():
        x = x_ref[...]
        u_scr[...] = _rms(x, gam_ref[...]).astype(BF16)
        o_ref[...] = x
        if tps > 1:
            @pl.when(i == 0)
            def _():
                carry_scr[...] = jnp.zeros_like(carry_scr)

    col_ok = lax.broadcasted_iota(jnp.int32, (1, tf2), 1) < D_FF - f * tf2
    cw = cw_ref[...]
    cb = cb_ref[...]
    if tps == 1:
        gext_scr[0:sr, :] = st_ref[...]
    else:
        gext_scr[0:sr, :] = jnp.where((i % tps) == 0, st_ref[...], carry_scr[f])
    ts = tm // nsub
    for h in range(nsub):
        r0 = h * ts
        ga = jnp.dot(u_scr[r0:r0 + ts, :], wup_ref[...], preferred_element_type=F32)
        gext_scr[sr + r0:sr + r0 + ts, 0:TF] = ga[:, 0:TF]
        gext_scr[sr + r0:sr + r0 + ts, TF:tf2] = ga[:, 2 * TF:3 * TF]
        a_scr[r0:r0 + ts, 0:TF] = ga[:, TF:2 * TF]
        a_scr[r0:r0 + ts, TF:tf2] = ga[:, 3 * TF:4 * TF]
    for h in range(nsub):
        r0 = h * ts
        g = gext_scr[sr + r0:sr + r0 + ts, :]
        g_m2 = gext_scr[sr + r0 - 2 * shift:sr + r0 - 2 * shift + ts, :]
        g_m1 = gext_scr[sr + r0 - shift:sr + r0 - shift + ts, :]
        gc = cb + ((cw[0:1, :] * g_m2 + cw[1:2, :] * g_m1) + cw[2:3, :] * g)
        hmid = jnp.where(col_ok, (gc * jax.nn.sigmoid(gc)) * a_scr[r0:r0 + ts, :], 0.0)
        o_ref[r0:r0 + ts, :] += jnp.dot(hmid.astype(BF16), wdn_ref[...], preferred_element_type=F32)
    g_tail = gext_scr[tm:tm + sr, :]
    so_ref[...] = g_tail
    if tps > 1:
        carry_scr[f] = g_tail

    if final_norm:
        @pl.when(f == pl.num_programs(1) - 1)
        def _():
            o_ref[...] = _rms(o_ref[...], gfin_ref[...])


def _ffn_bf16(x, layer, gamma, wupc, conv_w, conv_b, wdnc, st, gfin, *, tm, sr, shift, tps, st_per_tile,
              final_norm):
    assert NF % 2 == 0
    m = x.shape[0]
    nt = m // tm
    tf2 = 2 * TF
    nf2 = NF // 2
    kern = functools.partial(_ffn_bf16_kernel, tm=tm, sr=sr, shift=shift, tps=tps,
                             nsub=2 if tm >= 64 else 1, final_norm=final_norm)
    st_map = (lambda i, f: (i, f)) if st_per_tile else (lambda i, f: (0, f))
    return pl.pallas_call(
        kern,
        grid=(nt, nf2),
        in_specs=[pl.BlockSpec((tm, D), lambda i, f: (i, 0)),
                  pl.BlockSpec((1, D), lambda i, f: (0, 0)),
                  pl.BlockSpec((D, 2 * tf2), lambda i, f: (0, f)),
                  pl.BlockSpec((None, 3, tf2), lambda i, f: (layer, 0, f)),
                  pl.BlockSpec((None, 1, tf2), lambda i, f: (layer, 0, f)),
                  pl.BlockSpec((tf2, D), lambda i, f: (f, 0)),
                  pl.BlockSpec((sr, tf2), st_map),
                  pl.BlockSpec((1, D), lambda i, f: (0, 0))],
        out_specs=[pl.BlockSpec((tm, D), lambda i, f: (i, 0)),
                   pl.BlockSpec((sr, tf2), lambda i, f: (i, f))],
        out_shape=[jax.ShapeDtypeStruct((m, D), F32),
                   jax.ShapeDtypeStruct((nt * sr, D_FF), F32)],
        scratch_shapes=[pltpu.VMEM((tm, D), BF16),
                        pltpu.VMEM((sr + tm, tf2), F32),
                        pltpu.VMEM((tm, tf2), F32),
                        pltpu.VMEM((nf2, sr, tf2), F32)],
        compiler_params=_params(2),
        name="conv_ffn",
    )(x, gamma, wupc, conv_w, conv_b, wdnc, st, gfin)


def _window_sum(ext_scr, tmp_scrs, base, tm, shift, w):
    starts = {w: base}
    v = w
    while v > 2:
        starts[v // 2] = (starts[v] - (v // 2) * shift) // 8 * 8
        v //= 2
    src = ext_scr
    v = 1
    k = 0
    while True:
        lo = starts[2 * v]
        n = base + tm - lo
        val = src[lo:lo + n, :] + src[lo - v * shift:lo - v * shift + n, :]
        v *= 2
        if v == w:
            return val
        dst = tmp_scrs[k % 2]
        dst[lo:lo + n, :] = val
        src = dst
        k += 1


def _pool_kernel(x_ref, gam_ref, wp_ref, sc_ref, st_ref, o_ref, ut_ref, *scratch, tm, shift, tps, pos0, tr):
    ng = len(POOL_WINDOWS)
    uext_scrs = scratch[0:ng]
    tmp_scrs = scratch[ng:ng + 2]
    carry_scr = scratch[ng + 2]
    i = pl.program_id(0)
    hr = POOL_HALO * shift
    base = POOL_PAD_ROWS + hr

    if tps > 1:
        @pl.when(i == 0)
        def _():
            carry_scr[...] = jnp.zeros_like(carry_scr)

    x = x_ref[...]
    rinv = lax.rsqrt(jnp.mean(x * x, axis=1, keepdims=True) + EPS)
    for kk, w in enumerate(POOL_WINDOWS):
        cs = slice(kk * PG, (kk + 1) * PG)
        uext_scr = uext_scrs[kk]
        xg = x_ref[:, cs]
        ug = xg * rinv * gam_ref[:, cs]
        uext_scr[0:POOL_PAD_ROWS, :] = jnp.zeros((POOL_PAD_ROWS, PG), F32)
        if tps == 1:
            uext_scr[POOL_PAD_ROWS:base, :] = st_ref[:, cs]
        else:
            uext_scr[POOL_PAD_ROWS:base, :] = jnp.where((i % tps) == 0, st_ref[:, cs], carry_scr[:, cs])
        uext_scr[base:base + tm, :] = ug
        acc = _window_sum(uext_scr, tmp_scrs, base, tm, shift, w)
        if pos0 + 1 >= w:
            pooled = acc / float(w) - ug
        else:
            step = (i % tps) * (tm // shift) + lax.broadcasted_iota(jnp.int32, (tm, 1), 0) // shift
            cnt = jnp.minimum(w, pos0 + step + 1).astype(F32)
            pooled = acc / cnt - ug
        y = jnp.dot(pooled.astype(BF16), wp_ref[kk].astype(BF16), preferred_element_type=F32)
        o_ref[:, cs] = xg + y * sc_ref[:, cs]
        ut_ref[:, cs] = ug[tm - tr:tm, :]
        if tps > 1:
            carry_scr[:, cs] = ug[tm - hr:tm, :]


def _pool(x, gamma, wp, sc, st, *, tm, shift, tps, pos0, tr):
    m = x.shape[0]
    nt = m // tm
    hr = POOL_HALO * shift
    ng = len(POOL_WINDOWS)
    kern = functools.partial(_pool_kernel, tm=tm, shift=shift, tps=tps, pos0=pos0, tr=tr)
    return pl.pallas_call(
        kern,
        grid=(nt,),
        in_specs=[pl.BlockSpec((tm, D), lambda i: (i, 0)),
                  pl.BlockSpec((1, D), lambda i: (0, 0)),
                  pl.BlockSpec((None, ng, PG, PG), lambda i: (0, 0, 0, 0)),
                  pl.BlockSpec((1, D), lambda i: (0, 0)),
                  pl.BlockSpec((hr, D), lambda i: (0, 0))],
        out_specs=[pl.BlockSpec((tm, D), lambda i: (i, 0)),
                   pl.BlockSpec((tr, D), lambda i: (i, 0))],
        out_shape=[jax.ShapeDtypeStruct((m, D), F32),
                   jax.ShapeDtypeStruct((nt * tr, D), F32)],
        scratch_shapes=[pltpu.VMEM((POOL_PAD_ROWS + hr + tm, PG), F32)] * (ng + 2)
        + [pltpu.VMEM((hr, D), F32)],
        compiler_params=_params(1),
        name="pool_mixer",
    )(x, gamma, wp, sc, st)


def _pool_sample_kernel(x_ref, gam_ref, wp_ref, sc_ref, st_ref, o_ref, new_ref, rinv_scr, uext_scr,
                        *, nseq, t_dec):
    nh = POOL_HALO - 1
    grp = pl.program_id(0)
    tm = nseq * t_dec
    hr = POOL_HALO * nseq

    @pl.when(grp == 0)
    def _():
        x = x_ref[...]
        rinv_scr[...] = lax.rsqrt(jnp.mean(x * x, axis=1, keepdims=True) + EPS)

    for kk, w in enumerate(POOL_WINDOWS):
        @pl.when(grp == kk)
        def _(kk=kk, w=w):
            cs = slice(kk * PG, (kk + 1) * PG)
            xg = x_ref[:, cs]
            ug = xg * rinv_scr[...] * gam_ref[:, cs]
            for t in range(nh):
                uext_scr[(t + 1) * nseq:(t + 2) * nseq, :] = st_ref[t]
            uext_scr[hr:hr + tm, :] = ug
            acc = ug
            for j in range(1, w):
                acc = acc + uext_scr[hr - j * nseq:hr - j * nseq + tm, :]
            pooled = acc / float(w) - ug
            y = jnp.dot(pooled.astype(BF16), wp_ref[kk].astype(BF16), preferred_element_type=F32)
            o_ref[...] = xg + y * sc_ref[:, cs]
            for t in range(nh):
                src = (t + t_dec + 1) * nseq
                new_ref[t] = uext_scr[src:src + nseq, :]


def _pool_sample(x, gamma, wp, sc, state, *, nseq, t_dec):
    assert PAST_LEN + 1 >= max(POOL_WINDOWS)
    nh = POOL_HALO - 1
    ng = len(POOL_WINDOWS)
    tm = nseq * t_dec
    hr = POOL_HALO * nseq
    kern = functools.partial(_pool_sample_kernel, nseq=nseq, t_dec=t_dec)
    hist_spec = pl.BlockSpec((nh, nseq, PG), lambda g: (0, 0, g))
    return pl.pallas_call(
        kern,
        grid=(ng,),
        in_specs=[pl.BlockSpec((tm, D), lambda g: (0, 0)),
                  pl.BlockSpec((1, D), lambda g: (0, 0)),
                  pl.BlockSpec((None, ng, PG, PG), lambda g: (0, 0, 0, 0)),
                  pl.BlockSpec((1, D), lambda g: (0, 0)),
                  hist_spec],
        out_specs=[pl.BlockSpec((tm, PG), lambda g: (0, g)),
                   hist_spec],
        out_shape=[jax.ShapeDtypeStruct((tm, D), F32),
                   jax.ShapeDtypeStruct((nh, nseq, D), F32)],
        scratch_shapes=[pltpu.VMEM((tm, 1), F32),
                        pltpu.VMEM((hr + tm, PG), F32)],
        compiler_params=_params(1),
        name="pool_mixer_sample",
    )(x, gamma, wp, sc, state)


def _pad_cols(a, n):
    return jnp.pad(a, ((0, 0), (0, n - a.shape[1])))


def _from_time_major(a, nseq, t):
    return jnp.swapaxes(a.reshape((t, nseq) + a.shape[1:]), 0, 1)


def kernel(x_prompt, x_sample, state_mlstm_C, state_mlstm_n, state_mlstm_m, state_pool, state_ffn_conv,
           meta_tokens, norm_mix, norm_ffn, norm_final, w_mlstm_in, b_mlstm_gate, g_mlstm_out, w_mlstm_out,
           w_pool, pool_scale, w_up, conv_w, conv_b, w_down):
    bsz, seq, _ = x_prompt.shape
    nseq, t_dec, _ = x_sample.shape

    w_in_t = jnp.swapaxes(w_mlstm_in[0], 0, 1)
    bias = _pad_cols(b_mlstm_gate[0][None, :], GATE_LANES)
    w_out = w_mlstm_out
    gout = g_mlstm_out[0][None, :]
    wp = w_pool
    psc = pool_scale[0][None, :]
    gfin = norm_final[None, :]
    conv_b3 = conv_b[:, None, :]
    ffn_w = [(layer, norm_ffn[layer][None, :], w_up, conv_w, conv_b3, w_down) for layer in range(2)]
    g_mix0 = norm_mix[0][None, :]
    g_mix1 = norm_mix[1][None, :]

    ffn_cache = {}
    mix_cache = {}

    def proj(x, **kw):
        if "w_in" not in mix_cache:
            p, gates, wtc, wgc = _proj(x, g_mix0, w_in_t, w_in_t, **kw)
            mix_cache["w_in"] = (wtc, wgc)
            return p, gates
        return _proj(x, g_mix0, *mix_cache["w_in"], **kw)

    def out_proj(a, x, *, tm):
        if "w_out" not in mix_cache:
            y, mix_cache["w_out"] = _mmres_cast(a, w_out, x, tn=512)
            return y
        return _mmres(a, mix_cache["w_out"], x, tm=tm)

    def ffn_first(x, layer, st, *, final_norm):
        lyr, gamma, wu, cwt, cbs, wd = ffn_w[layer]
        y, cs, wupc, wdnc = _ffn(x, lyr, gamma, wu, cwt, cbs, wd, st, gfin, final_norm=final_norm)
        ffn_cache[layer] = (wupc, wdnc)
        return y, cs

    def ffn(x, layer, st, **kw):
        lyr, gamma, _, cwt, cbs, _ = ffn_w[layer]
        wupc, wdnc = ffn_cache[layer]
        return _ffn_bf16(x, lyr, gamma, wupc, cwt, cbs, wdnc, st, gfin, **kw)

    def long_stream(x, nb, s, st, *, tm, chunk, lead_pad, pos0):
        c0, n0, m0, conv0, pool0, conv1 = st
        tm_ffn = tm
        p, gates = proj(x, tm=tm, tn=1024, out_dtype=BF16)
        p = p.reshape(nb, s, PW)
        gates = gates.reshape(nb, s, GATE_LANES)
        if lead_pad:
            p = jnp.pad(p, ((0, 0), (lead_pad, 0), (0, 0)))
            gates = jnp.pad(gates, ((0, 0), (lead_pad, 0), (0, 0)))
        hg, c_new, n_new, m_new = _scan(p, gates, bias, gout, c0, n0, m0, L=chunk, lead_pad=lead_pad)
        hg = hg[:, lead_pad:].reshape(nb * s, VW)
        x1 = out_proj(hg, x, tm=min(tm, 512))
        x2, cs0 = ffn(x1, 0, conv0, tm=tm_ffn, sr=8, shift=1, tps=s // tm_ffn,
                      st_per_tile=False, final_norm=False)
        x3, ut = _pool(x2, g_mix1, wp, psc, pool0, tm=tm, shift=1, tps=s // tm, pos0=pos0,
                       tr=POOL_HALO)
        y, cs1 = ffn(x3, 1, conv1, tm=tm_ffn, sr=8, shift=1, tps=s // tm_ffn,
                     st_per_tile=False, final_norm=True)
        cs0 = cs0.reshape(nb, s // tm_ffn, 8, D_FF)[:, -1]
        cs1 = cs1.reshape(nb, s // tm_ffn, 8, D_FF)[:, -1]
        ut = ut.reshape(nb, s // tm, POOL_HALO, D)[:, -1]
        return y, (c_new, n_new, m_new, cs0, ut, cs1)

    def prompt_streams():
        zero_state = (jnp.zeros((H, DK, DV), F32), jnp.zeros((H, 8, DK), F32), jnp.zeros((H, 8, GATE_LANES), F32),
                      jnp.zeros((8, D_FF), F32), jnp.zeros((POOL_HALO, D), F32), jnp.zeros((8, D_FF), F32))
        _, (c_m, n_m, m_m, cs0_m, ut_m, cs1_m) = long_stream(
            meta_tokens, 1, N_META, zero_state, tm=N_META, chunk=128, lead_pad=128 - N_META, pos0=0)

        y_p, (c_p, n_p, m_p, cs0_p, ut_p, cs1_p) = long_stream(
            x_prompt.reshape(bsz * seq, D), bsz, seq, (c_m[0], n_m[0], m_m[0], cs0_m[0], ut_m[0], cs1_m[0]),
            tm=1024, chunk=SCAN_CHUNK, lead_pad=0, pos0=N_META)
        return (y_p.reshape(bsz, seq, D), c_p[None], n_p[:, :, 0][None], m_p[:, :, 0, 0][None],
                ut_p[:, 1:][None], jnp.stack([cs0_p[:, 6:], cs1_p[:, 6:]]))

    xs = x_sample.reshape(nseq * t_dec, D)
    p_s, gates_s = proj(xs, tm=nseq * t_dec, tn=512, out_dtype=F32)
    mtok = _pad_cols(jnp.repeat(state_mlstm_m[0], t_dec, axis=0), GATE_LANES)
    n_hm = jnp.swapaxes(state_mlstm_n[0], 0, 1)
    hg_s, C_s, n_s_hm, m_s_hm = _scan_s(p_s, gates_s, bias, gout, mtok, state_mlstm_C[0], n_hm, T=t_dec)
    assert nseq == SAMPLE_TILE_SEQS
    x1 = out_proj(hg_s.reshape(t_dec * nseq, VW), x_sample, tm=512)
    x2, cs0_s = ffn_first(x1, 0, state_ffn_conv[0], final_norm=False)
    x3, pool_new = _pool_sample(x2, g_mix1, wp, psc, jnp.swapaxes(state_pool[0], 0, 1), nseq=nseq, t_dec=t_dec)
    y_s, cs1_s = ffn_first(x3, 1, state_ffn_conv[1], final_norm=True)
    y_sample = _from_time_major(y_s, nseq, t_dec)
    n_s = jnp.swapaxes(n_s_hm, 0, 1)[None]
    m_s = jnp.swapaxes(m_s_hm[:, :, 0], 0, 1)[None]
    pool_s = jnp.swapaxes(pool_new, 0, 1)[None]
    conv_s = jnp.stack([cs0_s, cs1_s])

    y_prompt, C_p, n_p, m_p, pool_p, conv_p = prompt_streams()
    return (y_prompt, y_sample, C_p, n_p, m_p, pool_p, conv_p,
            C_s[None], n_s, m_s, pool_s, conv_s)
```

```python
import functools

import jax
import jax.numpy as jnp
from jax import lax
from jax.experimental import pallas as pl
from jax.experimental.pallas import tpu as pltpu

F32 = jnp.float32
BF16 = jnp.bfloat16

EPS = 1e-6
D = 2048
H = 4
DK = 256
DV = 512
QKW = H * DK
VW = H * DV
PW = 2 * QKW + 2 * VW
GATE_LANES = 128
SCALE = DK ** -0.5
POOL_WINDOWS = (2, 4, 8, 16)
PG = D // len(POOL_WINDOWS)
POOL_HALO = 16
POOL_PAD_ROWS = 16
D_FF = 5504
TF = 256
NF = -(-D_FF // TF)
LANES = 128
FF_LANE_BLOCKS = D_FF // LANES
CONV_TAIL = 2
N_META = 16
PAST_LEN = 16384
SCAN_CHUNK = 512
SAMPLE_GROUP = 16
SAMPLE_TILE_SEQS = 128
VMEM_LIMIT = 60 * 1024 * 1024


def _params(n_axes):
    return pltpu.CompilerParams(dimension_semantics=("arbitrary",) * n_axes,
                                vmem_limit_bytes=VMEM_LIMIT)


def _rms(x, g):
    return x * lax.rsqrt(jnp.mean(x * x, axis=-1, keepdims=True) + EPS) * g


def _log_sigmoid(x):
    return jnp.minimum(x, 0.0) - jnp.log(1.0 + jnp.exp(-jnp.abs(x)))


_NT = (((1,), (1,)), ((), ()))


def _proj_kernel(x_ref, g_ref, wt_ref, wgt_ref, p_ref, gate_ref, *rest, emit_bf16):
    u_scr = rest[-1]

    @pl.when(pl.program_id(1) == 0)
    def _():
        ub = _rms(x_ref[...], g_ref[...]).astype(BF16)
        u_scr[...] = ub
        wg = wgt_ref[...]
        if emit_bf16:
            row_ok = lax.broadcasted_iota(jnp.int32, (GATE_LANES, 1), 0) < 2 * H
            wg = jnp.where(row_ok, wg, 0.0).astype(BF16)
            rest[1][...] = wg
        gate_ref[...] = lax.dot_general(ub, wg, _NT, preferred_element_type=F32)

    w = wt_ref[...].astype(BF16)
    if emit_bf16:
        rest[0][...] = w
    p_ref[...] = lax.dot_general(u_scr[...], w, _NT, preferred_element_type=F32).astype(p_ref.dtype)


def _proj(x, gamma, wt, wgt, *, tm, tn, out_dtype):
    m = x.shape[0]
    emit = wt.dtype != BF16
    gate_spec = pl.BlockSpec((GATE_LANES, D), (lambda i, j: (PW // GATE_LANES, 0)) if emit
                             else (lambda i, j: (0, 0)))
    out_specs = [pl.BlockSpec((tm, tn), lambda i, j: (i, j)),
                 pl.BlockSpec((tm, GATE_LANES), lambda i, j: (i, 0))]
    out_shape = [jax.ShapeDtypeStruct((m, PW), out_dtype),
                 jax.ShapeDtypeStruct((m, GATE_LANES), F32)]
    if emit:
        out_specs += [pl.BlockSpec((tn, D), lambda i, j: (j, 0)),
                      pl.BlockSpec((GATE_LANES, D), lambda i, j: (0, 0))]
        out_shape += [jax.ShapeDtypeStruct((PW, D), BF16),
                      jax.ShapeDtypeStruct((GATE_LANES, D), BF16)]
    return pl.pallas_call(
        functools.partial(_proj_kernel, emit_bf16=emit),
        grid=(m // tm, PW // tn),
        in_specs=[pl.BlockSpec((tm, D), lambda i, j: (i, 0)),
                  pl.BlockSpec((1, D), lambda i, j: (0, 0)),
                  pl.BlockSpec((tn, D), lambda i, j: (j, 0)),
                  gate_spec],
        out_specs=out_specs,
        out_shape=out_shape,
        scratch_shapes=[pltpu.VMEM((tm, D), BF16)],
        compiler_params=_params(2),
        name="proj_cast" if emit else "proj",
    )(x, gamma, wt, wgt)


def _head_output(num, den, m_t, gout, o):
    hv = num * (1.0 / jnp.maximum(jnp.abs(den), jnp.exp(-m_t)))
    hv = hv * lax.rsqrt(jnp.mean(hv * hv, axis=1, keepdims=True) + EPS)
    return hv * gout * jax.nn.sigmoid(o.astype(F32))


def _scan_kernel(q_ref, k_ref, v_ref, o_ref, gt_ref, bias_ref, gout_ref, c0_ref, n0_ref, m0_ref,
                 hg_ref, cout_ref, nout_ref, mout_ref, c_scr, n_scr, m_scr, *, L, lead_pad, nc):
    c = pl.program_id(1)

    @pl.when(c == 0)
    def _():
        c_scr[...] = c0_ref[...]
        n_scr[...] = n0_ref[...]
        m_scr[...] = m0_ref[...]

    gates = gt_ref[...] + bias_ref[...]
    gates_t = gates.T
    row = lax.broadcasted_iota(jnp.int32, (L, L), 0)
    col = lax.broadcasted_iota(jnp.int32, (L, L), 1)
    causal = row >= col
    causal_t = row <= col
    if lead_pad:
        live_col = lax.broadcasted_iota(jnp.int32, (L, 1), 0) >= lead_pad
        live_row = lax.broadcasted_iota(jnp.int32, (1, L), 1) >= lead_pad

    for hh in range(H):
        ig_col = gates[:, hh:hh + 1]
        ig_row = gates_t[hh:hh + 1, :]
        lf_col = _log_sigmoid(gates[:, H + hh:H + hh + 1])
        lf_row = _log_sigmoid(gates_t[H + hh:H + hh + 1, :])
        if lead_pad:
            ig_col = jnp.where(live_col, ig_col, -jnp.inf)
            ig_row = jnp.where(live_row, ig_row, -jnp.inf)
            lf_col = jnp.where(live_col, lf_col, 0.0)
            lf_row = jnp.where(live_row, lf_row, 0.0)
        b_col = jnp.sum(jnp.where(causal, lf_row, 0.0), axis=1, keepdims=True)
        b_row = jnp.sum(jnp.where(causal_t, lf_col, 0.0), axis=0, keepdims=True)
        m_prev = m_scr[hh, 0:1, 0:1]
        d = jnp.where(causal, b_col - b_row + ig_row, -jnp.inf)
        inter = b_col + m_prev
        m_t = jnp.maximum(inter, jnp.max(d, axis=1, keepdims=True))
        w_inter = jnp.exp(inter - m_t) * SCALE

        q = q_ref[:, hh * DK:(hh + 1) * DK]
        k = k_ref[:, hh * DK:(hh + 1) * DK]
        v = v_ref[:, hh * DV:(hh + 1) * DV]
        qk = lax.dot_general(q, k, _NT, preferred_element_type=F32)
        s = qk * (jnp.exp(d - m_t) * SCALE)
        cmat = c_scr[hh]
        nvec = n_scr[hh, 0:1, :]
        num = w_inter * jnp.dot(q, cmat.astype(BF16), preferred_element_type=F32) \
            + jnp.dot(s.astype(BF16), v, preferred_element_type=F32)
        den = w_inter * jnp.sum(q.astype(F32) * nvec, axis=1, keepdims=True) \
            + jnp.sum(s, axis=1, keepdims=True)
        hout = _head_output(num, den, m_t, gout_ref[:, hh * DV:(hh + 1) * DV],
                            o_ref[:, hh * DV:(hh + 1) * DV])
        hg_ref[:, hh * DV:(hh + 1) * DV] = hout.astype(hg_ref.dtype)

        m_new = m_t[L - 1:L, :]
        b_last = b_col[L - 1:L, :]
        decay = jnp.exp(b_last + m_prev - m_new)
        wk = jnp.exp(b_last - b_col + ig_col - m_new) * k.astype(F32)
        c_scr[hh] = decay * cmat + jnp.dot(wk.T.astype(BF16), v, preferred_element_type=F32)
        n_scr[hh] = jnp.broadcast_to(decay * nvec + jnp.sum(wk, axis=0, keepdims=True), (8, DK))
        m_scr[hh] = jnp.broadcast_to(m_new, (8, GATE_LANES))

    @pl.when(c == nc - 1)
    def _():
        cout_ref[...] = c_scr[...]
        nout_ref[...] = n_scr[...]
        mout_ref[...] = m_scr[...]


def _scan(p, gates, bias, gout, c0, n0, m0, *, L, lead_pad=0):
    b, s, _ = p.shape
    nc = s // L
    kern = functools.partial(_scan_kernel, L=L, lead_pad=lead_pad, nc=nc)
    return pl.pallas_call(
        kern,
        grid=(b, nc),
        in_specs=[pl.BlockSpec((None, L, QKW), lambda i, c: (i, c, 0)),
                  pl.BlockSpec((None, L, QKW), lambda i, c: (i, c, 1)),
                  pl.BlockSpec((None, L, VW), lambda i, c: (i, c, 1)),
                  pl.BlockSpec((None, L, VW), lambda i, c: (i, c, 2)),
                  pl.BlockSpec((None, L, GATE_LANES), lambda i, c: (i, c, 0)),
                  pl.BlockSpec((1, GATE_LANES), lambda i, c: (0, 0)),
                  pl.BlockSpec((1, VW), lambda i, c: (0, 0)),
                  pl.BlockSpec((H, DK, DV), lambda i, c: (0, 0, 0)),
                  pl.BlockSpec((H, 8, DK), lambda i, c: (0, 0, 0)),
                  pl.BlockSpec((H, 8, GATE_LANES), lambda i, c: (0, 0, 0))],
        out_specs=[pl.BlockSpec((None, L, VW), lambda i, c: (i, c, 0)),
                   pl.BlockSpec((None, H, DK, DV), lambda i, c: (i, 0, 0, 0)),
                   pl.BlockSpec((None, H, 8, DK), lambda i, c: (i, 0, 0, 0)),
                   pl.BlockSpec((None, H, 8, GATE_LANES), lambda i, c: (i, 0, 0, 0))],
        out_shape=[jax.ShapeDtypeStruct((b, s, VW), BF16),
                   jax.ShapeDtypeStruct((b, H, DK, DV), F32),
                   jax.ShapeDtypeStruct((b, H, 8, DK), F32),
                   jax.ShapeDtypeStruct((b, H, 8, GATE_LANES), F32)],
        scratch_shapes=[pltpu.VMEM((H, DK, DV), F32),
                        pltpu.VMEM((H, 8, DK), F32),
                        pltpu.VMEM((H, 8, GATE_LANES), F32)],
        compiler_params=_params(2),
        name="scan",
    )(p, p, p, p, gates, bias, gout, c0, n0, m0)


def _scan_s_kernel(q_ref, k_ref, v_ref, o_ref, gt_ref, bias_ref, gout_ref, mtok_ref, c_ref, n_ref,
                   hg_ref, cout_ref, nout_ref, mout_ref, qc_scr, ntok_scr, *, T, NB, head_axis_mod=None):
    hh = pl.program_id(1)
    if head_axis_mod is not None:
        hh = hh % head_axis_mod
    LT = NB * T
    gates = gt_ref[...] + bias_ref[...]
    gates_t = gates.T
    lane = lax.broadcasted_iota(jnp.int32, (LT, GATE_LANES), 1)
    sub = lax.broadcasted_iota(jnp.int32, (GATE_LANES, LT), 0)

    def pick_col(a, idx):
        return jnp.sum(jnp.where(lane == idx, a, 0.0), axis=1, keepdims=True)

    def pick_row(a, idx):
        return jnp.sum(jnp.where(sub == idx, a, 0.0), axis=0, keepdims=True)

    ig_col = pick_col(gates, hh)
    ig_row = pick_row(gates_t, hh)
    lf_col = _log_sigmoid(pick_col(gates, hh + H))
    lf_row = _log_sigmoid(pick_row(gates_t, hh + H))
    m_prev = pick_col(mtok_ref[...], hh)

    row = lax.broadcasted_iota(jnp.int32, (LT, LT), 0)
    col = lax.broadcasted_iota(jnp.int32, (LT, LT), 1)
    same = (row // T) == (col // T)
    causal = jnp.logical_and(same, row >= col)
    causal_t = jnp.logical_and(same, row <= col)
    b_col = jnp.sum(jnp.where(causal, lf_row, 0.0), axis=1, keepdims=True)
    b_row = jnp.sum(jnp.where(causal_t, lf_col, 0.0), axis=0, keepdims=True)
    b_end = jnp.sum(jnp.where(same, lf_row, 0.0), axis=1, keepdims=True)
    d = jnp.where(causal, b_col - b_row + ig_row, -jnp.inf)
    inter = b_col + m_prev
    m_t = jnp.maximum(inter, jnp.max(d, axis=1, keepdims=True))
    d_end = jnp.where(same, b_end - b_row + ig_row, -jnp.inf)
    m_new = jnp.maximum(b_end + m_prev, jnp.max(d_end, axis=1, keepdims=True))
    w_inter = jnp.exp(inter - m_t) * SCALE

    q32 = q_ref[...]
    k32 = k_ref[...]
    q = q32.astype(BF16)
    v = v_ref[...].astype(BF16)
    qk = lax.dot_general(q, k32.astype(BF16), (((1,), (1,)), ((), ())), preferred_element_type=F32)
    s = qk * (jnp.exp(d - m_t) * SCALE)
    num_intra = jnp.dot(s.astype(BF16), v, preferred_element_type=F32)
    den_intra = jnp.sum(s, axis=1, keepdims=True)

    decay = jnp.exp(b_end + m_prev - m_new)
    wk = jnp.exp(b_end - b_col + ig_col - m_new) * k32
    wk_t = wk.T
    col_seq = lax.broadcasted_iota(jnp.int32, (DK, LT), 1) // T

    for bb in range(NB):
        r0 = bb * T
        cmat = c_ref[bb]
        nvec = n_ref[bb:bb + 1, :]
        qc_scr[r0:r0 + T, :] = jnp.dot(q32[r0:r0 + T, :].astype(BF16), cmat.astype(BF16),
                                       preferred_element_type=F32)
        ntok_scr[r0:r0 + T, :] = jnp.broadcast_to(nvec, (T, DK))
        upd = jnp.dot(jnp.where(col_seq == bb, wk_t, 0.0).astype(BF16), v, preferred_element_type=F32)
        dec = decay[r0:r0 + 1, :]
        cout_ref[bb] = dec * cmat + upd
        nout_ref[bb:bb + 1, :] = dec * nvec + jnp.sum(wk[r0:r0 + T, :], axis=0, keepdims=True)
        mout_ref[bb:bb + 1, :] = jnp.broadcast_to(m_new[r0:r0 + 1, :], (1, GATE_LANES))

    num = w_inter * qc_scr[...] + num_intra
    den = w_inter * jnp.sum(q32 * ntok_scr[...], axis=1, keepdims=True) + den_intra
    hout = _head_output(num, den, m_t, gout_ref[...], o_ref[...]).astype(hg_ref.dtype)
    src_row = (row % NB) * T + row // NB
    perm = jnp.where(col == src_row, 1.0, 0.0).astype(hg_ref.dtype)
    moved = jnp.dot(perm, hout, preferred_element_type=F32)
    hg_ref[...] = moved.reshape(T, NB, DV).astype(hg_ref.dtype)


def _scan_s(p, gates, bias, gout, mtok, c, n_hm, *, T):
    nseq = c.shape[0]
    nb = SAMPLE_GROUP
    lt = nb * T
    kern = functools.partial(_scan_s_kernel, T=T, NB=nb)
    return pl.pallas_call(
        kern,
        grid=(nseq // nb, H),
        in_specs=[pl.BlockSpec((lt, DK), lambda g, h: (g, h)),
                  pl.BlockSpec((lt, DK), lambda g, h: (g, H + h)),
                  pl.BlockSpec((lt, DV), lambda g, h: (g, H + h)),
                  pl.BlockSpec((lt, DV), lambda g, h: (g, 2 * H + h)),
                  pl.BlockSpec((lt, GATE_LANES), lambda g, h: (g, 0)),
                  pl.BlockSpec((1, GATE_LANES), lambda g, h: (0, 0)),
                  pl.BlockSpec((1, DV), lambda g, h: (0, h)),
                  pl.BlockSpec((lt, GATE_LANES), lambda g, h: (g, 0)),
                  pl.BlockSpec((nb, None, DK, DV), lambda g, h: (g, h, 0, 0)),
                  pl.BlockSpec((None, nb, DK), lambda g, h: (h, g, 0))],
        out_specs=[pl.BlockSpec((T, nb, DV), lambda g, h: (0, g, h)),
                   pl.BlockSpec((nb, None, DK, DV), lambda g, h: (g, h, 0, 0)),
                   pl.BlockSpec((None, nb, DK), lambda g, h: (h, g, 0)),
                   pl.BlockSpec((None, nb, GATE_LANES), lambda g, h: (h, g, 0))],
        out_shape=[jax.ShapeDtypeStruct((T, nseq, VW), BF16),
                   jax.ShapeDtypeStruct((nseq, H, DK, DV), F32),
                   jax.ShapeDtypeStruct((H, nseq, DK), F32),
                   jax.ShapeDtypeStruct((H, nseq, GATE_LANES), F32)],
        scratch_shapes=[pltpu.VMEM((lt, DV), F32), pltpu.VMEM((lt, DK), F32)],
        compiler_params=_params(2),
        name="scan_sample",
    )(p, p, p, p, gates, bias, gout, mtok, c, n_hm)


def _proj_scan_s_kernel(x_ref, g_ref, wt_ref, wgt_ref,
                        q_ref, k_ref, v_ref, o_ref, gt_ref, bias_ref, gout_ref, mtok_ref, c_ref, n_ref,
                        p_ref, gate_ref, hg_ref, cout_ref, nout_ref, mout_ref,
                        u_scr, qc_scr, ntok_scr, *, T, NB):
    _proj_kernel(x_ref, g_ref, wt_ref, wgt_ref, p_ref, gate_ref, u_scr, emit_bf16=False)
    _scan_s_kernel(q_ref, k_ref, v_ref, o_ref, gt_ref, bias_ref, gout_ref, mtok_ref, c_ref, n_ref,
                   hg_ref, cout_ref, nout_ref, mout_ref, qc_scr, ntok_scr, T=T, NB=NB, head_axis_mod=H)


def _proj_scan_s(x, gamma, wt, wgt, p_s, gates_s, bias, gout, mtok, c, n_hm, *, tm, T):
    m = x.shape[0]
    nseq = c.shape[0]
    nt = m // tm
    nj = 2 * H
    tn = PW // nj
    gpt = nj // H
    nb = nseq // (nt * gpt)
    assert nb * nt * gpt == nseq and nb % 8 == 0 and tn % LANES == 0
    lt = nb * T
    kern = functools.partial(_proj_scan_s_kernel, T=T, NB=nb)

    def grp(i, j):
        return i * gpt + j // H

    def head(j):
        return j % H

    return pl.pallas_call(
        kern,
        grid=(nt, nj),
        in_specs=[pl.BlockSpec((tm, D), lambda i, j: (i, 0), pipeline_mode=pl.Buffered(1)),
                  pl.BlockSpec((1, D), lambda i, j: (0, 0)),
                  pl.BlockSpec((tn, D), lambda i, j: (j, 0)),
                  pl.BlockSpec((GATE_LANES, D), lambda i, j: (0, 0)),
                  pl.BlockSpec((lt, DK), lambda i, j: (grp(i, j), head(j))),
                  pl.BlockSpec((lt, DK), lambda i, j: (grp(i, j), H + head(j))),
                  pl.BlockSpec((lt, DV), lambda i, j: (grp(i, j), H + head(j))),
                  pl.BlockSpec((lt, DV), lambda i, j: (grp(i, j), 2 * H + head(j))),
                  pl.BlockSpec((lt, GATE_LANES), lambda i, j: (grp(i, j), 0)),
                  pl.BlockSpec((1, GATE_LANES), lambda i, j: (0, 0)),
                  pl.BlockSpec((1, DV), lambda i, j: (0, head(j))),
                  pl.BlockSpec((lt, GATE_LANES), lambda i, j: (grp(i, j), 0)),
                  pl.BlockSpec((nb, None, DK, DV), lambda i, j: (grp(i, j), head(j), 0, 0)),
                  pl.BlockSpec((None, nb, DK), lambda i, j: (head(j), grp(i, j), 0))],
        out_specs=[pl.BlockSpec((tm, tn), lambda i, j: (i, j)),
                   pl.BlockSpec((tm, GATE_LANES), lambda i, j: (i, 0)),
                   pl.BlockSpec((T, nb, DV), lambda i, j: (0, grp(i, j), head(j))),
                   pl.BlockSpec((nb, None, DK, DV), lambda i, j: (grp(i, j), head(j), 0, 0)),
                   pl.BlockSpec((None, nb, DK), lambda i, j: (head(j), grp(i, j), 0)),
                   pl.BlockSpec((None, nb, GATE_LANES), lambda i, j: (head(j), grp(i, j), 0))],
        out_shape=[jax.ShapeDtypeStruct((m, PW), BF16),
                   jax.ShapeDtypeStruct((m, GATE_LANES), F32),
                   jax.ShapeDtypeStruct((T, nseq, VW), BF16),
                   jax.ShapeDtypeStruct((nseq, H, DK, DV), F32),
                   jax.ShapeDtypeStruct((H, nseq, DK), F32),
                   jax.ShapeDtypeStruct((H, nseq, GATE_LANES), F32)],
        scratch_shapes=[pltpu.VMEM((tm, D), BF16),
                        pltpu.VMEM((lt, DV), F32),
                        pltpu.VMEM((lt, DK), F32)],
        compiler_params=_params(2),
        name="proj_scan_sample",
    )(x, gamma, wt, wgt, p_s, p_s, p_s, p_s, gates_s, bias, gout, mtok, c, n_hm)


def _mmres_cast_kernel(a_ref, w_ref, x_ref, o_ref, wc_ref):
    w = w_ref[...].astype(BF16)
    wc_ref[...] = w
    y = jnp.dot(a_ref[...], w, preferred_element_type=F32)
    nseq, t_dec, _ = x_ref.shape
    for t in range(t_dec):
        o_ref[t * nseq:(t + 1) * nseq, :] = x_ref[:, t, :] + y[t * nseq:(t + 1) * nseq, :]


def _mmres_cast(a, w, x, *, tn):
    m, kdim = a.shape
    n = w.shape[2]
    nseq, t_dec, _ = x.shape
    return pl.pallas_call(
        _mmres_cast_kernel,
        grid=(n // tn,),
        in_specs=[pl.BlockSpec((m, kdim), lambda j: (0, 0)),
                  pl.BlockSpec((None, kdim, tn), lambda j: (0, 0, j)),
                  pl.BlockSpec((nseq, t_dec, tn), lambda j: (0, 0, j))],
        out_specs=[pl.BlockSpec((m, tn), lambda j: (0, j)),
                   pl.BlockSpec((kdim, tn), lambda j: (0, j))],
        out_shape=[jax.ShapeDtypeStruct((m, n), F32),
                   jax.ShapeDtypeStruct((kdim, n), BF16)],
        compiler_params=_params(1),
        name="out_proj_cast",
    )(a, w, x)


def _mmres_kernel(a_ref, w_ref, x_ref, o_ref):
    o_ref[...] = x_ref[...] + jnp.dot(a_ref[...], w_ref[...], preferred_element_type=F32)


def _mmres(a, w, x, *, tm):
    m, kdim = a.shape
    n = w.shape[1]
    return pl.pallas_call(
        _mmres_kernel,
        grid=(m // tm,),
        in_specs=[pl.BlockSpec((tm, kdim), lambda i: (i, 0)),
                  pl.BlockSpec((kdim, n), lambda i: (0, 0)),
                  pl.BlockSpec((tm, n), lambda i: (i, 0))],
        out_specs=pl.BlockSpec((tm, n), lambda i: (i, 0)),
        out_shape=jax.ShapeDtypeStruct((m, n), F32),
        compiler_params=_params(1),
        name="out_proj",
    )(a, w, x)


def _ffn_kernel(x_ref, gam_ref, wg_ref, wa0_ref, wa1_ref, cw_ref, cb_ref, wd_ref, st_ref, gfin_ref,
                o_ref, so_ref, wupc_ref, wdnc_ref,
                u_scr, gext_scr, a_scr, wup0_scr, wup1_scr, wdn0_scr, wdn1_scr,
                *, tm, shift, nsub, final_norm):
    s = pl.program_id(1)
    sr = CONV_TAIL * shift
    wup_slots = (wup0_scr, wup1_scr)
    wdn_slots = (wdn0_scr, wdn1_scr)

    def cast_steps(slot):
        wup_scr = wup_slots[slot]
        wdn_scr = wdn_slots[slot]
        valid = D_FF - jnp.minimum(s, NF - 1) * TF

        def cast_gate():
            ok = lax.broadcasted_iota(jnp.int32, (1, TF), 1) < valid
            w = jnp.where(ok, wg_ref[...], 0.0).astype(BF16)
            wup_scr[:, 0:TF] = w
            wupc_ref[:, 0:TF] = w

        def cast_value():
            lane = lax.broadcasted_iota(jnp.int32, (1, LANES), 1)
            w0 = jnp.where(lane < valid, wa0_ref[...], 0.0).astype(BF16)
            w1 = jnp.where(lane + LANES < valid, wa1_ref[...], 0.0).astype(BF16)
            wup_scr[:, TF:TF + LANES] = w0
            wup_scr[:, TF + LANES:2 * TF] = w1
            wupc_ref[:, TF:TF + LANES] = w0
            wupc_ref[:, TF + LANES:2 * TF] = w1

        def cast_down(r0, rows):
            row_ok = r0 + lax.broadcasted_iota(jnp.int32, (rows, 1), 0) < valid
            w = jnp.where(row_ok, wd_ref[r0:r0 + rows, :], 0.0).astype(BF16)
            wdn_scr[r0:r0 + rows, :] = w
            wdnc_ref[r0:r0 + rows, :] = w

        half = TF // 2
        return [cast_gate, cast_value, functools.partial(cast_down, 0, half),
                functools.partial(cast_down, half, half)]

    def run_tile(slot, fillers=()):
        fillers = list(fillers)

        def fill():
            if fillers:
                fillers.pop(0)()

        wup_scr = wup_slots[slot]
        wdn_scr = wdn_slots[slot]
        f = s - 1
        col_ok = lax.broadcasted_iota(jnp.int32, (1, TF), 1) < D_FF - f * TF
        cw = cw_ref[...]
        cb = cb_ref[...]
        for t in range(CONV_TAIL):
            gext_scr[t * shift:(t + 1) * shift, :] = st_ref[:, t, :]
        ts = tm // nsub
        for h in range(nsub):
            r0 = h * ts
            ga = jnp.dot(u_scr[r0:r0 + ts, :], wup_scr[...], preferred_element_type=F32)
            gext_scr[sr + r0:sr + r0 + ts, :] = ga[:, 0:TF]
            a_scr[r0:r0 + ts, :] = ga[:, TF:2 * TF]
            fill()
        for h in range(nsub):
            r0 = h * ts
            g = gext_scr[sr + r0:sr + r0 + ts, :]
            g_m2 = gext_scr[sr + r0 - 2 * shift:sr + r0 - 2 * shift + ts, :]
            g_m1 = gext_scr[sr + r0 - shift:sr + r0 - shift + ts, :]
            gc = cb + ((cw[0:1, :] * g_m2 + cw[1:2, :] * g_m1) + cw[2:3, :] * g)
            hmid = jnp.where(col_ok, (gc * jax.nn.sigmoid(gc)) * a_scr[r0:r0 + ts, :], 0.0)
            o_ref[r0:r0 + ts, :] += jnp.dot(hmid.astype(BF16), wdn_scr[...], preferred_element_type=F32)
            fill()
        while fillers:
            fill()
        for t in range(CONV_TAIL):
            so_ref[:, t, :] = gext_scr[tm + t * shift:tm + (t + 1) * shift, :]

    @pl.when(s == 0)
    def _():
        x = x_ref[...]
        u_scr[...] = _rms(x, gam_ref[...]).astype(BF16)
        o_ref[...] = x
        for step in cast_steps(0):
            step()

    for parity in range(2):
        @pl.when(jnp.logical_and(s > 0, s % 2 == parity))
        def _(parity=parity):
            run_tile(1 - parity, cast_steps(parity))

    if final_norm:
        @pl.when(s == NF)
        def _():
            o_ref[...] = _rms(o_ref[...], gfin_ref[...])


def _ffn(x, layer, gamma, w_up, conv_w, conv_b, w_down, st, gfin, *, final_norm):
    assert TF == 2 * LANES
    tm = x.shape[0]
    shift = st.shape[0]
    sr = CONV_TAIL * shift
    kern = functools.partial(_ffn_kernel, tm=tm, shift=shift, nsub=2, final_norm=final_norm)
    last_a = 2 * FF_LANE_BLOCKS - 1

    def wt(s):
        return jnp.minimum(s, NF - 1)

    def ft(s):
        return jnp.maximum(s - 1, 0)

    st_spec = pl.BlockSpec((shift, CONV_TAIL, TF), lambda i, s: (0, 0, ft(s)))
    return pl.pallas_call(
        kern,
        grid=(1, NF + 1),
        in_specs=[pl.BlockSpec((tm, D), lambda i, s: (i, 0), pipeline_mode=pl.Buffered(1)),
                  pl.BlockSpec((1, D), lambda i, s: (0, 0)),
                  pl.BlockSpec((None, D, TF), lambda i, s: (layer, 0, wt(s))),
                  pl.BlockSpec((None, D, LANES), lambda i, s: (layer, 0, FF_LANE_BLOCKS + 2 * wt(s))),
                  pl.BlockSpec((None, D, LANES),
                               lambda i, s: (layer, 0, jnp.minimum(FF_LANE_BLOCKS + 2 * wt(s) + 1, last_a))),
                  pl.BlockSpec((None, 3, TF), lambda i, s: (layer, 0, ft(s))),
                  pl.BlockSpec((None, 1, TF), lambda i, s: (layer, 0, ft(s))),
                  pl.BlockSpec((None, TF, D), lambda i, s: (layer, wt(s), 0)),
                  st_spec,
                  pl.BlockSpec((1, D), lambda i, s: (0, 0))],
        out_specs=[pl.BlockSpec((tm, D), lambda i, s: (i, 0)),
                   st_spec,
                   pl.BlockSpec((D, 2 * TF), lambda i, s: (0, wt(s))),
                   pl.BlockSpec((TF, D), lambda i, s: (wt(s), 0))],
        out_shape=[jax.ShapeDtypeStruct((tm, D), F32),
                   jax.ShapeDtypeStruct(st.shape, F32),
                   jax.ShapeDtypeStruct((D, NF * 2 * TF), BF16),
                   jax.ShapeDtypeStruct((NF * TF, D), BF16)],
        scratch_shapes=[pltpu.VMEM((tm, D), BF16),
                        pltpu.VMEM((sr + tm, TF), F32),
                        pltpu.VMEM((tm, TF), F32),
                        pltpu.VMEM((D, 2 * TF), BF16),
                        pltpu.VMEM((D, 2 * TF), BF16),
                        pltpu.VMEM((TF, D), BF16),
                        pltpu.VMEM((TF, D), BF16)],
        compiler_params=_params(2),
        name="conv_ffn_cast",
    )(x, gamma, w_up, w_up, w_up, conv_w, conv_b, w_down, st, gfin)


def _ffn_bf16_kernel(x_ref, gam_ref, wup_ref, cw_ref, cb_ref, wdn_ref, st_ref, gfin_ref,
                     o_ref, so_ref, u_scr, gext_scr, a_scr, carry_scr,
                     *, tm, sr, shift, tps, nsub, final_norm):
    i = pl.program_id(0)
    f = pl.program_id(1)
    tf2 = 2 * TF

    @pl.when(f == 0)
    def _():
        x = x_ref[...]
        u_scr[...] = _rms(x, gam_ref[...]).astype(BF16)
        o_ref[...] = x
        if tps > 1:
            @pl.when(i == 0)
            def _():
                carry_scr[...] = jnp.zeros_like(carry_scr)

    col_ok = lax.broadcasted_iota(jnp.int32, (1, tf2), 1) < D_FF - f * tf2
    cw = cw_ref[...]
    cb = cb_ref[...]
    if tps == 1:
        gext_scr[0:sr, :] = st_ref[...]
    else:
        gext_scr[0:sr, :] = jnp.where((i % tps) == 0, st_ref[...], carry_scr[f])
    ts = tm // nsub
    for h in range(nsub):
        r0 = h * ts
        ga = jnp.dot(u_scr[r0:r0 + ts, :], wup_ref[...], preferred_element_type=F32)
        gext_scr[sr + r0:sr + r0 + ts, 0:TF] = ga[:, 0:TF]
        gext_scr[sr + r0:sr + r0 + ts, TF:tf2] = ga[:, 2 * TF:3 * TF]
        a_scr[r0:r0 + ts, 0:TF] = ga[:, TF:2 * TF]
        a_scr[r0:r0 + ts, TF:tf2] = ga[:, 3 * TF:4 * TF]
    for h in range(nsub):
        r0 = h * ts
        g = gext_scr[sr + r0:sr + r0 + ts, :]
        g_m2 = gext_scr[sr + r0 - 2 * shift:sr + r0 - 2 * shift + ts, :]
        g_m1 = gext_scr[sr + r0 - shift:sr + r0 - shift + ts, :]
        gc = cb + ((cw[0:1, :] * g_m2 + cw[1:2, :] * g_m1) + cw[2:3, :] * g)
        hmid = jnp.where(col_ok, (gc * jax.nn.sigmoid(gc)) * a_scr[r0:r0 + ts, :], 0.0)
        o_ref[r0:r0 + ts, :] += jnp.dot(hmid.astype(BF16), wdn_ref[...], preferred_element_type=F32)
    g_tail = gext_scr[tm:tm + sr, :]
    so_ref[...] = g_tail
    if tps > 1:
        carry_scr[f] = g_tail

    if final_norm:
        @pl.when(f == pl.num_programs(1) - 1)
        def _():
            o_ref[...] = _rms(o_ref[...], gfin_ref[...])


def _ffn_bf16(x, layer, gamma, wupc, conv_w, conv_b, wdnc, st, gfin, *, tm, sr, shift, tps, st_per_tile,
              final_norm):
    assert NF % 2 == 0
    m = x.shape[0]
    nt = m // tm
    tf2 = 2 * TF
    nf2 = NF // 2
    kern = functools.partial(_ffn_bf16_kernel, tm=tm, sr=sr, shift=shift, tps=tps,
                             nsub=2 if tm >= 64 else 1, final_norm=final_norm)
    st_map = (lambda i, f: (i, f)) if st_per_tile else (lambda i, f: (0, f))
    return pl.pallas_call(
        kern,
        grid=(nt, nf2),
        in_specs=[pl.BlockSpec((tm, D), lambda i, f: (i, 0)),
                  pl.BlockSpec((1, D), lambda i, f: (0, 0)),
                  pl.BlockSpec((D, 2 * tf2), lambda i, f: (0, f)),
                  pl.BlockSpec((None, 3, tf2), lambda i, f: (layer, 0, f)),
                  pl.BlockSpec((None, 1, tf2), lambda i, f: (layer, 0, f)),
                  pl.BlockSpec((tf2, D), lambda i, f: (f, 0)),
                  pl.BlockSpec((sr, tf2), st_map),
                  pl.BlockSpec((1, D), lambda i, f: (0, 0))],
        out_specs=[pl.BlockSpec((tm, D), lambda i, f: (i, 0)),
                   pl.BlockSpec((sr, tf2), lambda i, f: (i, f))],
        out_shape=[jax.ShapeDtypeStruct((m, D), F32),
                   jax.ShapeDtypeStruct((nt * sr, D_FF), F32)],
        scratch_shapes=[pltpu.VMEM((tm, D), BF16),
                        pltpu.VMEM((sr + tm, tf2), F32),
                        pltpu.VMEM((tm, tf2), F32),
                        pltpu.VMEM((nf2, sr, tf2), F32)],
        compiler_params=_params(2),
        name="conv_ffn",
    )(x, gamma, wupc, conv_w, conv_b, wdnc, st, gfin)


def _window_sum(ext_scr, tmp_scrs, base, tm, shift, w):
    starts = {w: base}
    v = w
    while v > 2:
        starts[v // 2] = (starts[v] - (v // 2) * shift) // 8 * 8
        v //= 2
    src = ext_scr
    v = 1
    k = 0
    while True:
        lo = starts[2 * v]
        n = base + tm - lo
        val = src[lo:lo + n, :] + src[lo - v * shift:lo - v * shift + n, :]
        v *= 2
        if v == w:
            return val
        dst = tmp_scrs[k % 2]
        dst[lo:lo + n, :] = val
        src = dst
        k += 1


def _pool_kernel(x_ref, gam_ref, wp_ref, sc_ref, st_ref, o_ref, ut_ref, *scratch, tm, shift, tps, pos0, tr):
    ng = len(POOL_WINDOWS)
    uext_scrs = scratch[0:ng]
    tmp_scrs = scratch[ng:ng + 2]
    carry_scr = scratch[ng + 2]
    i = pl.program_id(0)
    hr = POOL_HALO * shift
    base = POOL_PAD_ROWS + hr

    if tps > 1:
        @pl.when(i == 0)
        def _():
            carry_scr[...] = jnp.zeros_like(carry_scr)

    x = x_ref[...]
    rinv = lax.rsqrt(jnp.mean(x * x, axis=1, keepdims=True) + EPS)
    for kk, w in enumerate(POOL_WINDOWS):
        cs = slice(kk * PG, (kk + 1) * PG)
        uext_scr = uext_scrs[kk]
        xg = x_ref[:, cs]
        ug = xg * rinv * gam_ref[:, cs]
        uext_scr[0:POOL_PAD_ROWS, :] = jnp.zeros((POOL_PAD_ROWS, PG), F32)
        if tps == 1:
            uext_scr[POOL_PAD_ROWS:base, :] = st_ref[:, cs]
        else:
            uext_scr[POOL_PAD_ROWS:base, :] = jnp.where((i % tps) == 0, st_ref[:, cs], carry_scr[:, cs])
        uext_scr[base:base + tm, :] = ug
        acc = _window_sum(uext_scr, tmp_scrs, base, tm, shift, w)
        if pos0 + 1 >= w:
            pooled = acc / float(w) - ug
        else:
            step = (i % tps) * (tm // shift) + lax.broadcasted_iota(jnp.int32, (tm, 1), 0) // shift
            cnt = jnp.minimum(w, pos0 + step + 1).astype(F32)
            pooled = acc / cnt - ug
        y = jnp.dot(pooled.astype(BF16), wp_ref[kk].astype(BF16), preferred_element_type=F32)
        o_ref[:, cs] = xg + y * sc_ref[:, cs]
        ut_ref[:, cs] = ug[tm - tr:tm, :]
        if tps > 1:
            carry_scr[:, cs] = ug[tm - hr:tm, :]


def _pool(x, gamma, wp, sc, st, *, tm, shift, tps, pos0, tr):
    m = x.shape[0]
    nt = m // tm
    hr = POOL_HALO * shift
    ng = len(POOL_WINDOWS)
    kern = functools.partial(_pool_kernel, tm=tm, shift=shift, tps=tps, pos0=pos0, tr=tr)
    return pl.pallas_call(
        kern,
        grid=(nt,),
        in_specs=[pl.BlockSpec((tm, D), lambda i: (i, 0)),
                  pl.BlockSpec((1, D), lambda i: (0, 0)),
                  pl.BlockSpec((None, ng, PG, PG), lambda i: (0, 0, 0, 0)),
                  pl.BlockSpec((1, D), lambda i: (0, 0)),
                  pl.BlockSpec((hr, D), lambda i: (0, 0))],
        out_specs=[pl.BlockSpec((tm, D), lambda i: (i, 0)),
                   pl.BlockSpec((tr, D), lambda i: (i, 0))],
        out_shape=[jax.ShapeDtypeStruct((m, D), F32),
                   jax.ShapeDtypeStruct((nt * tr, D), F32)],
        scratch_shapes=[pltpu.VMEM((POOL_PAD_ROWS + hr + tm, PG), F32)] * (ng + 2)
        + [pltpu.VMEM((hr, D), F32)],
        compiler_params=_params(1),
        name="pool_mixer",
    )(x, gamma, wp, sc, st)


def _pool_sample_kernel(x_ref, gam_ref, wp_ref, sc_ref, st_ref, o_ref, new_ref, rinv_scr, uext_scr,
                        *, nseq, t_dec):
    nh = POOL_HALO - 1
    grp = pl.program_id(0)
    tm = nseq * t_dec
    hr = POOL_HALO * nseq

    @pl.when(grp == 0)
    def _():
        x = x_ref[...]
        rinv_scr[...] = lax.rsqrt(jnp.mean(x * x, axis=1, keepdims=True) + EPS)

    for kk, w in enumerate(POOL_WINDOWS):
        @pl.when(grp == kk)
        def _(kk=kk, w=w):
            cs = slice(kk * PG, (kk + 1) * PG)
            xg = x_ref[:, cs]
            ug = xg * rinv_scr[...] * gam_ref[:, cs]
            for t in range(nh):
                uext_scr[(t + 1) * nseq:(t + 2) * nseq, :] = st_ref[t]
            uext_scr[hr:hr + tm, :] = ug
            acc = ug
            for j in range(1, w):
                acc = acc + uext_scr[hr - j * nseq:hr - j * nseq + tm, :]
            pooled = acc / float(w) - ug
            y = jnp.dot(pooled.astype(BF16), wp_ref[kk].astype(BF16), preferred_element_type=F32)
            o_ref[...] = xg + y * sc_ref[:, cs]
            for t in range(nh):
                src = (t + t_dec + 1) * nseq
                new_ref[t] = uext_scr[src:src + nseq, :]


def _pool_sample(x, gamma, wp, sc, state, *, nseq, t_dec):
    assert PAST_LEN + 1 >= max(POOL_WINDOWS)
    nh = POOL_HALO - 1
    ng = len(POOL_WINDOWS)
    tm = nseq * t_dec
    hr = POOL_HALO * nseq
    kern = functools.partial(_pool_sample_kernel, nseq=nseq, t_dec=t_dec)
    hist_spec = pl.BlockSpec((nh, nseq, PG), lambda g: (0, 0, g))
    return pl.pallas_call(
        kern,
        grid=(ng,),
        in_specs=[pl.BlockSpec((tm, D), lambda g: (0, 0)),
                  pl.BlockSpec((1, D), lambda g: (0, 0)),
                  pl.BlockSpec((None, ng, PG, PG), lambda g: (0, 0, 0, 0)),
                  pl.BlockSpec((1, D), lambda g: (0, 0)),
                  hist_spec],
        out_specs=[pl.BlockSpec((tm, PG), lambda g: (0, g)),
                   hist_spec],
        out_shape=[jax.ShapeDtypeStruct((tm, D), F32),
                   jax.ShapeDtypeStruct((nh, nseq, D), F32)],
        scratch_shapes=[pltpu.VMEM((tm, 1), F32),
                        pltpu.VMEM((hr + tm, PG), F32)],
        compiler_params=_params(1),
        name="pool_mixer_sample",
    )(x, gamma, wp, sc, state)


def _pad_cols(a, n):
    return jnp.pad(a, ((0, 0), (0, n - a.shape[1])))


def _from_time_major(a, nseq, t):
    return jnp.swapaxes(a.reshape((t, nseq) + a.shape[1:]), 0, 1)


def kernel(x_prompt, x_sample, state_mlstm_C, state_mlstm_n, state_mlstm_m, state_pool, state_ffn_conv,
           meta_tokens, norm_mix, norm_ffn, norm_final, w_mlstm_in, b_mlstm_gate, g_mlstm_out, w_mlstm_out,
           w_pool, pool_scale, w_up, conv_w, conv_b, w_down):
    bsz, seq, _ = x_prompt.shape
    nseq, t_dec, _ = x_sample.shape

    w_in_t = jnp.swapaxes(w_mlstm_in[0], 0, 1)
    bias = _pad_cols(b_mlstm_gate[0][None, :], GATE_LANES)
    w_out = w_mlstm_out
    gout = g_mlstm_out[0][None, :]
    wp = w_pool
    psc = pool_scale[0][None, :]
    gfin = norm_final[None, :]
    conv_b3 = conv_b[:, None, :]
    ffn_w = [(layer, norm_ffn[layer][None, :], w_up, conv_w, conv_b3, w_down) for layer in range(2)]
    g_mix0 = norm_mix[0][None, :]
    g_mix1 = norm_mix[1][None, :]

    ffn_cache = {}
    mix_cache = {}

    def proj(x, **kw):
        if "w_in" not in mix_cache:
            p, gates, wtc, wgc = _proj(x, g_mix0, w_in_t, w_in_t, **kw)
            mix_cache["w_in"] = (wtc, wgc)
            return p, gates
        return _proj(x, g_mix0, *mix_cache["w_in"], **kw)

    def out_proj(a, x, *, tm):
        if "w_out" not in mix_cache:
            y, mix_cache["w_out"] = _mmres_cast(a, w_out, x, tn=512)
            return y
        return _mmres(a, mix_cache["w_out"], x, tm=tm)

    def ffn_first(x, layer, st, *, final_norm):
        lyr, gamma, wu, cwt, cbs, wd = ffn_w[layer]
        y, cs, wupc, wdnc = _ffn(x, lyr, gamma, wu, cwt, cbs, wd, st, gfin, final_norm=final_norm)
        ffn_cache[layer] = (wupc, wdnc)
        return y, cs

    def ffn(x, layer, st, **kw):
        lyr, gamma, _, cwt, cbs, _ = ffn_w[layer]
        wupc, wdnc = ffn_cache[layer]
        return _ffn_bf16(x, lyr, gamma, wupc, cwt, cbs, wdnc, st, gfin, **kw)

    def long_stream(x, nb, s, st, *, tm, chunk, lead_pad, pos0, projected=None):
        c0, n0, m0, conv0, pool0, conv1 = st
        tm_ffn = tm
        p, gates = projected if projected is not None else proj(x, tm=tm, tn=1024, out_dtype=BF16)
        p = p.reshape(nb, s, PW)
        gates = gates.reshape(nb, s, GATE_LANES)
        if lead_pad:
            p = jnp.pad(p, ((0, 0), (lead_pad, 0), (0, 0)))
            gates = jnp.pad(gates, ((0, 0), (lead_pad, 0), (0, 0)))
        hg, c_new, n_new, m_new = _scan(p, gates, bias, gout, c0, n0, m0, L=chunk, lead_pad=lead_pad)
        hg = hg[:, lead_pad:].reshape(nb * s, VW)
        x1 = out_proj(hg, x, tm=min(tm, 512))
        x2, cs0 = ffn(x1, 0, conv0, tm=tm_ffn, sr=8, shift=1, tps=s // tm_ffn,
                      st_per_tile=False, final_norm=False)
        x3, ut = _pool(x2, g_mix1, wp, psc, pool0, tm=tm, shift=1, tps=s // tm, pos0=pos0,
                       tr=POOL_HALO)
        y, cs1 = ffn(x3, 1, conv1, tm=tm_ffn, sr=8, shift=1, tps=s // tm_ffn,
                     st_per_tile=False, final_norm=True)
        cs0 = cs0.reshape(nb, s // tm_ffn, 8, D_FF)[:, -1]
        cs1 = cs1.reshape(nb, s // tm_ffn, 8, D_FF)[:, -1]
        ut = ut.reshape(nb, s // tm, POOL_HALO, D)[:, -1]
        return y, (c_new, n_new, m_new, cs0, ut, cs1)

    def prompt_streams():
        zero_state = (jnp.zeros((H, DK, DV), F32), jnp.zeros((H, 8, DK), F32), jnp.zeros((H, 8, GATE_LANES), F32),
                      jnp.zeros((8, D_FF), F32), jnp.zeros((POOL_HALO, D), F32), jnp.zeros((8, D_FF), F32))
        _, (c_m, n_m, m_m, cs0_m, ut_m, cs1_m) = long_stream(
            meta_tokens, 1, N_META, zero_state, tm=N_META, chunk=128, lead_pad=128 - N_META, pos0=0)

        y_p, (c_p, n_p, m_p, cs0_p, ut_p, cs1_p) = long_stream(
            x_prompt.reshape(bsz * seq, D), bsz, seq, (c_m[0], n_m[0], m_m[0], cs0_m[0], ut_m[0], cs1_m[0]),
            tm=1024, chunk=SCAN_CHUNK, lead_pad=0, pos0=N_META, projected=prompt_projected)
        return (y_p.reshape(bsz, seq, D), c_p[None], n_p[:, :, 0][None], m_p[:, :, 0, 0][None],
                ut_p[:, 1:][None], jnp.stack([cs0_p[:, 6:], cs1_p[:, 6:]]))

    xs = x_sample.reshape(nseq * t_dec, D)
    p_s, gates_s = proj(xs, tm=nseq * t_dec, tn=512, out_dtype=F32)
    mtok = _pad_cols(jnp.repeat(state_mlstm_m[0], t_dec, axis=0), GATE_LANES)
    n_hm = jnp.swapaxes(state_mlstm_n[0], 0, 1)
    p_p, gates_p, hg_s, C_s, n_s_hm, m_s_hm = _proj_scan_s(
        x_prompt.reshape(bsz * seq, D), g_mix0, *mix_cache["w_in"], p_s, gates_s, bias, gout, mtok,
        state_mlstm_C[0], n_hm, tm=1024, T=t_dec)
    prompt_projected = (p_p, gates_p)
    assert nseq == SAMPLE_TILE_SEQS
    x1 = out_proj(hg_s.reshape(t_dec * nseq, VW), x_sample, tm=512)
    x2, cs0_s = ffn_first(x1, 0, state_ffn_conv[0], final_norm=False)
    x3, pool_new = _pool_sample(x2, g_mix1, wp, psc, jnp.swapaxes(state_pool[0], 0, 1), nseq=nseq, t_dec=t_dec)
    y_s, cs1_s = ffn_first(x3, 1, state_ffn_conv[1], final_norm=True)
    y_sample = _from_time_major(y_s, nseq, t_dec)
    n_s = jnp.swapaxes(n_s_hm, 0, 1)[None]
    m_s = jnp.swapaxes(m_s_hm[:, :, 0], 0, 1)[None]
    pool_s = jnp.swapaxes(pool_new, 0, 1)[None]
    conv_s = jnp.stack([cs0_s, cs1_s])

    y_prompt, C_p, n_p, m_p, pool_p, conv_p = prompt_streams()
    return (y_prompt, y_sample, C_p, n_p, m_p, pool_p, conv_p,
            C_s[None], n_s, m_s, pool_s, conv_s)
```

```python
import functools

import jax
import jax.numpy as jnp
from jax import lax
from jax.experimental import pallas as pl
from jax.experimental.pallas import tpu as pltpu

F32 = jnp.float32
BF16 = jnp.bfloat16

EPS = 1e-6
D = 2048
H = 4
DK = 256
DV = 512
QKW = H * DK
VW = H * DV
PW = 2 * QKW + 2 * VW
GATE_LANES = 128
SCALE = DK ** -0.5
POOL_WINDOWS = (2, 4, 8, 16)
PG = D // len(POOL_WINDOWS)
POOL_HALO = 16
POOL_PAD_ROWS = 16
D_FF = 5504
TF = 256
NF = -(-D_FF // TF)
LANES = 128
FF_LANE_BLOCKS = D_FF // LANES
CONV_TAIL = 2
N_META = 16
PAST_LEN = 16384
SCAN_CHUNK = 512
SAMPLE_TILE_SEQS = 128
VMEM_LIMIT = 60 * 1024 * 1024


def _params(n_axes):
    return pltpu.CompilerParams(dimension_semantics=("arbitrary",) * n_axes,
                                vmem_limit_bytes=VMEM_LIMIT)


def _rms(x, g):
    return x * lax.rsqrt(jnp.mean(x * x, axis=-1, keepdims=True) + EPS) * g


def _log_sigmoid(x):
    return jnp.minimum(x, 0.0) - jnp.log(1.0 + jnp.exp(-jnp.abs(x)))


_NT = (((1,), (1,)), ((), ()))


def _proj_kernel(x_ref, g_ref, wt_ref, wgt_ref, p_ref, gate_ref, *rest, emit_bf16):
    u_scr = rest[-1]

    @pl.when(pl.program_id(1) == 0)
    def _():
        ub = _rms(x_ref[...], g_ref[...]).astype(BF16)
        u_scr[...] = ub
        wg = wgt_ref[...]
        if emit_bf16:
            row_ok = lax.broadcasted_iota(jnp.int32, (GATE_LANES, 1), 0) < 2 * H
            wg = jnp.where(row_ok, wg, 0.0).astype(BF16)
            rest[1][...] = wg
        gate_ref[...] = lax.dot_general(ub, wg, _NT, preferred_element_type=F32)

    w = wt_ref[...].astype(BF16)
    if emit_bf16:
        rest[0][...] = w
    p_ref[...] = lax.dot_general(u_scr[...], w, _NT, preferred_element_type=F32).astype(p_ref.dtype)


def _proj(x, gamma, wt, wgt, *, tm, tn, out_dtype):
    m = x.shape[0]
    emit = wt.dtype != BF16
    gate_spec = pl.BlockSpec((GATE_LANES, D), (lambda i, j: (PW // GATE_LANES, 0)) if emit
                             else (lambda i, j: (0, 0)))
    out_specs = [pl.BlockSpec((tm, tn), lambda i, j: (i, j)),
                 pl.BlockSpec((tm, GATE_LANES), lambda i, j: (i, 0))]
    out_shape = [jax.ShapeDtypeStruct((m, PW), out_dtype),
                 jax.ShapeDtypeStruct((m, GATE_LANES), F32)]
    if emit:
        out_specs += [pl.BlockSpec((tn, D), lambda i, j: (j, 0)),
                      pl.BlockSpec((GATE_LANES, D), lambda i, j: (0, 0))]
        out_shape += [jax.ShapeDtypeStruct((PW, D), BF16),
                      jax.ShapeDtypeStruct((GATE_LANES, D), BF16)]
    return pl.pallas_call(
        functools.partial(_proj_kernel, emit_bf16=emit),
        grid=(m // tm, PW // tn),
        in_specs=[pl.BlockSpec((tm, D), lambda i, j: (i, 0)),
                  pl.BlockSpec((1, D), lambda i, j: (0, 0)),
                  pl.BlockSpec((tn, D), lambda i, j: (j, 0)),
                  gate_spec],
        out_specs=out_specs,
        out_shape=out_shape,
        scratch_shapes=[pltpu.VMEM((tm, D), BF16)],
        compiler_params=_params(2),
        name="proj_cast" if emit else "proj",
    )(x, gamma, wt, wgt)


def _head_output(num, den, m_t, gout, o):
    hv = num * (1.0 / jnp.maximum(jnp.abs(den), jnp.exp(-m_t)))
    hv = hv * lax.rsqrt(jnp.mean(hv * hv, axis=1, keepdims=True) + EPS)
    return hv * gout * jax.nn.sigmoid(o.astype(F32))


def _scan_kernel(q_ref, k_ref, v_ref, o_ref, gt_ref, bias_ref, gout_ref, c0_ref, n0_ref, m0_ref,
                 hg_ref, cout_ref, nout_ref, mout_ref, c_scr, n_scr, m_scr, *, L, lead_pad, nc):
    c = pl.program_id(1)

    @pl.when(c == 0)
    def _():
        c_scr[...] = c0_ref[...]
        n_scr[...] = n0_ref[...]
        m_scr[...] = m0_ref[...]

    gates = gt_ref[...] + bias_ref[...]
    gates_t = gates.T
    row = lax.broadcasted_iota(jnp.int32, (L, L), 0)
    col = lax.broadcasted_iota(jnp.int32, (L, L), 1)
    causal = row >= col
    causal_t = row <= col
    if lead_pad:
        live_col = lax.broadcasted_iota(jnp.int32, (L, 1), 0) >= lead_pad
        live_row = lax.broadcasted_iota(jnp.int32, (1, L), 1) >= lead_pad

    for hh in range(H):
        ig_col = gates[:, hh:hh + 1]
        ig_row = gates_t[hh:hh + 1, :]
        lf_col = _log_sigmoid(gates[:, H + hh:H + hh + 1])
        lf_row = _log_sigmoid(gates_t[H + hh:H + hh + 1, :])
        if lead_pad:
            ig_col = jnp.where(live_col, ig_col, -jnp.inf)
            ig_row = jnp.where(live_row, ig_row, -jnp.inf)
            lf_col = jnp.where(live_col, lf_col, 0.0)
            lf_row = jnp.where(live_row, lf_row, 0.0)
        b_col = jnp.sum(jnp.where(causal, lf_row, 0.0), axis=1, keepdims=True)
        b_row = jnp.sum(jnp.where(causal_t, lf_col, 0.0), axis=0, keepdims=True)
        m_prev = m_scr[hh, 0:1, 0:1]
        d = jnp.where(causal, b_col - b_row + ig_row, -jnp.inf)
        inter = b_col + m_prev
        m_t = jnp.maximum(inter, jnp.max(d, axis=1, keepdims=True))
        w_inter = jnp.exp(inter - m_t) * SCALE

        q = q_ref[:, hh * DK:(hh + 1) * DK]
        k = k_ref[:, hh * DK:(hh + 1) * DK]
        v = v_ref[:, hh * DV:(hh + 1) * DV]
        qk = lax.dot_general(q, k, _NT, preferred_element_type=F32)
        s = qk * (jnp.exp(d - m_t) * SCALE)
        cmat = c_scr[hh]
        nvec = n_scr[hh, 0:1, :]
        num = w_inter * jnp.dot(q, cmat.astype(BF16), preferred_element_type=F32) \
            + jnp.dot(s.astype(BF16), v, preferred_element_type=F32)
        den = w_inter * jnp.sum(q.astype(F32) * nvec, axis=1, keepdims=True) \
            + jnp.sum(s, axis=1, keepdims=True)
        hout = _head_output(num, den, m_t, gout_ref[:, hh * DV:(hh + 1) * DV],
                            o_ref[:, hh * DV:(hh + 1) * DV])
        hg_ref[:, hh * DV:(hh + 1) * DV] = hout.astype(hg_ref.dtype)

        m_new = m_t[L - 1:L, :]
        b_last = b_col[L - 1:L, :]
        decay = jnp.exp(b_last + m_prev - m_new)
        wk = jnp.exp(b_last - b_col + ig_col - m_new) * k.astype(F32)
        c_scr[hh] = decay * cmat + jnp.dot(wk.T.astype(BF16), v, preferred_element_type=F32)
        n_scr[hh] = jnp.broadcast_to(decay * nvec + jnp.sum(wk, axis=0, keepdims=True), (8, DK))
        m_scr[hh] = jnp.broadcast_to(m_new, (8, GATE_LANES))

    @pl.when(c == nc - 1)
    def _():
        cout_ref[...] = c_scr[...]
        nout_ref[...] = n_scr[...]
        mout_ref[...] = m_scr[...]


def _scan(p, gates, bias, gout, c0, n0, m0, *, L, lead_pad=0):
    b, s, _ = p.shape
    nc = s // L
    kern = functools.partial(_scan_kernel, L=L, lead_pad=lead_pad, nc=nc)
    return pl.pallas_call(
        kern,
        grid=(b, nc),
        in_specs=[pl.BlockSpec((None, L, QKW), lambda i, c: (i, c, 0)),
                  pl.BlockSpec((None, L, QKW), lambda i, c: (i, c, 1)),
                  pl.BlockSpec((None, L, VW), lambda i, c: (i, c, 1)),
                  pl.BlockSpec((None, L, VW), lambda i, c: (i, c, 2)),
                  pl.BlockSpec((None, L, GATE_LANES), lambda i, c: (i, c, 0)),
                  pl.BlockSpec((1, GATE_LANES), lambda i, c: (0, 0)),
                  pl.BlockSpec((1, VW), lambda i, c: (0, 0)),
                  pl.BlockSpec((H, DK, DV), lambda i, c: (0, 0, 0)),
                  pl.BlockSpec((H, 8, DK), lambda i, c: (0, 0, 0)),
                  pl.BlockSpec((H, 8, GATE_LANES), lambda i, c: (0, 0, 0))],
        out_specs=[pl.BlockSpec((None, L, VW), lambda i, c: (i, c, 0)),
                   pl.BlockSpec((None, H, DK, DV), lambda i, c: (i, 0, 0, 0)),
                   pl.BlockSpec((None, H, 8, DK), lambda i, c: (i, 0, 0, 0)),
                   pl.BlockSpec((None, H, 8, GATE_LANES), lambda i, c: (i, 0, 0, 0))],
        out_shape=[jax.ShapeDtypeStruct((b, s, VW), BF16),
                   jax.ShapeDtypeStruct((b, H, DK, DV), F32),
                   jax.ShapeDtypeStruct((b, H, 8, DK), F32),
                   jax.ShapeDtypeStruct((b, H, 8, GATE_LANES), F32)],
        scratch_shapes=[pltpu.VMEM((H, DK, DV), F32),
                        pltpu.VMEM((H, 8, DK), F32),
                        pltpu.VMEM((H, 8, GATE_LANES), F32)],
        compiler_params=_params(2),
        name="scan",
    )(p, p, p, p, gates, bias, gout, c0, n0, m0)


def _scan_s_kernel(q_ref, k_ref, v_ref, o_ref, gt_ref, bias_ref, gout_ref, mtok_ref, c_ref, n_ref,
                   hg_ref, cout_ref, nout_ref, mout_ref, qc_scr, ntok_scr, *, T, NB):
    hh = pl.program_id(1) % H
    LT = NB * T
    gates = gt_ref[...] + bias_ref[...]
    gates_t = gates.T
    lane = lax.broadcasted_iota(jnp.int32, (LT, GATE_LANES), 1)
    sub = lax.broadcasted_iota(jnp.int32, (GATE_LANES, LT), 0)

    def pick_col(a, idx):
        return jnp.sum(jnp.where(lane == idx, a, 0.0), axis=1, keepdims=True)

    def pick_row(a, idx):
        return jnp.sum(jnp.where(sub == idx, a, 0.0), axis=0, keepdims=True)

    ig_col = pick_col(gates, hh)
    ig_row = pick_row(gates_t, hh)
    lf_col = _log_sigmoid(pick_col(gates, hh + H))
    lf_row = _log_sigmoid(pick_row(gates_t, hh + H))
    m_prev = pick_col(mtok_ref[...], hh)

    row = lax.broadcasted_iota(jnp.int32, (LT, LT), 0)
    col = lax.broadcasted_iota(jnp.int32, (LT, LT), 1)
    same = (row // T) == (col // T)
    causal = jnp.logical_and(same, row >= col)
    causal_t = jnp.logical_and(same, row <= col)
    b_col = jnp.sum(jnp.where(causal, lf_row, 0.0), axis=1, keepdims=True)
    b_row = jnp.sum(jnp.where(causal_t, lf_col, 0.0), axis=0, keepdims=True)
    b_end = jnp.sum(jnp.where(same, lf_row, 0.0), axis=1, keepdims=True)
    d = jnp.where(causal, b_col - b_row + ig_row, -jnp.inf)
    inter = b_col + m_prev
    m_t = jnp.maximum(inter, jnp.max(d, axis=1, keepdims=True))
    d_end = jnp.where(same, b_end - b_row + ig_row, -jnp.inf)
    m_new = jnp.maximum(b_end + m_prev, jnp.max(d_end, axis=1, keepdims=True))
    w_inter = jnp.exp(inter - m_t) * SCALE

    q32 = q_ref[...]
    k32 = k_ref[...]
    q = q32.astype(BF16)
    v = v_ref[...].astype(BF16)
    qk = lax.dot_general(q, k32.astype(BF16), (((1,), (1,)), ((), ())), preferred_element_type=F32)
    s = qk * (jnp.exp(d - m_t) * SCALE)
    num_intra = jnp.dot(s.astype(BF16), v, preferred_element_type=F32)
    den_intra = jnp.sum(s, axis=1, keepdims=True)

    decay = jnp.exp(b_end + m_prev - m_new)
    wk = jnp.exp(b_end - b_col + ig_col - m_new) * k32
    wk_t = wk.T
    col_seq = lax.broadcasted_iota(jnp.int32, (DK, LT), 1) // T

    for bb in range(NB):
        r0 = bb * T
        cmat = c_ref[bb]
        nvec = n_ref[bb:bb + 1, :]
        qc_scr[r0:r0 + T, :] = jnp.dot(q32[r0:r0 + T, :].astype(BF16), cmat.astype(BF16),
                                       preferred_element_type=F32)
        ntok_scr[r0:r0 + T, :] = jnp.broadcast_to(nvec, (T, DK))
        upd = jnp.dot(jnp.where(col_seq == bb, wk_t, 0.0).astype(BF16), v, preferred_element_type=F32)
        dec = decay[r0:r0 + 1, :]
        cout_ref[bb] = dec * cmat + upd
        nout_ref[bb:bb + 1, :] = dec * nvec + jnp.sum(wk[r0:r0 + T, :], axis=0, keepdims=True)
        mout_ref[bb:bb + 1, :] = jnp.broadcast_to(m_new[r0:r0 + 1, :], (1, GATE_LANES))

    num = w_inter * qc_scr[...] + num_intra
    den = w_inter * jnp.sum(q32 * ntok_scr[...], axis=1, keepdims=True) + den_intra
    hout = _head_output(num, den, m_t, gout_ref[...], o_ref[...]).astype(hg_ref.dtype)
    src_row = (row % NB) * T + row // NB
    perm = jnp.where(col == src_row, 1.0, 0.0).astype(hg_ref.dtype)
    moved = jnp.dot(perm, hout, preferred_element_type=F32)
    hg_ref[...] = moved.reshape(T, NB, DV).astype(hg_ref.dtype)


def _proj_scan_s_kernel(x_ref, g_ref, wt_ref, wgt_ref,
                        q_ref, k_ref, v_ref, o_ref, gt_ref, bias_ref, gout_ref, mtok_ref, c_ref, n_ref,
                        p_ref, gate_ref, hg_ref, cout_ref, nout_ref, mout_ref,
                        u_scr, qc_scr, ntok_scr, *, T, NB):
    _proj_kernel(x_ref, g_ref, wt_ref, wgt_ref, p_ref, gate_ref, u_scr, emit_bf16=False)
    _scan_s_kernel(q_ref, k_ref, v_ref, o_ref, gt_ref, bias_ref, gout_ref, mtok_ref, c_ref, n_ref,
                   hg_ref, cout_ref, nout_ref, mout_ref, qc_scr, ntok_scr, T=T, NB=NB)


def _proj_scan_s(x, gamma, wt, wgt, p_s, gates_s, bias, gout, mtok, c, n_hm, *, tm, T):
    m = x.shape[0]
    nseq = c.shape[0]
    nt = m // tm
    nj = 2 * H
    tn = PW // nj
    gpt = nj // H
    nb = nseq // (nt * gpt)
    assert nb * nt * gpt == nseq and nb % 8 == 0 and tn % LANES == 0
    lt = nb * T
    kern = functools.partial(_proj_scan_s_kernel, T=T, NB=nb)

    def grp(i, j):
        return i * gpt + j // H

    def head(j):
        return j % H

    return pl.pallas_call(
        kern,
        grid=(nt, nj),
        in_specs=[pl.BlockSpec((tm, D), lambda i, j: (i, 0), pipeline_mode=pl.Buffered(1)),
                  pl.BlockSpec((1, D), lambda i, j: (0, 0)),
                  pl.BlockSpec((tn, D), lambda i, j: (j, 0)),
                  pl.BlockSpec((GATE_LANES, D), lambda i, j: (0, 0)),
                  pl.BlockSpec((lt, DK), lambda i, j: (grp(i, j), head(j))),
                  pl.BlockSpec((lt, DK), lambda i, j: (grp(i, j), H + head(j))),
                  pl.BlockSpec((lt, DV), lambda i, j: (grp(i, j), H + head(j))),
                  pl.BlockSpec((lt, DV), lambda i, j: (grp(i, j), 2 * H + head(j))),
                  pl.BlockSpec((lt, GATE_LANES), lambda i, j: (grp(i, j), 0)),
                  pl.BlockSpec((1, GATE_LANES), lambda i, j: (0, 0)),
                  pl.BlockSpec((1, DV), lambda i, j: (0, head(j))),
                  pl.BlockSpec((lt, GATE_LANES), lambda i, j: (grp(i, j), 0)),
                  pl.BlockSpec((nb, None, DK, DV), lambda i, j: (grp(i, j), head(j), 0, 0)),
                  pl.BlockSpec((None, nb, DK), lambda i, j: (head(j), grp(i, j), 0))],
        out_specs=[pl.BlockSpec((tm, tn), lambda i, j: (i, j)),
                   pl.BlockSpec((tm, GATE_LANES), lambda i, j: (i, 0)),
                   pl.BlockSpec((T, nb, DV), lambda i, j: (0, grp(i, j), head(j))),
                   pl.BlockSpec((nb, None, DK, DV), lambda i, j: (grp(i, j), head(j), 0, 0)),
                   pl.BlockSpec((None, nb, DK), lambda i, j: (head(j), grp(i, j), 0)),
                   pl.BlockSpec((None, nb, GATE_LANES), lambda i, j: (head(j), grp(i, j), 0))],
        out_shape=[jax.ShapeDtypeStruct((m, PW), BF16),
                   jax.ShapeDtypeStruct((m, GATE_LANES), F32),
                   jax.ShapeDtypeStruct((T, nseq, VW), BF16),
                   jax.ShapeDtypeStruct((nseq, H, DK, DV), F32),
                   jax.ShapeDtypeStruct((H, nseq, DK), F32),
                   jax.ShapeDtypeStruct((H, nseq, GATE_LANES), F32)],
        scratch_shapes=[pltpu.VMEM((tm, D), BF16),
                        pltpu.VMEM((lt, DV), F32),
                        pltpu.VMEM((lt, DK), F32)],
        compiler_params=_params(2),
        name="proj_scan_sample",
    )(x, gamma, wt, wgt, p_s, p_s, p_s, p_s, gates_s, bias, gout, mtok, c, n_hm)


def _mmres_cast_kernel(a_ref, w_ref, x_ref, o_ref, wc_ref):
    w = w_ref[...].astype(BF16)
    wc_ref[...] = w
    y = jnp.dot(a_ref[...], w, preferred_element_type=F32)
    nseq, t_dec, _ = x_ref.shape
    for t in range(t_dec):
        o_ref[t * nseq:(t + 1) * nseq, :] = x_ref[:, t, :] + y[t * nseq:(t + 1) * nseq, :]


def _mmres_cast(a, w, x, *, tn):
    m, kdim = a.shape
    n = w.shape[2]
    nseq, t_dec, _ = x.shape
    return pl.pallas_call(
        _mmres_cast_kernel,
        grid=(n // tn,),
        in_specs=[pl.BlockSpec((m, kdim), lambda j: (0, 0)),
                  pl.BlockSpec((None, kdim, tn), lambda j: (0, 0, j)),
                  pl.BlockSpec((nseq, t_dec, tn), lambda j: (0, 0, j))],
        out_specs=[pl.BlockSpec((m, tn), lambda j: (0, j)),
                   pl.BlockSpec((kdim, tn), lambda j: (0, j))],
        out_shape=[jax.ShapeDtypeStruct((m, n), F32),
                   jax.ShapeDtypeStruct((kdim, n), BF16)],
        compiler_params=_params(1),
        name="out_proj_cast",
    )(a, w, x)


def _mmres_kernel(a_ref, w_ref, x_ref, o_ref):
    o_ref[...] = x_ref[...] + jnp.dot(a_ref[...], w_ref[...], preferred_element_type=F32)


def _mmres(a, w, x, *, tm):
    m, kdim = a.shape
    n = w.shape[1]
    return pl.pallas_call(
        _mmres_kernel,
        grid=(m // tm,),
        in_specs=[pl.BlockSpec((tm, kdim), lambda i: (i, 0)),
                  pl.BlockSpec((kdim, n), lambda i: (0, 0)),
                  pl.BlockSpec((tm, n), lambda i: (i, 0))],
        out_specs=pl.BlockSpec((tm, n), lambda i: (i, 0)),
        out_shape=jax.ShapeDtypeStruct((m, n), F32),
        compiler_params=_params(1),
        name="out_proj",
    )(a, w, x)


def _ffn_kernel(x_ref, gam_ref, wg_ref, wa0_ref, wa1_ref, cw_ref, cb_ref, wd_ref, st_ref, gfin_ref,
                o_ref, so_ref, wupc_ref, wdnc_ref,
                u_scr, gext_scr, a_scr, wup0_scr, wup1_scr, wdn0_scr, wdn1_scr,
                *, tm, shift, nsub, final_norm):
    s = pl.program_id(1)
    sr = CONV_TAIL * shift
    wup_slots = (wup0_scr, wup1_scr)
    wdn_slots = (wdn0_scr, wdn1_scr)

    def cast_steps(slot):
        wup_scr = wup_slots[slot]
        wdn_scr = wdn_slots[slot]
        valid = D_FF - jnp.minimum(s, NF - 1) * TF

        def cast_gate():
            ok = lax.broadcasted_iota(jnp.int32, (1, TF), 1) < valid
            w = jnp.where(ok, wg_ref[...], 0.0).astype(BF16)
            wup_scr[:, 0:TF] = w
            wupc_ref[:, 0:TF] = w

        def cast_value():
            lane = lax.broadcasted_iota(jnp.int32, (1, LANES), 1)
            w0 = jnp.where(lane < valid, wa0_ref[...], 0.0).astype(BF16)
            w1 = jnp.where(lane + LANES < valid, wa1_ref[...], 0.0).astype(BF16)
            wup_scr[:, TF:TF + LANES] = w0
            wup_scr[:, TF + LANES:2 * TF] = w1
            wupc_ref[:, TF:TF + LANES] = w0
            wupc_ref[:, TF + LANES:2 * TF] = w1

        def cast_down(r0, rows):
            row_ok = r0 + lax.broadcasted_iota(jnp.int32, (rows, 1), 0) < valid
            w = jnp.where(row_ok, wd_ref[r0:r0 + rows, :], 0.0).astype(BF16)
            wdn_scr[r0:r0 + rows, :] = w
            wdnc_ref[r0:r0 + rows, :] = w

        half = TF // 2
        return [cast_gate, cast_value, functools.partial(cast_down, 0, half),
                functools.partial(cast_down, half, half)]

    def run_tile(slot, fillers=()):
        fillers = list(fillers)

        def fill():
            if fillers:
                fillers.pop(0)()

        wup_scr = wup_slots[slot]
        wdn_scr = wdn_slots[slot]
        f = s - 1
        col_ok = lax.broadcasted_iota(jnp.int32, (1, TF), 1) < D_FF - f * TF
        cw = cw_ref[...]
        cb = cb_ref[...]
        for t in range(CONV_TAIL):
            gext_scr[t * shift:(t + 1) * shift, :] = st_ref[:, t, :]
        ts = tm // nsub
        for h in range(nsub):
            r0 = h * ts
            ga = jnp.dot(u_scr[r0:r0 + ts, :], wup_scr[...], preferred_element_type=F32)
            gext_scr[sr + r0:sr + r0 + ts, :] = ga[:, 0:TF]
            a_scr[r0:r0 + ts, :] = ga[:, TF:2 * TF]
            fill()
        for h in range(nsub):
            r0 = h * ts
            g = gext_scr[sr + r0:sr + r0 + ts, :]
            g_m2 = gext_scr[sr + r0 - 2 * shift:sr + r0 - 2 * shift + ts, :]
            g_m1 = gext_scr[sr + r0 - shift:sr + r0 - shift + ts, :]
            gc = cb + ((cw[0:1, :] * g_m2 + cw[1:2, :] * g_m1) + cw[2:3, :] * g)
            hmid = jnp.where(col_ok, (gc * jax.nn.sigmoid(gc)) * a_scr[r0:r0 + ts, :], 0.0)
            o_ref[r0:r0 + ts, :] += jnp.dot(hmid.astype(BF16), wdn_scr[...], preferred_element_type=F32)
            fill()
        while fillers:
            fill()
        for t in range(CONV_TAIL):
            so_ref[:, t, :] = gext_scr[tm + t * shift:tm + (t + 1) * shift, :]

    @pl.when(s == 0)
    def _():
        x = x_ref[...]
        u_scr[...] = _rms(x, gam_ref[...]).astype(BF16)
        o_ref[...] = x
        for step in cast_steps(0):
            step()

    for parity in range(2):
        @pl.when(jnp.logical_and(s > 0, s % 2 == parity))
        def _(parity=parity):
            run_tile(1 - parity, cast_steps(parity))

    if final_norm:
        @pl.when(s == NF)
        def _():
            o_ref[...] = _rms(o_ref[...], gfin_ref[...])


def _ffn(x, layer, gamma, w_up, conv_w, conv_b, w_down, st, gfin, *, final_norm):
    assert TF == 2 * LANES
    tm = x.shape[0]
    shift = st.shape[0]
    sr = CONV_TAIL * shift
    kern = functools.partial(_ffn_kernel, tm=tm, shift=shift, nsub=2, final_norm=final_norm)
    last_a = 2 * FF_LANE_BLOCKS - 1

    def wt(s):
        return jnp.minimum(s, NF - 1)

    def ft(s):
        return jnp.maximum(s - 1, 0)

    st_spec = pl.BlockSpec((shift, CONV_TAIL, TF), lambda i, s: (0, 0, ft(s)))
    return pl.pallas_call(
        kern,
        grid=(1, NF + 1),
        in_specs=[pl.BlockSpec((tm, D), lambda i, s: (i, 0), pipeline_mode=pl.Buffered(1)),
                  pl.BlockSpec((1, D), lambda i, s: (0, 0)),
                  pl.BlockSpec((None, D, TF), lambda i, s: (layer, 0, wt(s))),
                  pl.BlockSpec((None, D, LANES), lambda i, s: (layer, 0, FF_LANE_BLOCKS + 2 * wt(s))),
                  pl.BlockSpec((None, D, LANES),
                               lambda i, s: (layer, 0, jnp.minimum(FF_LANE_BLOCKS + 2 * wt(s) + 1, last_a))),
                  pl.BlockSpec((None, 3, TF), lambda i, s: (layer, 0, ft(s))),
                  pl.BlockSpec((None, 1, TF), lambda i, s: (layer, 0, ft(s))),
                  pl.BlockSpec((None, TF, D), lambda i, s: (layer, wt(s), 0)),
                  st_spec,
                  pl.BlockSpec((1, D), lambda i, s: (0, 0))],
        out_specs=[pl.BlockSpec((tm, D), lambda i, s: (i, 0)),
                   st_spec,
                   pl.BlockSpec((D, 2 * TF), lambda i, s: (0, wt(s))),
                   pl.BlockSpec((TF, D), lambda i, s: (wt(s), 0))],
        out_shape=[jax.ShapeDtypeStruct((tm, D), F32),
                   jax.ShapeDtypeStruct(st.shape, F32),
                   jax.ShapeDtypeStruct((D, NF * 2 * TF), BF16),
                   jax.ShapeDtypeStruct((NF * TF, D), BF16)],
        scratch_shapes=[pltpu.VMEM((tm, D), BF16),
                        pltpu.VMEM((sr + tm, TF), F32),
                        pltpu.VMEM((tm, TF), F32),
                        pltpu.VMEM((D, 2 * TF), BF16),
                        pltpu.VMEM((D, 2 * TF), BF16),
                        pltpu.VMEM((TF, D), BF16),
                        pltpu.VMEM((TF, D), BF16)],
        compiler_params=_params(2),
        name="conv_ffn_cast",
    )(x, gamma, w_up, w_up, w_up, conv_w, conv_b, w_down, st, gfin)


def _ffn_bf16_kernel(x_ref, gam_ref, wup_ref, cw_ref, cb_ref, wdn_ref, st_ref, gfin_ref,
                     o_ref, so_ref, u_scr, gext_scr, a_scr, carry_scr,
                     *, tm, sr, shift, tps, nsub, final_norm):
    i = pl.program_id(0)
    f = pl.program_id(1)
    tf2 = 2 * TF

    @pl.when(f == 0)
    def _():
        x = x_ref[...]
        u_scr[...] = _rms(x, gam_ref[...]).astype(BF16)
        o_ref[...] = x
        if tps > 1:
            @pl.when(i == 0)
            def _():
                carry_scr[...] = jnp.zeros_like(carry_scr)

    col_ok = lax.broadcasted_iota(jnp.int32, (1, tf2), 1) < D_FF - f * tf2
    cw = cw_ref[...]
    cb = cb_ref[...]
    if tps == 1:
        gext_scr[0:sr, :] = st_ref[...]
    else:
        gext_scr[0:sr, :] = jnp.where((i % tps) == 0, st_ref[...], carry_scr[f])
    ts = tm // nsub
    for h in range(nsub):
        r0 = h * ts
        ga = jnp.dot(u_scr[r0:r0 + ts, :], wup_ref[...], preferred_element_type=F32)
        gext_scr[sr + r0:sr + r0 + ts, 0:TF] = ga[:, 0:TF]
        gext_scr[sr + r0:sr + r0 + ts, TF:tf2] = ga[:, 2 * TF:3 * TF]
        a_scr[r0:r0 + ts, 0:TF] = ga[:, TF:2 * TF]
        a_scr[r0:r0 + ts, TF:tf2] = ga[:, 3 * TF:4 * TF]
    for h in range(nsub):
        r0 = h * ts
        g = gext_scr[sr + r0:sr + r0 + ts, :]
        g_m2 = gext_scr[sr + r0 - 2 * shift:sr + r0 - 2 * shift + ts, :]
        g_m1 = gext_scr[sr + r0 - shift:sr + r0 - shift + ts, :]
        gc = cb + ((cw[0:1, :] * g_m2 + cw[1:2, :] * g_m1) + cw[2:3, :] * g)
        hmid = jnp.where(col_ok, (gc * jax.nn.sigmoid(gc)) * a_scr[r0:r0 + ts, :], 0.0)
        o_ref[r0:r0 + ts, :] += jnp.dot(hmid.astype(BF16), wdn_ref[...], preferred_element_type=F32)
    g_tail = gext_scr[tm:tm + sr, :]
    so_ref[...] = g_tail
    if tps > 1:
        carry_scr[f] = g_tail

    if final_norm:
        @pl.when(f == pl.num_programs(1) - 1)
        def _():
            o_ref[...] = _rms(o_ref[...], gfin_ref[...])


def _ffn_bf16(x, layer, gamma, wupc, conv_w, conv_b, wdnc, st, gfin, *, tm, sr, shift, tps, st_per_tile,
              final_norm):
    assert NF % 2 == 0
    m = x.shape[0]
    nt = m // tm
    tf2 = 2 * TF
    nf2 = NF // 2
    kern = functools.partial(_ffn_bf16_kernel, tm=tm, sr=sr, shift=shift, tps=tps,
                             nsub=2 if tm >= 64 else 1, final_norm=final_norm)
    st_map = (lambda i, f: (i, f)) if st_per_tile else (lambda i, f: (0, f))
    return pl.pallas_call(
        kern,
        grid=(nt, nf2),
        in_specs=[pl.BlockSpec((tm, D), lambda i, f: (i, 0)),
                  pl.BlockSpec((1, D), lambda i, f: (0, 0)),
                  pl.BlockSpec((D, 2 * tf2), lambda i, f: (0, f)),
                  pl.BlockSpec((None, 3, tf2), lambda i, f: (layer, 0, f)),
                  pl.BlockSpec((None, 1, tf2), lambda i, f: (layer, 0, f)),
                  pl.BlockSpec((tf2, D), lambda i, f: (f, 0)),
                  pl.BlockSpec((sr, tf2), st_map),
                  pl.BlockSpec((1, D), lambda i, f: (0, 0))],
        out_specs=[pl.BlockSpec((tm, D), lambda i, f: (i, 0)),
                   pl.BlockSpec((sr, tf2), lambda i, f: (i, f))],
        out_shape=[jax.ShapeDtypeStruct((m, D), F32),
                   jax.ShapeDtypeStruct((nt * sr, D_FF), F32)],
        scratch_shapes=[pltpu.VMEM((tm, D), BF16),
                        pltpu.VMEM((sr + tm, tf2), F32),
                        pltpu.VMEM((tm, tf2), F32),
                        pltpu.VMEM((nf2, sr, tf2), F32)],
        compiler_params=_params(2),
        name="conv_ffn",
    )(x, gamma, wupc, conv_w, conv_b, wdnc, st, gfin)


def _window_sum(ext_scr, tmp_scrs, base, tm, shift, w):
    starts = {w: base}
    v = w
    while v > 2:
        starts[v // 2] = (starts[v] - (v // 2) * shift) // 8 * 8
        v //= 2
    src = ext_scr
    v = 1
    k = 0
    while True:
        lo = starts[2 * v]
        n = base + tm - lo
        val = src[lo:lo + n, :] + src[lo - v * shift:lo - v * shift + n, :]
        v *= 2
        if v == w:
            return val
        dst = tmp_scrs[k % 2]
        dst[lo:lo + n, :] = val
        src = dst
        k += 1


def _pool_kernel(x_ref, gam_ref, wp_ref, sc_ref, st_ref, o_ref, ut_ref, *scratch, tm, shift, tps, pos0, tr):
    ng = len(POOL_WINDOWS)
    uext_scrs = scratch[0:ng]
    tmp_scrs = scratch[ng:ng + 2]
    carry_scr = scratch[ng + 2]
    i = pl.program_id(0)
    hr = POOL_HALO * shift
    base = POOL_PAD_ROWS + hr

    if tps > 1:
        @pl.when(i == 0)
        def _():
            carry_scr[...] = jnp.zeros_like(carry_scr)

    x = x_ref[...]
    rinv = lax.rsqrt(jnp.mean(x * x, axis=1, keepdims=True) + EPS)
    for kk, w in enumerate(POOL_WINDOWS):
        cs = slice(kk * PG, (kk + 1) * PG)
        uext_scr = uext_scrs[kk]
        xg = x_ref[:, cs]
        ug = xg * rinv * gam_ref[:, cs]
        uext_scr[0:POOL_PAD_ROWS, :] = jnp.zeros((POOL_PAD_ROWS, PG), F32)
        if tps == 1:
            uext_scr[POOL_PAD_ROWS:base, :] = st_ref[:, cs]
        else:
            uext_scr[POOL_PAD_ROWS:base, :] = jnp.where((i % tps) == 0, st_ref[:, cs], carry_scr[:, cs])
        uext_scr[base:base + tm, :] = ug
        acc = _window_sum(uext_scr, tmp_scrs, base, tm, shift, w)
        if pos0 + 1 >= w:
            pooled = acc / float(w) - ug
        else:
            step = (i % tps) * (tm // shift) + lax.broadcasted_iota(jnp.int32, (tm, 1), 0) // shift
            cnt = jnp.minimum(w, pos0 + step + 1).astype(F32)
            pooled = acc / cnt - ug
        y = jnp.dot(pooled.astype(BF16), wp_ref[kk].astype(BF16), preferred_element_type=F32)
        o_ref[:, cs] = xg + y * sc_ref[:, cs]
        ut_ref[:, cs] = ug[tm - tr:tm, :]
        if tps > 1:
            carry_scr[:, cs] = ug[tm - hr:tm, :]


def _pool(x, gamma, wp, sc, st, *, tm, shift, tps, pos0, tr):
    m = x.shape[0]
    nt = m // tm
    hr = POOL_HALO * shift
    ng = len(POOL_WINDOWS)
    kern = functools.partial(_pool_kernel, tm=tm, shift=shift, tps=tps, pos0=pos0, tr=tr)
    return pl.pallas_call(
        kern,
        grid=(nt,),
        in_specs=[pl.BlockSpec((tm, D), lambda i: (i, 0)),
                  pl.BlockSpec((1, D), lambda i: (0, 0)),
                  pl.BlockSpec((None, ng, PG, PG), lambda i: (0, 0, 0, 0)),
                  pl.BlockSpec((1, D), lambda i: (0, 0)),
                  pl.BlockSpec((hr, D), lambda i: (0, 0))],
        out_specs=[pl.BlockSpec((tm, D), lambda i: (i, 0)),
                   pl.BlockSpec((tr, D), lambda i: (i, 0))],
        out_shape=[jax.ShapeDtypeStruct((m, D), F32),
                   jax.ShapeDtypeStruct((nt * tr, D), F32)],
        scratch_shapes=[pltpu.VMEM((POOL_PAD_ROWS + hr + tm, PG), F32)] * (ng + 2)
        + [pltpu.VMEM((hr, D), F32)],
        compiler_params=_params(1),
        name="pool_mixer",
    )(x, gamma, wp, sc, st)


def _pool_sample_kernel(x_ref, gam_ref, wp_ref, sc_ref, st_ref, o_ref, new_ref, rinv_scr, uext_scr,
                        *, nseq, t_dec):
    nh = POOL_HALO - 1
    grp = pl.program_id(0)
    tm = nseq * t_dec
    hr = POOL_HALO * nseq

    @pl.when(grp == 0)
    def _():
        x = x_ref[...]
        rinv_scr[...] = lax.rsqrt(jnp.mean(x * x, axis=1, keepdims=True) + EPS)

    for kk, w in enumerate(POOL_WINDOWS):
        @pl.when(grp == kk)
        def _(kk=kk, w=w):
            cs = slice(kk * PG, (kk + 1) * PG)
            xg = x_ref[:, cs]
            ug = xg * rinv_scr[...] * gam_ref[:, cs]
            for t in range(nh):
                uext_scr[(t + 1) * nseq:(t + 2) * nseq, :] = st_ref[t]
            uext_scr[hr:hr + tm, :] = ug
            acc = ug
            for j in range(1, w):
                acc = acc + uext_scr[hr - j * nseq:hr - j * nseq + tm, :]
            pooled = acc / float(w) - ug
            y = jnp.dot(pooled.astype(BF16), wp_ref[kk].astype(BF16), preferred_element_type=F32)
            o_ref[...] = xg + y * sc_ref[:, cs]
            for t in range(nh):
                src = (t + t_dec + 1) * nseq
                new_ref[t] = uext_scr[src:src + nseq, :]


def _pool_sample(x, gamma, wp, sc, state, *, nseq, t_dec):
    assert PAST_LEN + 1 >= max(POOL_WINDOWS)
    nh = POOL_HALO - 1
    ng = len(POOL_WINDOWS)
    tm = nseq * t_dec
    hr = POOL_HALO * nseq
    kern = functools.partial(_pool_sample_kernel, nseq=nseq, t_dec=t_dec)
    hist_spec = pl.BlockSpec((nh, nseq, PG), lambda g: (0, 0, g))
    return pl.pallas_call(
        kern,
        grid=(ng,),
        in_specs=[pl.BlockSpec((tm, D), lambda g: (0, 0)),
                  pl.BlockSpec((1, D), lambda g: (0, 0)),
                  pl.BlockSpec((None, ng, PG, PG), lambda g: (0, 0, 0, 0)),
                  pl.BlockSpec((1, D), lambda g: (0, 0)),
                  hist_spec],
        out_specs=[pl.BlockSpec((tm, PG), lambda g: (0, g)),
                   hist_spec],
        out_shape=[jax.ShapeDtypeStruct((tm, D), F32),
                   jax.ShapeDtypeStruct((nh, nseq, D), F32)],
        scratch_shapes=[pltpu.VMEM((tm, 1), F32),
                        pltpu.VMEM((hr + tm, PG), F32)],
        compiler_params=_params(1),
        name="pool_mixer_sample",
    )(x, gamma, wp, sc, state)


def _pad_cols(a, n):
    return jnp.pad(a, ((0, 0), (0, n - a.shape[1])))


def _from_time_major(a, nseq, t):
    return jnp.swapaxes(a.reshape((t, nseq) + a.shape[1:]), 0, 1)


def kernel(x_prompt, x_sample, state_mlstm_C, state_mlstm_n, state_mlstm_m, state_pool, state_ffn_conv,
           meta_tokens, norm_mix, norm_ffn, norm_final, w_mlstm_in, b_mlstm_gate, g_mlstm_out, w_mlstm_out,
           w_pool, pool_scale, w_up, conv_w, conv_b, w_down):
    bsz, seq, _ = x_prompt.shape
    nseq, t_dec, _ = x_sample.shape

    w_in_t = jnp.swapaxes(w_mlstm_in[0], 0, 1)
    bias = _pad_cols(b_mlstm_gate[0][None, :], GATE_LANES)
    w_out = w_mlstm_out
    gout = g_mlstm_out[0][None, :]
    wp = w_pool
    psc = pool_scale[0][None, :]
    gfin = norm_final[None, :]
    conv_b3 = conv_b[:, None, :]
    ffn_w = [(layer, norm_ffn[layer][None, :], w_up, conv_w, conv_b3, w_down) for layer in range(2)]
    g_mix0 = norm_mix[0][None, :]
    g_mix1 = norm_mix[1][None, :]

    ffn_cache = {}
    mix_cache = {}

    def proj(x, **kw):
        if "w_in" not in mix_cache:
            p, gates, wtc, wgc = _proj(x, g_mix0, w_in_t, w_in_t, **kw)
            mix_cache["w_in"] = (wtc, wgc)
            return p, gates
        return _proj(x, g_mix0, *mix_cache["w_in"], **kw)

    def out_proj(a, x, *, tm):
        if "w_out" not in mix_cache:
            y, mix_cache["w_out"] = _mmres_cast(a, w_out, x, tn=512)
            return y
        return _mmres(a, mix_cache["w_out"], x, tm=tm)

    def ffn_first(x, layer, st, *, final_norm):
        lyr, gamma, wu, cwt, cbs, wd = ffn_w[layer]
        y, cs, wupc, wdnc = _ffn(x, lyr, gamma, wu, cwt, cbs, wd, st, gfin, final_norm=final_norm)
        ffn_cache[layer] = (wupc, wdnc)
        return y, cs

    def ffn(x, layer, st, **kw):
        lyr, gamma, _, cwt, cbs, _ = ffn_w[layer]
        wupc, wdnc = ffn_cache[layer]
        return _ffn_bf16(x, lyr, gamma, wupc, cwt, cbs, wdnc, st, gfin, **kw)

    def long_stream(x, nb, s, st, *, tm, chunk, lead_pad, pos0, projected=None):
        c0, n0, m0, conv0, pool0, conv1 = st
        tm_ffn = tm
        p, gates = projected if projected is not None else proj(x, tm=tm, tn=1024, out_dtype=BF16)
        p = p.reshape(nb, s, PW)
        gates = gates.reshape(nb, s, GATE_LANES)
        if lead_pad:
            p = jnp.pad(p, ((0, 0), (lead_pad, 0), (0, 0)))
            gates = jnp.pad(gates, ((0, 0), (lead_pad, 0), (0, 0)))
        hg, c_new, n_new, m_new = _scan(p, gates, bias, gout, c0, n0, m0, L=chunk, lead_pad=lead_pad)
        hg = hg[:, lead_pad:].reshape(nb * s, VW)
        x1 = out_proj(hg, x, tm=min(tm, 512))
        x2, cs0 = ffn(x1, 0, conv0, tm=tm_ffn, sr=8, shift=1, tps=s // tm_ffn,
                      st_per_tile=False, final_norm=False)
        x3, ut = _pool(x2, g_mix1, wp, psc, pool0, tm=tm, shift=1, tps=s // tm, pos0=pos0,
                       tr=POOL_HALO)
        y, cs1 = ffn(x3, 1, conv1, tm=tm_ffn, sr=8, shift=1, tps=s // tm_ffn,
                     st_per_tile=False, final_norm=True)
        cs0 = cs0.reshape(nb, s // tm_ffn, 8, D_FF)[:, -1]
        cs1 = cs1.reshape(nb, s // tm_ffn, 8, D_FF)[:, -1]
        ut = ut.reshape(nb, s // tm, POOL_HALO, D)[:, -1]
        return y, (c_new, n_new, m_new, cs0, ut, cs1)

    def prompt_streams():
        zero_state = (jnp.zeros((H, DK, DV), F32), jnp.zeros((H, 8, DK), F32), jnp.zeros((H, 8, GATE_LANES), F32),
                      jnp.zeros((8, D_FF), F32), jnp.zeros((POOL_HALO, D), F32), jnp.zeros((8, D_FF), F32))
        _, (c_m, n_m, m_m, cs0_m, ut_m, cs1_m) = long_stream(
            meta_tokens, 1, N_META, zero_state, tm=N_META, chunk=128, lead_pad=128 - N_META, pos0=0)

        y_p, (c_p, n_p, m_p, cs0_p, ut_p, cs1_p) = long_stream(
            x_prompt.reshape(bsz * seq, D), bsz, seq, (c_m[0], n_m[0], m_m[0], cs0_m[0], ut_m[0], cs1_m[0]),
            tm=1024, chunk=SCAN_CHUNK, lead_pad=0, pos0=N_META, projected=prompt_projected)
        return (y_p.reshape(bsz, seq, D), c_p[None], n_p[:, :, 0][None], m_p[:, :, 0, 0][None],
                ut_p[:, 1:][None], jnp.stack([cs0_p[:, 6:], cs1_p[:, 6:]]))

    xs = x_sample.reshape(nseq * t_dec, D)
    p_s, gates_s = proj(xs, tm=nseq * t_dec, tn=512, out_dtype=F32)
    mtok = _pad_cols(jnp.repeat(state_mlstm_m[0], t_dec, axis=0), GATE_LANES)
    n_hm = jnp.swapaxes(state_mlstm_n[0], 0, 1)
    p_p, gates_p, hg_s, C_s, n_s_hm, m_s_hm = _proj_scan_s(
        x_prompt.reshape(bsz * seq, D), g_mix0, *mix_cache["w_in"], p_s, gates_s, bias, gout, mtok,
        state_mlstm_C[0], n_hm, tm=1024, T=t_dec)
    prompt_projected = (p_p, gates_p)
    assert nseq == SAMPLE_TILE_SEQS
    x1 = out_proj(hg_s.reshape(t_dec * nseq, VW), x_sample, tm=512)
    x2, cs0_s = ffn_first(x1, 0, state_ffn_conv[0], final_norm=False)
    x3, pool_new = _pool_sample(x2, g_mix1, wp, psc, jnp.swapaxes(state_pool[0], 0, 1), nseq=nseq, t_dec=t_dec)
    y_s, cs1_s = ffn_first(x3, 1, state_ffn_conv[1], final_norm=True)
    y_sample = _from_time_major(y_s, nseq, t_dec)
    n_s = jnp.swapaxes(n_s_hm, 0, 1)[None]
    m_s = jnp.swapaxes(m_s_hm[:, :, 0], 0, 1)[None]
    pool_s = jnp.swapaxes(pool_new, 0, 1)[None]
    conv_s = jnp.stack([cs0_s, cs1_s])

    y_prompt, C_p, n_p, m_p, pool_p, conv_p = prompt_streams()
    return (y_prompt, y_sample, C_p, n_p, m_p, pool_p, conv_p,
            C_s[None], n_s, m_s, pool_s, conv_s)
```

```python
import functools

import jax
import jax.numpy as jnp
from jax import lax
from jax.experimental import pallas as pl
from jax.experimental.pallas import tpu as pltpu

F32 = jnp.float32
BF16 = jnp.bfloat16

EPS = 1e-6
D = 2048
H = 4
DK = 256
DV = 512
QKW = H * DK
VW = H * DV
PW = 2 * QKW + 2 * VW
GATE_LANES = 128
SCALE = DK ** -0.5
POOL_WINDOWS = (2, 4, 8, 16)
PG = D // len(POOL_WINDOWS)
POOL_HALO = 16
POOL_PAD_ROWS = 16
D_FF = 5504
TF = 256
NF = -(-D_FF // TF)
LANES = 128
FF_LANE_BLOCKS = D_FF // LANES
CONV_TAIL = 2
N_META = 16
PAST_LEN = 16384
SCAN_CHUNK = 512
SAMPLE_TILE_SEQS = 128
VMEM_LIMIT = 60 * 1024 * 1024


def _params(n_axes):
    return pltpu.CompilerParams(dimension_semantics=("arbitrary",) * n_axes,
                                vmem_limit_bytes=VMEM_LIMIT)


def _rms(x, g):
    return x * lax.rsqrt(jnp.mean(x * x, axis=-1, keepdims=True) + EPS) * g


def _log_sigmoid(x):
    return jnp.minimum(x, 0.0) - jnp.log(1.0 + jnp.exp(-jnp.abs(x)))


_NT = (((1,), (1,)), ((), ()))


def _proj_kernel(x_ref, g_ref, wt_ref, wgt_ref, p_ref, gate_ref, *rest, emit_bf16):
    u_scr = rest[-1]

    @pl.when(pl.program_id(1) == 0)
    def _():
        ub = _rms(x_ref[...], g_ref[...]).astype(BF16)
        u_scr[...] = ub
        wg = wgt_ref[...]
        if emit_bf16:
            row_ok = lax.broadcasted_iota(jnp.int32, (GATE_LANES, 1), 0) < 2 * H
            wg = jnp.where(row_ok, wg, 0.0).astype(BF16)
            rest[1][...] = wg
        gate_ref[...] = lax.dot_general(ub, wg, _NT, preferred_element_type=F32)

    w = wt_ref[...].astype(BF16)
    if emit_bf16:
        rest[0][...] = w
    p_ref[...] = lax.dot_general(u_scr[...], w, _NT, preferred_element_type=F32).astype(p_ref.dtype)


def _proj(x, gamma, wt, wgt, *, tm, tn, out_dtype):
    m = x.shape[0]
    emit = wt.dtype != BF16
    gate_spec = pl.BlockSpec((GATE_LANES, D), (lambda i, j: (PW // GATE_LANES, 0)) if emit
                             else (lambda i, j: (0, 0)))
    out_specs = [pl.BlockSpec((tm, tn), lambda i, j: (i, j)),
                 pl.BlockSpec((tm, GATE_LANES), lambda i, j: (i, 0))]
    out_shape = [jax.ShapeDtypeStruct((m, PW), out_dtype),
                 jax.ShapeDtypeStruct((m, GATE_LANES), F32)]
    if emit:
        out_specs += [pl.BlockSpec((tn, D), lambda i, j: (j, 0)),
                      pl.BlockSpec((GATE_LANES, D), lambda i, j: (0, 0))]
        out_shape += [jax.ShapeDtypeStruct((PW, D), BF16),
                      jax.ShapeDtypeStruct((GATE_LANES, D), BF16)]
    return pl.pallas_call(
        functools.partial(_proj_kernel, emit_bf16=emit),
        grid=(m // tm, PW // tn),
        in_specs=[pl.BlockSpec((tm, D), lambda i, j: (i, 0)),
                  pl.BlockSpec((1, D), lambda i, j: (0, 0)),
                  pl.BlockSpec((tn, D), lambda i, j: (j, 0)),
                  gate_spec],
        out_specs=out_specs,
        out_shape=out_shape,
        scratch_shapes=[pltpu.VMEM((tm, D), BF16)],
        compiler_params=_params(2),
        name="proj_cast" if emit else "proj",
    )(x, gamma, wt, wgt)


def _head_output(num, den, m_t, gout, o):
    hv = num * (1.0 / jnp.maximum(jnp.abs(den), jnp.exp(-m_t)))
    hv = hv * lax.rsqrt(jnp.mean(hv * hv, axis=1, keepdims=True) + EPS)
    return hv * gout * jax.nn.sigmoid(o.astype(F32))


def _scan_kernel(q_ref, k_ref, v_ref, o_ref, gt_ref, bias_ref, gout_ref, c0_ref, n0_ref, m0_ref,
                 hg_ref, cout_ref, nout_ref, mout_ref, c_scr, n_scr, m_scr, *, L, lead_pad, nc):
    c = pl.program_id(1)

    @pl.when(c == 0)
    def _():
        c_scr[...] = c0_ref[...]
        n_scr[...] = n0_ref[...]
        m_scr[...] = m0_ref[...]

    gates = gt_ref[...] + bias_ref[...]
    gates_t = gates.T
    row = lax.broadcasted_iota(jnp.int32, (L, L), 0)
    col = lax.broadcasted_iota(jnp.int32, (L, L), 1)
    causal = row >= col
    causal_t = row <= col
    if lead_pad:
        live_col = lax.broadcasted_iota(jnp.int32, (L, 1), 0) >= lead_pad
        live_row = lax.broadcasted_iota(jnp.int32, (1, L), 1) >= lead_pad

    for hh in range(H):
        ig_col = gates[:, hh:hh + 1]
        ig_row = gates_t[hh:hh + 1, :]
        lf_col = _log_sigmoid(gates[:, H + hh:H + hh + 1])
        lf_row = _log_sigmoid(gates_t[H + hh:H + hh + 1, :])
        if lead_pad:
            ig_col = jnp.where(live_col, ig_col, -jnp.inf)
            ig_row = jnp.where(live_row, ig_row, -jnp.inf)
            lf_col = jnp.where(live_col, lf_col, 0.0)
            lf_row = jnp.where(live_row, lf_row, 0.0)
        b_col = jnp.sum(jnp.where(causal, lf_row, 0.0), axis=1, keepdims=True)
        b_row = jnp.sum(jnp.where(causal_t, lf_col, 0.0), axis=0, keepdims=True)
        m_prev = m_scr[hh, 0:1, 0:1]
        d = jnp.where(causal, b_col - b_row + ig_row, -jnp.inf)
        inter = b_col + m_prev
        m_t = jnp.maximum(inter, jnp.max(d, axis=1, keepdims=True))
        w_inter = jnp.exp(inter - m_t) * SCALE

        q = q_ref[:, hh * DK:(hh + 1) * DK]
        k = k_ref[:, hh * DK:(hh + 1) * DK]
        v = v_ref[:, hh * DV:(hh + 1) * DV]
        qk = lax.dot_general(q, k, _NT, preferred_element_type=F32)
        s = qk * (jnp.exp(d - m_t) * SCALE)
        cmat = c_scr[hh]
        nvec = n_scr[hh, 0:1, :]
        num = w_inter * jnp.dot(q, cmat.astype(BF16), preferred_element_type=F32) \
            + jnp.dot(s.astype(BF16), v, preferred_element_type=F32)
        den = w_inter * jnp.sum(q.astype(F32) * nvec, axis=1, keepdims=True) \
            + jnp.sum(s, axis=1, keepdims=True)
        hout = _head_output(num, den, m_t, gout_ref[:, hh * DV:(hh + 1) * DV],
                            o_ref[:, hh * DV:(hh + 1) * DV])
        hg_ref[:, hh * DV:(hh + 1) * DV] = hout.astype(hg_ref.dtype)

        m_new = m_t[L - 1:L, :]
        b_last = b_col[L - 1:L, :]
        decay = jnp.exp(b_last + m_prev - m_new)
        wk = jnp.exp(b_last - b_col + ig_col - m_new) * k.astype(F32)
        c_scr[hh] = decay * cmat + jnp.dot(wk.T.astype(BF16), v, preferred_element_type=F32)
        n_scr[hh] = jnp.broadcast_to(decay * nvec + jnp.sum(wk, axis=0, keepdims=True), (8, DK))
        m_scr[hh] = jnp.broadcast_to(m_new, (8, GATE_LANES))

    @pl.when(c == nc - 1)
    def _():
        cout_ref[...] = c_scr[...]
        nout_ref[...] = n_scr[...]
        mout_ref[...] = m_scr[...]


def _scan(p, gates, bias, gout, c0, n0, m0, *, L, lead_pad=0):
    b, s, _ = p.shape
    nc = s // L
    kern = functools.partial(_scan_kernel, L=L, lead_pad=lead_pad, nc=nc)
    return pl.pallas_call(
        kern,
        grid=(b, nc),
        in_specs=[pl.BlockSpec((None, L, QKW), lambda i, c: (i, c, 0)),
                  pl.BlockSpec((None, L, QKW), lambda i, c: (i, c, 1)),
                  pl.BlockSpec((None, L, VW), lambda i, c: (i, c, 1)),
                  pl.BlockSpec((None, L, VW), lambda i, c: (i, c, 2)),
                  pl.BlockSpec((None, L, GATE_LANES), lambda i, c: (i, c, 0)),
                  pl.BlockSpec((1, GATE_LANES), lambda i, c: (0, 0)),
                  pl.BlockSpec((1, VW), lambda i, c: (0, 0)),
                  pl.BlockSpec((H, DK, DV), lambda i, c: (0, 0, 0)),
                  pl.BlockSpec((H, 8, DK), lambda i, c: (0, 0, 0)),
                  pl.BlockSpec((H, 8, GATE_LANES), lambda i, c: (0, 0, 0))],
        out_specs=[pl.BlockSpec((None, L, VW), lambda i, c: (i, c, 0)),
                   pl.BlockSpec((None, H, DK, DV), lambda i, c: (i, 0, 0, 0)),
                   pl.BlockSpec((None, H, 8, DK), lambda i, c: (i, 0, 0, 0)),
                   pl.BlockSpec((None, H, 8, GATE_LANES), lambda i, c: (i, 0, 0, 0))],
        out_shape=[jax.ShapeDtypeStruct((b, s, VW), BF16),
                   jax.ShapeDtypeStruct((b, H, DK, DV), F32),
                   jax.ShapeDtypeStruct((b, H, 8, DK), F32),
                   jax.ShapeDtypeStruct((b, H, 8, GATE_LANES), F32)],
        scratch_shapes=[pltpu.VMEM((H, DK, DV), F32),
                        pltpu.VMEM((H, 8, DK), F32),
                        pltpu.VMEM((H, 8, GATE_LANES), F32)],
        compiler_params=_params(2),
        name="scan",
    )(p, p, p, p, gates, bias, gout, c0, n0, m0)


def _scan_s_kernel(q_ref, k_ref, v_ref, o_ref, gt_ref, bias_ref, gout_ref, mtok_ref, c_ref, n_ref,
                   hg_ref, cout_ref, nout_ref, mout_ref, qc_scr, ntok_scr, *, T, NB):
    hh = pl.program_id(1) % H
    LT = NB * T
    gates = gt_ref[...] + bias_ref[...]
    gates_t = gates.T
    lane = lax.broadcasted_iota(jnp.int32, (LT, GATE_LANES), 1)
    sub = lax.broadcasted_iota(jnp.int32, (GATE_LANES, LT), 0)

    def pick_col(a, idx):
        return jnp.sum(jnp.where(lane == idx, a, 0.0), axis=1, keepdims=True)

    def pick_row(a, idx):
        return jnp.sum(jnp.where(sub == idx, a, 0.0), axis=0, keepdims=True)

    ig_col = pick_col(gates, hh)
    ig_row = pick_row(gates_t, hh)
    lf_col = _log_sigmoid(pick_col(gates, hh + H))
    lf_row = _log_sigmoid(pick_row(gates_t, hh + H))
    m_prev = pick_col(mtok_ref[...], hh)

    row = lax.broadcasted_iota(jnp.int32, (LT, LT), 0)
    col = lax.broadcasted_iota(jnp.int32, (LT, LT), 1)
    same = (row // T) == (col // T)
    causal = jnp.logical_and(same, row >= col)
    causal_t = jnp.logical_and(same, row <= col)
    b_col = jnp.sum(jnp.where(causal, lf_row, 0.0), axis=1, keepdims=True)
    b_row = jnp.sum(jnp.where(causal_t, lf_col, 0.0), axis=0, keepdims=True)
    b_end = jnp.sum(jnp.where(same, lf_row, 0.0), axis=1, keepdims=True)
    d = jnp.where(causal, b_col - b_row + ig_row, -jnp.inf)
    inter = b_col + m_prev
    m_t = jnp.maximum(inter, jnp.max(d, axis=1, keepdims=True))
    d_end = jnp.where(same, b_end - b_row + ig_row, -jnp.inf)
    m_new = jnp.maximum(b_end + m_prev, jnp.max(d_end, axis=1, keepdims=True))
    w_inter = jnp.exp(inter - m_t) * SCALE

    q32 = q_ref[...]
    k32 = k_ref[...]
    q = q32.astype(BF16)
    v = v_ref[...].astype(BF16)
    qk = lax.dot_general(q, k32.astype(BF16), (((1,), (1,)), ((), ())), preferred_element_type=F32)
    s = qk * (jnp.exp(d - m_t) * SCALE)
    num_intra = jnp.dot(s.astype(BF16), v, preferred_element_type=F32)
    den_intra = jnp.sum(s, axis=1, keepdims=True)

    decay = jnp.exp(b_end + m_prev - m_new)
    wk = jnp.exp(b_end - b_col + ig_col - m_new) * k32
    wk_t = wk.T
    col_seq = lax.broadcasted_iota(jnp.int32, (DK, LT), 1) // T

    for bb in range(NB):
        r0 = bb * T
        cmat = c_ref[bb]
        nvec = n_ref[bb:bb + 1, :]
        qc_scr[r0:r0 + T, :] = jnp.dot(q32[r0:r0 + T, :].astype(BF16), cmat.astype(BF16),
                                       preferred_element_type=F32)
        ntok_scr[r0:r0 + T, :] = jnp.broadcast_to(nvec, (T, DK))
        upd = jnp.dot(jnp.where(col_seq == bb, wk_t, 0.0).astype(BF16), v, preferred_element_type=F32)
        dec = decay[r0:r0 + 1, :]
        cout_ref[bb] = dec * cmat + upd
        nout_ref[bb:bb + 1, :] = dec * nvec + jnp.sum(wk[r0:r0 + T, :], axis=0, keepdims=True)
        mout_ref[bb:bb + 1, :] = jnp.broadcast_to(m_new[r0:r0 + 1, :], (1, GATE_LANES))

    num = w_inter * qc_scr[...] + num_intra
    den = w_inter * jnp.sum(q32 * ntok_scr[...], axis=1, keepdims=True) + den_intra
    hout = _head_output(num, den, m_t, gout_ref[...], o_ref[...]).astype(hg_ref.dtype)
    src_row = (row % NB) * T + row // NB
    perm = jnp.where(col == src_row, 1.0, 0.0).astype(hg_ref.dtype)
    moved = jnp.dot(perm, hout, preferred_element_type=F32)
    hg_ref[...] = moved.reshape(T, NB, DV).astype(hg_ref.dtype)


def _proj_scan_s_kernel(x_ref, g_ref, wt_ref, wgt_ref,
                        q_ref, k_ref, v_ref, o_ref, gt_ref, bias_ref, gout_ref, mtok_ref, c_ref, n_ref,
                        p_ref, gate_ref, hg_ref, cout_ref, nout_ref, mout_ref,
                        u_scr, qc_scr, ntok_scr, *, T, NB):
    _proj_kernel(x_ref, g_ref, wt_ref, wgt_ref, p_ref, gate_ref, u_scr, emit_bf16=False)
    _scan_s_kernel(q_ref, k_ref, v_ref, o_ref, gt_ref, bias_ref, gout_ref, mtok_ref, c_ref, n_ref,
                   hg_ref, cout_ref, nout_ref, mout_ref, qc_scr, ntok_scr, T=T, NB=NB)


def _proj_scan_s(x, gamma, wt, wgt, p_s, gates_s, bias, gout, mtok, c, n_hm, *, tm, T):
    m = x.shape[0]
    nseq = c.shape[0]
    nt = m // tm
    nj = 2 * H
    tn = PW // nj
    gpt = nj // H
    nb = nseq // (nt * gpt)
    assert nb * nt * gpt == nseq and nb % 8 == 0 and tn % LANES == 0
    lt = nb * T
    kern = functools.partial(_proj_scan_s_kernel, T=T, NB=nb)

    def grp(i, j):
        return i * gpt + j // H

    def head(j):
        return j % H

    return pl.pallas_call(
        kern,
        grid=(nt, nj),
        in_specs=[pl.BlockSpec((tm, D), lambda i, j: (i, 0)),
                  pl.BlockSpec((1, D), lambda i, j: (0, 0)),
                  pl.BlockSpec((tn, D), lambda i, j: (j, 0)),
                  pl.BlockSpec((GATE_LANES, D), lambda i, j: (0, 0)),
                  pl.BlockSpec((lt, DK), lambda i, j: (grp(i, j), head(j))),
                  pl.BlockSpec((lt, DK), lambda i, j: (grp(i, j), H + head(j))),
                  pl.BlockSpec((lt, DV), lambda i, j: (grp(i, j), H + head(j))),
                  pl.BlockSpec((lt, DV), lambda i, j: (grp(i, j), 2 * H + head(j))),
                  pl.BlockSpec((lt, GATE_LANES), lambda i, j: (grp(i, j), 0)),
                  pl.BlockSpec((1, GATE_LANES), lambda i, j: (0, 0)),
                  pl.BlockSpec((1, DV), lambda i, j: (0, head(j))),
                  pl.BlockSpec((lt, GATE_LANES), lambda i, j: (grp(i, j), 0)),
                  pl.BlockSpec((nb, None, DK, DV), lambda i, j: (grp(i, j), head(j), 0, 0)),
                  pl.BlockSpec((None, nb, DK), lambda i, j: (head(j), grp(i, j), 0))],
        out_specs=[pl.BlockSpec((tm, tn), lambda i, j: (i, j)),
                   pl.BlockSpec((tm, GATE_LANES), lambda i, j: (i, 0)),
                   pl.BlockSpec((T, nb, DV), lambda i, j: (0, grp(i, j), head(j))),
                   pl.BlockSpec((nb, None, DK, DV), lambda i, j: (grp(i, j), head(j), 0, 0)),
                   pl.BlockSpec((None, nb, DK), lambda i, j: (head(j), grp(i, j), 0)),
                   pl.BlockSpec((None, nb, GATE_LANES), lambda i, j: (head(j), grp(i, j), 0))],
        out_shape=[jax.ShapeDtypeStruct((m, PW), BF16),
                   jax.ShapeDtypeStruct((m, GATE_LANES), F32),
                   jax.ShapeDtypeStruct((T, nseq, VW), BF16),
                   jax.ShapeDtypeStruct((nseq, H, DK, DV), F32),
                   jax.ShapeDtypeStruct((H, nseq, DK), F32),
                   jax.ShapeDtypeStruct((H, nseq, GATE_LANES), F32)],
        scratch_shapes=[pltpu.VMEM((tm, D), BF16),
                        pltpu.VMEM((lt, DV), F32),
                        pltpu.VMEM((lt, DK), F32)],
        compiler_params=_params(2),
        name="proj_scan_sample",
    )(x, gamma, wt, wgt, p_s, p_s, p_s, p_s, gates_s, bias, gout, mtok, c, n_hm)


def _mmres_cast_kernel(a_ref, w_ref, x_ref, o_ref, wc_ref):
    w = w_ref[...].astype(BF16)
    wc_ref[...] = w
    y = jnp.dot(a_ref[...], w, preferred_element_type=F32)
    nseq, t_dec, _ = x_ref.shape
    for t in range(t_dec):
        o_ref[t * nseq:(t + 1) * nseq, :] = x_ref[:, t, :] + y[t * nseq:(t + 1) * nseq, :]


def _mmres_cast(a, w, x, *, tn):
    m, kdim = a.shape
    n = w.shape[2]
    nseq, t_dec, _ = x.shape
    return pl.pallas_call(
        _mmres_cast_kernel,
        grid=(n // tn,),
        in_specs=[pl.BlockSpec((m, kdim), lambda j: (0, 0)),
                  pl.BlockSpec((None, kdim, tn), lambda j: (0, 0, j)),
                  pl.BlockSpec((nseq, t_dec, tn), lambda j: (0, 0, j))],
        out_specs=[pl.BlockSpec((m, tn), lambda j: (0, j)),
                   pl.BlockSpec((kdim, tn), lambda j: (0, j))],
        out_shape=[jax.ShapeDtypeStruct((m, n), F32),
                   jax.ShapeDtypeStruct((kdim, n), BF16)],
        compiler_params=_params(1),
        name="out_proj_cast",
    )(a, w, x)


def _mmres_kernel(a_ref, w_ref, x_ref, o_ref):
    o_ref[...] = x_ref[...] + jnp.dot(a_ref[...], w_ref[...], preferred_element_type=F32)


def _mmres(a, w, x, *, tm):
    m, kdim = a.shape
    n = w.shape[1]
    return pl.pallas_call(
        _mmres_kernel,
        grid=(m // tm,),
        in_specs=[pl.BlockSpec((tm, kdim), lambda i: (i, 0)),
                  pl.BlockSpec((kdim, n), lambda i: (0, 0)),
                  pl.BlockSpec((tm, n), lambda i: (i, 0))],
        out_specs=pl.BlockSpec((tm, n), lambda i: (i, 0)),
        out_shape=jax.ShapeDtypeStruct((m, n), F32),
        compiler_params=_params(1),
        name="out_proj",
    )(a, w, x)


def _ffn_kernel(x_ref, gam_ref, wg_ref, wa0_ref, wa1_ref, cw_ref, cb_ref, wd_ref, st_ref, gfin_ref,
                o_ref, so_ref, wupc_ref, wdnc_ref,
                u_scr, gext_scr, a_scr, wup0_scr, wup1_scr, wdn0_scr, wdn1_scr,
                *, tm, shift, nsub, final_norm):
    s = pl.program_id(1)
    sr = CONV_TAIL * shift
    wup_slots = (wup0_scr, wup1_scr)
    wdn_slots = (wdn0_scr, wdn1_scr)

    def cast_steps(slot):
        wup_scr = wup_slots[slot]
        wdn_scr = wdn_slots[slot]
        valid = D_FF - jnp.minimum(s, NF - 1) * TF

        def cast_gate():
            ok = lax.broadcasted_iota(jnp.int32, (1, TF), 1) < valid
            w = jnp.where(ok, wg_ref[...], 0.0).astype(BF16)
            wup_scr[:, 0:TF] = w
            wupc_ref[:, 0:TF] = w

        def cast_value():
            lane = lax.broadcasted_iota(jnp.int32, (1, LANES), 1)
            w0 = jnp.where(lane < valid, wa0_ref[...], 0.0).astype(BF16)
            w1 = jnp.where(lane + LANES < valid, wa1_ref[...], 0.0).astype(BF16)
            wup_scr[:, TF:TF + LANES] = w0
            wup_scr[:, TF + LANES:2 * TF] = w1
            wupc_ref[:, TF:TF + LANES] = w0
            wupc_ref[:, TF + LANES:2 * TF] = w1

        def cast_down(r0, rows):
            row_ok = r0 + lax.broadcasted_iota(jnp.int32, (rows, 1), 0) < valid
            w = jnp.where(row_ok, wd_ref[r0:r0 + rows, :], 0.0).astype(BF16)
            wdn_scr[r0:r0 + rows, :] = w
            wdnc_ref[r0:r0 + rows, :] = w

        half = TF // 2
        return [cast_gate, cast_value, functools.partial(cast_down, 0, half),
                functools.partial(cast_down, half, half)]

    def run_tile(slot, fillers=()):
        fillers = list(fillers)

        def fill():
            if fillers:
                fillers.pop(0)()

        wup_scr = wup_slots[slot]
        wdn_scr = wdn_slots[slot]
        f = s - 1
        col_ok = lax.broadcasted_iota(jnp.int32, (1, TF), 1) < D_FF - f * TF
        cw = cw_ref[...]
        cb = cb_ref[...]
        for t in range(CONV_TAIL):
            gext_scr[t * shift:(t + 1) * shift, :] = st_ref[:, t, :]
        ts = tm // nsub
        for h in range(nsub):
            r0 = h * ts
            ga = jnp.dot(u_scr[r0:r0 + ts, :], wup_scr[...], preferred_element_type=F32)
            gext_scr[sr + r0:sr + r0 + ts, :] = ga[:, 0:TF]
            a_scr[r0:r0 + ts, :] = ga[:, TF:2 * TF]
            fill()
        for h in range(nsub):
            r0 = h * ts
            g = gext_scr[sr + r0:sr + r0 + ts, :]
            g_m2 = gext_scr[sr + r0 - 2 * shift:sr + r0 - 2 * shift + ts, :]
            g_m1 = gext_scr[sr + r0 - shift:sr + r0 - shift + ts, :]
            gc = cb + ((cw[0:1, :] * g_m2 + cw[1:2, :] * g_m1) + cw[2:3, :] * g)
            hmid = jnp.where(col_ok, (gc * jax.nn.sigmoid(gc)) * a_scr[r0:r0 + ts, :], 0.0)
            o_ref[r0:r0 + ts, :] += jnp.dot(hmid.astype(BF16), wdn_scr[...], preferred_element_type=F32)
            fill()
        while fillers:
            fill()
        for t in range(CONV_TAIL):
            so_ref[:, t, :] = gext_scr[tm + t * shift:tm + (t + 1) * shift, :]

    @pl.when(s == 0)
    def _():
        x = x_ref[...]
        u_scr[...] = _rms(x, gam_ref[...]).astype(BF16)
        o_ref[...] = x
        for step in cast_steps(0):
            step()

    for parity in range(2):
        @pl.when(jnp.logical_and(s > 0, s % 2 == parity))
        def _(parity=parity):
            run_tile(1 - parity, cast_steps(parity))

    if final_norm:
        @pl.when(s == NF)
        def _():
            o_ref[...] = _rms(o_ref[...], gfin_ref[...])


def _ffn(x, layer, gamma, w_up, conv_w, conv_b, w_down, st, gfin, *, final_norm):
    assert TF == 2 * LANES
    tm = x.shape[0]
    shift = st.shape[0]
    sr = CONV_TAIL * shift
    kern = functools.partial(_ffn_kernel, tm=tm, shift=shift, nsub=2, final_norm=final_norm)
    last_a = 2 * FF_LANE_BLOCKS - 1

    def wt(s):
        return jnp.minimum(s, NF - 1)

    def ft(s):
        return jnp.maximum(s - 1, 0)

    st_spec = pl.BlockSpec((shift, CONV_TAIL, TF), lambda i, s: (0, 0, ft(s)))
    return pl.pallas_call(
        kern,
        grid=(1, NF + 1),
        in_specs=[pl.BlockSpec((tm, D), lambda i, s: (i, 0), pipeline_mode=pl.Buffered(1)),
                  pl.BlockSpec((1, D), lambda i, s: (0, 0)),
                  pl.BlockSpec((None, D, TF), lambda i, s: (layer, 0, wt(s))),
                  pl.BlockSpec((None, D, LANES), lambda i, s: (layer, 0, FF_LANE_BLOCKS + 2 * wt(s))),
                  pl.BlockSpec((None, D, LANES),
                               lambda i, s: (layer, 0, jnp.minimum(FF_LANE_BLOCKS + 2 * wt(s) + 1, last_a))),
                  pl.BlockSpec((None, 3, TF), lambda i, s: (layer, 0, ft(s))),
                  pl.BlockSpec((None, 1, TF), lambda i, s: (layer, 0, ft(s))),
                  pl.BlockSpec((None, TF, D), lambda i, s: (layer, wt(s), 0)),
                  st_spec,
                  pl.BlockSpec((1, D), lambda i, s: (0, 0))],
        out_specs=[pl.BlockSpec((tm, D), lambda i, s: (i, 0)),
                   st_spec,
                   pl.BlockSpec((D, 2 * TF), lambda i, s: (0, wt(s))),
                   pl.BlockSpec((TF, D), lambda i, s: (wt(s), 0))],
        out_shape=[jax.ShapeDtypeStruct((tm, D), F32),
                   jax.ShapeDtypeStruct(st.shape, F32),
                   jax.ShapeDtypeStruct((D, NF * 2 * TF), BF16),
                   jax.ShapeDtypeStruct((NF * TF, D), BF16)],
        scratch_shapes=[pltpu.VMEM((tm, D), BF16),
                        pltpu.VMEM((sr + tm, TF), F32),
                        pltpu.VMEM((tm, TF), F32),
                        pltpu.VMEM((D, 2 * TF), BF16),
                        pltpu.VMEM((D, 2 * TF), BF16),
                        pltpu.VMEM((TF, D), BF16),
                        pltpu.VMEM((TF, D), BF16)],
        compiler_params=_params(2),
        name="conv_ffn_cast",
    )(x, gamma, w_up, w_up, w_up, conv_w, conv_b, w_down, st, gfin)


def _ffn_bf16_kernel(x_ref, gam_ref, wup_ref, cw_ref, cb_ref, wdn_ref, st_ref, gfin_ref,
                     o_ref, so_ref, u_scr, gext_scr, a_scr, carry_scr,
                     *, tm, sr, shift, tps, nsub, final_norm):
    i = pl.program_id(0)
    f = pl.program_id(1)
    tf2 = 2 * TF

    @pl.when(f == 0)
    def _():
        x = x_ref[...]
        u_scr[...] = _rms(x, gam_ref[...]).astype(BF16)
        o_ref[...] = x
        if tps > 1:
            @pl.when(i == 0)
            def _():
                carry_scr[...] = jnp.zeros_like(carry_scr)

    col_ok = lax.broadcasted_iota(jnp.int32, (1, tf2), 1) < D_FF - f * tf2
    cw = cw_ref[...]
    cb = cb_ref[...]
    if tps == 1:
        gext_scr[0:sr, :] = st_ref[...]
    else:
        gext_scr[0:sr, :] = jnp.where((i % tps) == 0, st_ref[...], carry_scr[f])
    ts = tm // nsub
    for h in range(nsub):
        r0 = h * ts
        ga = jnp.dot(u_scr[r0:r0 + ts, :], wup_ref[...], preferred_element_type=F32)
        gext_scr[sr + r0:sr + r0 + ts, 0:TF] = ga[:, 0:TF]
        gext_scr[sr + r0:sr + r0 + ts, TF:tf2] = ga[:, 2 * TF:3 * TF]
        a_scr[r0:r0 + ts, 0:TF] = ga[:, TF:2 * TF]
        a_scr[r0:r0 + ts, TF:tf2] = ga[:, 3 * TF:4 * TF]
    for h in range(nsub):
        r0 = h * ts
        g = gext_scr[sr + r0:sr + r0 + ts, :]
        g_m2 = gext_scr[sr + r0 - 2 * shift:sr + r0 - 2 * shift + ts, :]
        g_m1 = gext_scr[sr + r0 - shift:sr + r0 - shift + ts, :]
        gc = cb + ((cw[0:1, :] * g_m2 + cw[1:2, :] * g_m1) + cw[2:3, :] * g)
        hmid = jnp.where(col_ok, (gc * jax.nn.sigmoid(gc)) * a_scr[r0:r0 + ts, :], 0.0)
        o_ref[r0:r0 + ts, :] += jnp.dot(hmid.astype(BF16), wdn_ref[...], preferred_element_type=F32)
    g_tail = gext_scr[tm:tm + sr, :]
    so_ref[...] = g_tail
    if tps > 1:
        carry_scr[f] = g_tail

    if final_norm:
        @pl.when(f == pl.num_programs(1) - 1)
        def _():
            o_ref[...] = _rms(o_ref[...], gfin_ref[...])


def _ffn_bf16(x, layer, gamma, wupc, conv_w, conv_b, wdnc, st, gfin, *, tm, sr, shift, tps, st_per_tile,
              final_norm):
    assert NF % 2 == 0
    m = x.shape[0]
    nt = m // tm
    tf2 = 2 * TF
    nf2 = NF // 2
    kern = functools.partial(_ffn_bf16_kernel, tm=tm, sr=sr, shift=shift, tps=tps,
                             nsub=2 if tm >= 64 else 1, final_norm=final_norm)
    st_map = (lambda i, f: (i, f)) if st_per_tile else (lambda i, f: (0, f))
    return pl.pallas_call(
        kern,
        grid=(nt, nf2),
        in_specs=[pl.BlockSpec((tm, D), lambda i, f: (i, 0)),
                  pl.BlockSpec((1, D), lambda i, f: (0, 0)),
                  pl.BlockSpec((D, 2 * tf2), lambda i, f: (0, f)),
                  pl.BlockSpec((None, 3, tf2), lambda i, f: (layer, 0, f)),
                  pl.BlockSpec((None, 1, tf2), lambda i, f: (layer, 0, f)),
                  pl.BlockSpec((tf2, D), lambda i, f: (f, 0)),
                  pl.BlockSpec((sr, tf2), st_map),
                  pl.BlockSpec((1, D), lambda i, f: (0, 0))],
        out_specs=[pl.BlockSpec((tm, D), lambda i, f: (i, 0)),
                   pl.BlockSpec((sr, tf2), lambda i, f: (i, f))],
        out_shape=[jax.ShapeDtypeStruct((m, D), F32),
                   jax.ShapeDtypeStruct((nt * sr, D_FF), F32)],
        scratch_shapes=[pltpu.VMEM((tm, D), BF16),
                        pltpu.VMEM((sr + tm, tf2), F32),
                        pltpu.VMEM((tm, tf2), F32),
                        pltpu.VMEM((nf2, sr, tf2), F32)],
        compiler_params=_params(2),
        name="conv_ffn",
    )(x, gamma, wupc, conv_w, conv_b, wdnc, st, gfin)


def _window_sum(ext_scr, tmp_scrs, base, tm, shift, w):
    starts = {w: base}
    v = w
    while v > 2:
        starts[v // 2] = (starts[v] - (v // 2) * shift) // 8 * 8
        v //= 2
    src = ext_scr
    v = 1
    k = 0
    while True:
        lo = starts[2 * v]
        n = base + tm - lo
        val = src[lo:lo + n, :] + src[lo - v * shift:lo - v * shift + n, :]
        v *= 2
        if v == w:
            return val
        dst = tmp_scrs[k % 2]
        dst[lo:lo + n, :] = val
        src = dst
        k += 1


def _pool_kernel(x_ref, gam_ref, wp_ref, sc_ref, st_ref, o_ref, ut_ref, *scratch, tm, shift, tps, pos0, tr):
    ng = len(POOL_WINDOWS)
    uext_scrs = scratch[0:ng]
    tmp_scrs = scratch[ng:ng + 2]
    carry_scr = scratch[ng + 2]
    i = pl.program_id(0)
    hr = POOL_HALO * shift
    base = POOL_PAD_ROWS + hr

    if tps > 1:
        @pl.when(i == 0)
        def _():
            carry_scr[...] = jnp.zeros_like(carry_scr)

    x = x_ref[...]
    rinv = lax.rsqrt(jnp.mean(x * x, axis=1, keepdims=True) + EPS)
    for kk, w in enumerate(POOL_WINDOWS):
        cs = slice(kk * PG, (kk + 1) * PG)
        uext_scr = uext_scrs[kk]
        xg = x_ref[:, cs]
        ug = xg * rinv * gam_ref[:, cs]
        uext_scr[0:POOL_PAD_ROWS, :] = jnp.zeros((POOL_PAD_ROWS, PG), F32)
        if tps == 1:
            uext_scr[POOL_PAD_ROWS:base, :] = st_ref[:, cs]
        else:
            uext_scr[POOL_PAD_ROWS:base, :] = jnp.where((i % tps) == 0, st_ref[:, cs], carry_scr[:, cs])
        uext_scr[base:base + tm, :] = ug
        acc = _window_sum(uext_scr, tmp_scrs, base, tm, shift, w)
        if pos0 + 1 >= w:
            pooled = acc / float(w) - ug
        else:
            step = (i % tps) * (tm // shift) + lax.broadcasted_iota(jnp.int32, (tm, 1), 0) // shift
            cnt = jnp.minimum(w, pos0 + step + 1).astype(F32)
            pooled = acc / cnt - ug
        y = jnp.dot(pooled.astype(BF16), wp_ref[kk].astype(BF16), preferred_element_type=F32)
        o_ref[:, cs] = xg + y * sc_ref[:, cs]
        ut_ref[:, cs] = ug[tm - tr:tm, :]
        if tps > 1:
            carry_scr[:, cs] = ug[tm - hr:tm, :]


def _pool(x, gamma, wp, sc, st, *, tm, shift, tps, pos0, tr):
    m = x.shape[0]
    nt = m // tm
    hr = POOL_HALO * shift
    ng = len(POOL_WINDOWS)
    kern = functools.partial(_pool_kernel, tm=tm, shift=shift, tps=tps, pos0=pos0, tr=tr)
    return pl.pallas_call(
        kern,
        grid=(nt,),
        in_specs=[pl.BlockSpec((tm, D), lambda i: (i, 0)),
                  pl.BlockSpec((1, D), lambda i: (0, 0)),
                  pl.BlockSpec((None, ng, PG, PG), lambda i: (0, 0, 0, 0)),
                  pl.BlockSpec((1, D), lambda i: (0, 0)),
                  pl.BlockSpec((hr, D), lambda i: (0, 0))],
        out_specs=[pl.BlockSpec((tm, D), lambda i: (i, 0)),
                   pl.BlockSpec((tr, D), lambda i: (i, 0))],
        out_shape=[jax.ShapeDtypeStruct((m, D), F32),
                   jax.ShapeDtypeStruct((nt * tr, D), F32)],
        scratch_shapes=[pltpu.VMEM((POOL_PAD_ROWS + hr + tm, PG), F32)] * (ng + 2)
        + [pltpu.VMEM((hr, D), F32)],
        compiler_params=_params(1),
        name="pool_mixer",
    )(x, gamma, wp, sc, st)


def _pool_sample_kernel(x_ref, gam_ref, wp_ref, sc_ref, st_ref, o_ref, new_ref, rinv_scr, uext_scr,
                        *, nseq, t_dec):
    nh = POOL_HALO - 1
    grp = pl.program_id(0)
    tm = nseq * t_dec
    hr = POOL_HALO * nseq

    @pl.when(grp == 0)
    def _():
        x = x_ref[...]
        rinv_scr[...] = lax.rsqrt(jnp.mean(x * x, axis=1, keepdims=True) + EPS)

    for kk, w in enumerate(POOL_WINDOWS):
        @pl.when(grp == kk)
        def _(kk=kk, w=w):
            cs = slice(kk * PG, (kk + 1) * PG)
            xg = x_ref[:, cs]
            ug = xg * rinv_scr[...] * gam_ref[:, cs]
            for t in range(nh):
                uext_scr[(t + 1) * nseq:(t + 2) * nseq, :] = st_ref[t]
            uext_scr[hr:hr + tm, :] = ug
            acc = ug
            for j in range(1, w):
                acc = acc + uext_scr[hr - j * nseq:hr - j * nseq + tm, :]
            pooled = acc / float(w) - ug
            y = jnp.dot(pooled.astype(BF16), wp_ref[kk].astype(BF16), preferred_element_type=F32)
            o_ref[...] = xg + y * sc_ref[:, cs]
            for t in range(nh):
                src = (t + t_dec + 1) * nseq
                new_ref[t] = uext_scr[src:src + nseq, :]


def _pool_sample(x, gamma, wp, sc, state, *, nseq, t_dec):
    assert PAST_LEN + 1 >= max(POOL_WINDOWS)
    nh = POOL_HALO - 1
    ng = len(POOL_WINDOWS)
    tm = nseq * t_dec
    hr = POOL_HALO * nseq
    kern = functools.partial(_pool_sample_kernel, nseq=nseq, t_dec=t_dec)
    hist_spec = pl.BlockSpec((nh, nseq, PG), lambda g: (0, 0, g))
    return pl.pallas_call(
        kern,
        grid=(ng,),
        in_specs=[pl.BlockSpec((tm, D), lambda g: (0, 0)),
                  pl.BlockSpec((1, D), lambda g: (0, 0)),
                  pl.BlockSpec((None, ng, PG, PG), lambda g: (0, 0, 0, 0)),
                  pl.BlockSpec((1, D), lambda g: (0, 0)),
                  hist_spec],
        out_specs=[pl.BlockSpec((tm, PG), lambda g: (0, g)),
                   hist_spec],
        out_shape=[jax.ShapeDtypeStruct((tm, D), F32),
                   jax.ShapeDtypeStruct((nh, nseq, D), F32)],
        scratch_shapes=[pltpu.VMEM((tm, 1), F32),
                        pltpu.VMEM((hr + tm, PG), F32)],
        compiler_params=_params(1),
        name="pool_mixer_sample",
    )(x, gamma, wp, sc, state)


def _pad_cols(a, n):
    return jnp.pad(a, ((0, 0), (0, n - a.shape[1])))


def _from_time_major(a, nseq, t):
    return jnp.swapaxes(a.reshape((t, nseq) + a.shape[1:]), 0, 1)


def kernel(x_prompt, x_sample, state_mlstm_C, state_mlstm_n, state_mlstm_m, state_pool, state_ffn_conv,
           meta_tokens, norm_mix, norm_ffn, norm_final, w_mlstm_in, b_mlstm_gate, g_mlstm_out, w_mlstm_out,
           w_pool, pool_scale, w_up, conv_w, conv_b, w_down):
    bsz, seq, _ = x_prompt.shape
    nseq, t_dec, _ = x_sample.shape

    w_in_t = jnp.swapaxes(w_mlstm_in[0], 0, 1)
    bias = _pad_cols(b_mlstm_gate[0][None, :], GATE_LANES)
    w_out = w_mlstm_out
    gout = g_mlstm_out[0][None, :]
    wp = w_pool
    psc = pool_scale[0][None, :]
    gfin = norm_final[None, :]
    conv_b3 = conv_b[:, None, :]
    ffn_w = [(layer, norm_ffn[layer][None, :], w_up, conv_w, conv_b3, w_down) for layer in range(2)]
    g_mix0 = norm_mix[0][None, :]
    g_mix1 = norm_mix[1][None, :]

    ffn_cache = {}
    mix_cache = {}

    def proj(x, **kw):
        if "w_in" not in mix_cache:
            p, gates, wtc, wgc = _proj(x, g_mix0, w_in_t, w_in_t, **kw)
            mix_cache["w_in"] = (wtc, wgc)
            return p, gates
        return _proj(x, g_mix0, *mix_cache["w_in"], **kw)

    def out_proj(a, x, *, tm):
        if "w_out" not in mix_cache:
            y, mix_cache["w_out"] = _mmres_cast(a, w_out, x, tn=512)
            return y
        return _mmres(a, mix_cache["w_out"], x, tm=tm)

    def ffn_first(x, layer, st, *, final_norm):
        lyr, gamma, wu, cwt, cbs, wd = ffn_w[layer]
        y, cs, wupc, wdnc = _ffn(x, lyr, gamma, wu, cwt, cbs, wd, st, gfin, final_norm=final_norm)
        ffn_cache[layer] = (wupc, wdnc)
        return y, cs

    def ffn(x, layer, st, **kw):
        lyr, gamma, _, cwt, cbs, _ = ffn_w[layer]
        wupc, wdnc = ffn_cache[layer]
        return _ffn_bf16(x, lyr, gamma, wupc, cwt, cbs, wdnc, st, gfin, **kw)

    def long_stream(x, nb, s, st, *, tm, chunk, lead_pad, pos0, projected=None):
        c0, n0, m0, conv0, pool0, conv1 = st
        tm_ffn = tm
        p, gates = projected if projected is not None else proj(x, tm=tm, tn=1024, out_dtype=BF16)
        p = p.reshape(nb, s, PW)
        gates = gates.reshape(nb, s, GATE_LANES)
        if lead_pad:
            p = jnp.pad(p, ((0, 0), (lead_pad, 0), (0, 0)))
            gates = jnp.pad(gates, ((0, 0), (lead_pad, 0), (0, 0)))
        hg, c_new, n_new, m_new = _scan(p, gates, bias, gout, c0, n0, m0, L=chunk, lead_pad=lead_pad)
        hg = hg[:, lead_pad:].reshape(nb * s, VW)
        x1 = out_proj(hg, x, tm=min(tm, 512))
        x2, cs0 = ffn(x1, 0, conv0, tm=tm_ffn, sr=8, shift=1, tps=s // tm_ffn,
                      st_per_tile=False, final_norm=False)
        x3, ut = _pool(x2, g_mix1, wp, psc, pool0, tm=tm, shift=1, tps=s // tm, pos0=pos0,
                       tr=POOL_HALO)
        y, cs1 = ffn(x3, 1, conv1, tm=tm_ffn, sr=8, shift=1, tps=s // tm_ffn,
                     st_per_tile=False, final_norm=True)
        cs0 = cs0.reshape(nb, s // tm_ffn, 8, D_FF)[:, -1]
        cs1 = cs1.reshape(nb, s // tm_ffn, 8, D_FF)[:, -1]
        ut = ut.reshape(nb, s // tm, POOL_HALO, D)[:, -1]
        return y, (c_new, n_new, m_new, cs0, ut, cs1)

    def prompt_streams():
        zero_state = (jnp.zeros((H, DK, DV), F32), jnp.zeros((H, 8, DK), F32), jnp.zeros((H, 8, GATE_LANES), F32),
                      jnp.zeros((8, D_FF), F32), jnp.zeros((POOL_HALO, D), F32), jnp.zeros((8, D_FF), F32))
        _, (c_m, n_m, m_m, cs0_m, ut_m, cs1_m) = long_stream(
            meta_tokens, 1, N_META, zero_state, tm=N_META, chunk=128, lead_pad=128 - N_META, pos0=0)

        y_p, (c_p, n_p, m_p, cs0_p, ut_p, cs1_p) = long_stream(
            x_prompt.reshape(bsz * seq, D), bsz, seq, (c_m[0], n_m[0], m_m[0], cs0_m[0], ut_m[0], cs1_m[0]),
            tm=1024, chunk=SCAN_CHUNK, lead_pad=0, pos0=N_META, projected=prompt_projected)
        return (y_p.reshape(bsz, seq, D), c_p[None], n_p[:, :, 0][None], m_p[:, :, 0, 0][None],
                ut_p[:, 1:][None], jnp.stack([cs0_p[:, 6:], cs1_p[:, 6:]]))

    xs = x_sample.reshape(nseq * t_dec, D)
    p_s, gates_s = proj(xs, tm=nseq * t_dec, tn=512, out_dtype=F32)
    mtok = _pad_cols(jnp.repeat(state_mlstm_m[0], t_dec, axis=0), GATE_LANES)
    n_hm = jnp.swapaxes(state_mlstm_n[0], 0, 1)
    p_p, gates_p, hg_s, C_s, n_s_hm, m_s_hm = _proj_scan_s(
        x_prompt.reshape(bsz * seq, D), g_mix0, *mix_cache["w_in"], p_s, gates_s, bias, gout, mtok,
        state_mlstm_C[0], n_hm, tm=1024, T=t_dec)
    prompt_projected = (p_p, gates_p)
    assert nseq == SAMPLE_TILE_SEQS
    x1 = out_proj(hg_s.reshape(t_dec * nseq, VW), x_sample, tm=512)
    x2, cs0_s = ffn_first(x1, 0, state_ffn_conv[0], final_norm=False)
    x3, pool_new = _pool_sample(x2, g_mix1, wp, psc, jnp.swapaxes(state_pool[0], 0, 1), nseq=nseq, t_dec=t_dec)
    y_s, cs1_s = ffn_first(x3, 1, state_ffn_conv[1], final_norm=True)
    y_sample = _from_time_major(y_s, nseq, t_dec)
    n_s = jnp.swapaxes(n_s_hm, 0, 1)[None]
    m_s = jnp.swapaxes(m_s_hm[:, :, 0], 0, 1)[None]
    pool_s = jnp.swapaxes(pool_new, 0, 1)[None]
    conv_s = jnp.stack([cs0_s, cs1_s])

    y_prompt, C_p, n_p, m_p, pool_p, conv_p = prompt_streams()
    return (y_prompt, y_sample, C_p, n_p, m_p, pool_p, conv_p,
            C_s[None], n_s, m_s, pool_s, conv_s)
```

```python
import functools

import jax
import jax.numpy as jnp
from jax import lax
from jax.experimental import pallas as pl
from jax.experimental.pallas import tpu as pltpu

F32 = jnp.float32
BF16 = jnp.bfloat16

EPS = 1e-6
D = 2048
H = 4
DK = 256
DV = 512
QKW = H * DK
VW = H * DV
PW = 2 * QKW + 2 * VW
GATE_LANES = 128
SCALE = DK ** -0.5
POOL_WINDOWS = (2, 4, 8, 16)
PG = D // len(POOL_WINDOWS)
POOL_HALO = 16
POOL_PAD_ROWS = 16
D_FF = 5504
TF = 256
NF = -(-D_FF // TF)
LANES = 128
FF_LANE_BLOCKS = D_FF // LANES
CONV_TAIL = 2
N_META = 16
PAST_LEN = 16384
SCAN_CHUNK = 512
SAMPLE_TILE_SEQS = 128
VMEM_LIMIT = 60 * 1024 * 1024


def _params(n_axes):
    return pltpu.CompilerParams(dimension_semantics=("arbitrary",) * n_axes,
                                vmem_limit_bytes=VMEM_LIMIT)


def _rms(x, g):
    return x * lax.rsqrt(jnp.mean(x * x, axis=-1, keepdims=True) + EPS) * g


def _log_sigmoid(x):
    return jnp.minimum(x, 0.0) - jnp.log(1.0 + jnp.exp(-jnp.abs(x)))


_NT = (((1,), (1,)), ((), ()))


def _proj_kernel(x_ref, g_ref, wt_ref, wgt_ref, p_ref, gate_ref, *rest, emit_bf16):
    u_scr = rest[-1]

    @pl.when(pl.program_id(1) == 0)
    def _():
        ub = _rms(x_ref[...], g_ref[...]).astype(BF16)
        u_scr[...] = ub
        wg = wgt_ref[...]
        if emit_bf16:
            row_ok = lax.broadcasted_iota(jnp.int32, (GATE_LANES, 1), 0) < 2 * H
            wg = jnp.where(row_ok, wg, 0.0).astype(BF16)
            rest[1][...] = wg
        gate_ref[...] = lax.dot_general(ub, wg, _NT, preferred_element_type=F32)

    w = wt_ref[...].astype(BF16)
    if emit_bf16:
        rest[0][...] = w
    p_ref[...] = lax.dot_general(u_scr[...], w, _NT, preferred_element_type=F32).astype(p_ref.dtype)


def _proj(x, gamma, wt, wgt, *, tm, tn, out_dtype):
    m = x.shape[0]
    emit = wt.dtype != BF16
    gate_spec = pl.BlockSpec((GATE_LANES, D), (lambda i, j: (PW // GATE_LANES, 0)) if emit
                             else (lambda i, j: (0, 0)))
    out_specs = [pl.BlockSpec((tm, tn), lambda i, j: (i, j)),
                 pl.BlockSpec((tm, GATE_LANES), lambda i, j: (i, 0))]
    out_shape = [jax.ShapeDtypeStruct((m, PW), out_dtype),
                 jax.ShapeDtypeStruct((m, GATE_LANES), F32)]
    if emit:
        out_specs += [pl.BlockSpec((tn, D), lambda i, j: (j, 0)),
                      pl.BlockSpec((GATE_LANES, D), lambda i, j: (0, 0))]
        out_shape += [jax.ShapeDtypeStruct((PW, D), BF16),
                      jax.ShapeDtypeStruct((GATE_LANES, D), BF16)]
    return pl.pallas_call(
        functools.partial(_proj_kernel, emit_bf16=emit),
        grid=(m // tm, PW // tn),
        in_specs=[pl.BlockSpec((tm, D), lambda i, j: (i, 0)),
                  pl.BlockSpec((1, D), lambda i, j: (0, 0)),
                  pl.BlockSpec((tn, D), lambda i, j: (j, 0)),
                  gate_spec],
        out_specs=out_specs,
        out_shape=out_shape,
        scratch_shapes=[pltpu.VMEM((tm, D), BF16)],
        compiler_params=_params(2),
        name="proj_cast" if emit else "proj",
    )(x, gamma, wt, wgt)


def _head_output(num, den, m_t, gout, o):
    hv = num * (1.0 / jnp.maximum(jnp.abs(den), jnp.exp(-m_t)))
    hv = hv * lax.rsqrt(jnp.mean(hv * hv, axis=1, keepdims=True) + EPS)
    return hv * gout * jax.nn.sigmoid(o.astype(F32))


def _scan_kernel(q_ref, k_ref, v_ref, o_ref, gt_ref, bias_ref, gout_ref, c0_ref, n0_ref, m0_ref,
                 hg_ref, cout_ref, nout_ref, mout_ref, c_scr, n_scr, m_scr, *, L, lead_pad, nc):
    c = pl.program_id(1)

    @pl.when(c == 0)
    def _():
        c_scr[...] = c0_ref[...]
        n_scr[...] = n0_ref[...]
        m_scr[...] = m0_ref[...]

    gates = gt_ref[...] + bias_ref[...]
    gates_t = gates.T
    row = lax.broadcasted_iota(jnp.int32, (L, L), 0)
    col = lax.broadcasted_iota(jnp.int32, (L, L), 1)
    causal = row >= col
    causal_t = row <= col
    if lead_pad:
        live_col = lax.broadcasted_iota(jnp.int32, (L, 1), 0) >= lead_pad
        live_row = lax.broadcasted_iota(jnp.int32, (1, L), 1) >= lead_pad

    for hh in range(H):
        ig_col = gates[:, hh:hh + 1]
        ig_row = gates_t[hh:hh + 1, :]
        lf_col = _log_sigmoid(gates[:, H + hh:H + hh + 1])
        lf_row = _log_sigmoid(gates_t[H + hh:H + hh + 1, :])
        if lead_pad:
            ig_col = jnp.where(live_col, ig_col, -jnp.inf)
            ig_row = jnp.where(live_row, ig_row, -jnp.inf)
            lf_col = jnp.where(live_col, lf_col, 0.0)
            lf_row = jnp.where(live_row, lf_row, 0.0)
        b_col = jnp.sum(jnp.where(causal, lf_row, 0.0), axis=1, keepdims=True)
        b_row = jnp.sum(jnp.where(causal_t, lf_col, 0.0), axis=0, keepdims=True)
        m_prev = m_scr[hh, 0:1, 0:1]
        d = jnp.where(causal, b_col - b_row + ig_row, -jnp.inf)
        inter = b_col + m_prev
        m_t = jnp.maximum(inter, jnp.max(d, axis=1, keepdims=True))
        w_inter = jnp.exp(inter - m_t) * SCALE

        q = q_ref[:, hh * DK:(hh + 1) * DK]
        k = k_ref[:, hh * DK:(hh + 1) * DK]
        v = v_ref[:, hh * DV:(hh + 1) * DV]
        qk = lax.dot_general(q, k, _NT, preferred_element_type=F32)
        s = qk * (jnp.exp(d - m_t) * SCALE)
        cmat = c_scr[hh]
        nvec = n_scr[hh, 0:1, :]
        num = w_inter * jnp.dot(q, cmat.astype(BF16), preferred_element_type=F32) \
            + jnp.dot(s.astype(BF16), v, preferred_element_type=F32)
        den = w_inter * jnp.sum(q.astype(F32) * nvec, axis=1, keepdims=True) \
            + jnp.sum(s, axis=1, keepdims=True)
        hout = _head_output(num, den, m_t, gout_ref[:, hh * DV:(hh + 1) * DV],
                            o_ref[:, hh * DV:(hh + 1) * DV])
        hg_ref[:, hh * DV:(hh + 1) * DV] = hout.astype(hg_ref.dtype)

        m_new = m_t[L - 1:L, :]
        b_last = b_col[L - 1:L, :]
        decay = jnp.exp(b_last + m_prev - m_new)
        wk = jnp.exp(b_last - b_col + ig_col - m_new) * k.astype(F32)
        c_scr[hh] = decay * cmat + jnp.dot(wk.T.astype(BF16), v, preferred_element_type=F32)
        n_scr[hh] = jnp.broadcast_to(decay * nvec + jnp.sum(wk, axis=0, keepdims=True), (8, DK))
        m_scr[hh] = jnp.broadcast_to(m_new, (8, GATE_LANES))

    @pl.when(c == nc - 1)
    def _():
        cout_ref[...] = c_scr[...]
        nout_ref[...] = n_scr[...]
        mout_ref[...] = m_scr[...]


def _scan(p, gates, bias, gout, c0, n0, m0, *, L, lead_pad=0):
    b, s, _ = p.shape
    nc = s // L
    kern = functools.partial(_scan_kernel, L=L, lead_pad=lead_pad, nc=nc)
    return pl.pallas_call(
        kern,
        grid=(b, nc),
        in_specs=[pl.BlockSpec((None, L, QKW), lambda i, c: (i, c, 0)),
                  pl.BlockSpec((None, L, QKW), lambda i, c: (i, c, 1)),
                  pl.BlockSpec((None, L, VW), lambda i, c: (i, c, 1)),
                  pl.BlockSpec((None, L, VW), lambda i, c: (i, c, 2)),
                  pl.BlockSpec((None, L, GATE_LANES), lambda i, c: (i, c, 0)),
                  pl.BlockSpec((1, GATE_LANES), lambda i, c: (0, 0)),
                  pl.BlockSpec((1, VW), lambda i, c: (0, 0)),
                  pl.BlockSpec((H, DK, DV), lambda i, c: (0, 0, 0)),
                  pl.BlockSpec((H, 8, DK), lambda i, c: (0, 0, 0)),
                  pl.BlockSpec((H, 8, GATE_LANES), lambda i, c: (0, 0, 0))],
        out_specs=[pl.BlockSpec((None, L, VW), lambda i, c: (i, c, 0)),
                   pl.BlockSpec((None, H, DK, DV), lambda i, c: (i, 0, 0, 0)),
                   pl.BlockSpec((None, H, 8, DK), lambda i, c: (i, 0, 0, 0)),
                   pl.BlockSpec((None, H, 8, GATE_LANES), lambda i, c: (i, 0, 0, 0))],
        out_shape=[jax.ShapeDtypeStruct((b, s, VW), BF16),
                   jax.ShapeDtypeStruct((b, H, DK, DV), F32),
                   jax.ShapeDtypeStruct((b, H, 8, DK), F32),
                   jax.ShapeDtypeStruct((b, H, 8, GATE_LANES), F32)],
        scratch_shapes=[pltpu.VMEM((H, DK, DV), F32),
                        pltpu.VMEM((H, 8, DK), F32),
                        pltpu.VMEM((H, 8, GATE_LANES), F32)],
        compiler_params=_params(2),
        name="scan",
    )(p, p, p, p, gates, bias, gout, c0, n0, m0)


def _scan_s_kernel(q_ref, k_ref, v_ref, o_ref, gt_ref, bias_ref, gout_ref, mtok_ref, c_ref, n_ref,
                   hg_ref, cout_ref, nout_ref, mout_ref, qc_scr, ntok_scr, *, T, NB):
    hh = pl.program_id(1) % H
    LT = NB * T
    gates = gt_ref[...] + bias_ref[...]
    gates_t = gates.T
    lane = lax.broadcasted_iota(jnp.int32, (LT, GATE_LANES), 1)
    sub = lax.broadcasted_iota(jnp.int32, (GATE_LANES, LT), 0)

    def pick_col(a, idx):
        return jnp.sum(jnp.where(lane == idx, a, 0.0), axis=1, keepdims=True)

    def pick_row(a, idx):
        return jnp.sum(jnp.where(sub == idx, a, 0.0), axis=0, keepdims=True)

    ig_col = pick_col(gates, hh)
    ig_row = pick_row(gates_t, hh)
    lf_col = _log_sigmoid(pick_col(gates, hh + H))
    lf_row = _log_sigmoid(pick_row(gates_t, hh + H))
    m_prev = pick_col(mtok_ref[...], hh)

    row = lax.broadcasted_iota(jnp.int32, (LT, LT), 0)
    col = lax.broadcasted_iota(jnp.int32, (LT, LT), 1)
    same = (row // T) == (col // T)
    causal = jnp.logical_and(same, row >= col)
    causal_t = jnp.logical_and(same, row <= col)
    b_col = jnp.sum(jnp.where(causal, lf_row, 0.0), axis=1, keepdims=True)
    b_row = jnp.sum(jnp.where(causal_t, lf_col, 0.0), axis=0, keepdims=True)
    b_end = jnp.sum(jnp.where(same, lf_row, 0.0), axis=1, keepdims=True)
    d = jnp.where(causal, b_col - b_row + ig_row, -jnp.inf)
    inter = b_col + m_prev
    m_t = jnp.maximum(inter, jnp.max(d, axis=1, keepdims=True))
    d_end = jnp.where(same, b_end - b_row + ig_row, -jnp.inf)
    m_new = jnp.maximum(b_end + m_prev, jnp.max(d_end, axis=1, keepdims=True))
    w_inter = jnp.exp(inter - m_t) * SCALE

    q32 = q_ref[...]
    k32 = k_ref[...]
    q = q32.astype(BF16)
    v = v_ref[...].astype(BF16)
    qk = lax.dot_general(q, k32.astype(BF16), (((1,), (1,)), ((), ())), preferred_element_type=F32)
    s = qk * (jnp.exp(d - m_t) * SCALE)
    num_intra = jnp.dot(s.astype(BF16), v, preferred_element_type=F32)
    den_intra = jnp.sum(s, axis=1, keepdims=True)

    decay = jnp.exp(b_end + m_prev - m_new)
    wk = jnp.exp(b_end - b_col + ig_col - m_new) * k32
    wk_t = wk.T
    col_seq = lax.broadcasted_iota(jnp.int32, (DK, LT), 1) // T

    for bb in range(NB):
        r0 = bb * T
        cmat = c_ref[bb]
        nvec = n_ref[bb:bb + 1, :]
        qc_scr[r0:r0 + T, :] = jnp.dot(q32[r0:r0 + T, :].astype(BF16), cmat.astype(BF16),
                                       preferred_element_type=F32)
        ntok_scr[r0:r0 + T, :] = jnp.broadcast_to(nvec, (T, DK))
        upd = jnp.dot(jnp.where(col_seq == bb, wk_t, 0.0).astype(BF16), v, preferred_element_type=F32)
        dec = decay[r0:r0 + 1, :]
        cout_ref[bb] = dec * cmat + upd
        nout_ref[bb:bb + 1, :] = dec * nvec + jnp.sum(wk[r0:r0 + T, :], axis=0, keepdims=True)
        mout_ref[bb:bb + 1, :] = jnp.broadcast_to(m_new[r0:r0 + 1, :], (1, GATE_LANES))

    num = w_inter * qc_scr[...] + num_intra
    den = w_inter * jnp.sum(q32 * ntok_scr[...], axis=1, keepdims=True) + den_intra
    hout = _head_output(num, den, m_t, gout_ref[...], o_ref[...]).astype(hg_ref.dtype)
    src_row = (row % NB) * T + row // NB
    perm = jnp.where(col == src_row, 1.0, 0.0).astype(hg_ref.dtype)
    moved = jnp.dot(perm, hout, preferred_element_type=F32)
    hg_ref[...] = moved.reshape(T, NB, DV).astype(hg_ref.dtype)


def _proj_scan_s_kernel(x_ref, g_ref, wt_ref, wgt_ref,
                        q_ref, k_ref, v_ref, o_ref, gt_ref, bias_ref, gout_ref, mtok_ref, c_ref, n_ref,
                        p_ref, gate_ref, hg_ref, cout_ref, nout_ref, mout_ref,
                        u_scr, qc_scr, ntok_scr, *, T, NB):
    _proj_kernel(x_ref, g_ref, wt_ref, wgt_ref, p_ref, gate_ref, u_scr, emit_bf16=False)
    _scan_s_kernel(q_ref, k_ref, v_ref, o_ref, gt_ref, bias_ref, gout_ref, mtok_ref, c_ref, n_ref,
                   hg_ref, cout_ref, nout_ref, mout_ref, qc_scr, ntok_scr, T=T, NB=NB)


def _proj_scan_s(x, gamma, wt, wgt, p_s, gates_s, bias, gout, mtok, c, n_hm, *, tm, T):
    m = x.shape[0]
    nseq = c.shape[0]
    nt = m // tm
    nj = 2 * H
    tn = PW // nj
    gpt = nj // H
    nb = nseq // (nt * gpt)
    assert nb * nt * gpt == nseq and nb % 8 == 0 and tn % LANES == 0
    lt = nb * T
    kern = functools.partial(_proj_scan_s_kernel, T=T, NB=nb)

    def grp(i, j):
        return i * gpt + j // H

    def head(j):
        return j % H

    return pl.pallas_call(
        kern,
        grid=(nt, nj),
        in_specs=[pl.BlockSpec((tm, D), lambda i, j: (i, 0), pipeline_mode=pl.Buffered(1)),
                  pl.BlockSpec((1, D), lambda i, j: (0, 0)),
                  pl.BlockSpec((tn, D), lambda i, j: (j, 0)),
                  pl.BlockSpec((GATE_LANES, D), lambda i, j: (0, 0)),
                  pl.BlockSpec((lt, DK), lambda i, j: (grp(i, j), head(j))),
                  pl.BlockSpec((lt, DK), lambda i, j: (grp(i, j), H + head(j))),
                  pl.BlockSpec((lt, DV), lambda i, j: (grp(i, j), H + head(j))),
                  pl.BlockSpec((lt, DV), lambda i, j: (grp(i, j), 2 * H + head(j))),
                  pl.BlockSpec((lt, GATE_LANES), lambda i, j: (grp(i, j), 0)),
                  pl.BlockSpec((1, GATE_LANES), lambda i, j: (0, 0)),
                  pl.BlockSpec((1, DV), lambda i, j: (0, head(j))),
                  pl.BlockSpec((lt, GATE_LANES), lambda i, j: (grp(i, j), 0)),
                  pl.BlockSpec((nb, None, DK, DV), lambda i, j: (grp(i, j), head(j), 0, 0)),
                  pl.BlockSpec((None, nb, DK), lambda i, j: (head(j), grp(i, j), 0))],
        out_specs=[pl.BlockSpec((tm, tn), lambda i, j: (i, j)),
                   pl.BlockSpec((tm, GATE_LANES), lambda i, j: (i, 0)),
                   pl.BlockSpec((T, nb, DV), lambda i, j: (0, grp(i, j), head(j))),
                   pl.BlockSpec((nb, None, DK, DV), lambda i, j: (grp(i, j), head(j), 0, 0)),
                   pl.BlockSpec((None, nb, DK), lambda i, j: (head(j), grp(i, j), 0)),
                   pl.BlockSpec((None, nb, GATE_LANES), lambda i, j: (head(j), grp(i, j), 0))],
        out_shape=[jax.ShapeDtypeStruct((m, PW), BF16),
                   jax.ShapeDtypeStruct((m, GATE_LANES), F32),
                   jax.ShapeDtypeStruct((T, nseq, VW), BF16),
                   jax.ShapeDtypeStruct((nseq, H, DK, DV), F32),
                   jax.ShapeDtypeStruct((H, nseq, DK), F32),
                   jax.ShapeDtypeStruct((H, nseq, GATE_LANES), F32)],
        scratch_shapes=[pltpu.VMEM((tm, D), BF16),
                        pltpu.VMEM((lt, DV), F32),
                        pltpu.VMEM((lt, DK), F32)],
        compiler_params=_params(2),
        name="proj_scan_sample",
    )(x, gamma, wt, wgt, p_s, p_s, p_s, p_s, gates_s, bias, gout, mtok, c, n_hm)


def _mmres_cast_kernel(a_ref, w_ref, x_ref, o_ref, wc_ref):
    w = w_ref[...].astype(BF16)
    wc_ref[...] = w
    y = jnp.dot(a_ref[...], w, preferred_element_type=F32)
    nseq, t_dec, _ = x_ref.shape
    for t in range(t_dec):
        o_ref[t * nseq:(t + 1) * nseq, :] = x_ref[:, t, :] + y[t * nseq:(t + 1) * nseq, :]


def _mmres_cast(a, w, x, *, tn):
    m, kdim = a.shape
    n = w.shape[2]
    nseq, t_dec, _ = x.shape
    return pl.pallas_call(
        _mmres_cast_kernel,
        grid=(n // tn,),
        in_specs=[pl.BlockSpec((m, kdim), lambda j: (0, 0)),
                  pl.BlockSpec((None, kdim, tn), lambda j: (0, 0, j)),
                  pl.BlockSpec((nseq, t_dec, tn), lambda j: (0, 0, j))],
        out_specs=[pl.BlockSpec((m, tn), lambda j: (0, j)),
                   pl.BlockSpec((kdim, tn), lambda j: (0, j))],
        out_shape=[jax.ShapeDtypeStruct((m, n), F32),
                   jax.ShapeDtypeStruct((kdim, n), BF16)],
        compiler_params=_params(1),
        name="out_proj_cast",
    )(a, w, x)


def _mmres_kernel(a_ref, w_ref, x_ref, o_ref):
    o_ref[...] = x_ref[...] + jnp.dot(a_ref[...], w_ref[...], preferred_element_type=F32)


def _mmres(a, w, x, *, tm):
    m, kdim = a.shape
    n = w.shape[1]
    return pl.pallas_call(
        _mmres_kernel,
        grid=(m // tm,),
        in_specs=[pl.BlockSpec((tm, kdim), lambda i: (i, 0)),
                  pl.BlockSpec((kdim, n), lambda i: (0, 0)),
                  pl.BlockSpec((tm, n), lambda i: (i, 0))],
        out_specs=pl.BlockSpec((tm, n), lambda i: (i, 0)),
        out_shape=jax.ShapeDtypeStruct((m, n), F32),
        compiler_params=_params(1),
        name="out_proj",
    )(a, w, x)


def _ffn_kernel(x_ref, gam_ref, wg_ref, wa0_ref, wa1_ref, cw_ref, cb_ref, wd_ref, st_ref, gfin_ref,
                o_ref, so_ref, wupc_ref, wdnc_ref,
                u_scr, gext_scr, a_scr, wup0_scr, wup1_scr, wdn0_scr, wdn1_scr,
                *, tm, shift, nsub, final_norm):
    s = pl.program_id(1)
    sr = CONV_TAIL * shift
    wup_slots = (wup0_scr, wup1_scr)
    wdn_slots = (wdn0_scr, wdn1_scr)

    def cast_steps(slot):
        wup_scr = wup_slots[slot]
        wdn_scr = wdn_slots[slot]
        valid = D_FF - jnp.minimum(s, NF - 1) * TF

        def cast_gate():
            ok = lax.broadcasted_iota(jnp.int32, (1, TF), 1) < valid
            w = jnp.where(ok, wg_ref[...], 0.0).astype(BF16)
            wup_scr[:, 0:TF] = w
            wupc_ref[:, 0:TF] = w

        def cast_value():
            lane = lax.broadcasted_iota(jnp.int32, (1, LANES), 1)
            w0 = jnp.where(lane < valid, wa0_ref[...], 0.0).astype(BF16)
            w1 = jnp.where(lane + LANES < valid, wa1_ref[...], 0.0).astype(BF16)
            wup_scr[:, TF:TF + LANES] = w0
            wup_scr[:, TF + LANES:2 * TF] = w1
            wupc_ref[:, TF:TF + LANES] = w0
            wupc_ref[:, TF + LANES:2 * TF] = w1

        def cast_down(r0, rows):
            row_ok = r0 + lax.broadcasted_iota(jnp.int32, (rows, 1), 0) < valid
            w = jnp.where(row_ok, wd_ref[r0:r0 + rows, :], 0.0).astype(BF16)
            wdn_scr[r0:r0 + rows, :] = w
            wdnc_ref[r0:r0 + rows, :] = w

        half = TF // 2
        return [cast_gate, cast_value, functools.partial(cast_down, 0, half),
                functools.partial(cast_down, half, half)]

    def run_tile(slot, fillers=()):
        fillers = list(fillers)

        def fill():
            if fillers:
                fillers.pop(0)()

        wup_scr = wup_slots[slot]
        wdn_scr = wdn_slots[slot]
        f = s - 1
        col_ok = lax.broadcasted_iota(jnp.int32, (1, TF), 1) < D_FF - f * TF
        cw = cw_ref[...]
        cb = cb_ref[...]
        for t in range(CONV_TAIL):
            gext_scr[t * shift:(t + 1) * shift, :] = st_ref[:, t, :]
        ts = tm // nsub
        for h in range(nsub):
            r0 = h * ts
            ga = jnp.dot(u_scr[r0:r0 + ts, :], wup_scr[...], preferred_element_type=F32)
            gext_scr[sr + r0:sr + r0 + ts, :] = ga[:, 0:TF]
            a_scr[r0:r0 + ts, :] = ga[:, TF:2 * TF]
            fill()
        for h in range(nsub):
            r0 = h * ts
            g = gext_scr[sr + r0:sr + r0 + ts, :]
            g_m2 = gext_scr[sr + r0 - 2 * shift:sr + r0 - 2 * shift + ts, :]
            g_m1 = gext_scr[sr + r0 - shift:sr + r0 - shift + ts, :]
            gc = cb + ((cw[0:1, :] * g_m2 + cw[1:2, :] * g_m1) + cw[2:3, :] * g)
            hmid = jnp.where(col_ok, (gc * jax.nn.sigmoid(gc)) * a_scr[r0:r0 + ts, :], 0.0)
            o_ref[r0:r0 + ts, :] += jnp.dot(hmid.astype(BF16), wdn_scr[...], preferred_element_type=F32)
            fill()
        while fillers:
            fill()
        for t in range(CONV_TAIL):
            so_ref[:, t, :] = gext_scr[tm + t * shift:tm + (t + 1) * shift, :]

    @pl.when(s == 0)
    def _():
        x = x_ref[...]
        u_scr[...] = _rms(x, gam_ref[...]).astype(BF16)
        o_ref[...] = x
        for step in cast_steps(0):
            step()

    for parity in range(2):
        @pl.when(jnp.logical_and(s > 0, s % 2 == parity))
        def _(parity=parity):
            run_tile(1 - parity, cast_steps(parity))

    if final_norm:
        @pl.when(s == NF)
        def _():
            o_ref[...] = _rms(o_ref[...], gfin_ref[...])


def _ffn(x, layer, gamma, w_up, conv_w, conv_b, w_down, st, gfin, *, final_norm):
    assert TF == 2 * LANES
    tm = x.shape[0]
    shift = st.shape[0]
    sr = CONV_TAIL * shift
    kern = functools.partial(_ffn_kernel, tm=tm, shift=shift, nsub=2, final_norm=final_norm)
    last_a = 2 * FF_LANE_BLOCKS - 1

    def wt(s):
        return jnp.minimum(s, NF - 1)

    def ft(s):
        return jnp.maximum(s - 1, 0)

    st_spec = pl.BlockSpec((shift, CONV_TAIL, TF), lambda i, s: (0, 0, ft(s)))
    return pl.pallas_call(
        kern,
        grid=(1, NF + 1),
        in_specs=[pl.BlockSpec((tm, D), lambda i, s: (i, 0), pipeline_mode=pl.Buffered(1)),
                  pl.BlockSpec((1, D), lambda i, s: (0, 0)),
                  pl.BlockSpec((None, D, TF), lambda i, s: (layer, 0, wt(s))),
                  pl.BlockSpec((None, D, LANES), lambda i, s: (layer, 0, FF_LANE_BLOCKS + 2 * wt(s))),
                  pl.BlockSpec((None, D, LANES),
                               lambda i, s: (layer, 0, jnp.minimum(FF_LANE_BLOCKS + 2 * wt(s) + 1, last_a))),
                  pl.BlockSpec((None, 3, TF), lambda i, s: (layer, 0, ft(s))),
                  pl.BlockSpec((None, 1, TF), lambda i, s: (layer, 0, ft(s))),
                  pl.BlockSpec((None, TF, D), lambda i, s: (layer, wt(s), 0)),
                  st_spec,
                  pl.BlockSpec((1, D), lambda i, s: (0, 0))],
        out_specs=[pl.BlockSpec((tm, D), lambda i, s: (i, 0)),
                   st_spec,
                   pl.BlockSpec((D, 2 * TF), lambda i, s: (0, wt(s))),
                   pl.BlockSpec((TF, D), lambda i, s: (wt(s), 0))],
        out_shape=[jax.ShapeDtypeStruct((tm, D), F32),
                   jax.ShapeDtypeStruct(st.shape, F32),
                   jax.ShapeDtypeStruct((D, NF * 2 * TF), BF16),
                   jax.ShapeDtypeStruct((NF * TF, D), BF16)],
        scratch_shapes=[pltpu.VMEM((tm, D), BF16),
                        pltpu.VMEM((sr + tm, TF), F32),
                        pltpu.VMEM((tm, TF), F32),
                        pltpu.VMEM((D, 2 * TF), BF16),
                        pltpu.VMEM((D, 2 * TF), BF16),
                        pltpu.VMEM((TF, D), BF16),
                        pltpu.VMEM((TF, D), BF16)],
        compiler_params=_params(2),
        name="conv_ffn_cast",
    )(x, gamma, w_up, w_up, w_up, conv_w, conv_b, w_down, st, gfin)


def _ffn_bf16_kernel(x_ref, gam_ref, wup_ref, cw_ref, cb_ref, wdn_ref, st_ref, gfin_ref,
                     o_ref, so_ref, u_scr, gext_scr, a_scr, carry_scr,
                     *, tm, sr, shift, tps, nsub, final_norm, step=None):
    i = pl.program_id(0)
    f = pl.program_id(1) if step is None else step
    tf2 = 2 * TF

    @pl.when(f == 0)
    def _():
        x = x_ref[...]
        u_scr[...] = _rms(x, gam_ref[...]).astype(BF16)
        o_ref[...] = x
        if tps > 1:
            @pl.when(i == 0)
            def _():
                carry_scr[...] = jnp.zeros_like(carry_scr)

    col_ok = lax.broadcasted_iota(jnp.int32, (1, tf2), 1) < D_FF - f * tf2
    cw = cw_ref[...]
    cb = cb_ref[...]
    if tps == 1:
        gext_scr[0:sr, :] = st_ref[...]
    else:
        gext_scr[0:sr, :] = jnp.where((i % tps) == 0, st_ref[...], carry_scr[f])
    ts = tm // nsub
    for h in range(nsub):
        r0 = h * ts
        ga = jnp.dot(u_scr[r0:r0 + ts, :], wup_ref[...], preferred_element_type=F32)
        gext_scr[sr + r0:sr + r0 + ts, 0:TF] = ga[:, 0:TF]
        gext_scr[sr + r0:sr + r0 + ts, TF:tf2] = ga[:, 2 * TF:3 * TF]
        a_scr[r0:r0 + ts, 0:TF] = ga[:, TF:2 * TF]
        a_scr[r0:r0 + ts, TF:tf2] = ga[:, 3 * TF:4 * TF]
    for h in range(nsub):
        r0 = h * ts
        g = gext_scr[sr + r0:sr + r0 + ts, :]
        g_m2 = gext_scr[sr + r0 - 2 * shift:sr + r0 - 2 * shift + ts, :]
        g_m1 = gext_scr[sr + r0 - shift:sr + r0 - shift + ts, :]
        gc = cb + ((cw[0:1, :] * g_m2 + cw[1:2, :] * g_m1) + cw[2:3, :] * g)
        hmid = jnp.where(col_ok, (gc * jax.nn.sigmoid(gc)) * a_scr[r0:r0 + ts, :], 0.0)
        o_ref[r0:r0 + ts, :] += jnp.dot(hmid.astype(BF16), wdn_ref[...], preferred_element_type=F32)
    g_tail = gext_scr[tm:tm + sr, :]
    so_ref[...] = g_tail
    if tps > 1:
        carry_scr[f] = g_tail

    if final_norm:
        @pl.when(f == pl.num_programs(1) - 1)
        def _():
            o_ref[...] = _rms(o_ref[...], gfin_ref[...])


def _ffn_bf16(x, layer, gamma, wupc, conv_w, conv_b, wdnc, st, gfin, *, tm, sr, shift, tps, st_per_tile,
              final_norm):
    assert NF % 2 == 0
    m = x.shape[0]
    nt = m // tm
    tf2 = 2 * TF
    nf2 = NF // 2
    kern = functools.partial(_ffn_bf16_kernel, tm=tm, sr=sr, shift=shift, tps=tps,
                             nsub=2 if tm >= 64 else 1, final_norm=final_norm)
    st_map = (lambda i, f: (i, f)) if st_per_tile else (lambda i, f: (0, f))
    return pl.pallas_call(
        kern,
        grid=(nt, nf2),
        in_specs=[pl.BlockSpec((tm, D), lambda i, f: (i, 0)),
                  pl.BlockSpec((1, D), lambda i, f: (0, 0)),
                  pl.BlockSpec((D, 2 * tf2), lambda i, f: (0, f)),
                  pl.BlockSpec((None, 3, tf2), lambda i, f: (layer, 0, f)),
                  pl.BlockSpec((None, 1, tf2), lambda i, f: (layer, 0, f)),
                  pl.BlockSpec((tf2, D), lambda i, f: (f, 0)),
                  pl.BlockSpec((sr, tf2), st_map),
                  pl.BlockSpec((1, D), lambda i, f: (0, 0))],
        out_specs=[pl.BlockSpec((tm, D), lambda i, f: (i, 0)),
                   pl.BlockSpec((sr, tf2), lambda i, f: (i, f))],
        out_shape=[jax.ShapeDtypeStruct((m, D), F32),
                   jax.ShapeDtypeStruct((nt * sr, D_FF), F32)],
        scratch_shapes=[pltpu.VMEM((tm, D), BF16),
                        pltpu.VMEM((sr + tm, tf2), F32),
                        pltpu.VMEM((tm, tf2), F32),
                        pltpu.VMEM((nf2, sr, tf2), F32)],
        compiler_params=_params(2),
        name="conv_ffn",
    )(x, gamma, wupc, conv_w, conv_b, wdnc, st, gfin)


def _scan_ffn_kernel(*refs, L, nc, tm_r):
    scan_in, ffn_in = refs[0:10], refs[10:18]
    scan_out, ffn_out = refs[18:22], refs[22:24]
    scan_scr, ffn_scr = refs[24:27], refs[27:31]
    _scan_kernel(*scan_in, *scan_out, *scan_scr, L=L, lead_pad=0, nc=nc)
    step = pl.program_id(0) * nc + pl.program_id(1)

    @pl.when(step < NF // 2)
    def _():
        _ffn_bf16_kernel(*ffn_in, *ffn_out, *ffn_scr, tm=tm_r, sr=8, shift=1, tps=1, nsub=1,
                         final_norm=False, step=step)


def _scan_ffn(p, gates, bias, gout, c0, n0, m0, xr, layer, gamma, wupc, conv_w, conv_b, wdnc, st, gfin, *, L):
    b, s, _ = p.shape
    nc = s // L
    tm_r = xr.shape[0]
    tf2 = 2 * TF
    nf2 = NF // 2
    assert b * nc >= nf2

    def fi(i, c):
        return jnp.minimum(i * nc + c, nf2 - 1)

    kern = functools.partial(_scan_ffn_kernel, L=L, nc=nc, tm_r=tm_r)
    return pl.pallas_call(
        kern,
        grid=(b, nc),
        in_specs=[pl.BlockSpec((None, L, QKW), lambda i, c: (i, c, 0)),
                  pl.BlockSpec((None, L, QKW), lambda i, c: (i, c, 1)),
                  pl.BlockSpec((None, L, VW), lambda i, c: (i, c, 1)),
                  pl.BlockSpec((None, L, VW), lambda i, c: (i, c, 2)),
                  pl.BlockSpec((None, L, GATE_LANES), lambda i, c: (i, c, 0)),
                  pl.BlockSpec((1, GATE_LANES), lambda i, c: (0, 0)),
                  pl.BlockSpec((1, VW), lambda i, c: (0, 0)),
                  pl.BlockSpec((H, DK, DV), lambda i, c: (0, 0, 0)),
                  pl.BlockSpec((H, 8, DK), lambda i, c: (0, 0, 0)),
                  pl.BlockSpec((H, 8, GATE_LANES), lambda i, c: (0, 0, 0)),
                  pl.BlockSpec((tm_r, D), lambda i, c: (0, 0)),
                  pl.BlockSpec((1, D), lambda i, c: (0, 0)),
                  pl.BlockSpec((D, 2 * tf2), lambda i, c: (0, fi(i, c))),
                  pl.BlockSpec((None, 3, tf2), lambda i, c: (layer, 0, fi(i, c))),
                  pl.BlockSpec((None, 1, tf2), lambda i, c: (layer, 0, fi(i, c))),
                  pl.BlockSpec((tf2, D), lambda i, c: (fi(i, c), 0)),
                  pl.BlockSpec((8, tf2), lambda i, c: (0, fi(i, c))),
                  pl.BlockSpec((1, D), lambda i, c: (0, 0))],
        out_specs=[pl.BlockSpec((None, L, VW), lambda i, c: (i, c, 0)),
                   pl.BlockSpec((None, H, DK, DV), lambda i, c: (i, 0, 0, 0)),
                   pl.BlockSpec((None, H, 8, DK), lambda i, c: (i, 0, 0, 0)),
                   pl.BlockSpec((None, H, 8, GATE_LANES), lambda i, c: (i, 0, 0, 0)),
                   pl.BlockSpec((tm_r, D), lambda i, c: (0, 0)),
                   pl.BlockSpec((8, tf2), lambda i, c: (0, fi(i, c)))],
        out_shape=[jax.ShapeDtypeStruct((b, s, VW), BF16),
                   jax.ShapeDtypeStruct((b, H, DK, DV), F32),
                   jax.ShapeDtypeStruct((b, H, 8, DK), F32),
                   jax.ShapeDtypeStruct((b, H, 8, GATE_LANES), F32),
                   jax.ShapeDtypeStruct((tm_r, D), F32),
                   jax.ShapeDtypeStruct((8, D_FF), F32)],
        scratch_shapes=[pltpu.VMEM((H, DK, DV), F32),
                        pltpu.VMEM((H, 8, DK), F32),
                        pltpu.VMEM((H, 8, GATE_LANES), F32),
                        pltpu.VMEM((tm_r, D), BF16),
                        pltpu.VMEM((8 + tm_r, tf2), F32),
                        pltpu.VMEM((tm_r, tf2), F32),
                        pltpu.VMEM((1, 8, tf2), F32)],
        compiler_params=_params(2),
        name="scan_meta_ffn",
    )(p, p, p, p, gates, bias, gout, c0, n0, m0, xr, gamma, wupc, conv_w, conv_b, wdnc, st, gfin)


def _window_sum(ext_scr, tmp_scrs, base, tm, shift, w):
    starts = {w: base}
    v = w
    while v > 2:
        starts[v // 2] = (starts[v] - (v // 2) * shift) // 8 * 8
        v //= 2
    src = ext_scr
    v = 1
    k = 0
    while True:
        lo = starts[2 * v]
        n = base + tm - lo
        val = src[lo:lo + n, :] + src[lo - v * shift:lo - v * shift + n, :]
        v *= 2
        if v == w:
            return val
        dst = tmp_scrs[k % 2]
        dst[lo:lo + n, :] = val
        src = dst
        k += 1


def _pool_kernel(x_ref, gam_ref, wp_ref, sc_ref, st_ref, o_ref, ut_ref, *scratch, tm, shift, tps, pos0, tr):
    ng = len(POOL_WINDOWS)
    uext_scrs = scratch[0:ng]
    tmp_scrs = scratch[ng:ng + 2]
    carry_scr = scratch[ng + 2]
    i = pl.program_id(0)
    hr = POOL_HALO * shift
    base = POOL_PAD_ROWS + hr

    if tps > 1:
        @pl.when(i == 0)
        def _():
            carry_scr[...] = jnp.zeros_like(carry_scr)

    x = x_ref[...]
    rinv = lax.rsqrt(jnp.mean(x * x, axis=1, keepdims=True) + EPS)
    for kk, w in enumerate(POOL_WINDOWS):
        cs = slice(kk * PG, (kk + 1) * PG)
        uext_scr = uext_scrs[kk]
        xg = x_ref[:, cs]
        ug = xg * rinv * gam_ref[:, cs]
        uext_scr[0:POOL_PAD_ROWS, :] = jnp.zeros((POOL_PAD_ROWS, PG), F32)
        if tps == 1:
            uext_scr[POOL_PAD_ROWS:base, :] = st_ref[:, cs]
        else:
            uext_scr[POOL_PAD_ROWS:base, :] = jnp.where((i % tps) == 0, st_ref[:, cs], carry_scr[:, cs])
        uext_scr[base:base + tm, :] = ug
        acc = _window_sum(uext_scr, tmp_scrs, base, tm, shift, w)
        if pos0 + 1 >= w:
            pooled = acc / float(w) - ug
        else:
            step = (i % tps) * (tm // shift) + lax.broadcasted_iota(jnp.int32, (tm, 1), 0) // shift
            cnt = jnp.minimum(w, pos0 + step + 1).astype(F32)
            pooled = acc / cnt - ug
        y = jnp.dot(pooled.astype(BF16), wp_ref[kk].astype(BF16), preferred_element_type=F32)
        o_ref[:, cs] = xg + y * sc_ref[:, cs]
        ut_ref[:, cs] = ug[tm - tr:tm, :]
        if tps > 1:
            carry_scr[:, cs] = ug[tm - hr:tm, :]


def _pool(x, gamma, wp, sc, st, *, tm, shift, tps, pos0, tr):
    m = x.shape[0]
    nt = m // tm
    hr = POOL_HALO * shift
    ng = len(POOL_WINDOWS)
    kern = functools.partial(_pool_kernel, tm=tm, shift=shift, tps=tps, pos0=pos0, tr=tr)
    return pl.pallas_call(
        kern,
        grid=(nt,),
        in_specs=[pl.BlockSpec((tm, D), lambda i: (i, 0)),
                  pl.BlockSpec((1, D), lambda i: (0, 0)),
                  pl.BlockSpec((None, ng, PG, PG), lambda i: (0, 0, 0, 0)),
                  pl.BlockSpec((1, D), lambda i: (0, 0)),
                  pl.BlockSpec((hr, D), lambda i: (0, 0))],
        out_specs=[pl.BlockSpec((tm, D), lambda i: (i, 0)),
                   pl.BlockSpec((tr, D), lambda i: (i, 0))],
        out_shape=[jax.ShapeDtypeStruct((m, D), F32),
                   jax.ShapeDtypeStruct((nt * tr, D), F32)],
        scratch_shapes=[pltpu.VMEM((POOL_PAD_ROWS + hr + tm, PG), F32)] * (ng + 2)
        + [pltpu.VMEM((hr, D), F32)],
        compiler_params=_params(1),
        name="pool_mixer",
    )(x, gamma, wp, sc, st)


def _pool_sample_kernel(x_ref, gam_ref, wp_ref, sc_ref, st_ref, o_ref, new_ref, rinv_scr, uext_scr,
                        *, nseq, t_dec):
    nh = POOL_HALO - 1
    grp = pl.program_id(0)
    tm = nseq * t_dec
    hr = POOL_HALO * nseq

    @pl.when(grp == 0)
    def _():
        x = x_ref[...]
        rinv_scr[...] = lax.rsqrt(jnp.mean(x * x, axis=1, keepdims=True) + EPS)

    for kk, w in enumerate(POOL_WINDOWS):
        @pl.when(grp == kk)
        def _(kk=kk, w=w):
            cs = slice(kk * PG, (kk + 1) * PG)
            xg = x_ref[:, cs]
            ug = xg * rinv_scr[...] * gam_ref[:, cs]
            for t in range(nh):
                uext_scr[(t + 1) * nseq:(t + 2) * nseq, :] = st_ref[t]
            uext_scr[hr:hr + tm, :] = ug
            acc = ug
            for j in range(1, w):
                acc = acc + uext_scr[hr - j * nseq:hr - j * nseq + tm, :]
            pooled = acc / float(w) - ug
            y = jnp.dot(pooled.astype(BF16), wp_ref[kk].astype(BF16), preferred_element_type=F32)
            o_ref[...] = xg + y * sc_ref[:, cs]
            for t in range(nh):
                src = (t + t_dec + 1) * nseq
                new_ref[t] = uext_scr[src:src + nseq, :]


def _pool_sample(x, gamma, wp, sc, state, *, nseq, t_dec):
    assert PAST_LEN + 1 >= max(POOL_WINDOWS)
    nh = POOL_HALO - 1
    ng = len(POOL_WINDOWS)
    tm = nseq * t_dec
    hr = POOL_HALO * nseq
    kern = functools.partial(_pool_sample_kernel, nseq=nseq, t_dec=t_dec)
    hist_spec = pl.BlockSpec((nh, nseq, PG), lambda g: (0, 0, g))
    return pl.pallas_call(
        kern,
        grid=(ng,),
        in_specs=[pl.BlockSpec((tm, D), lambda g: (0, 0)),
                  pl.BlockSpec((1, D), lambda g: (0, 0)),
                  pl.BlockSpec((None, ng, PG, PG), lambda g: (0, 0, 0, 0)),
                  pl.BlockSpec((1, D), lambda g: (0, 0)),
                  hist_spec],
        out_specs=[pl.BlockSpec((tm, PG), lambda g: (0, g)),
                   hist_spec],
        out_shape=[jax.ShapeDtypeStruct((tm, D), F32),
                   jax.ShapeDtypeStruct((nh, nseq, D), F32)],
        scratch_shapes=[pltpu.VMEM((tm, 1), F32),
                        pltpu.VMEM((hr + tm, PG), F32)],
        compiler_params=_params(1),
        name="pool_mixer_sample",
    )(x, gamma, wp, sc, state)


def _pad_cols(a, n):
    return jnp.pad(a, ((0, 0), (0, n - a.shape[1])))


def _from_time_major(a, nseq, t):
    return jnp.swapaxes(a.reshape((t, nseq) + a.shape[1:]), 0, 1)


def kernel(x_prompt, x_sample, state_mlstm_C, state_mlstm_n, state_mlstm_m, state_pool, state_ffn_conv,
           meta_tokens, norm_mix, norm_ffn, norm_final, w_mlstm_in, b_mlstm_gate, g_mlstm_out, w_mlstm_out,
           w_pool, pool_scale, w_up, conv_w, conv_b, w_down):
    bsz, seq, _ = x_prompt.shape
    nseq, t_dec, _ = x_sample.shape

    w_in_t = jnp.swapaxes(w_mlstm_in[0], 0, 1)
    bias = _pad_cols(b_mlstm_gate[0][None, :], GATE_LANES)
    w_out = w_mlstm_out
    gout = g_mlstm_out[0][None, :]
    wp = w_pool
    psc = pool_scale[0][None, :]
    gfin = norm_final[None, :]
    conv_b3 = conv_b[:, None, :]
    ffn_w = [(layer, norm_ffn[layer][None, :], w_up, conv_w, conv_b3, w_down) for layer in range(2)]
    g_mix0 = norm_mix[0][None, :]
    g_mix1 = norm_mix[1][None, :]

    ffn_cache = {}
    mix_cache = {}

    def proj(x, **kw):
        if "w_in" not in mix_cache:
            p, gates, wtc, wgc = _proj(x, g_mix0, w_in_t, w_in_t, **kw)
            mix_cache["w_in"] = (wtc, wgc)
            return p, gates
        return _proj(x, g_mix0, *mix_cache["w_in"], **kw)

    def out_proj(a, x, *, tm):
        if "w_out" not in mix_cache:
            y, mix_cache["w_out"] = _mmres_cast(a, w_out, x, tn=512)
            return y
        return _mmres(a, mix_cache["w_out"], x, tm=tm)

    def ffn_first(x, layer, st, *, final_norm):
        lyr, gamma, wu, cwt, cbs, wd = ffn_w[layer]
        y, cs, wupc, wdnc = _ffn(x, lyr, gamma, wu, cwt, cbs, wd, st, gfin, final_norm=final_norm)
        ffn_cache[layer] = (wupc, wdnc)
        return y, cs

    def ffn(x, layer, st, **kw):
        lyr, gamma, _, cwt, cbs, _ = ffn_w[layer]
        wupc, wdnc = ffn_cache[layer]
        return _ffn_bf16(x, lyr, gamma, wupc, cwt, cbs, wdnc, st, gfin, **kw)

    def meta_rest(x2_m, cs0_m):
        x3_m, ut_m = _pool(x2_m, g_mix1, wp, psc, jnp.zeros((POOL_HALO, D), F32), tm=N_META, shift=1, tps=1,
                           pos0=0, tr=POOL_HALO)
        _, cs1_m = ffn(x3_m, 1, jnp.zeros((8, D_FF), F32), tm=N_META, sr=8, shift=1, tps=1,
                       st_per_tile=False, final_norm=True)
        return cs0_m, ut_m, cs1_m

    def long_stream(x, nb, s, st, *, tm, chunk, lead_pad, pos0, projected=None, rider=None):
        c0, n0, m0, conv0, pool0, conv1 = st
        tm_ffn = tm
        p, gates = projected if projected is not None else proj(x, tm=tm, tn=1024, out_dtype=BF16)
        p = p.reshape(nb, s, PW)
        gates = gates.reshape(nb, s, GATE_LANES)
        if lead_pad:
            p = jnp.pad(p, ((0, 0), (lead_pad, 0), (0, 0)))
            gates = jnp.pad(gates, ((0, 0), (lead_pad, 0), (0, 0)))
        if rider is None:
            hg, c_new, n_new, m_new = _scan(p, gates, bias, gout, c0, n0, m0, L=chunk, lead_pad=lead_pad)
        else:
            lyr, gamma, _, cwt, cbs, _ = ffn_w[0]
            wupc, wdnc = ffn_cache[0]
            hg, c_new, n_new, m_new, x2_m, cs0_m = _scan_ffn(
                p, gates, bias, gout, c0, n0, m0, rider, lyr, gamma, wupc, cwt, cbs, wdnc,
                jnp.zeros((8, D_FF), F32), gfin, L=chunk)
            conv0, pool0, conv1 = meta_rest(x2_m, cs0_m)
        hg = hg[:, lead_pad:].reshape(nb * s, VW)
        x1 = out_proj(hg, x, tm=min(tm, 512))
        x2, cs0 = ffn(x1, 0, conv0, tm=tm_ffn, sr=8, shift=1, tps=s // tm_ffn,
                      st_per_tile=False, final_norm=False)
        x3, ut = _pool(x2, g_mix1, wp, psc, pool0, tm=tm, shift=1, tps=s // tm, pos0=pos0,
                       tr=POOL_HALO)
        y, cs1 = ffn(x3, 1, conv1, tm=tm_ffn, sr=8, shift=1, tps=s // tm_ffn,
                     st_per_tile=False, final_norm=True)
        cs0 = cs0.reshape(nb, s // tm_ffn, 8, D_FF)[:, -1]
        cs1 = cs1.reshape(nb, s // tm_ffn, 8, D_FF)[:, -1]
        ut = ut.reshape(nb, s // tm, POOL_HALO, D)[:, -1]
        return y, (c_new, n_new, m_new, cs0, ut, cs1)

    def prompt_streams():
        lead = 128 - N_META
        p_m, gates_m = proj(meta_tokens, tm=N_META, tn=1024, out_dtype=BF16)
        hg_m, c_m, n_m, m_m = _scan(jnp.pad(p_m, ((lead, 0), (0, 0)))[None], jnp.pad(gates_m, ((lead, 0), (0, 0)))[None],
                                    bias, gout, jnp.zeros((H, DK, DV), F32), jnp.zeros((H, 8, DK), F32),
                                    jnp.zeros((H, 8, GATE_LANES), F32), L=128, lead_pad=lead)
        x1_m = out_proj(hg_m[0, lead:], meta_tokens, tm=N_META)

        y_p, (c_p, n_p, m_p, cs0_p, ut_p, cs1_p) = long_stream(
            x_prompt.reshape(bsz * seq, D), bsz, seq, (c_m[0], n_m[0], m_m[0], None, None, None),
            tm=1024, chunk=SCAN_CHUNK, lead_pad=0, pos0=N_META, projected=prompt_projected, rider=x1_m)
        return (y_p.reshape(bsz, seq, D), c_p[None], n_p[:, :, 0][None], m_p[:, :, 0, 0][None],
                ut_p[:, 1:][None], jnp.stack([cs0_p[:, 6:], cs1_p[:, 6:]]))

    xs = x_sample.reshape(nseq * t_dec, D)
    p_s, gates_s = proj(xs, tm=nseq * t_dec, tn=512, out_dtype=F32)
    mtok = _pad_cols(jnp.repeat(state_mlstm_m[0], t_dec, axis=0), GATE_LANES)
    n_hm = jnp.swapaxes(state_mlstm_n[0], 0, 1)
    p_p, gates_p, hg_s, C_s, n_s_hm, m_s_hm = _proj_scan_s(
        x_prompt.reshape(bsz * seq, D), g_mix0, *mix_cache["w_in"], p_s, gates_s, bias, gout, mtok,
        state_mlstm_C[0], n_hm, tm=1024, T=t_dec)
    prompt_projected = (p_p, gates_p)
    assert nseq == SAMPLE_TILE_SEQS
    x1 = out_proj(hg_s.reshape(t_dec * nseq, VW), x_sample, tm=512)
    x2, cs0_s = ffn_first(x1, 0, state_ffn_conv[0], final_norm=False)
    x3, pool_new = _pool_sample(x2, g_mix1, wp, psc, jnp.swapaxes(state_pool[0], 0, 1), nseq=nseq, t_dec=t_dec)
    y_s, cs1_s = ffn_first(x3, 1, state_ffn_conv[1], final_norm=True)
    y_sample = _from_time_major(y_s, nseq, t_dec)
    n_s = jnp.swapaxes(n_s_hm, 0, 1)[None]
    m_s = jnp.swapaxes(m_s_hm[:, :, 0], 0, 1)[None]
    pool_s = jnp.swapaxes(pool_new, 0, 1)[None]
    conv_s = jnp.stack([cs0_s, cs1_s])

    y_prompt, C_p, n_p, m_p, pool_p, conv_p = prompt_streams()
    return (y_prompt, y_sample, C_p, n_p, m_p, pool_p, conv_p,
            C_s[None], n_s, m_s, pool_s, conv_s)
```

```python
import functools

import jax
import jax.numpy as jnp
from jax import lax
from jax.experimental import pallas as pl
from jax.experimental.pallas import tpu as pltpu

F32 = jnp.float32
BF16 = jnp.bfloat16

EPS = 1e-6
D = 2048
H = 4
DK = 256
DV = 512
QKW = H * DK
VW = H * DV
PW = 2 * QKW + 2 * VW
GATE_LANES = 128
SCALE = DK ** -0.5
POOL_WINDOWS = (2, 4, 8, 16)
PG = D // len(POOL_WINDOWS)
POOL_HALO = 16
POOL_PAD_ROWS = 16
D_FF = 5504
TF = 256
NF = -(-D_FF // TF)
LANES = 128
FF_LANE_BLOCKS = D_FF // LANES
CONV_TAIL = 2
N_META = 16
PAST_LEN = 16384
SCAN_CHUNK = 512
SAMPLE_TILE_SEQS = 128
VMEM_LIMIT = 60 * 1024 * 1024


def _params(n_axes):
    return pltpu.CompilerParams(dimension_semantics=("arbitrary",) * n_axes,
                                vmem_limit_bytes=VMEM_LIMIT)


def _rms(x, g):
    return x * lax.rsqrt(jnp.mean(x * x, axis=-1, keepdims=True) + EPS) * g


def _log_sigmoid(x):
    return jnp.minimum(x, 0.0) - jnp.log(1.0 + jnp.exp(-jnp.abs(x)))


_NT = (((1,), (1,)), ((), ()))


def _proj_kernel(x_ref, g_ref, wt_ref, wgt_ref, p_ref, gate_ref, *rest, emit_bf16):
    u_scr = rest[-1]

    @pl.when(pl.program_id(1) == 0)
    def _():
        ub = _rms(x_ref[...], g_ref[...]).astype(BF16)
        u_scr[...] = ub
        wg = wgt_ref[...]
        if emit_bf16:
            row_ok = lax.broadcasted_iota(jnp.int32, (GATE_LANES, 1), 0) < 2 * H
            wg = jnp.where(row_ok, wg, 0.0).astype(BF16)
            rest[1][...] = wg
        gate_ref[...] = lax.dot_general(ub, wg, _NT, preferred_element_type=F32)

    w = wt_ref[...].astype(BF16)
    if emit_bf16:
        rest[0][...] = w
    p_ref[...] = lax.dot_general(u_scr[...], w, _NT, preferred_element_type=F32).astype(p_ref.dtype)


def _proj(x, gamma, wt, wgt, *, tm, tn, out_dtype):
    m = x.shape[0]
    emit = wt.dtype != BF16
    gate_spec = pl.BlockSpec((GATE_LANES, D), (lambda i, j: (PW // GATE_LANES, 0)) if emit
                             else (lambda i, j: (0, 0)))
    out_specs = [pl.BlockSpec((tm, tn), lambda i, j: (i, j)),
                 pl.BlockSpec((tm, GATE_LANES), lambda i, j: (i, 0))]
    out_shape = [jax.ShapeDtypeStruct((m, PW), out_dtype),
                 jax.ShapeDtypeStruct((m, GATE_LANES), F32)]
    if emit:
        out_specs += [pl.BlockSpec((tn, D), lambda i, j: (j, 0)),
                      pl.BlockSpec((GATE_LANES, D), lambda i, j: (0, 0))]
        out_shape += [jax.ShapeDtypeStruct((PW, D), BF16),
                      jax.ShapeDtypeStruct((GATE_LANES, D), BF16)]
    return pl.pallas_call(
        functools.partial(_proj_kernel, emit_bf16=emit),
        grid=(m // tm, PW // tn),
        in_specs=[pl.BlockSpec((tm, D), lambda i, j: (i, 0)),
                  pl.BlockSpec((1, D), lambda i, j: (0, 0)),
                  pl.BlockSpec((tn, D), lambda i, j: (j, 0)),
                  gate_spec],
        out_specs=out_specs,
        out_shape=out_shape,
        scratch_shapes=[pltpu.VMEM((tm, D), BF16)],
        compiler_params=_params(2),
        name="proj_cast" if emit else "proj",
    )(x, gamma, wt, wgt)


def _head_output(num, den, m_t, gout, o):
    hv = num * (1.0 / jnp.maximum(jnp.abs(den), jnp.exp(-m_t)))
    hv = hv * lax.rsqrt(jnp.mean(hv * hv, axis=1, keepdims=True) + EPS)
    return hv * gout * jax.nn.sigmoid(o.astype(F32))


def _scan_kernel(q_ref, k_ref, v_ref, o_ref, gt_ref, bias_ref, gout_ref, c0_ref, n0_ref, m0_ref,
                 hg_ref, cout_ref, nout_ref, mout_ref, c_scr, n_scr, m_scr, *, L, lead_pad, nc):
    c = pl.program_id(1)

    @pl.when(c == 0)
    def _():
        c_scr[...] = c0_ref[...]
        n_scr[...] = n0_ref[...]
        m_scr[...] = m0_ref[...]

    gates = gt_ref[...] + bias_ref[...]
    gates_t = gates.T
    row = lax.broadcasted_iota(jnp.int32, (L, L), 0)
    col = lax.broadcasted_iota(jnp.int32, (L, L), 1)
    causal = row >= col
    causal_t = row <= col
    if lead_pad:
        live_col = lax.broadcasted_iota(jnp.int32, (L, 1), 0) >= lead_pad
        live_row = lax.broadcasted_iota(jnp.int32, (1, L), 1) >= lead_pad

    for hh in range(H):
        ig_col = gates[:, hh:hh + 1]
        ig_row = gates_t[hh:hh + 1, :]
        lf_col = _log_sigmoid(gates[:, H + hh:H + hh + 1])
        lf_row = _log_sigmoid(gates_t[H + hh:H + hh + 1, :])
        if lead_pad:
            ig_col = jnp.where(live_col, ig_col, -jnp.inf)
            ig_row = jnp.where(live_row, ig_row, -jnp.inf)
            lf_col = jnp.where(live_col, lf_col, 0.0)
            lf_row = jnp.where(live_row, lf_row, 0.0)
        b_col = jnp.sum(jnp.where(causal, lf_row, 0.0), axis=1, keepdims=True)
        b_row = jnp.sum(jnp.where(causal_t, lf_col, 0.0), axis=0, keepdims=True)
        m_prev = m_scr[hh, 0:1, 0:1]
        d = jnp.where(causal, b_col - b_row + ig_row, -jnp.inf)
        inter = b_col + m_prev
        m_t = jnp.maximum(inter, jnp.max(d, axis=1, keepdims=True))
        w_inter = jnp.exp(inter - m_t) * SCALE

        q = q_ref[:, hh * DK:(hh + 1) * DK]
        k = k_ref[:, hh * DK:(hh + 1) * DK]
        v = v_ref[:, hh * DV:(hh + 1) * DV]
        qk = lax.dot_general(q, k, _NT, preferred_element_type=F32)
        s = qk * (jnp.exp(d - m_t) * SCALE)
        cmat = c_scr[hh]
        nvec = n_scr[hh, 0:1, :]
        num = w_inter * jnp.dot(q, cmat.astype(BF16), preferred_element_type=F32) \
            + jnp.dot(s.astype(BF16), v, preferred_element_type=F32)
        den = w_inter * jnp.sum(q.astype(F32) * nvec, axis=1, keepdims=True) \
            + jnp.sum(s, axis=1, keepdims=True)
        hout = _head_output(num, den, m_t, gout_ref[:, hh * DV:(hh + 1) * DV],
                            o_ref[:, hh * DV:(hh + 1) * DV])
        hg_ref[:, hh * DV:(hh + 1) * DV] = hout.astype(hg_ref.dtype)

        m_new = m_t[L - 1:L, :]
        b_last = b_col[L - 1:L, :]
        decay = jnp.exp(b_last + m_prev - m_new)
        wk = jnp.exp(b_last - b_col + ig_col - m_new) * k.astype(F32)
        c_scr[hh] = decay * cmat + jnp.dot(wk.T.astype(BF16), v, preferred_element_type=F32)
        n_scr[hh] = jnp.broadcast_to(decay * nvec + jnp.sum(wk, axis=0, keepdims=True), (8, DK))
        m_scr[hh] = jnp.broadcast_to(m_new, (8, GATE_LANES))

    @pl.when(c == nc - 1)
    def _():
        cout_ref[...] = c_scr[...]
        nout_ref[...] = n_scr[...]
        mout_ref[...] = m_scr[...]


def _scan(p, gates, bias, gout, c0, n0, m0, *, L, lead_pad=0):
    b, s, _ = p.shape
    nc = s // L
    kern = functools.partial(_scan_kernel, L=L, lead_pad=lead_pad, nc=nc)
    return pl.pallas_call(
        kern,
        grid=(b, nc),
        in_specs=[pl.BlockSpec((None, L, QKW), lambda i, c: (i, c, 0)),
                  pl.BlockSpec((None, L, QKW), lambda i, c: (i, c, 1)),
                  pl.BlockSpec((None, L, VW), lambda i, c: (i, c, 1)),
                  pl.BlockSpec((None, L, VW), lambda i, c: (i, c, 2)),
                  pl.BlockSpec((None, L, GATE_LANES), lambda i, c: (i, c, 0)),
                  pl.BlockSpec((1, GATE_LANES), lambda i, c: (0, 0)),
                  pl.BlockSpec((1, VW), lambda i, c: (0, 0)),
                  pl.BlockSpec((H, DK, DV), lambda i, c: (0, 0, 0)),
                  pl.BlockSpec((H, 8, DK), lambda i, c: (0, 0, 0)),
                  pl.BlockSpec((H, 8, GATE_LANES), lambda i, c: (0, 0, 0))],
        out_specs=[pl.BlockSpec((None, L, VW), lambda i, c: (i, c, 0)),
                   pl.BlockSpec((None, H, DK, DV), lambda i, c: (i, 0, 0, 0)),
                   pl.BlockSpec((None, H, 8, DK), lambda i, c: (i, 0, 0, 0)),
                   pl.BlockSpec((None, H, 8, GATE_LANES), lambda i, c: (i, 0, 0, 0))],
        out_shape=[jax.ShapeDtypeStruct((b, s, VW), BF16),
                   jax.ShapeDtypeStruct((b, H, DK, DV), F32),
                   jax.ShapeDtypeStruct((b, H, 8, DK), F32),
                   jax.ShapeDtypeStruct((b, H, 8, GATE_LANES), F32)],
        scratch_shapes=[pltpu.VMEM((H, DK, DV), F32),
                        pltpu.VMEM((H, 8, DK), F32),
                        pltpu.VMEM((H, 8, GATE_LANES), F32)],
        compiler_params=_params(2),
        name="scan",
    )(p, p, p, p, gates, bias, gout, c0, n0, m0)


def _scan_s_kernel(q_ref, k_ref, v_ref, o_ref, gt_ref, bias_ref, gout_ref, mtok_ref, c_ref, n_ref,
                   hg_ref, cout_ref, nout_ref, mout_ref, qc_scr, ntok_scr, *, T, NB):
    hh = pl.program_id(1) % H
    LT = NB * T
    gates = gt_ref[...] + bias_ref[...]
    gates_t = gates.T
    lane = lax.broadcasted_iota(jnp.int32, (LT, GATE_LANES), 1)
    sub = lax.broadcasted_iota(jnp.int32, (GATE_LANES, LT), 0)

    def pick_col(a, idx):
        return jnp.sum(jnp.where(lane == idx, a, 0.0), axis=1, keepdims=True)

    def pick_row(a, idx):
        return jnp.sum(jnp.where(sub == idx, a, 0.0), axis=0, keepdims=True)

    ig_col = pick_col(gates, hh)
    ig_row = pick_row(gates_t, hh)
    lf_col = _log_sigmoid(pick_col(gates, hh + H))
    lf_row = _log_sigmoid(pick_row(gates_t, hh + H))
    m_prev = pick_col(mtok_ref[...], hh)

    row = lax.broadcasted_iota(jnp.int32, (LT, LT), 0)
    col = lax.broadcasted_iota(jnp.int32, (LT, LT), 1)
    same = (row // T) == (col // T)
    causal = jnp.logical_and(same, row >= col)
    causal_t = jnp.logical_and(same, row <= col)
    b_col = jnp.sum(jnp.where(causal, lf_row, 0.0), axis=1, keepdims=True)
    b_row = jnp.sum(jnp.where(causal_t, lf_col, 0.0), axis=0, keepdims=True)
    b_end = jnp.sum(jnp.where(same, lf_row, 0.0), axis=1, keepdims=True)
    d = jnp.where(causal, b_col - b_row + ig_row, -jnp.inf)
    inter = b_col + m_prev
    m_t = jnp.maximum(inter, jnp.max(d, axis=1, keepdims=True))
    d_end = jnp.where(same, b_end - b_row + ig_row, -jnp.inf)
    m_new = jnp.maximum(b_end + m_prev, jnp.max(d_end, axis=1, keepdims=True))
    w_inter = jnp.exp(inter - m_t) * SCALE

    q32 = q_ref[...]
    k32 = k_ref[...]
    q = q32.astype(BF16)
    v = v_ref[...].astype(BF16)
    qk = lax.dot_general(q, k32.astype(BF16), (((1,), (1,)), ((), ())), preferred_element_type=F32)
    s = qk * (jnp.exp(d - m_t) * SCALE)
    num_intra = jnp.dot(s.astype(BF16), v, preferred_element_type=F32)
    den_intra = jnp.sum(s, axis=1, keepdims=True)

    decay = jnp.exp(b_end + m_prev - m_new)
    wk = jnp.exp(b_end - b_col + ig_col - m_new) * k32
    wk_t = wk.T
    col_seq = lax.broadcasted_iota(jnp.int32, (DK, LT), 1) // T

    for bb in range(NB):
        r0 = bb * T
        cmat = c_ref[bb]
        nvec = n_ref[bb:bb + 1, :]
        qc_scr[r0:r0 + T, :] = jnp.dot(q32[r0:r0 + T, :].astype(BF16), cmat.astype(BF16),
                                       preferred_element_type=F32)
        ntok_scr[r0:r0 + T, :] = jnp.broadcast_to(nvec, (T, DK))
        upd = jnp.dot(jnp.where(col_seq == bb, wk_t, 0.0).astype(BF16), v, preferred_element_type=F32)
        dec = decay[r0:r0 + 1, :]
        cout_ref[bb] = dec * cmat + upd
        nout_ref[bb:bb + 1, :] = dec * nvec + jnp.sum(wk[r0:r0 + T, :], axis=0, keepdims=True)
        mout_ref[bb:bb + 1, :] = jnp.broadcast_to(m_new[r0:r0 + 1, :], (1, GATE_LANES))

    num = w_inter * qc_scr[...] + num_intra
    den = w_inter * jnp.sum(q32 * ntok_scr[...], axis=1, keepdims=True) + den_intra
    hout = _head_output(num, den, m_t, gout_ref[...], o_ref[...]).astype(hg_ref.dtype)
    src_row = (row % NB) * T + row // NB
    perm = jnp.where(col == src_row, 1.0, 0.0).astype(hg_ref.dtype)
    moved = jnp.dot(perm, hout, preferred_element_type=F32)
    hg_ref[...] = moved.reshape(T, NB, DV).astype(hg_ref.dtype)


def _proj_scan_s_kernel(x_ref, g_ref, wt_ref, wgt_ref,
                        q_ref, k_ref, v_ref, o_ref, gt_ref, bias_ref, gout_ref, mtok_ref, c_ref, n_ref,
                        p_ref, gate_ref, hg_ref, cout_ref, nout_ref, mout_ref,
                        u_scr, qc_scr, ntok_scr, *, T, NB):
    _proj_kernel(x_ref, g_ref, wt_ref, wgt_ref, p_ref, gate_ref, u_scr, emit_bf16=False)
    _scan_s_kernel(q_ref, k_ref, v_ref, o_ref, gt_ref, bias_ref, gout_ref, mtok_ref, c_ref, n_ref,
                   hg_ref, cout_ref, nout_ref, mout_ref, qc_scr, ntok_scr, T=T, NB=NB)


def _proj_scan_s(x, gamma, wt, wgt, p_s, gates_s, bias, gout, mtok, c, n_hm, *, tm, T):
    m = x.shape[0]
    nseq = c.shape[0]
    nt = m // tm
    nj = 2 * H
    tn = PW // nj
    gpt = nj // H
    nb = nseq // (nt * gpt)
    assert nb * nt * gpt == nseq and nb % 8 == 0 and tn % LANES == 0
    lt = nb * T
    kern = functools.partial(_proj_scan_s_kernel, T=T, NB=nb)

    def grp(i, j):
        return i * gpt + j // H

    def head(j):
        return j % H

    return pl.pallas_call(
        kern,
        grid=(nt, nj),
        in_specs=[pl.BlockSpec((tm, D), lambda i, j: (i, 0), pipeline_mode=pl.Buffered(1)),
                  pl.BlockSpec((1, D), lambda i, j: (0, 0)),
                  pl.BlockSpec((tn, D), lambda i, j: (j, 0)),
                  pl.BlockSpec((GATE_LANES, D), lambda i, j: (0, 0)),
                  pl.BlockSpec((lt, DK), lambda i, j: (grp(i, j), head(j))),
                  pl.BlockSpec((lt, DK), lambda i, j: (grp(i, j), H + head(j))),
                  pl.BlockSpec((lt, DV), lambda i, j: (grp(i, j), H + head(j))),
                  pl.BlockSpec((lt, DV), lambda i, j: (grp(i, j), 2 * H + head(j))),
                  pl.BlockSpec((lt, GATE_LANES), lambda i, j: (grp(i, j), 0)),
                  pl.BlockSpec((1, GATE_LANES), lambda i, j: (0, 0)),
                  pl.BlockSpec((1, DV), lambda i, j: (0, head(j))),
                  pl.BlockSpec((lt, GATE_LANES), lambda i, j: (grp(i, j), 0)),
                  pl.BlockSpec((nb, None, DK, DV), lambda i, j: (grp(i, j), head(j), 0, 0)),
                  pl.BlockSpec((None, nb, DK), lambda i, j: (head(j), grp(i, j), 0))],
        out_specs=[pl.BlockSpec((tm, tn), lambda i, j: (i, j)),
                   pl.BlockSpec((tm, GATE_LANES), lambda i, j: (i, 0)),
                   pl.BlockSpec((T, nb, DV), lambda i, j: (0, grp(i, j), head(j))),
                   pl.BlockSpec((nb, None, DK, DV), lambda i, j: (grp(i, j), head(j), 0, 0)),
                   pl.BlockSpec((None, nb, DK), lambda i, j: (head(j), grp(i, j), 0)),
                   pl.BlockSpec((None, nb, GATE_LANES), lambda i, j: (head(j), grp(i, j), 0))],
        out_shape=[jax.ShapeDtypeStruct((m, PW), BF16),
                   jax.ShapeDtypeStruct((m, GATE_LANES), F32),
                   jax.ShapeDtypeStruct((T, nseq, VW), BF16),
                   jax.ShapeDtypeStruct((nseq, H, DK, DV), F32),
                   jax.ShapeDtypeStruct((H, nseq, DK), F32),
                   jax.ShapeDtypeStruct((H, nseq, GATE_LANES), F32)],
        scratch_shapes=[pltpu.VMEM((tm, D), BF16),
                        pltpu.VMEM((lt, DV), F32),
                        pltpu.VMEM((lt, DK), F32)],
        compiler_params=_params(2),
        name="proj_scan_sample",
    )(x, gamma, wt, wgt, p_s, p_s, p_s, p_s, gates_s, bias, gout, mtok, c, n_hm)


def _mmres_cast_kernel(a_ref, w_ref, x_ref, o_ref, wc_ref):
    w = w_ref[...].astype(BF16)
    wc_ref[...] = w
    y = jnp.dot(a_ref[...], w, preferred_element_type=F32)
    nseq, t_dec, _ = x_ref.shape
    for t in range(t_dec):
        o_ref[t * nseq:(t + 1) * nseq, :] = x_ref[:, t, :] + y[t * nseq:(t + 1) * nseq, :]


def _mmres_cast(a, w, x, *, tn):
    m, kdim = a.shape
    n = w.shape[2]
    nseq, t_dec, _ = x.shape
    return pl.pallas_call(
        _mmres_cast_kernel,
        grid=(n // tn,),
        in_specs=[pl.BlockSpec((m, kdim), lambda j: (0, 0)),
                  pl.BlockSpec((None, kdim, tn), lambda j: (0, 0, j)),
                  pl.BlockSpec((nseq, t_dec, tn), lambda j: (0, 0, j))],
        out_specs=[pl.BlockSpec((m, tn), lambda j: (0, j)),
                   pl.BlockSpec((kdim, tn), lambda j: (0, j))],
        out_shape=[jax.ShapeDtypeStruct((m, n), F32),
                   jax.ShapeDtypeStruct((kdim, n), BF16)],
        compiler_params=_params(1),
        name="out_proj_cast",
    )(a, w, x)


def _mmres_kernel(a_ref, w_ref, x_ref, o_ref):
    o_ref[...] = x_ref[...] + jnp.dot(a_ref[...], w_ref[...], preferred_element_type=F32)


def _mmres(a, w, x, *, tm):
    m, kdim = a.shape
    n = w.shape[1]
    return pl.pallas_call(
        _mmres_kernel,
        grid=(m // tm,),
        in_specs=[pl.BlockSpec((tm, kdim), lambda i: (i, 0)),
                  pl.BlockSpec((kdim, n), lambda i: (0, 0)),
                  pl.BlockSpec((tm, n), lambda i: (i, 0))],
        out_specs=pl.BlockSpec((tm, n), lambda i: (i, 0)),
        out_shape=jax.ShapeDtypeStruct((m, n), F32),
        compiler_params=_params(1),
        name="out_proj",
    )(a, w, x)


def _ffn_kernel(x_ref, gam_ref, wg_ref, wa0_ref, wa1_ref, cw_ref, cb_ref, wd_ref, st_ref, gfin_ref,
                o_ref, so_ref, wupc_ref, wdnc_ref,
                u_scr, gext_scr, a_scr, wup0_scr, wup1_scr, wdn0_scr, wdn1_scr,
                *, tm, shift, nsub, final_norm):
    s = pl.program_id(1)
    sr = CONV_TAIL * shift
    wup_slots = (wup0_scr, wup1_scr)
    wdn_slots = (wdn0_scr, wdn1_scr)

    def cast_steps(slot):
        wup_scr = wup_slots[slot]
        wdn_scr = wdn_slots[slot]
        valid = D_FF - jnp.minimum(s, NF - 1) * TF

        def cast_gate():
            ok = lax.broadcasted_iota(jnp.int32, (1, TF), 1) < valid
            w = jnp.where(ok, wg_ref[...], 0.0).astype(BF16)
            wup_scr[:, 0:TF] = w
            wupc_ref[:, 0:TF] = w

        def cast_value():
            lane = lax.broadcasted_iota(jnp.int32, (1, LANES), 1)
            w0 = jnp.where(lane < valid, wa0_ref[...], 0.0).astype(BF16)
            w1 = jnp.where(lane + LANES < valid, wa1_ref[...], 0.0).astype(BF16)
            wup_scr[:, TF:TF + LANES] = w0
            wup_scr[:, TF + LANES:2 * TF] = w1
            wupc_ref[:, TF:TF + LANES] = w0
            wupc_ref[:, TF + LANES:2 * TF] = w1

        def cast_down(r0, rows):
            row_ok = r0 + lax.broadcasted_iota(jnp.int32, (rows, 1), 0) < valid
            w = jnp.where(row_ok, wd_ref[r0:r0 + rows, :], 0.0).astype(BF16)
            wdn_scr[r0:r0 + rows, :] = w
            wdnc_ref[r0:r0 + rows, :] = w

        half = TF // 2
        return [cast_gate, cast_value, functools.partial(cast_down, 0, half),
                functools.partial(cast_down, half, half)]

    def run_tile(slot, fillers=()):
        fillers = list(fillers)

        def fill():
            if fillers:
                fillers.pop(0)()

        wup_scr = wup_slots[slot]
        wdn_scr = wdn_slots[slot]
        f = s - 1
        col_ok = lax.broadcasted_iota(jnp.int32, (1, TF), 1) < D_FF - f * TF
        cw = cw_ref[...]
        cb = cb_ref[...]
        for t in range(CONV_TAIL):
            gext_scr[t * shift:(t + 1) * shift, :] = st_ref[:, t, :]
        ts = tm // nsub
        for h in range(nsub):
            r0 = h * ts
            ga = jnp.dot(u_scr[r0:r0 + ts, :], wup_scr[...], preferred_element_type=F32)
            gext_scr[sr + r0:sr + r0 + ts, :] = ga[:, 0:TF]
            a_scr[r0:r0 + ts, :] = ga[:, TF:2 * TF]
            fill()
        for h in range(nsub):
            r0 = h * ts
            g = gext_scr[sr + r0:sr + r0 + ts, :]
            g_m2 = gext_scr[sr + r0 - 2 * shift:sr + r0 - 2 * shift + ts, :]
            g_m1 = gext_scr[sr + r0 - shift:sr + r0 - shift + ts, :]
            gc = cb + ((cw[0:1, :] * g_m2 + cw[1:2, :] * g_m1) + cw[2:3, :] * g)
            hmid = jnp.where(col_ok, (gc * jax.nn.sigmoid(gc)) * a_scr[r0:r0 + ts, :], 0.0)
            o_ref[r0:r0 + ts, :] += jnp.dot(hmid.astype(BF16), wdn_scr[...], preferred_element_type=F32)
            fill()
        while fillers:
            fill()
        for t in range(CONV_TAIL):
            so_ref[:, t, :] = gext_scr[tm + t * shift:tm + (t + 1) * shift, :]

    @pl.when(s == 0)
    def _():
        x = x_ref[...]
        u_scr[...] = _rms(x, gam_ref[...]).astype(BF16)
        o_ref[...] = x
        for step in cast_steps(0):
            step()

    for parity in range(2):
        @pl.when(jnp.logical_and(s > 0, s % 2 == parity))
        def _(parity=parity):
            run_tile(1 - parity, cast_steps(parity))

    if final_norm:
        @pl.when(s == NF)
        def _():
            o_ref[...] = _rms(o_ref[...], gfin_ref[...])


def _ffn(x, layer, gamma, w_up, conv_w, conv_b, w_down, st, gfin, *, final_norm):
    assert TF == 2 * LANES
    tm = x.shape[0]
    shift = st.shape[0]
    sr = CONV_TAIL * shift
    kern = functools.partial(_ffn_kernel, tm=tm, shift=shift, nsub=2, final_norm=final_norm)
    last_a = 2 * FF_LANE_BLOCKS - 1

    def wt(s):
        return jnp.minimum(s, NF - 1)

    def ft(s):
        return jnp.maximum(s - 1, 0)

    st_spec = pl.BlockSpec((shift, CONV_TAIL, TF), lambda i, s: (0, 0, ft(s)))
    return pl.pallas_call(
        kern,
        grid=(1, NF + 1),
        in_specs=[pl.BlockSpec((tm, D), lambda i, s: (i, 0), pipeline_mode=pl.Buffered(1)),
                  pl.BlockSpec((1, D), lambda i, s: (0, 0)),
                  pl.BlockSpec((None, D, TF), lambda i, s: (layer, 0, wt(s))),
                  pl.BlockSpec((None, D, LANES), lambda i, s: (layer, 0, FF_LANE_BLOCKS + 2 * wt(s))),
                  pl.BlockSpec((None, D, LANES),
                               lambda i, s: (layer, 0, jnp.minimum(FF_LANE_BLOCKS + 2 * wt(s) + 1, last_a))),
                  pl.BlockSpec((None, 3, TF), lambda i, s: (layer, 0, ft(s))),
                  pl.BlockSpec((None, 1, TF), lambda i, s: (layer, 0, ft(s))),
                  pl.BlockSpec((None, TF, D), lambda i, s: (layer, wt(s), 0)),
                  st_spec,
                  pl.BlockSpec((1, D), lambda i, s: (0, 0))],
        out_specs=[pl.BlockSpec((tm, D), lambda i, s: (i, 0)),
                   st_spec,
                   pl.BlockSpec((D, 2 * TF), lambda i, s: (0, wt(s))),
                   pl.BlockSpec((TF, D), lambda i, s: (wt(s), 0))],
        out_shape=[jax.ShapeDtypeStruct((tm, D), F32),
                   jax.ShapeDtypeStruct(st.shape, F32),
                   jax.ShapeDtypeStruct((D, NF * 2 * TF), BF16),
                   jax.ShapeDtypeStruct((NF * TF, D), BF16)],
        scratch_shapes=[pltpu.VMEM((tm, D), BF16),
                        pltpu.VMEM((sr + tm, TF), F32),
                        pltpu.VMEM((tm, TF), F32),
                        pltpu.VMEM((D, 2 * TF), BF16),
                        pltpu.VMEM((D, 2 * TF), BF16),
                        pltpu.VMEM((TF, D), BF16),
                        pltpu.VMEM((TF, D), BF16)],
        compiler_params=_params(2),
        name="conv_ffn_cast",
    )(x, gamma, w_up, w_up, w_up, conv_w, conv_b, w_down, st, gfin)


def _ffn_bf16_kernel(x_ref, gam_ref, wup_ref, cw_ref, cb_ref, wdn_ref, st_ref, gfin_ref,
                     o_ref, so_ref, u_scr, gext_scr, a_scr, carry_scr,
                     *, tm, sr, shift, tps, nsub, final_norm, step=None):
    i = pl.program_id(0)
    f = pl.program_id(1) if step is None else step
    tf2 = 2 * TF

    @pl.when(f == 0)
    def _():
        x = x_ref[...]
        u_scr[...] = _rms(x, gam_ref[...]).astype(BF16)
        o_ref[...] = x
        if tps > 1:
            @pl.when(i == 0)
            def _():
                carry_scr[...] = jnp.zeros_like(carry_scr)

    col_ok = lax.broadcasted_iota(jnp.int32, (1, tf2), 1) < D_FF - f * tf2
    cw = cw_ref[...]
    cb = cb_ref[...]
    if tps == 1:
        gext_scr[0:sr, :] = st_ref[...]
    else:
        gext_scr[0:sr, :] = jnp.where((i % tps) == 0, st_ref[...], carry_scr[f])
    ts = tm // nsub
    for h in range(nsub):
        r0 = h * ts
        ga = jnp.dot(u_scr[r0:r0 + ts, :], wup_ref[...], preferred_element_type=F32)
        gext_scr[sr + r0:sr + r0 + ts, 0:TF] = ga[:, 0:TF]
        gext_scr[sr + r0:sr + r0 + ts, TF:tf2] = ga[:, 2 * TF:3 * TF]
        a_scr[r0:r0 + ts, 0:TF] = ga[:, TF:2 * TF]
        a_scr[r0:r0 + ts, TF:tf2] = ga[:, 3 * TF:4 * TF]
    for h in range(nsub):
        r0 = h * ts
        g = gext_scr[sr + r0:sr + r0 + ts, :]
        g_m2 = gext_scr[sr + r0 - 2 * shift:sr + r0 - 2 * shift + ts, :]
        g_m1 = gext_scr[sr + r0 - shift:sr + r0 - shift + ts, :]
        gc = cb + ((cw[0:1, :] * g_m2 + cw[1:2, :] * g_m1) + cw[2:3, :] * g)
        hmid = jnp.where(col_ok, (gc * jax.nn.sigmoid(gc)) * a_scr[r0:r0 + ts, :], 0.0)
        o_ref[r0:r0 + ts, :] += jnp.dot(hmid.astype(BF16), wdn_ref[...], preferred_element_type=F32)
    g_tail = gext_scr[tm:tm + sr, :]
    so_ref[...] = g_tail
    if tps > 1:
        carry_scr[f] = g_tail

    if final_norm:
        @pl.when(f == pl.num_programs(1) - 1)
        def _():
            o_ref[...] = _rms(o_ref[...], gfin_ref[...])


def _ffn_bf16(x, layer, gamma, wupc, conv_w, conv_b, wdnc, st, gfin, *, tm, sr, shift, tps, st_per_tile,
              final_norm):
    assert NF % 2 == 0
    m = x.shape[0]
    nt = m // tm
    tf2 = 2 * TF
    nf2 = NF // 2
    kern = functools.partial(_ffn_bf16_kernel, tm=tm, sr=sr, shift=shift, tps=tps,
                             nsub=2 if tm >= 64 else 1, final_norm=final_norm)
    st_map = (lambda i, f: (i, f)) if st_per_tile else (lambda i, f: (0, f))
    return pl.pallas_call(
        kern,
        grid=(nt, nf2),
        in_specs=[pl.BlockSpec((tm, D), lambda i, f: (i, 0)),
                  pl.BlockSpec((1, D), lambda i, f: (0, 0)),
                  pl.BlockSpec((D, 2 * tf2), lambda i, f: (0, f)),
                  pl.BlockSpec((None, 3, tf2), lambda i, f: (layer, 0, f)),
                  pl.BlockSpec((None, 1, tf2), lambda i, f: (layer, 0, f)),
                  pl.BlockSpec((tf2, D), lambda i, f: (f, 0)),
                  pl.BlockSpec((sr, tf2), st_map),
                  pl.BlockSpec((1, D), lambda i, f: (0, 0))],
        out_specs=[pl.BlockSpec((tm, D), lambda i, f: (i, 0)),
                   pl.BlockSpec((sr, tf2), lambda i, f: (i, f))],
        out_shape=[jax.ShapeDtypeStruct((m, D), F32),
                   jax.ShapeDtypeStruct((nt * sr, D_FF), F32)],
        scratch_shapes=[pltpu.VMEM((tm, D), BF16),
                        pltpu.VMEM((sr + tm, tf2), F32),
                        pltpu.VMEM((tm, tf2), F32),
                        pltpu.VMEM((nf2, sr, tf2), F32)],
        compiler_params=_params(2),
        name="conv_ffn",
    )(x, gamma, wupc, conv_w, conv_b, wdnc, st, gfin)


def _scan_ffn_kernel(*refs, L, nc, tm_r):
    scan_in, ffn_in = refs[0:10], refs[10:18]
    scan_out, ffn_out = refs[18:22], refs[22:24]
    scan_scr, ffn_scr = refs[24:27], refs[27:31]
    _scan_kernel(*scan_in, *scan_out, *scan_scr, L=L, lead_pad=0, nc=nc)
    step = pl.program_id(0) * nc + pl.program_id(1)

    @pl.when(step < NF // 2)
    def _():
        _ffn_bf16_kernel(*ffn_in, *ffn_out, *ffn_scr, tm=tm_r, sr=8, shift=1, tps=1, nsub=1,
                         final_norm=False, step=step)


def _scan_ffn(p, gates, bias, gout, c0, n0, m0, xr, layer, gamma, wupc, conv_w, conv_b, wdnc, st, gfin, *, L):
    b, s, _ = p.shape
    nc = s // L
    tm_r = xr.shape[0]
    tf2 = 2 * TF
    nf2 = NF // 2
    assert b * nc >= nf2

    def fi(i, c):
        return jnp.minimum(i * nc + c, nf2 - 1)

    kern = functools.partial(_scan_ffn_kernel, L=L, nc=nc, tm_r=tm_r)
    return pl.pallas_call(
        kern,
        grid=(b, nc),
        in_specs=[pl.BlockSpec((None, L, QKW), lambda i, c: (i, c, 0)),
                  pl.BlockSpec((None, L, QKW), lambda i, c: (i, c, 1)),
                  pl.BlockSpec((None, L, VW), lambda i, c: (i, c, 1)),
                  pl.BlockSpec((None, L, VW), lambda i, c: (i, c, 2)),
                  pl.BlockSpec((None, L, GATE_LANES), lambda i, c: (i, c, 0)),
                  pl.BlockSpec((1, GATE_LANES), lambda i, c: (0, 0)),
                  pl.BlockSpec((1, VW), lambda i, c: (0, 0)),
                  pl.BlockSpec((H, DK, DV), lambda i, c: (0, 0, 0)),
                  pl.BlockSpec((H, 8, DK), lambda i, c: (0, 0, 0)),
                  pl.BlockSpec((H, 8, GATE_LANES), lambda i, c: (0, 0, 0)),
                  pl.BlockSpec((tm_r, D), lambda i, c: (0, 0)),
                  pl.BlockSpec((1, D), lambda i, c: (0, 0)),
                  pl.BlockSpec((D, 2 * tf2), lambda i, c: (0, fi(i, c))),
                  pl.BlockSpec((None, 3, tf2), lambda i, c: (layer, 0, fi(i, c))),
                  pl.BlockSpec((None, 1, tf2), lambda i, c: (layer, 0, fi(i, c))),
                  pl.BlockSpec((tf2, D), lambda i, c: (fi(i, c), 0)),
                  pl.BlockSpec((8, tf2), lambda i, c: (0, fi(i, c))),
                  pl.BlockSpec((1, D), lambda i, c: (0, 0))],
        out_specs=[pl.BlockSpec((None, L, VW), lambda i, c: (i, c, 0)),
                   pl.BlockSpec((None, H, DK, DV), lambda i, c: (i, 0, 0, 0)),
                   pl.BlockSpec((None, H, 8, DK), lambda i, c: (i, 0, 0, 0)),
                   pl.BlockSpec((None, H, 8, GATE_LANES), lambda i, c: (i, 0, 0, 0)),
                   pl.BlockSpec((tm_r, D), lambda i, c: (0, 0)),
                   pl.BlockSpec((8, tf2), lambda i, c: (0, fi(i, c)))],
        out_shape=[jax.ShapeDtypeStruct((b, s, VW), BF16),
                   jax.ShapeDtypeStruct((b, H, DK, DV), F32),
                   jax.ShapeDtypeStruct((b, H, 8, DK), F32),
                   jax.ShapeDtypeStruct((b, H, 8, GATE_LANES), F32),
                   jax.ShapeDtypeStruct((tm_r, D), F32),
                   jax.ShapeDtypeStruct((8, D_FF), F32)],
        scratch_shapes=[pltpu.VMEM((H, DK, DV), F32),
                        pltpu.VMEM((H, 8, DK), F32),
                        pltpu.VMEM((H, 8, GATE_LANES), F32),
                        pltpu.VMEM((tm_r, D), BF16),
                        pltpu.VMEM((8 + tm_r, tf2), F32),
                        pltpu.VMEM((tm_r, tf2), F32),
                        pltpu.VMEM((1, 8, tf2), F32)],
        compiler_params=_params(2),
        name="scan_meta_ffn",
    )(p, p, p, p, gates, bias, gout, c0, n0, m0, xr, gamma, wupc, conv_w, conv_b, wdnc, st, gfin)


def _mmres_ffn_kernel(*refs, tm_r):
    a_ref, w_ref, x_ref = refs[0:3]
    ffn_in = refs[3:11]
    o_ref = refs[11]
    ffn_out = refs[12:14]
    ffn_scr = refs[14:18]
    _mmres_kernel(a_ref, w_ref, x_ref, o_ref)
    step = pl.program_id(0)

    @pl.when(step < NF // 2)
    def _():
        _ffn_bf16_kernel(*ffn_in, *ffn_out, *ffn_scr, tm=tm_r, sr=8, shift=1, tps=1, nsub=1,
                         final_norm=False, step=step)


def _mmres_ffn(a, w, x, xr, layer, gamma, wupc, conv_w, conv_b, wdnc, st, gfin, *, tm):
    m, kdim = a.shape
    n = w.shape[1]
    tm_r = xr.shape[0]
    tf2 = 2 * TF
    nf2 = NF // 2
    assert m // tm >= nf2

    def fi(i):
        return jnp.minimum(i, nf2 - 1)

    return pl.pallas_call(
        functools.partial(_mmres_ffn_kernel, tm_r=tm_r),
        grid=(m // tm,),
        in_specs=[pl.BlockSpec((tm, kdim), lambda i: (i, 0)),
                  pl.BlockSpec((kdim, n), lambda i: (0, 0)),
                  pl.BlockSpec((tm, n), lambda i: (i, 0)),
                  pl.BlockSpec((tm_r, D), lambda i: (0, 0)),
                  pl.BlockSpec((1, D), lambda i: (0, 0)),
                  pl.BlockSpec((D, 2 * tf2), lambda i: (0, fi(i))),
                  pl.BlockSpec((None, 3, tf2), lambda i: (layer, 0, fi(i))),
                  pl.BlockSpec((None, 1, tf2), lambda i: (layer, 0, fi(i))),
                  pl.BlockSpec((tf2, D), lambda i: (fi(i), 0)),
                  pl.BlockSpec((8, tf2), lambda i: (0, fi(i))),
                  pl.BlockSpec((1, D), lambda i: (0, 0))],
        out_specs=[pl.BlockSpec((tm, n), lambda i: (i, 0)),
                   pl.BlockSpec((tm_r, D), lambda i: (0, 0)),
                   pl.BlockSpec((8, tf2), lambda i: (0, fi(i)))],
        out_shape=[jax.ShapeDtypeStruct((m, n), F32),
                   jax.ShapeDtypeStruct((tm_r, D), F32),
                   jax.ShapeDtypeStruct((8, D_FF), F32)],
        scratch_shapes=[pltpu.VMEM((tm_r, D), BF16),
                        pltpu.VMEM((8 + tm_r, tf2), F32),
                        pltpu.VMEM((tm_r, tf2), F32),
                        pltpu.VMEM((1, 8, tf2), F32)],
        compiler_params=_params(1),
        name="out_proj_meta_ffn",
    )(a, w, x, xr, gamma, wupc, conv_w, conv_b, wdnc, st, gfin)


def _window_sum(ext_scr, tmp_scrs, base, tm, shift, w):
    starts = {w: base}
    v = w
    while v > 2:
        starts[v // 2] = (starts[v] - (v // 2) * shift) // 8 * 8
        v //= 2
    src = ext_scr
    v = 1
    k = 0
    while True:
        lo = starts[2 * v]
        n = base + tm - lo
        val = src[lo:lo + n, :] + src[lo - v * shift:lo - v * shift + n, :]
        v *= 2
        if v == w:
            return val
        dst = tmp_scrs[k % 2]
        dst[lo:lo + n, :] = val
        src = dst
        k += 1


def _pool_kernel(x_ref, gam_ref, wp_ref, sc_ref, st_ref, o_ref, ut_ref, *scratch, tm, shift, tps, pos0, tr):
    ng = len(POOL_WINDOWS)
    uext_scrs = scratch[0:ng]
    tmp_scrs = scratch[ng:ng + 2]
    carry_scr = scratch[ng + 2]
    i = pl.program_id(0)
    hr = POOL_HALO * shift
    base = POOL_PAD_ROWS + hr

    if tps > 1:
        @pl.when(i == 0)
        def _():
            carry_scr[...] = jnp.zeros_like(carry_scr)

    x = x_ref[...]
    rinv = lax.rsqrt(jnp.mean(x * x, axis=1, keepdims=True) + EPS)
    for kk, w in enumerate(POOL_WINDOWS):
        cs = slice(kk * PG, (kk + 1) * PG)
        uext_scr = uext_scrs[kk]
        xg = x_ref[:, cs]
        ug = xg * rinv * gam_ref[:, cs]
        uext_scr[0:POOL_PAD_ROWS, :] = jnp.zeros((POOL_PAD_ROWS, PG), F32)
        if tps == 1:
            uext_scr[POOL_PAD_ROWS:base, :] = st_ref[:, cs]
        else:
            uext_scr[POOL_PAD_ROWS:base, :] = jnp.where((i % tps) == 0, st_ref[:, cs], carry_scr[:, cs])
        uext_scr[base:base + tm, :] = ug
        acc = _window_sum(uext_scr, tmp_scrs, base, tm, shift, w)
        if pos0 + 1 >= w:
            pooled = acc / float(w) - ug
        else:
            step = (i % tps) * (tm // shift) + lax.broadcasted_iota(jnp.int32, (tm, 1), 0) // shift
            cnt = jnp.minimum(w, pos0 + step + 1).astype(F32)
            pooled = acc / cnt - ug
        y = jnp.dot(pooled.astype(BF16), wp_ref[kk].astype(BF16), preferred_element_type=F32)
        o_ref[:, cs] = xg + y * sc_ref[:, cs]
        ut_ref[:, cs] = ug[tm - tr:tm, :]
        if tps > 1:
            carry_scr[:, cs] = ug[tm - hr:tm, :]


def _pool(x, gamma, wp, sc, st, *, tm, shift, tps, pos0, tr):
    m = x.shape[0]
    nt = m // tm
    hr = POOL_HALO * shift
    ng = len(POOL_WINDOWS)
    kern = functools.partial(_pool_kernel, tm=tm, shift=shift, tps=tps, pos0=pos0, tr=tr)
    return pl.pallas_call(
        kern,
        grid=(nt,),
        in_specs=[pl.BlockSpec((tm, D), lambda i: (i, 0)),
                  pl.BlockSpec((1, D), lambda i: (0, 0)),
                  pl.BlockSpec((None, ng, PG, PG), lambda i: (0, 0, 0, 0)),
                  pl.BlockSpec((1, D), lambda i: (0, 0)),
                  pl.BlockSpec((hr, D), lambda i: (0, 0))],
        out_specs=[pl.BlockSpec((tm, D), lambda i: (i, 0)),
                   pl.BlockSpec((tr, D), lambda i: (i, 0))],
        out_shape=[jax.ShapeDtypeStruct((m, D), F32),
                   jax.ShapeDtypeStruct((nt * tr, D), F32)],
        scratch_shapes=[pltpu.VMEM((POOL_PAD_ROWS + hr + tm, PG), F32)] * (ng + 2)
        + [pltpu.VMEM((hr, D), F32)],
        compiler_params=_params(1),
        name="pool_mixer",
    )(x, gamma, wp, sc, st)


def _pool_sample_kernel(x_ref, gam_ref, wp_ref, sc_ref, st_ref, o_ref, new_ref, rinv_scr, uext_scr,
                        *, nseq, t_dec):
    nh = POOL_HALO - 1
    grp = pl.program_id(0)
    tm = nseq * t_dec
    hr = POOL_HALO * nseq

    @pl.when(grp == 0)
    def _():
        x = x_ref[...]
        rinv_scr[...] = lax.rsqrt(jnp.mean(x * x, axis=1, keepdims=True) + EPS)

    for kk, w in enumerate(POOL_WINDOWS):
        @pl.when(grp == kk)
        def _(kk=kk, w=w):
            cs = slice(kk * PG, (kk + 1) * PG)
            xg = x_ref[:, cs]
            ug = xg * rinv_scr[...] * gam_ref[:, cs]
            for t in range(nh):
                uext_scr[(t + 1) * nseq:(t + 2) * nseq, :] = st_ref[t]
            uext_scr[hr:hr + tm, :] = ug
            acc = ug
            for j in range(1, w):
                acc = acc + uext_scr[hr - j * nseq:hr - j * nseq + tm, :]
            pooled = acc / float(w) - ug
            y = jnp.dot(pooled.astype(BF16), wp_ref[kk].astype(BF16), preferred_element_type=F32)
            o_ref[...] = xg + y * sc_ref[:, cs]
            for t in range(nh):
                src = (t + t_dec + 1) * nseq
                new_ref[t] = uext_scr[src:src + nseq, :]


def _pool_sample(x, gamma, wp, sc, state, *, nseq, t_dec):
    assert PAST_LEN + 1 >= max(POOL_WINDOWS)
    nh = POOL_HALO - 1
    ng = len(POOL_WINDOWS)
    tm = nseq * t_dec
    hr = POOL_HALO * nseq
    kern = functools.partial(_pool_sample_kernel, nseq=nseq, t_dec=t_dec)
    hist_spec = pl.BlockSpec((nh, nseq, PG), lambda g: (0, 0, g))
    return pl.pallas_call(
        kern,
        grid=(ng,),
        in_specs=[pl.BlockSpec((tm, D), lambda g: (0, 0)),
                  pl.BlockSpec((1, D), lambda g: (0, 0)),
                  pl.BlockSpec((None, ng, PG, PG), lambda g: (0, 0, 0, 0)),
                  pl.BlockSpec((1, D), lambda g: (0, 0)),
                  hist_spec],
        out_specs=[pl.BlockSpec((tm, PG), lambda g: (0, g)),
                   hist_spec],
        out_shape=[jax.ShapeDtypeStruct((tm, D), F32),
                   jax.ShapeDtypeStruct((nh, nseq, D), F32)],
        scratch_shapes=[pltpu.VMEM((tm, 1), F32),
                        pltpu.VMEM((hr + tm, PG), F32)],
        compiler_params=_params(1),
        name="pool_mixer_sample",
    )(x, gamma, wp, sc, state)


def _pad_cols(a, n):
    return jnp.pad(a, ((0, 0), (0, n - a.shape[1])))


def _from_time_major(a, nseq, t):
    return jnp.swapaxes(a.reshape((t, nseq) + a.shape[1:]), 0, 1)


def kernel(x_prompt, x_sample, state_mlstm_C, state_mlstm_n, state_mlstm_m, state_pool, state_ffn_conv,
           meta_tokens, norm_mix, norm_ffn, norm_final, w_mlstm_in, b_mlstm_gate, g_mlstm_out, w_mlstm_out,
           w_pool, pool_scale, w_up, conv_w, conv_b, w_down):
    bsz, seq, _ = x_prompt.shape
    nseq, t_dec, _ = x_sample.shape

    w_in_t = jnp.swapaxes(w_mlstm_in[0], 0, 1)
    bias = _pad_cols(b_mlstm_gate[0][None, :], GATE_LANES)
    w_out = w_mlstm_out
    gout = g_mlstm_out[0][None, :]
    wp = w_pool
    psc = pool_scale[0][None, :]
    gfin = norm_final[None, :]
    conv_b3 = conv_b[:, None, :]
    ffn_w = [(layer, norm_ffn[layer][None, :], w_up, conv_w, conv_b3, w_down) for layer in range(2)]
    g_mix0 = norm_mix[0][None, :]
    g_mix1 = norm_mix[1][None, :]

    ffn_cache = {}
    mix_cache = {}

    def proj(x, **kw):
        if "w_in" not in mix_cache:
            p, gates, wtc, wgc = _proj(x, g_mix0, w_in_t, w_in_t, **kw)
            mix_cache["w_in"] = (wtc, wgc)
            return p, gates
        return _proj(x, g_mix0, *mix_cache["w_in"], **kw)

    def out_proj(a, x, *, tm):
        if "w_out" not in mix_cache:
            y, mix_cache["w_out"] = _mmres_cast(a, w_out, x, tn=512)
            return y
        return _mmres(a, mix_cache["w_out"], x, tm=tm)

    def ffn_first(x, layer, st, *, final_norm):
        lyr, gamma, wu, cwt, cbs, wd = ffn_w[layer]
        y, cs, wupc, wdnc = _ffn(x, lyr, gamma, wu, cwt, cbs, wd, st, gfin, final_norm=final_norm)
        ffn_cache[layer] = (wupc, wdnc)
        return y, cs

    def ffn(x, layer, st, **kw):
        lyr, gamma, _, cwt, cbs, _ = ffn_w[layer]
        wupc, wdnc = ffn_cache[layer]
        return _ffn_bf16(x, lyr, gamma, wupc, cwt, cbs, wdnc, st, gfin, **kw)

    def long_stream(x, nb, s, st, *, tm, chunk, lead_pad, pos0, projected=None, rider=None):
        c0, n0, m0, conv0, pool0, conv1 = st
        tm_ffn = tm
        p, gates = projected if projected is not None else proj(x, tm=tm, tn=1024, out_dtype=BF16)
        p = p.reshape(nb, s, PW)
        gates = gates.reshape(nb, s, GATE_LANES)
        if lead_pad:
            p = jnp.pad(p, ((0, 0), (lead_pad, 0), (0, 0)))
            gates = jnp.pad(gates, ((0, 0), (lead_pad, 0), (0, 0)))
        if rider is None:
            hg, c_new, n_new, m_new = _scan(p, gates, bias, gout, c0, n0, m0, L=chunk, lead_pad=lead_pad)
        else:
            lyr, gamma, _, cwt, cbs, _ = ffn_w[0]
            wupc, wdnc = ffn_cache[0]
            hg, c_new, n_new, m_new, x2_m, cs0_m = _scan_ffn(
                p, gates, bias, gout, c0, n0, m0, rider, lyr, gamma, wupc, cwt, cbs, wdnc,
                jnp.zeros((8, D_FF), F32), gfin, L=chunk)
            x3_m, pool0 = _pool(x2_m, g_mix1, wp, psc, jnp.zeros((POOL_HALO, D), F32), tm=N_META, shift=1,
                                tps=1, pos0=0, tr=POOL_HALO)
            conv0 = cs0_m
        hg = hg[:, lead_pad:].reshape(nb * s, VW)
        if rider is None:
            x1 = out_proj(hg, x, tm=min(tm, 512))
        else:
            lyr, gamma, _, cwt, cbs, _ = ffn_w[1]
            wupc, wdnc = ffn_cache[1]
            x1, _, conv1 = _mmres_ffn(hg, mix_cache["w_out"], x, x3_m, lyr, gamma, wupc, cwt, cbs, wdnc,
                                      jnp.zeros((8, D_FF), F32), gfin, tm=min(tm, 512))
        x2, cs0 = ffn(x1, 0, conv0, tm=tm_ffn, sr=8, shift=1, tps=s // tm_ffn,
                      st_per_tile=False, final_norm=False)
        x3, ut = _pool(x2, g_mix1, wp, psc, pool0, tm=tm, shift=1, tps=s // tm, pos0=pos0,
                       tr=POOL_HALO)
        y, cs1 = ffn(x3, 1, conv1, tm=tm_ffn, sr=8, shift=1, tps=s // tm_ffn,
                     st_per_tile=False, final_norm=True)
        cs0 = cs0.reshape(nb, s // tm_ffn, 8, D_FF)[:, -1]
        cs1 = cs1.reshape(nb, s // tm_ffn, 8, D_FF)[:, -1]
        ut = ut.reshape(nb, s // tm, POOL_HALO, D)[:, -1]
        return y, (c_new, n_new, m_new, cs0, ut, cs1)

    def prompt_streams():
        lead = 128 - N_META
        p_m, gates_m = proj(meta_tokens, tm=N_META, tn=1024, out_dtype=BF16)
        hg_m, c_m, n_m, m_m = _scan(jnp.pad(p_m, ((lead, 0), (0, 0)))[None], jnp.pad(gates_m, ((lead, 0), (0, 0)))[None],
                                    bias, gout, jnp.zeros((H, DK, DV), F32), jnp.zeros((H, 8, DK), F32),
                                    jnp.zeros((H, 8, GATE_LANES), F32), L=128, lead_pad=lead)
        x1_m = out_proj(hg_m[0, lead:], meta_tokens, tm=N_META)

        y_p, (c_p, n_p, m_p, cs0_p, ut_p, cs1_p) = long_stream(
            x_prompt.reshape(bsz * seq, D), bsz, seq, (c_m[0], n_m[0], m_m[0], None, None, None),
            tm=1024, chunk=SCAN_CHUNK, lead_pad=0, pos0=N_META, projected=prompt_projected, rider=x1_m)
        return (y_p.reshape(bsz, seq, D), c_p[None], n_p[:, :, 0][None], m_p[:, :, 0, 0][None],
                ut_p[:, 1:][None], jnp.stack([cs0_p[:, 6:], cs1_p[:, 6:]]))

    xs = x_sample.reshape(nseq * t_dec, D)
    p_s, gates_s = proj(xs, tm=nseq * t_dec, tn=512, out_dtype=F32)
    mtok = _pad_cols(jnp.repeat(state_mlstm_m[0], t_dec, axis=0), GATE_LANES)
    n_hm = jnp.swapaxes(state_mlstm_n[0], 0, 1)
    p_p, gates_p, hg_s, C_s, n_s_hm, m_s_hm = _proj_scan_s(
        x_prompt.reshape(bsz * seq, D), g_mix0, *mix_cache["w_in"], p_s, gates_s, bias, gout, mtok,
        state_mlstm_C[0], n_hm, tm=1024, T=t_dec)
    prompt_projected = (p_p, gates_p)
    assert nseq == SAMPLE_TILE_SEQS
    x1 = out_proj(hg_s.reshape(t_dec * nseq, VW), x_sample, tm=512)
    x2, cs0_s = ffn_first(x1, 0, state_ffn_conv[0], final_norm=False)
    x3, pool_new = _pool_sample(x2, g_mix1, wp, psc, jnp.swapaxes(state_pool[0], 0, 1), nseq=nseq, t_dec=t_dec)
    y_s, cs1_s = ffn_first(x3, 1, state_ffn_conv[1], final_norm=True)
    y_sample = _from_time_major(y_s, nseq, t_dec)
    n_s = jnp.swapaxes(n_s_hm, 0, 1)[None]
    m_s = jnp.swapaxes(m_s_hm[:, :, 0], 0, 1)[None]
    pool_s = jnp.swapaxes(pool_new, 0, 1)[None]
    conv_s = jnp.stack([cs0_s, cs1_s])

    y_prompt, C_p, n_p, m_p, pool_p, conv_p = prompt_streams()
    return (y_prompt, y_sample, C_p, n_p, m_p, pool_p, conv_p,
            C_s[None], n_s, m_s, pool_s, conv_s)
```

```python
import functools

import jax
import jax.numpy as jnp
from jax import lax
from jax.experimental import pallas as pl
from jax.experimental.pallas import tpu as pltpu

F32 = jnp.float32
BF16 = jnp.bfloat16

EPS = 1e-6
D = 2048
H = 4
DK = 256
DV = 512
QKW = H * DK
VW = H * DV
PW = 2 * QKW + 2 * VW
GATE_LANES = 128
SCALE = DK ** -0.5
POOL_WINDOWS = (2, 4, 8, 16)
PG = D // len(POOL_WINDOWS)
POOL_HALO = 16
POOL_PAD_ROWS = 16
D_FF = 5504
TF = 256
NF = -(-D_FF // TF)
LANES = 128
FF_LANE_BLOCKS = D_FF // LANES
CONV_TAIL = 2
N_META = 16
PAST_LEN = 16384
SCAN_CHUNK = 512
SAMPLE_TILE_SEQS = 128
VMEM_LIMIT = 60 * 1024 * 1024


def _params(n_axes):
    return pltpu.CompilerParams(dimension_semantics=("arbitrary",) * n_axes,
                                vmem_limit_bytes=VMEM_LIMIT)


def _rms(x, g):
    return x * lax.rsqrt(jnp.mean(x * x, axis=-1, keepdims=True) + EPS) * g


def _log_sigmoid(x):
    return jnp.minimum(x, 0.0) - jnp.log(1.0 + jnp.exp(-jnp.abs(x)))


_NT = (((1,), (1,)), ((), ()))


def _proj_kernel(x_ref, g_ref, wt_ref, wgt_ref, p_ref, gate_ref, *rest, emit_bf16):
    u_scr = rest[-1]

    @pl.when(pl.program_id(1) == 0)
    def _():
        ub = _rms(x_ref[...], g_ref[...]).astype(BF16)
        u_scr[...] = ub
        wg = wgt_ref[...]
        if emit_bf16:
            row_ok = lax.broadcasted_iota(jnp.int32, (GATE_LANES, 1), 0) < 2 * H
            wg = jnp.where(row_ok, wg, 0.0).astype(BF16)
            rest[1][...] = wg
        gate_ref[...] = lax.dot_general(ub, wg, _NT, preferred_element_type=F32)

    w = wt_ref[...].astype(BF16)
    if emit_bf16:
        rest[0][...] = w
    p_ref[...] = lax.dot_general(u_scr[...], w, _NT, preferred_element_type=F32).astype(p_ref.dtype)


def _proj(x, gamma, wt, wgt, *, tm, tn, out_dtype):
    m = x.shape[0]
    emit = wt.dtype != BF16
    gate_spec = pl.BlockSpec((GATE_LANES, D), (lambda i, j: (PW // GATE_LANES, 0)) if emit
                             else (lambda i, j: (0, 0)))
    out_specs = [pl.BlockSpec((tm, tn), lambda i, j: (i, j)),
                 pl.BlockSpec((tm, GATE_LANES), lambda i, j: (i, 0))]
    out_shape = [jax.ShapeDtypeStruct((m, PW), out_dtype),
                 jax.ShapeDtypeStruct((m, GATE_LANES), F32)]
    if emit:
        out_specs += [pl.BlockSpec((tn, D), lambda i, j: (j, 0)),
                      pl.BlockSpec((GATE_LANES, D), lambda i, j: (0, 0))]
        out_shape += [jax.ShapeDtypeStruct((PW, D), BF16),
                      jax.ShapeDtypeStruct((GATE_LANES, D), BF16)]
    return pl.pallas_call(
        functools.partial(_proj_kernel, emit_bf16=emit),
        grid=(m // tm, PW // tn),
        in_specs=[pl.BlockSpec((tm, D), lambda i, j: (i, 0)),
                  pl.BlockSpec((1, D), lambda i, j: (0, 0)),
                  pl.BlockSpec((tn, D), lambda i, j: (j, 0)),
                  gate_spec],
        out_specs=out_specs,
        out_shape=out_shape,
        scratch_shapes=[pltpu.VMEM((tm, D), BF16)],
        compiler_params=_params(2),
        name="proj_cast" if emit else "proj",
    )(x, gamma, wt, wgt)


def _head_output(num, den, m_t, gout, o):
    hv = num * (1.0 / jnp.maximum(jnp.abs(den), jnp.exp(-m_t)))
    hv = hv * lax.rsqrt(jnp.mean(hv * hv, axis=1, keepdims=True) + EPS)
    return hv * gout * jax.nn.sigmoid(o.astype(F32))


def _scan_kernel(q_ref, k_ref, v_ref, o_ref, gt_ref, bias_ref, gout_ref, c0_ref, n0_ref, m0_ref,
                 hg_ref, cout_ref, nout_ref, mout_ref, c_scr, n_scr, m_scr, *, L, lead_pad, nc):
    c = pl.program_id(1)

    @pl.when(c == 0)
    def _():
        c_scr[...] = c0_ref[...]
        n_scr[...] = n0_ref[...]
        m_scr[...] = m0_ref[...]

    gates = gt_ref[...] + bias_ref[...]
    gates_t = gates.T
    row = lax.broadcasted_iota(jnp.int32, (L, L), 0)
    col = lax.broadcasted_iota(jnp.int32, (L, L), 1)
    causal = row >= col
    causal_t = row <= col
    if lead_pad:
        live_col = lax.broadcasted_iota(jnp.int32, (L, 1), 0) >= lead_pad
        live_row = lax.broadcasted_iota(jnp.int32, (1, L), 1) >= lead_pad

    for hh in range(H):
        ig_col = gates[:, hh:hh + 1]
        ig_row = gates_t[hh:hh + 1, :]
        lf_col = _log_sigmoid(gates[:, H + hh:H + hh + 1])
        lf_row = _log_sigmoid(gates_t[H + hh:H + hh + 1, :])
        if lead_pad:
            ig_col = jnp.where(live_col, ig_col, -jnp.inf)
            ig_row = jnp.where(live_row, ig_row, -jnp.inf)
            lf_col = jnp.where(live_col, lf_col, 0.0)
            lf_row = jnp.where(live_row, lf_row, 0.0)
        b_col = jnp.sum(jnp.where(causal, lf_row, 0.0), axis=1, keepdims=True)
        b_row = jnp.sum(jnp.where(causal_t, lf_col, 0.0), axis=0, keepdims=True)
        m_prev = m_scr[hh, 0:1, 0:1]
        d = jnp.where(causal, b_col - b_row + ig_row, -jnp.inf)
        inter = b_col + m_prev
        m_t = jnp.maximum(inter, jnp.max(d, axis=1, keepdims=True))
        w_inter = jnp.exp(inter - m_t) * SCALE

        q = q_ref[:, hh * DK:(hh + 1) * DK]
        k = k_ref[:, hh * DK:(hh + 1) * DK]
        v = v_ref[:, hh * DV:(hh + 1) * DV]
        qk = lax.dot_general(q, k, _NT, preferred_element_type=F32)
        s = qk * (jnp.exp(d - m_t) * SCALE)
        cmat = c_scr[hh]
        nvec = n_scr[hh, 0:1, :]
        num = w_inter * jnp.dot(q, cmat.astype(BF16), preferred_element_type=F32) \
            + jnp.dot(s.astype(BF16), v, preferred_element_type=F32)
        den = w_inter * jnp.sum(q.astype(F32) * nvec, axis=1, keepdims=True) \
            + jnp.sum(s, axis=1, keepdims=True)
        hout = _head_output(num, den, m_t, gout_ref[:, hh * DV:(hh + 1) * DV],
                            o_ref[:, hh * DV:(hh + 1) * DV])
        hg_ref[:, hh * DV:(hh + 1) * DV] = hout.astype(hg_ref.dtype)

        m_new = m_t[L - 1:L, :]
        b_last = b_col[L - 1:L, :]
        decay = jnp.exp(b_last + m_prev - m_new)
        wk = jnp.exp(b_last - b_col + ig_col - m_new) * k.astype(F32)
        c_scr[hh] = decay * cmat + jnp.dot(wk.T.astype(BF16), v, preferred_element_type=F32)
        n_scr[hh] = jnp.broadcast_to(decay * nvec + jnp.sum(wk, axis=0, keepdims=True), (8, DK))
        m_scr[hh] = jnp.broadcast_to(m_new, (8, GATE_LANES))

    @pl.when(c == nc - 1)
    def _():
        cout_ref[...] = c_scr[...]
        nout_ref[...] = n_scr[...]
        mout_ref[...] = m_scr[...]


def _scan(p, gates, bias, gout, c0, n0, m0, *, L, lead_pad=0):
    b, s, _ = p.shape
    nc = s // L
    kern = functools.partial(_scan_kernel, L=L, lead_pad=lead_pad, nc=nc)
    return pl.pallas_call(
        kern,
        grid=(b, nc),
        in_specs=[pl.BlockSpec((None, L, QKW), lambda i, c: (i, c, 0)),
                  pl.BlockSpec((None, L, QKW), lambda i, c: (i, c, 1)),
                  pl.BlockSpec((None, L, VW), lambda i, c: (i, c, 1)),
                  pl.BlockSpec((None, L, VW), lambda i, c: (i, c, 2)),
                  pl.BlockSpec((None, L, GATE_LANES), lambda i, c: (i, c, 0)),
                  pl.BlockSpec((1, GATE_LANES), lambda i, c: (0, 0)),
                  pl.BlockSpec((1, VW), lambda i, c: (0, 0)),
                  pl.BlockSpec((H, DK, DV), lambda i, c: (0, 0, 0)),
                  pl.BlockSpec((H, 8, DK), lambda i, c: (0, 0, 0)),
                  pl.BlockSpec((H, 8, GATE_LANES), lambda i, c: (0, 0, 0))],
        out_specs=[pl.BlockSpec((None, L, VW), lambda i, c: (i, c, 0)),
                   pl.BlockSpec((None, H, DK, DV), lambda i, c: (i, 0, 0, 0)),
                   pl.BlockSpec((None, H, 8, DK), lambda i, c: (i, 0, 0, 0)),
                   pl.BlockSpec((None, H, 8, GATE_LANES), lambda i, c: (i, 0, 0, 0))],
        out_shape=[jax.ShapeDtypeStruct((b, s, VW), BF16),
                   jax.ShapeDtypeStruct((b, H, DK, DV), F32),
                   jax.ShapeDtypeStruct((b, H, 8, DK), F32),
                   jax.ShapeDtypeStruct((b, H, 8, GATE_LANES), F32)],
        scratch_shapes=[pltpu.VMEM((H, DK, DV), F32),
                        pltpu.VMEM((H, 8, DK), F32),
                        pltpu.VMEM((H, 8, GATE_LANES), F32)],
        compiler_params=_params(2),
        name="scan",
    )(p, p, p, p, gates, bias, gout, c0, n0, m0)


def _scan_s_kernel(q_ref, k_ref, v_ref, o_ref, gt_ref, bias_ref, gout_ref, mtok_ref, c_ref, n_ref,
                   hg_ref, cout_ref, nout_ref, mout_ref, qc_scr, ntok_scr, *, T, NB):
    hh = pl.program_id(1) % H
    LT = NB * T
    gates = gt_ref[...] + bias_ref[...]
    gates_t = gates.T
    lane = lax.broadcasted_iota(jnp.int32, (LT, GATE_LANES), 1)
    sub = lax.broadcasted_iota(jnp.int32, (GATE_LANES, LT), 0)

    def pick_col(a, idx):
        return jnp.sum(jnp.where(lane == idx, a, 0.0), axis=1, keepdims=True)

    def pick_row(a, idx):
        return jnp.sum(jnp.where(sub == idx, a, 0.0), axis=0, keepdims=True)

    ig_col = pick_col(gates, hh)
    ig_row = pick_row(gates_t, hh)
    lf_col = _log_sigmoid(pick_col(gates, hh + H))
    lf_row = _log_sigmoid(pick_row(gates_t, hh + H))
    m_prev = pick_col(mtok_ref[...], hh)

    row = lax.broadcasted_iota(jnp.int32, (LT, LT), 0)
    col = lax.broadcasted_iota(jnp.int32, (LT, LT), 1)
    same = (row // T) == (col // T)
    causal = jnp.logical_and(same, row >= col)
    causal_t = jnp.logical_and(same, row <= col)
    b_col = jnp.sum(jnp.where(causal, lf_row, 0.0), axis=1, keepdims=True)
    b_row = jnp.sum(jnp.where(causal_t, lf_col, 0.0), axis=0, keepdims=True)
    b_end = jnp.sum(jnp.where(same, lf_row, 0.0), axis=1, keepdims=True)
    d = jnp.where(causal, b_col - b_row + ig_row, -jnp.inf)
    inter = b_col + m_prev
    m_t = jnp.maximum(inter, jnp.max(d, axis=1, keepdims=True))
    d_end = jnp.where(same, b_end - b_row + ig_row, -jnp.inf)
    m_new = jnp.maximum(b_end + m_prev, jnp.max(d_end, axis=1, keepdims=True))
    w_inter = jnp.exp(inter - m_t) * SCALE

    q32 = q_ref[...]
    k32 = k_ref[...]
    q = q32.astype(BF16)
    v = v_ref[...].astype(BF16)
    qk = lax.dot_general(q, k32.astype(BF16), (((1,), (1,)), ((), ())), preferred_element_type=F32)
    s = qk * (jnp.exp(d - m_t) * SCALE)
    num_intra = jnp.dot(s.astype(BF16), v, preferred_element_type=F32)
    den_intra = jnp.sum(s, axis=1, keepdims=True)

    decay = jnp.exp(b_end + m_prev - m_new)
    wk = jnp.exp(b_end - b_col + ig_col - m_new) * k32
    wk_t = wk.T
    col_seq = lax.broadcasted_iota(jnp.int32, (DK, LT), 1) // T

    for bb in range(NB):
        r0 = bb * T
        cmat = c_ref[bb]
        nvec = n_ref[bb:bb + 1, :]
        qc_scr[r0:r0 + T, :] = jnp.dot(q32[r0:r0 + T, :].astype(BF16), cmat.astype(BF16),
                                       preferred_element_type=F32)
        ntok_scr[r0:r0 + T, :] = jnp.broadcast_to(nvec, (T, DK))
        upd = jnp.dot(jnp.where(col_seq == bb, wk_t, 0.0).astype(BF16), v, preferred_element_type=F32)
        dec = decay[r0:r0 + 1, :]
        cout_ref[bb] = dec * cmat + upd
        nout_ref[bb:bb + 1, :] = dec * nvec + jnp.sum(wk[r0:r0 + T, :], axis=0, keepdims=True)
        mout_ref[bb:bb + 1, :] = jnp.broadcast_to(m_new[r0:r0 + 1, :], (1, GATE_LANES))

    num = w_inter * qc_scr[...] + num_intra
    den = w_inter * jnp.sum(q32 * ntok_scr[...], axis=1, keepdims=True) + den_intra
    hout = _head_output(num, den, m_t, gout_ref[...], o_ref[...]).astype(hg_ref.dtype)
    src_row = (row % NB) * T + row // NB
    perm = jnp.where(col == src_row, 1.0, 0.0).astype(hg_ref.dtype)
    moved = jnp.dot(perm, hout, preferred_element_type=F32)
    hg_ref[...] = moved.reshape(T, NB, DV).astype(hg_ref.dtype)


def _proj_scan_s_kernel(x_ref, g_ref, wt_ref, wgt_ref,
                        q_ref, k_ref, v_ref, o_ref, gt_ref, bias_ref, gout_ref, mtok_ref, c_ref, n_ref,
                        p_ref, gate_ref, hg_ref, cout_ref, nout_ref, mout_ref,
                        u_scr, qc_scr, ntok_scr, *, T, NB):
    _proj_kernel(x_ref, g_ref, wt_ref, wgt_ref, p_ref, gate_ref, u_scr, emit_bf16=False)
    _scan_s_kernel(q_ref, k_ref, v_ref, o_ref, gt_ref, bias_ref, gout_ref, mtok_ref, c_ref, n_ref,
                   hg_ref, cout_ref, nout_ref, mout_ref, qc_scr, ntok_scr, T=T, NB=NB)


def _proj_scan_s(x, gamma, wt, wgt, p_s, gates_s, bias, gout, mtok, c, n_hm, *, tm, T):
    m = x.shape[0]
    nseq = c.shape[0]
    nt = m // tm
    nj = 2 * H
    tn = PW // nj
    gpt = nj // H
    nb = nseq // (nt * gpt)
    assert nb * nt * gpt == nseq and nb % 8 == 0 and tn % LANES == 0
    lt = nb * T
    kern = functools.partial(_proj_scan_s_kernel, T=T, NB=nb)

    def grp(i, j):
        return i * gpt + j // H

    def head(j):
        return j % H

    return pl.pallas_call(
        kern,
        grid=(nt, nj),
        in_specs=[pl.BlockSpec((tm, D), lambda i, j: (i, 0), pipeline_mode=pl.Buffered(1)),
                  pl.BlockSpec((1, D), lambda i, j: (0, 0)),
                  pl.BlockSpec((tn, D), lambda i, j: (j, 0)),
                  pl.BlockSpec((GATE_LANES, D), lambda i, j: (0, 0)),
                  pl.BlockSpec((lt, DK), lambda i, j: (grp(i, j), head(j))),
                  pl.BlockSpec((lt, DK), lambda i, j: (grp(i, j), H + head(j))),
                  pl.BlockSpec((lt, DV), lambda i, j: (grp(i, j), H + head(j))),
                  pl.BlockSpec((lt, DV), lambda i, j: (grp(i, j), 2 * H + head(j))),
                  pl.BlockSpec((lt, GATE_LANES), lambda i, j: (grp(i, j), 0)),
                  pl.BlockSpec((1, GATE_LANES), lambda i, j: (0, 0)),
                  pl.BlockSpec((1, DV), lambda i, j: (0, head(j))),
                  pl.BlockSpec((lt, GATE_LANES), lambda i, j: (grp(i, j), 0)),
                  pl.BlockSpec((nb, None, DK, DV), lambda i, j: (grp(i, j), head(j), 0, 0)),
                  pl.BlockSpec((None, nb, DK), lambda i, j: (head(j), grp(i, j), 0))],
        out_specs=[pl.BlockSpec((tm, tn), lambda i, j: (i, j)),
                   pl.BlockSpec((tm, GATE_LANES), lambda i, j: (i, 0)),
                   pl.BlockSpec((T, nb, DV), lambda i, j: (0, grp(i, j), head(j))),
                   pl.BlockSpec((nb, None, DK, DV), lambda i, j: (grp(i, j), head(j), 0, 0)),
                   pl.BlockSpec((None, nb, DK), lambda i, j: (head(j), grp(i, j), 0)),
                   pl.BlockSpec((None, nb, GATE_LANES), lambda i, j: (head(j), grp(i, j), 0))],
        out_shape=[jax.ShapeDtypeStruct((m, PW), BF16),
                   jax.ShapeDtypeStruct((m, GATE_LANES), F32),
                   jax.ShapeDtypeStruct((T, nseq, VW), BF16),
                   jax.ShapeDtypeStruct((nseq, H, DK, DV), F32),
                   jax.ShapeDtypeStruct((H, nseq, DK), F32),
                   jax.ShapeDtypeStruct((H, nseq, GATE_LANES), F32)],
        scratch_shapes=[pltpu.VMEM((tm, D), BF16),
                        pltpu.VMEM((lt, DV), F32),
                        pltpu.VMEM((lt, DK), F32)],
        compiler_params=_params(2),
        name="proj_scan_sample",
    )(x, gamma, wt, wgt, p_s, p_s, p_s, p_s, gates_s, bias, gout, mtok, c, n_hm)


def _mmres_cast_kernel(a_ref, w_ref, x_ref, o_ref, wc_ref):
    w = w_ref[...].astype(BF16)
    wc_ref[...] = w
    y = jnp.dot(a_ref[...], w, preferred_element_type=F32)
    nseq, t_dec, _ = x_ref.shape
    for t in range(t_dec):
        o_ref[t * nseq:(t + 1) * nseq, :] = x_ref[:, t, :] + y[t * nseq:(t + 1) * nseq, :]


def _mmres_cast(a, w, x, *, tn):
    m, kdim = a.shape
    n = w.shape[2]
    nseq, t_dec, _ = x.shape
    return pl.pallas_call(
        _mmres_cast_kernel,
        grid=(n // tn,),
        in_specs=[pl.BlockSpec((m, kdim), lambda j: (0, 0)),
                  pl.BlockSpec((None, kdim, tn), lambda j: (0, 0, j)),
                  pl.BlockSpec((nseq, t_dec, tn), lambda j: (0, 0, j))],
        out_specs=[pl.BlockSpec((m, tn), lambda j: (0, j)),
                   pl.BlockSpec((kdim, tn), lambda j: (0, j))],
        out_shape=[jax.ShapeDtypeStruct((m, n), F32),
                   jax.ShapeDtypeStruct((kdim, n), BF16)],
        compiler_params=_params(1),
        name="out_proj_cast",
    )(a, w, x)


def _mmres_kernel(a_ref, w_ref, x_ref, o_ref):
    o_ref[...] = x_ref[...] + jnp.dot(a_ref[...], w_ref[...], preferred_element_type=F32)


def _mmres(a, w, x, *, tm):
    m, kdim = a.shape
    n = w.shape[1]
    return pl.pallas_call(
        _mmres_kernel,
        grid=(m // tm,),
        in_specs=[pl.BlockSpec((tm, kdim), lambda i: (i, 0)),
                  pl.BlockSpec((kdim, n), lambda i: (0, 0)),
                  pl.BlockSpec((tm, n), lambda i: (i, 0))],
        out_specs=pl.BlockSpec((tm, n), lambda i: (i, 0)),
        out_shape=jax.ShapeDtypeStruct((m, n), F32),
        compiler_params=_params(1),
        name="out_proj",
    )(a, w, x)


def _ffn_kernel(x_ref, gam_ref, wg_ref, wa0_ref, wa1_ref, cw_ref, cb_ref, wd_ref, st_ref, gfin_ref,
                o_ref, so_ref, wupc_ref, wdnc_ref,
                u_scr, gext_scr, a_scr, wup0_scr, wup1_scr, wdn0_scr, wdn1_scr,
                *, tm, shift, nsub, final_norm):
    s = pl.program_id(1)
    sr = CONV_TAIL * shift
    wup_slots = (wup0_scr, wup1_scr)
    wdn_slots = (wdn0_scr, wdn1_scr)

    def cast_steps(slot):
        wup_scr = wup_slots[slot]
        wdn_scr = wdn_slots[slot]
        valid = D_FF - jnp.minimum(s, NF - 1) * TF

        def cast_gate():
            ok = lax.broadcasted_iota(jnp.int32, (1, TF), 1) < valid
            w = jnp.where(ok, wg_ref[...], 0.0).astype(BF16)
            wup_scr[:, 0:TF] = w
            wupc_ref[:, 0:TF] = w

        def cast_value():
            lane = lax.broadcasted_iota(jnp.int32, (1, LANES), 1)
            w0 = jnp.where(lane < valid, wa0_ref[...], 0.0).astype(BF16)
            w1 = jnp.where(lane + LANES < valid, wa1_ref[...], 0.0).astype(BF16)
            wup_scr[:, TF:TF + LANES] = w0
            wup_scr[:, TF + LANES:2 * TF] = w1
            wupc_ref[:, TF:TF + LANES] = w0
            wupc_ref[:, TF + LANES:2 * TF] = w1

        def cast_down(r0, rows):
            row_ok = r0 + lax.broadcasted_iota(jnp.int32, (rows, 1), 0) < valid
            w = jnp.where(row_ok, wd_ref[r0:r0 + rows, :], 0.0).astype(BF16)
            wdn_scr[r0:r0 + rows, :] = w
            wdnc_ref[r0:r0 + rows, :] = w

        half = TF // 2
        return [cast_gate, cast_value, functools.partial(cast_down, 0, half),
                functools.partial(cast_down, half, half)]

    def run_tile(slot, fillers=()):
        fillers = list(fillers)

        def fill():
            if fillers:
                fillers.pop(0)()

        wup_scr = wup_slots[slot]
        wdn_scr = wdn_slots[slot]
        f = s - 1
        col_ok = lax.broadcasted_iota(jnp.int32, (1, TF), 1) < D_FF - f * TF
        cw = cw_ref[...]
        cb = cb_ref[...]
        for t in range(CONV_TAIL):
            gext_scr[t * shift:(t + 1) * shift, :] = st_ref[:, t, :]
        ts = tm // nsub
        for h in range(nsub):
            r0 = h * ts
            ga = jnp.dot(u_scr[r0:r0 + ts, :], wup_scr[...], preferred_element_type=F32)
            gext_scr[sr + r0:sr + r0 + ts, :] = ga[:, 0:TF]
            a_scr[r0:r0 + ts, :] = ga[:, TF:2 * TF]
            fill()
        for h in range(nsub):
            r0 = h * ts
            g = gext_scr[sr + r0:sr + r0 + ts, :]
            g_m2 = gext_scr[sr + r0 - 2 * shift:sr + r0 - 2 * shift + ts, :]
            g_m1 = gext_scr[sr + r0 - shift:sr + r0 - shift + ts, :]
            gc = cb + ((cw[0:1, :] * g_m2 + cw[1:2, :] * g_m1) + cw[2:3, :] * g)
            hmid = jnp.where(col_ok, (gc * jax.nn.sigmoid(gc)) * a_scr[r0:r0 + ts, :], 0.0)
            o_ref[r0:r0 + ts, :] += jnp.dot(hmid.astype(BF16), wdn_scr[...], preferred_element_type=F32)
            fill()
        while fillers:
            fill()
        for t in range(CONV_TAIL):
            so_ref[:, t, :] = gext_scr[tm + t * shift:tm + (t + 1) * shift, :]

    @pl.when(s == 0)
    def _():
        x = x_ref[...]
        u_scr[...] = _rms(x, gam_ref[...]).astype(BF16)
        o_ref[...] = x
        for step in cast_steps(0):
            step()

    for parity in range(2):
        @pl.when(jnp.logical_and(s > 0, s % 2 == parity))
        def _(parity=parity):
            run_tile(1 - parity, cast_steps(parity))

    if final_norm:
        @pl.when(s == NF)
        def _():
            o_ref[...] = _rms(o_ref[...], gfin_ref[...])


def _ffn(x, layer, gamma, w_up, conv_w, conv_b, w_down, st, gfin, *, final_norm):
    assert TF == 2 * LANES
    tm = x.shape[0]
    shift = st.shape[0]
    sr = CONV_TAIL * shift
    kern = functools.partial(_ffn_kernel, tm=tm, shift=shift, nsub=2, final_norm=final_norm)
    last_a = 2 * FF_LANE_BLOCKS - 1

    def wt(s):
        return jnp.minimum(s, NF - 1)

    def ft(s):
        return jnp.maximum(s - 1, 0)

    st_spec = pl.BlockSpec((shift, CONV_TAIL, TF), lambda i, s: (0, 0, ft(s)))
    return pl.pallas_call(
        kern,
        grid=(1, NF + 1),
        in_specs=[pl.BlockSpec((tm, D), lambda i, s: (i, 0), pipeline_mode=pl.Buffered(1)),
                  pl.BlockSpec((1, D), lambda i, s: (0, 0)),
                  pl.BlockSpec((None, D, TF), lambda i, s: (layer, 0, wt(s))),
                  pl.BlockSpec((None, D, LANES), lambda i, s: (layer, 0, FF_LANE_BLOCKS + 2 * wt(s))),
                  pl.BlockSpec((None, D, LANES),
                               lambda i, s: (layer, 0, jnp.minimum(FF_LANE_BLOCKS + 2 * wt(s) + 1, last_a))),
                  pl.BlockSpec((None, 3, TF), lambda i, s: (layer, 0, ft(s))),
                  pl.BlockSpec((None, 1, TF), lambda i, s: (layer, 0, ft(s))),
                  pl.BlockSpec((None, TF, D), lambda i, s: (layer, wt(s), 0)),
                  st_spec,
                  pl.BlockSpec((1, D), lambda i, s: (0, 0))],
        out_specs=[pl.BlockSpec((tm, D), lambda i, s: (i, 0)),
                   st_spec,
                   pl.BlockSpec((D, 2 * TF), lambda i, s: (0, wt(s))),
                   pl.BlockSpec((TF, D), lambda i, s: (wt(s), 0))],
        out_shape=[jax.ShapeDtypeStruct((tm, D), F32),
                   jax.ShapeDtypeStruct(st.shape, F32),
                   jax.ShapeDtypeStruct((D, NF * 2 * TF), BF16),
                   jax.ShapeDtypeStruct((NF * TF, D), BF16)],
        scratch_shapes=[pltpu.VMEM((tm, D), BF16),
                        pltpu.VMEM((sr + tm, TF), F32),
                        pltpu.VMEM((tm, TF), F32),
                        pltpu.VMEM((D, 2 * TF), BF16),
                        pltpu.VMEM((D, 2 * TF), BF16),
                        pltpu.VMEM((TF, D), BF16),
                        pltpu.VMEM((TF, D), BF16)],
        compiler_params=_params(2),
        name="conv_ffn_cast",
    )(x, gamma, w_up, w_up, w_up, conv_w, conv_b, w_down, st, gfin)


def _ffn_bf16_kernel(x_ref, gam_ref, wup_ref, cw_ref, cb_ref, wdn_ref, st_ref, gfin_ref,
                     o_ref, so_ref, u_scr, gext_scr, a_scr, carry_scr,
                     *, tm, sr, shift, tps, nsub, final_norm, step=None):
    i = pl.program_id(0)
    f = pl.program_id(1) if step is None else step
    tf2 = 2 * TF

    @pl.when(f == 0)
    def _():
        x = x_ref[...]
        u_scr[...] = _rms(x, gam_ref[...]).astype(BF16)
        o_ref[...] = x
        if tps > 1:
            @pl.when(i == 0)
            def _():
                carry_scr[...] = jnp.zeros_like(carry_scr)

    col_ok = lax.broadcasted_iota(jnp.int32, (1, tf2), 1) < D_FF - f * tf2
    cw = cw_ref[...]
    cb = cb_ref[...]
    if tps == 1:
        gext_scr[0:sr, :] = st_ref[...]
    else:
        gext_scr[0:sr, :] = jnp.where((i % tps) == 0, st_ref[...], carry_scr[f])
    ts = tm // nsub
    for h in range(nsub):
        r0 = h * ts
        ga = jnp.dot(u_scr[r0:r0 + ts, :], wup_ref[...], preferred_element_type=F32)
        gext_scr[sr + r0:sr + r0 + ts, 0:TF] = ga[:, 0:TF]
        gext_scr[sr + r0:sr + r0 + ts, TF:tf2] = ga[:, 2 * TF:3 * TF]
        a_scr[r0:r0 + ts, 0:TF] = ga[:, TF:2 * TF]
        a_scr[r0:r0 + ts, TF:tf2] = ga[:, 3 * TF:4 * TF]
    for h in range(nsub):
        r0 = h * ts
        g = gext_scr[sr + r0:sr + r0 + ts, :]
        g_m2 = gext_scr[sr + r0 - 2 * shift:sr + r0 - 2 * shift + ts, :]
        g_m1 = gext_scr[sr + r0 - shift:sr + r0 - shift + ts, :]
        gc = cb + ((cw[0:1, :] * g_m2 + cw[1:2, :] * g_m1) + cw[2:3, :] * g)
        hmid = jnp.where(col_ok, (gc * jax.nn.sigmoid(gc)) * a_scr[r0:r0 + ts, :], 0.0)
        o_ref[r0:r0 + ts, :] += jnp.dot(hmid.astype(BF16), wdn_ref[...], preferred_element_type=F32)
    g_tail = gext_scr[tm:tm + sr, :]
    so_ref[...] = g_tail
    if tps > 1:
        carry_scr[f] = g_tail

    if final_norm:
        @pl.when(f == pl.num_programs(1) - 1)
        def _():
            o_ref[...] = _rms(o_ref[...], gfin_ref[...])


def _ffn_bf16(x, layer, gamma, wupc, conv_w, conv_b, wdnc, st, gfin, *, tm, sr, shift, tps, st_per_tile,
              final_norm):
    assert NF % 2 == 0
    m = x.shape[0]
    nt = m // tm
    tf2 = 2 * TF
    nf2 = NF // 2
    kern = functools.partial(_ffn_bf16_kernel, tm=tm, sr=sr, shift=shift, tps=tps,
                             nsub=2 if tm >= 64 else 1, final_norm=final_norm)
    st_map = (lambda i, f: (i, f)) if st_per_tile else (lambda i, f: (0, f))
    return pl.pallas_call(
        kern,
        grid=(nt, nf2),
        in_specs=[pl.BlockSpec((tm, D), lambda i, f: (i, 0)),
                  pl.BlockSpec((1, D), lambda i, f: (0, 0)),
                  pl.BlockSpec((D, 2 * tf2), lambda i, f: (0, f)),
                  pl.BlockSpec((None, 3, tf2), lambda i, f: (layer, 0, f)),
                  pl.BlockSpec((None, 1, tf2), lambda i, f: (layer, 0, f)),
                  pl.BlockSpec((tf2, D), lambda i, f: (f, 0)),
                  pl.BlockSpec((sr, tf2), st_map),
                  pl.BlockSpec((1, D), lambda i, f: (0, 0))],
        out_specs=[pl.BlockSpec((tm, D), lambda i, f: (i, 0)),
                   pl.BlockSpec((sr, tf2), lambda i, f: (i, f))],
        out_shape=[jax.ShapeDtypeStruct((m, D), F32),
                   jax.ShapeDtypeStruct((nt * sr, D_FF), F32)],
        scratch_shapes=[pltpu.VMEM((tm, D), BF16),
                        pltpu.VMEM((sr + tm, tf2), F32),
                        pltpu.VMEM((tm, tf2), F32),
                        pltpu.VMEM((nf2, sr, tf2), F32)],
        compiler_params=_params(2),
        name="conv_ffn",
    )(x, gamma, wupc, conv_w, conv_b, wdnc, st, gfin)


def _scan_ffn_kernel(*refs, L, nc, tm_r):
    scan_in, ffn_in = refs[0:10], refs[10:18]
    scan_out, ffn_out = refs[18:22], refs[22:24]
    scan_scr, ffn_scr = refs[24:27], refs[27:31]
    _scan_kernel(*scan_in, *scan_out, *scan_scr, L=L, lead_pad=0, nc=nc)
    step = pl.program_id(0) * nc + pl.program_id(1)

    @pl.when(step < NF // 2)
    def _():
        _ffn_bf16_kernel(*ffn_in, *ffn_out, *ffn_scr, tm=tm_r, sr=8, shift=1, tps=1, nsub=1,
                         final_norm=False, step=step)


def _scan_ffn(p, gates, bias, gout, c0, n0, m0, xr, layer, gamma, wupc, conv_w, conv_b, wdnc, st, gfin, *, L):
    b, s, _ = p.shape
    nc = s // L
    tm_r = xr.shape[0]
    tf2 = 2 * TF
    nf2 = NF // 2
    assert b * nc >= nf2

    def fi(i, c):
        return jnp.minimum(i * nc + c, nf2 - 1)

    kern = functools.partial(_scan_ffn_kernel, L=L, nc=nc, tm_r=tm_r)
    return pl.pallas_call(
        kern,
        grid=(b, nc),
        in_specs=[pl.BlockSpec((None, L, QKW), lambda i, c: (i, c, 0)),
                  pl.BlockSpec((None, L, QKW), lambda i, c: (i, c, 1)),
                  pl.BlockSpec((None, L, VW), lambda i, c: (i, c, 1)),
                  pl.BlockSpec((None, L, VW), lambda i, c: (i, c, 2)),
                  pl.BlockSpec((None, L, GATE_LANES), lambda i, c: (i, c, 0)),
                  pl.BlockSpec((1, GATE_LANES), lambda i, c: (0, 0)),
                  pl.BlockSpec((1, VW), lambda i, c: (0, 0)),
                  pl.BlockSpec((H, DK, DV), lambda i, c: (0, 0, 0)),
                  pl.BlockSpec((H, 8, DK), lambda i, c: (0, 0, 0)),
                  pl.BlockSpec((H, 8, GATE_LANES), lambda i, c: (0, 0, 0)),
                  pl.BlockSpec((tm_r, D), lambda i, c: (0, 0)),
                  pl.BlockSpec((1, D), lambda i, c: (0, 0)),
                  pl.BlockSpec((D, 2 * tf2), lambda i, c: (0, fi(i, c))),
                  pl.BlockSpec((None, 3, tf2), lambda i, c: (layer, 0, fi(i, c))),
                  pl.BlockSpec((None, 1, tf2), lambda i, c: (layer, 0, fi(i, c))),
                  pl.BlockSpec((tf2, D), lambda i, c: (fi(i, c), 0)),
                  pl.BlockSpec((8, tf2), lambda i, c: (0, fi(i, c))),
                  pl.BlockSpec((1, D), lambda i, c: (0, 0))],
        out_specs=[pl.BlockSpec((None, L, VW), lambda i, c: (i, c, 0)),
                   pl.BlockSpec((None, H, DK, DV), lambda i, c: (i, 0, 0, 0)),
                   pl.BlockSpec((None, H, 8, DK), lambda i, c: (i, 0, 0, 0)),
                   pl.BlockSpec((None, H, 8, GATE_LANES), lambda i, c: (i, 0, 0, 0)),
                   pl.BlockSpec((tm_r, D), lambda i, c: (0, 0)),
                   pl.BlockSpec((8, tf2), lambda i, c: (0, fi(i, c)))],
        out_shape=[jax.ShapeDtypeStruct((b, s, VW), BF16),
                   jax.ShapeDtypeStruct((b, H, DK, DV), F32),
                   jax.ShapeDtypeStruct((b, H, 8, DK), F32),
                   jax.ShapeDtypeStruct((b, H, 8, GATE_LANES), F32),
                   jax.ShapeDtypeStruct((tm_r, D), F32),
                   jax.ShapeDtypeStruct((8, D_FF), F32)],
        scratch_shapes=[pltpu.VMEM((H, DK, DV), F32),
                        pltpu.VMEM((H, 8, DK), F32),
                        pltpu.VMEM((H, 8, GATE_LANES), F32),
                        pltpu.VMEM((tm_r, D), BF16),
                        pltpu.VMEM((8 + tm_r, tf2), F32),
                        pltpu.VMEM((tm_r, tf2), F32),
                        pltpu.VMEM((1, 8, tf2), F32)],
        compiler_params=_params(2),
        name="scan_meta_ffn",
    )(p, p, p, p, gates, bias, gout, c0, n0, m0, xr, gamma, wupc, conv_w, conv_b, wdnc, st, gfin)


def _mmres_ffn_kernel(*refs, tm_r):
    a_ref, w_ref, x_ref = refs[0:3]
    ffn_in = refs[3:11]
    o_ref = refs[11]
    ffn_out = refs[12:14]
    ffn_scr = refs[14:18]
    _mmres_kernel(a_ref, w_ref, x_ref, o_ref)
    step = pl.program_id(0)

    @pl.when(step < NF // 2)
    def _():
        _ffn_bf16_kernel(*ffn_in, *ffn_out, *ffn_scr, tm=tm_r, sr=8, shift=1, tps=1, nsub=1,
                         final_norm=False, step=step)


def _mmres_ffn(a, w, x, xr, layer, gamma, wupc, conv_w, conv_b, wdnc, st, gfin, *, tm):
    m, kdim = a.shape
    n = w.shape[1]
    tm_r = xr.shape[0]
    tf2 = 2 * TF
    nf2 = NF // 2
    assert m // tm >= nf2

    def fi(i):
        return jnp.minimum(i, nf2 - 1)

    return pl.pallas_call(
        functools.partial(_mmres_ffn_kernel, tm_r=tm_r),
        grid=(m // tm,),
        in_specs=[pl.BlockSpec((tm, kdim), lambda i: (i, 0)),
                  pl.BlockSpec((kdim, n), lambda i: (0, 0)),
                  pl.BlockSpec((tm, n), lambda i: (i, 0)),
                  pl.BlockSpec((tm_r, D), lambda i: (0, 0)),
                  pl.BlockSpec((1, D), lambda i: (0, 0)),
                  pl.BlockSpec((D, 2 * tf2), lambda i: (0, fi(i))),
                  pl.BlockSpec((None, 3, tf2), lambda i: (layer, 0, fi(i))),
                  pl.BlockSpec((None, 1, tf2), lambda i: (layer, 0, fi(i))),
                  pl.BlockSpec((tf2, D), lambda i: (fi(i), 0)),
                  pl.BlockSpec((8, tf2), lambda i: (0, fi(i))),
                  pl.BlockSpec((1, D), lambda i: (0, 0))],
        out_specs=[pl.BlockSpec((tm, n), lambda i: (i, 0)),
                   pl.BlockSpec((tm_r, D), lambda i: (0, 0)),
                   pl.BlockSpec((8, tf2), lambda i: (0, fi(i)))],
        out_shape=[jax.ShapeDtypeStruct((m, n), F32),
                   jax.ShapeDtypeStruct((tm_r, D), F32),
                   jax.ShapeDtypeStruct((8, D_FF), F32)],
        scratch_shapes=[pltpu.VMEM((tm_r, D), BF16),
                        pltpu.VMEM((8 + tm_r, tf2), F32),
                        pltpu.VMEM((tm_r, tf2), F32),
                        pltpu.VMEM((1, 8, tf2), F32)],
        compiler_params=_params(1),
        name="out_proj_meta_ffn",
    )(a, w, x, xr, gamma, wupc, conv_w, conv_b, wdnc, st, gfin)


def _window_sum(ext_scr, tmp_scrs, base, tm, shift, w):
    starts = {w: base}
    v = w
    while v > 2:
        starts[v // 2] = (starts[v] - (v // 2) * shift) // 8 * 8
        v //= 2
    src = ext_scr
    v = 1
    k = 0
    while True:
        lo = starts[2 * v]
        n = base + tm - lo
        val = src[lo:lo + n, :] + src[lo - v * shift:lo - v * shift + n, :]
        v *= 2
        if v == w:
            return val
        dst = tmp_scrs[k % 2]
        dst[lo:lo + n, :] = val
        src = dst
        k += 1


def _pool_kernel(x_ref, gam_ref, wp_ref, sc_ref, st_ref, o_ref, ut_ref, *scratch, tm, shift, tps, pos0, tr):
    ng = len(POOL_WINDOWS)
    uext_scrs = scratch[0:ng]
    tmp_scrs = scratch[ng:ng + 2]
    carry_scr = scratch[ng + 2]
    i = pl.program_id(0)
    hr = POOL_HALO * shift
    base = POOL_PAD_ROWS + hr

    if tps > 1:
        @pl.when(i == 0)
        def _():
            carry_scr[...] = jnp.zeros_like(carry_scr)

    x = x_ref[...]
    rinv = lax.rsqrt(jnp.mean(x * x, axis=1, keepdims=True) + EPS)
    for kk, w in enumerate(POOL_WINDOWS):
        cs = slice(kk * PG, (kk + 1) * PG)
        uext_scr = uext_scrs[kk]
        xg = x_ref[:, cs]
        ug = xg * rinv * gam_ref[:, cs]
        uext_scr[0:POOL_PAD_ROWS, :] = jnp.zeros((POOL_PAD_ROWS, PG), F32)
        if tps == 1:
            uext_scr[POOL_PAD_ROWS:base, :] = st_ref[:, cs]
        else:
            uext_scr[POOL_PAD_ROWS:base, :] = jnp.where((i % tps) == 0, st_ref[:, cs], carry_scr[:, cs])
        uext_scr[base:base + tm, :] = ug
        acc = _window_sum(uext_scr, tmp_scrs, base, tm, shift, w)
        if pos0 + 1 >= w:
            pooled = acc / float(w) - ug
        else:
            step = (i % tps) * (tm // shift) + lax.broadcasted_iota(jnp.int32, (tm, 1), 0) // shift
            cnt = jnp.minimum(w, pos0 + step + 1).astype(F32)
            pooled = acc / cnt - ug
        y = jnp.dot(pooled.astype(BF16), wp_ref[kk].astype(BF16), preferred_element_type=F32)
        o_ref[:, cs] = xg + y * sc_ref[:, cs]
        ut_ref[:, cs] = ug[tm - tr:tm, :]
        if tps > 1:
            carry_scr[:, cs] = ug[tm - hr:tm, :]


def _pool(x, gamma, wp, sc, st, *, tm, shift, tps, pos0, tr):
    m = x.shape[0]
    nt = m // tm
    hr = POOL_HALO * shift
    ng = len(POOL_WINDOWS)
    kern = functools.partial(_pool_kernel, tm=tm, shift=shift, tps=tps, pos0=pos0, tr=tr)
    return pl.pallas_call(
        kern,
        grid=(nt,),
        in_specs=[pl.BlockSpec((tm, D), lambda i: (i, 0)),
                  pl.BlockSpec((1, D), lambda i: (0, 0)),
                  pl.BlockSpec((None, ng, PG, PG), lambda i: (0, 0, 0, 0)),
                  pl.BlockSpec((1, D), lambda i: (0, 0)),
                  pl.BlockSpec((hr, D), lambda i: (0, 0))],
        out_specs=[pl.BlockSpec((tm, D), lambda i: (i, 0)),
                   pl.BlockSpec((tr, D), lambda i: (i, 0))],
        out_shape=[jax.ShapeDtypeStruct((m, D), F32),
                   jax.ShapeDtypeStruct((nt * tr, D), F32)],
        scratch_shapes=[pltpu.VMEM((POOL_PAD_ROWS + hr + tm, PG), F32)] * (ng + 2)
        + [pltpu.VMEM((hr, D), F32)],
        compiler_params=_params(1),
        name="pool_mixer",
    )(x, gamma, wp, sc, st)


def _pool_sample_kernel(x_ref, gam_ref, wp_ref, sc_ref, st_ref, o_ref, new_ref, rinv_scr, uext_scr,
                        *, nseq, t_dec):
    nh = POOL_HALO - 1
    grp = pl.program_id(0)
    tm = nseq * t_dec
    hr = POOL_HALO * nseq

    @pl.when(grp == 0)
    def _():
        x = x_ref[...]
        rinv_scr[...] = lax.rsqrt(jnp.mean(x * x, axis=1, keepdims=True) + EPS)

    for kk, w in enumerate(POOL_WINDOWS):
        @pl.when(grp == kk)
        def _(kk=kk, w=w):
            cs = slice(kk * PG, (kk + 1) * PG)
            xg = x_ref[:, cs]
            ug = xg * rinv_scr[...] * gam_ref[:, cs]
            for t in range(nh):
                uext_scr[(t + 1) * nseq:(t + 2) * nseq, :] = st_ref[t]
            uext_scr[hr:hr + tm, :] = ug
            acc = ug
            for j in range(1, w):
                acc = acc + uext_scr[hr - j * nseq:hr - j * nseq + tm, :]
            pooled = acc / float(w) - ug
            y = jnp.dot(pooled.astype(BF16), wp_ref[kk].astype(BF16), preferred_element_type=F32)
            o_ref[...] = xg + y * sc_ref[:, cs]
            for t in range(nh):
                src = (t + t_dec + 1) * nseq
                new_ref[t] = uext_scr[src:src + nseq, :]


def _pool_sample(x, gamma, wp, sc, state, *, nseq, t_dec):
    assert PAST_LEN + 1 >= max(POOL_WINDOWS)
    nh = POOL_HALO - 1
    ng = len(POOL_WINDOWS)
    tm = nseq * t_dec
    hr = POOL_HALO * nseq
    kern = functools.partial(_pool_sample_kernel, nseq=nseq, t_dec=t_dec)
    hist_spec = pl.BlockSpec((nh, nseq, PG), lambda g: (0, 0, g))
    return pl.pallas_call(
        kern,
        grid=(ng,),
        in_specs=[pl.BlockSpec((tm, D), lambda g: (0, 0)),
                  pl.BlockSpec((1, D), lambda g: (0, 0)),
                  pl.BlockSpec((None, ng, PG, PG), lambda g: (0, 0, 0, 0)),
                  pl.BlockSpec((1, D), lambda g: (0, 0)),
                  hist_spec],
        out_specs=[pl.BlockSpec((tm, PG), lambda g: (0, g)),
                   hist_spec],
        out_shape=[jax.ShapeDtypeStruct((tm, D), F32),
                   jax.ShapeDtypeStruct((nh, nseq, D), F32)],
        scratch_shapes=[pltpu.VMEM((tm, 1), F32),
                        pltpu.VMEM((hr + tm, PG), F32)],
        compiler_params=_params(1),
        name="pool_mixer_sample",
    )(x, gamma, wp, sc, state)


def _pad_cols(a, n):
    return jnp.pad(a, ((0, 0), (0, n - a.shape[1])))


def _from_time_major(a, nseq, t):
    return jnp.swapaxes(a.reshape((t, nseq) + a.shape[1:]), 0, 1)


def kernel(x_prompt, x_sample, state_mlstm_C, state_mlstm_n, state_mlstm_m, state_pool, state_ffn_conv,
           meta_tokens, norm_mix, norm_ffn, norm_final, w_mlstm_in, b_mlstm_gate, g_mlstm_out, w_mlstm_out,
           w_pool, pool_scale, w_up, conv_w, conv_b, w_down):
    bsz, seq, _ = x_prompt.shape
    nseq, t_dec, _ = x_sample.shape

    w_in_t = jnp.swapaxes(w_mlstm_in[0], 0, 1)
    bias = _pad_cols(b_mlstm_gate[0][None, :], GATE_LANES)
    w_out = w_mlstm_out
    gout = g_mlstm_out[0][None, :]
    wp = w_pool
    psc = pool_scale[0][None, :]
    gfin = norm_final[None, :]
    conv_b3 = conv_b[:, None, :]
    ffn_w = [(layer, norm_ffn[layer][None, :], w_up, conv_w, conv_b3, w_down) for layer in range(2)]
    g_mix0 = norm_mix[0][None, :]
    g_mix1 = norm_mix[1][None, :]

    ffn_cache = {}
    mix_cache = {}

    def proj(x, **kw):
        if "w_in" not in mix_cache:
            p, gates, wtc, wgc = _proj(x, g_mix0, w_in_t, w_in_t, **kw)
            mix_cache["w_in"] = (wtc, wgc)
            return p, gates
        return _proj(x, g_mix0, *mix_cache["w_in"], **kw)

    def out_proj(a, x, *, tm):
        if "w_out" not in mix_cache:
            y, mix_cache["w_out"] = _mmres_cast(a, w_out, x, tn=512)
            return y
        return _mmres(a, mix_cache["w_out"], x, tm=tm)

    def ffn_first(x, layer, st, *, final_norm):
        lyr, gamma, wu, cwt, cbs, wd = ffn_w[layer]
        y, cs, wupc, wdnc = _ffn(x, lyr, gamma, wu, cwt, cbs, wd, st, gfin, final_norm=final_norm)
        ffn_cache[layer] = (wupc, wdnc)
        return y, cs

    def ffn(x, layer, st, **kw):
        lyr, gamma, _, cwt, cbs, _ = ffn_w[layer]
        wupc, wdnc = ffn_cache[layer]
        return _ffn_bf16(x, lyr, gamma, wupc, cwt, cbs, wdnc, st, gfin, **kw)

    def long_stream(x, nb, s, st, *, tm, chunk, lead_pad, pos0, projected=None, rider=None):
        c0, n0, m0, conv0, pool0, conv1 = st
        tm_ffn = tm
        p, gates = projected if projected is not None else proj(x, tm=tm, tn=1024, out_dtype=BF16)
        p = p.reshape(nb, s, PW)
        gates = gates.reshape(nb, s, GATE_LANES)
        if lead_pad:
            p = jnp.pad(p, ((0, 0), (lead_pad, 0), (0, 0)))
            gates = jnp.pad(gates, ((0, 0), (lead_pad, 0), (0, 0)))
        if rider is None:
            hg, c_new, n_new, m_new = _scan(p, gates, bias, gout, c0, n0, m0, L=chunk, lead_pad=lead_pad)
        else:
            lyr, gamma, _, cwt, cbs, _ = ffn_w[0]
            wupc, wdnc = ffn_cache[0]
            hg, c_new, n_new, m_new, x2_m, cs0_m = _scan_ffn(
                p, gates, bias, gout, c0, n0, m0, rider, lyr, gamma, wupc, cwt, cbs, wdnc,
                jnp.zeros((8, D_FF), F32), gfin, L=chunk)
            x3_m, pool0 = _pool(x2_m, g_mix1, wp, psc, jnp.zeros((POOL_HALO, D), F32), tm=N_META, shift=1,
                                tps=1, pos0=0, tr=POOL_HALO)
            conv0 = cs0_m
        hg = hg[:, lead_pad:].reshape(nb * s, VW)
        if rider is None:
            x1 = out_proj(hg, x, tm=min(tm, 512))
        else:
            lyr, gamma, _, cwt, cbs, _ = ffn_w[1]
            wupc, wdnc = ffn_cache[1]
            x1, _, conv1 = _mmres_ffn(hg, mix_cache["w_out"], x, x3_m, lyr, gamma, wupc, cwt, cbs, wdnc,
                                      jnp.zeros((8, D_FF), F32), gfin, tm=min(tm, 512))
        x2, cs0 = ffn(x1, 0, conv0, tm=tm_ffn, sr=8, shift=1, tps=s // tm_ffn,
                      st_per_tile=False, final_norm=False)
        x3, ut = _pool(x2, g_mix1, wp, psc, pool0, tm=tm, shift=1, tps=s // tm, pos0=pos0,
                       tr=POOL_HALO)
        y, cs1 = ffn(x3, 1, conv1, tm=tm_ffn, sr=8, shift=1, tps=s // tm_ffn,
                     st_per_tile=False, final_norm=True)
        cs0 = cs0.reshape(nb, s // tm_ffn, 8, D_FF)[:, -1]
        cs1 = cs1.reshape(nb, s // tm_ffn, 8, D_FF)[:, -1]
        ut = ut.reshape(nb, s // tm, POOL_HALO, D)[:, -1]
        return y, (c_new, n_new, m_new, cs0, ut, cs1)

    def prompt_streams():
        lead = 128 - N_META
        p_m, gates_m = proj(meta_tokens, tm=N_META, tn=1024, out_dtype=BF16)
        hg_m, c_m, n_m, m_m = _scan(jnp.pad(p_m, ((lead, 0), (0, 0)))[None], jnp.pad(gates_m, ((lead, 0), (0, 0)))[None],
                                    bias, gout, jnp.zeros((H, DK, DV), F32), jnp.zeros((H, 8, DK), F32),
                                    jnp.zeros((H, 8, GATE_LANES), F32), L=128, lead_pad=lead)
        x1_m = out_proj(hg_m[0, lead:], meta_tokens, tm=N_META)

        y_p, (c_p, n_p, m_p, cs0_p, ut_p, cs1_p) = long_stream(
            x_prompt.reshape(bsz * seq, D), bsz, seq, (c_m[0], n_m[0], m_m[0], None, None, None),
            tm=1024, chunk=SCAN_CHUNK, lead_pad=0, pos0=N_META, projected=prompt_projected, rider=x1_m)
        return (y_p.reshape(bsz, seq, D), c_p[None], n_p[:, :, 0][None], m_p[:, :, 0, 0][None],
                ut_p[:, 1:][None], jnp.stack([cs0_p[:, 6:], cs1_p[:, 6:]]))

    xs = x_sample.reshape(nseq * t_dec, D)
    p_s, gates_s = proj(xs, tm=nseq * t_dec, tn=1024, out_dtype=F32)
    mtok = _pad_cols(jnp.repeat(state_mlstm_m[0], t_dec, axis=0), GATE_LANES)
    n_hm = jnp.swapaxes(state_mlstm_n[0], 0, 1)
    p_p, gates_p, hg_s, C_s, n_s_hm, m_s_hm = _proj_scan_s(
        x_prompt.reshape(bsz * seq, D), g_mix0, *mix_cache["w_in"], p_s, gates_s, bias, gout, mtok,
        state_mlstm_C[0], n_hm, tm=1024, T=t_dec)
    prompt_projected = (p_p, gates_p)
    assert nseq == SAMPLE_TILE_SEQS
    x1 = out_proj(hg_s.reshape(t_dec * nseq, VW), x_sample, tm=512)
    x2, cs0_s = ffn_first(x1, 0, state_ffn_conv[0], final_norm=False)
    x3, pool_new = _pool_sample(x2, g_mix1, wp, psc, jnp.swapaxes(state_pool[0], 0, 1), nseq=nseq, t_dec=t_dec)
    y_s, cs1_s = ffn_first(x3, 1, state_ffn_conv[1], final_norm=True)
    y_sample = _from_time_major(y_s, nseq, t_dec)
    n_s = jnp.swapaxes(n_s_hm, 0, 1)[None]
    m_s = jnp.swapaxes(m_s_hm[:, :, 0], 0, 1)[None]
    pool_s = jnp.swapaxes(pool_new, 0, 1)[None]
    conv_s = jnp.stack([cs0_s, cs1_s])

    y_prompt, C_p, n_p, m_p, pool_p, conv_p = prompt_streams()
    return (y_prompt, y_sample, C_p, n_p, m_p, pool_p, conv_p,
            C_s[None], n_s, m_s, pool_s, conv_s)
```
